```python
import jax, jax.numpy as jnp
from jax import lax
import numpy as np

D_MODEL = 1024
BATCH = 8
SEQ = 8192
DEPTH = 1

EXPAND = 2
D_MIX = EXPAND * D_MODEL
C_CONV = D_MIX // 2
C_SGU = D_MIX - C_CONV
CONV_GROUPS = 8
SGU_HEADS = 8
SGU_HEAD_DIM = C_SGU // SGU_HEADS
CHUNK = 128
CONV_WIDTH = 31
CONV_PAD = CONV_WIDTH // 2
D_IN = 3 * C_CONV + 3 * C_SGU
EPS = 1e-6

kernel_name = "hybrid_conformer_conv_chunked_sgu_block"


def rms_norm(x, g):
    xf = x.astype(jnp.float32)
    y = xf * lax.rsqrt(jnp.mean(xf * xf, axis=-1, keepdims=True) + EPS)
    return (y * g.astype(jnp.float32)).astype(x.dtype)


def layer_norm(x, g, b):
    xf = x.astype(jnp.float32)
    mu = jnp.mean(xf, axis=-1, keepdims=True)
    xc = xf - mu
    var = jnp.mean(xc * xc, axis=-1, keepdims=True)
    y = xc * lax.rsqrt(var + EPS)
    return (y * g.astype(jnp.float32) + b.astype(jnp.float32)).astype(x.dtype)


def depthwise_conv(x, w, b):
    y = lax.conv_general_dilated(
        x, w[:, None, :].astype(x.dtype), window_strides=(1,),
        padding=[(CONV_PAD, CONV_PAD)],
        dimension_numbers=("NWC", "WIO", "NWC"),
        feature_group_count=x.shape[-1])
    return y + b.astype(x.dtype)


def conformer_conv_branch(a_val, a_gate, conv_w, conv_b, ln_g, ln_b):
    h = a_val * jax.nn.sigmoid(a_gate)
    h = depthwise_conv(h, conv_w, conv_b)
    h = layer_norm(h, ln_g, ln_b)
    return jax.nn.silu(h)


def chunked_sgu_branch(u, v, ln_g, ln_b, w_s, b_s):
    bsz, seq, _ = v.shape
    n_chunks = seq // CHUNK
    v = layer_norm(v, ln_g, ln_b)
    v = v.reshape(bsz, n_chunks, CHUNK, SGU_HEADS, SGU_HEAD_DIM)
    mixed = jnp.einsum("hpq,bcqhd->bcphd", w_s.astype(v.dtype), v)
    mixed = mixed + jnp.transpose(b_s).astype(v.dtype)[None, None, :, :, None]
    return u * mixed.reshape(bsz, seq, C_SGU)


def _fwd_setup_inputs(seed: int = 0) -> dict:
    key = jax.random.key(seed)
    ks = jax.random.split(key, 16)
    f32 = jnp.float32
    x = jax.random.normal(ks[0], (BATCH, SEQ, D_MODEL), f32)
    norm_g = 1.0 + 0.02 * jax.random.normal(ks[1], (DEPTH, D_MODEL), f32)
    w_in = jax.random.normal(ks[2], (DEPTH, D_MODEL, D_IN), f32) * D_MODEL ** -0.5
    conv_w = jax.random.normal(ks[3], (DEPTH, CONV_WIDTH, C_CONV), f32) * CONV_WIDTH ** -0.5
    conv_b = 0.02 * jax.random.normal(ks[4], (DEPTH, C_CONV), f32)
    conv_ln_g = 1.0 + 0.02 * jax.random.normal(ks[5], (DEPTH, C_CONV), f32)
    conv_ln_b = 0.02 * jax.random.normal(ks[6], (DEPTH, C_CONV), f32)
    sgu_ln_g = 1.0 + 0.02 * jax.random.normal(ks[7], (DEPTH, C_SGU), f32)
    sgu_ln_b = 0.02 * jax.random.normal(ks[8], (DEPTH, C_SGU), f32)
    w_s = jax.random.normal(ks[9], (DEPTH, SGU_HEADS, CHUNK, CHUNK), f32) * CHUNK ** -0.5
    b_s = 1.0 + 0.02 * jax.random.normal(ks[10], (DEPTH, SGU_HEADS, CHUNK), f32)
    w_out = jax.random.normal(ks[11], (DEPTH, D_MIX, D_MODEL), f32) * D_MIX ** -0.5
    final_g = 1.0 + 0.02 * jax.random.normal(ks[12], (D_MODEL,), f32)
    return {"x": x, "norm_g": norm_g, "w_in": w_in, "conv_w": conv_w, "conv_b": conv_b,
            "conv_ln_g": conv_ln_g, "conv_ln_b": conv_ln_b, "sgu_ln_g": sgu_ln_g,
            "sgu_ln_b": sgu_ln_b, "w_s": w_s, "b_s": b_s, "w_out": w_out, "final_g": final_g}


def _fwd_reference(x, norm_g, w_in, conv_w, conv_b, conv_ln_g, conv_ln_b, sgu_ln_g,
              sgu_ln_b, w_s, b_s, w_out, final_g):
    split_points = [C_CONV, 2 * C_CONV, 3 * C_CONV,
                    3 * C_CONV + C_SGU, 3 * C_CONV + 2 * C_SGU]
    for l in range(DEPTH):
        h = rms_norm(x, norm_g[l])
        proj = jnp.einsum("bsd,de->bse", h, w_in[l].astype(h.dtype))
        a_val, a_gate, g_conv, u, v, g_sgu = jnp.split(proj, split_points, axis=-1)
        y_conv = conformer_conv_branch(a_val, a_gate, conv_w[l], conv_b[l],
                                       conv_ln_g[l], conv_ln_b[l]) * jax.nn.silu(g_conv)
        y_sgu = chunked_sgu_branch(u, v, sgu_ln_g[l], sgu_ln_b[l],
                                   w_s[l], b_s[l]) * jax.nn.silu(g_sgu)
        y = jnp.concatenate([y_conv, y_sgu], axis=-1)
        x = x + jnp.einsum("bse,ed->bsd", y, w_out[l].astype(y.dtype))
    return rms_norm(x, final_g)


import jax as _jax
import jax.numpy as _jnp

TWIN_FORMAT = 'train_step'
FWD_PARAMS = ['x', 'norm_g', 'w_in', 'conv_w', 'conv_b', 'conv_ln_g', 'conv_ln_b', 'sgu_ln_g', 'sgu_ln_b', 'w_s', 'b_s', 'w_out', 'final_g']
TWIN_WEIGHTS = ['norm_g', 'w_in', 'conv_w', 'conv_b', 'conv_ln_g', 'conv_ln_b', 'sgu_ln_g', 'sgu_ln_b', 'w_s', 'b_s', 'w_out', 'final_g']
TWIN_DIFF_INPUT = 'x'
TWIN_INPUTS = ['x', 'norm_g', 'w_in', 'conv_w', 'conv_b', 'conv_ln_g', 'conv_ln_b', 'sgu_ln_g', 'sgu_ln_b', 'w_s', 'b_s', 'w_out', 'final_g', 'loss_target', 'm_norm_g', 'm_w_in', 'm_conv_w', 'm_conv_b', 'm_conv_ln_g', 'm_conv_ln_b', 'm_sgu_ln_g', 'm_sgu_ln_b', 'm_w_s', 'm_b_s', 'm_w_out', 'm_final_g', 'v_norm_g', 'v_w_in', 'v_conv_w', 'v_conv_b', 'v_conv_ln_g', 'v_conv_ln_b', 'v_sgu_ln_g', 'v_sgu_ln_b', 'v_w_s', 'v_b_s', 'v_w_out', 'v_final_g']
TWIN_OUTPUTS = ['loss', 'grad_x', 'grad_norm_g', 'grad_w_in', 'grad_conv_w', 'grad_conv_b', 'grad_conv_ln_g', 'grad_conv_ln_b', 'grad_sgu_ln_g', 'grad_sgu_ln_b', 'grad_w_s', 'grad_b_s', 'grad_w_out', 'grad_final_g', 'delta_norm_g', 'delta_w_in', 'delta_conv_w', 'delta_conv_b', 'delta_conv_ln_g', 'delta_conv_ln_b', 'delta_sgu_ln_g', 'delta_sgu_ln_b', 'delta_w_s', 'delta_b_s', 'delta_w_out', 'delta_final_g', 'new_m_norm_g', 'new_m_w_in', 'new_m_conv_w', 'new_m_conv_b', 'new_m_conv_ln_g', 'new_m_conv_ln_b', 'new_m_sgu_ln_g', 'new_m_sgu_ln_b', 'new_m_w_s', 'new_m_b_s', 'new_m_w_out', 'new_m_final_g', 'new_v_norm_g', 'new_v_w_in', 'new_v_conv_w', 'new_v_conv_b', 'new_v_conv_ln_g', 'new_v_conv_ln_b', 'new_v_sgu_ln_g', 'new_v_sgu_ln_b', 'new_v_w_s', 'new_v_b_s', 'new_v_w_out', 'new_v_final_g']
TWIN_LEAF_KINDS = {'loss': 'loss', 'grad_x': 'grad_x', 'grad_norm_g': 'grad_w', 'grad_w_in': 'grad_w', 'grad_conv_w': 'grad_w', 'grad_conv_b': 'grad_w', 'grad_conv_ln_g': 'grad_w', 'grad_conv_ln_b': 'grad_w', 'grad_sgu_ln_g': 'grad_w', 'grad_sgu_ln_b': 'grad_w', 'grad_w_s': 'grad_w', 'grad_b_s': 'grad_w', 'grad_w_out': 'grad_w', 'grad_final_g': 'grad_w', 'delta_norm_g': 'delta_w', 'delta_w_in': 'delta_w', 'delta_conv_w': 'delta_w', 'delta_conv_b': 'delta_w', 'delta_conv_ln_g': 'delta_w', 'delta_conv_ln_b': 'delta_w', 'delta_sgu_ln_g': 'delta_w', 'delta_sgu_ln_b': 'delta_w', 'delta_w_s': 'delta_w', 'delta_b_s': 'delta_w', 'delta_w_out': 'delta_w', 'delta_final_g': 'delta_w', 'new_m_norm_g': 'new_m', 'new_m_w_in': 'new_m', 'new_m_conv_w': 'new_m', 'new_m_conv_b': 'new_m', 'new_m_conv_ln_g': 'new_m', 'new_m_conv_ln_b': 'new_m', 'new_m_sgu_ln_g': 'new_m', 'new_m_sgu_ln_b': 'new_m', 'new_m_w_s': 'new_m', 'new_m_b_s': 'new_m', 'new_m_w_out': 'new_m', 'new_m_final_g': 'new_m', 'new_v_norm_g': 'new_v', 'new_v_w_in': 'new_v', 'new_v_conv_w': 'new_v', 'new_v_conv_b': 'new_v', 'new_v_conv_ln_g': 'new_v', 'new_v_conv_ln_b': 'new_v', 'new_v_sgu_ln_g': 'new_v', 'new_v_sgu_ln_b': 'new_v', 'new_v_w_s': 'new_v', 'new_v_b_s': 'new_v', 'new_v_w_out': 'new_v', 'new_v_final_g': 'new_v'}


def _forward(args):
    return _fwd_reference(*[args[k] for k in FWD_PARAMS])


def _output_shape():
    out = _jax.eval_shape(lambda: _forward(_fwd_setup_inputs(0)))
    return out.shape, out.dtype

N_MICROBATCH = 1
ADAM_LR = 0.001
ADAM_B1 = 0.9
ADAM_B2 = 0.999
ADAM_EPS = 1e-08
ADAM_WD = 0.01
ADAM_STEP = 10
PER_EXAMPLE_BATCH_AXIS = {'x': 0, 'loss_target': 0}
SHARED_INPUTS = []
_WEIGHT_DTYPES = {'norm_g': _jnp.float32, 'w_in': _jnp.float32, 'conv_w': _jnp.float32, 'conv_b': _jnp.float32, 'conv_ln_g': _jnp.float32, 'conv_ln_b': _jnp.float32, 'sgu_ln_g': _jnp.float32, 'sgu_ln_b': _jnp.float32, 'w_s': _jnp.float32, 'b_s': _jnp.float32, 'w_out': _jnp.float32, 'final_g': _jnp.float32}
MOMENT_SCALE = {'norm_g': 2.445677e-01, 'w_in': 8.903972e-02, 'conv_w': 5.529925e-02, 'conv_b': 1.047405e-01, 'conv_ln_g': 6.427219e-02, 'conv_ln_b': 5.513961e-02, 'sgu_ln_g': 9.106763e-02, 'sgu_ln_b': 9.292866e-02, 'w_s': 8.869919e-02, 'b_s': 8.658213e-02, 'w_out': 1.359821e-01, 'final_g': 6.397225e+01}


def _to_microbatches(a, axis):
    t = _jnp.moveaxis(a, axis, 0)
    t = t.reshape((N_MICROBATCH, t.shape[0] // N_MICROBATCH) + t.shape[1:])
    return _jnp.moveaxis(t, 1, axis + 1)


def setup_inputs(seed: int = 0) -> dict:
    inp = _fwd_setup_inputs(seed)
    key = _jax.random.fold_in(_jax.random.key(seed), 7919)
    shape, _ = _output_shape()
    out = dict(inp)
    out["loss_target"] = _jax.random.normal(_jax.random.fold_in(key, 0), shape, _jnp.float32)
    for i, name in enumerate(TWIN_WEIGHTS):
        w = inp[name].astype(_jnp.float32)
        if MOMENT_SCALE is None:
            s = _jnp.sqrt(_jnp.mean(_jnp.square(w)) + 1e-30)
        else:
            s = MOMENT_SCALE[name]
        km, kv = _jax.random.split(_jax.random.fold_in(key, i + 1))
        out[name] = w
        out["m_" + name] = s * _jax.random.normal(km, w.shape, _jnp.float32)
        out["v_" + name] = (s * s) * _jax.random.uniform(kv, w.shape, _jnp.float32, 0.5, 1.5)
    if N_MICROBATCH > 1:
        for name, axis in PER_EXAMPLE_BATCH_AXIS.items():
            out[name] = _to_microbatches(out[name], axis)
    return {'x': out['x'], 'norm_g': out['norm_g'], 'w_in': out['w_in'], 'conv_w': out['conv_w'], 'conv_b': out['conv_b'], 'conv_ln_g': out['conv_ln_g'], 'conv_ln_b': out['conv_ln_b'], 'sgu_ln_g': out['sgu_ln_g'], 'sgu_ln_b': out['sgu_ln_b'], 'w_s': out['w_s'], 'b_s': out['b_s'], 'w_out': out['w_out'], 'final_g': out['final_g'], 'loss_target': out['loss_target'], 'm_norm_g': out['m_norm_g'], 'm_w_in': out['m_w_in'], 'm_conv_w': out['m_conv_w'], 'm_conv_b': out['m_conv_b'], 'm_conv_ln_g': out['m_conv_ln_g'], 'm_conv_ln_b': out['m_conv_ln_b'], 'm_sgu_ln_g': out['m_sgu_ln_g'], 'm_sgu_ln_b': out['m_sgu_ln_b'], 'm_w_s': out['m_w_s'], 'm_b_s': out['m_b_s'], 'm_w_out': out['m_w_out'], 'm_final_g': out['m_final_g'], 'v_norm_g': out['v_norm_g'], 'v_w_in': out['v_w_in'], 'v_conv_w': out['v_conv_w'], 'v_conv_b': out['v_conv_b'], 'v_conv_ln_g': out['v_conv_ln_g'], 'v_conv_ln_b': out['v_conv_ln_b'], 'v_sgu_ln_g': out['v_sgu_ln_g'], 'v_sgu_ln_b': out['v_sgu_ln_b'], 'v_w_s': out['v_w_s'], 'v_b_s': out['v_b_s'], 'v_w_out': out['v_w_out'], 'v_final_g': out['v_final_g']}


def _loss(weights, diff, rest, loss_target):
    with _jax.named_scope("forward"):
        args = {**rest, TWIN_DIFF_INPUT: diff, **{k: w.astype(_WEIGHT_DTYPES[k]) for k, w in weights.items()}}
        y = _forward(args)
    with _jax.named_scope("loss_head"):
        err = _jnp.square(y.astype(_jnp.float32) - loss_target)
        return 0.5 * _jnp.sum(_jnp.mean(err, axis=-1)) if err.ndim else 0.5 * err


def _adamw(w, g, m, v):
    m = ADAM_B1 * m + (1.0 - ADAM_B1) * g
    v = ADAM_B2 * v + (1.0 - ADAM_B2) * _jnp.square(g)
    m_hat = m / (1.0 - ADAM_B1 ** ADAM_STEP)
    v_hat = v / (1.0 - ADAM_B2 ** ADAM_STEP)
    delta = -ADAM_LR * (m_hat / (_jnp.sqrt(v_hat) + ADAM_EPS) + ADAM_WD * w)
    return delta, m, v


def reference(x, norm_g, w_in, conv_w, conv_b, conv_ln_g, conv_ln_b, sgu_ln_g, sgu_ln_b, w_s, b_s, w_out, final_g, loss_target, m_norm_g, m_w_in, m_conv_w, m_conv_b, m_conv_ln_g, m_conv_ln_b, m_sgu_ln_g, m_sgu_ln_b, m_w_s, m_b_s, m_w_out, m_final_g, v_norm_g, v_w_in, v_conv_w, v_conv_b, v_conv_ln_g, v_conv_ln_b, v_sgu_ln_g, v_sgu_ln_b, v_w_s, v_b_s, v_w_out, v_final_g):
    given = dict(x=x, norm_g=norm_g, w_in=w_in, conv_w=conv_w, conv_b=conv_b, conv_ln_g=conv_ln_g, conv_ln_b=conv_ln_b, sgu_ln_g=sgu_ln_g, sgu_ln_b=sgu_ln_b, w_s=w_s, b_s=b_s, w_out=w_out, final_g=final_g, loss_target=loss_target, m_norm_g=m_norm_g, m_w_in=m_w_in, m_conv_w=m_conv_w, m_conv_b=m_conv_b, m_conv_ln_g=m_conv_ln_g, m_conv_ln_b=m_conv_ln_b, m_sgu_ln_g=m_sgu_ln_g, m_sgu_ln_b=m_sgu_ln_b, m_w_s=m_w_s, m_b_s=m_b_s, m_w_out=m_w_out, m_final_g=m_final_g, v_norm_g=v_norm_g, v_w_in=v_w_in, v_conv_w=v_conv_w, v_conv_b=v_conv_b, v_conv_ln_g=v_conv_ln_g, v_conv_ln_b=v_conv_ln_b, v_sgu_ln_g=v_sgu_ln_g, v_sgu_ln_b=v_sgu_ln_b, v_w_s=v_w_s, v_b_s=v_b_s, v_w_out=v_w_out, v_final_g=v_final_g)
    weights = {n: given[n] for n in TWIN_WEIGHTS}
    shared = {n: given[n] for n in SHARED_INPUTS}
    per_example = {n: given[n] for n in ['x']}
    grad_fn = _jax.value_and_grad(_loss, argnums=(0, 1))

    def one_microbatch(ex, loss_target):
        ex = dict(ex)
        diff = ex.pop(TWIN_DIFF_INPUT)
        return grad_fn(weights, diff, {**shared, **ex}, loss_target)

    if N_MICROBATCH == 1:
        loss, (grad_w, grad_x) = one_microbatch(per_example, given["loss_target"])
    else:
        def body(carry, xs):
            loss_sum, grad_sum = carry
            l_k, (gw_k, gx_k) = one_microbatch(xs[0], xs[1])
            with _jax.named_scope("update"):
                return (loss_sum + l_k, _jax.tree.map(_jnp.add, grad_sum, gw_k)), gx_k

        init = (_jnp.zeros((), _jnp.float32), _jax.tree.map(_jnp.zeros_like, weights))
        (loss, grad_w), grad_x = _jax.lax.scan(body, init, (per_example, given["loss_target"]))
    with _jax.named_scope("update"):
        delta_w, new_m, new_v = {}, {}, {}
        for n in TWIN_WEIGHTS:
            delta_w[n], new_m[n], new_v[n] = _adamw(weights[n], grad_w[n], given["m_" + n], given["v_" + n])
    return (loss, grad_x, *[grad_w[n] for n in TWIN_WEIGHTS], *[delta_w[n] for n in TWIN_WEIGHTS],
            *[new_m[n] for n in TWIN_WEIGHTS], *[new_v[n] for n in TWIN_WEIGHTS])
```

```python
import functools

import jax
import jax.numpy as jnp
from jax import lax
from jax.experimental import pallas as pl
from jax.experimental.pallas import tpu as pltpu

F32 = jnp.float32
BF16 = jnp.bfloat16
MESH = pl.DeviceIdType.MESH

EPS = 1e-6
D_MODEL = 1024
C_BR = 1024
D_IN = 6 * C_BR
N_SHARD = 4
W_IN_COLS = D_IN // N_SHARD
W_OUT_ROWS = 2 * C_BR // N_SHARD
CONV_W_COLS = C_BR // N_SHARD
KW = 31
HALO = 16
HEADS = 8
HEAD_DIM = 128
CHUNK = 128
LANE = 128
SUB = 8

ADAM_LR = 0.001
ADAM_B1 = 0.9
ADAM_B2 = 0.999
ADAM_EPS = 1e-08
ADAM_WD = 0.01
ADAM_STEP = 10

TS_PROJ = 512
TS_FWD = 256
TS_OUT = 256
TS_BWD = 256
TS_INB = 256
TK_GW = 1024
ROWS_A = 64
ROWS_B = 32
VMEM_LIMIT = 56 * 1024 * 1024


def _cparams(sem=None, vmem=VMEM_LIMIT):
    kw = dict(vmem_limit_bytes=vmem)
    if sem is not None:
        kw["dimension_semantics"] = sem
    return pltpu.CompilerParams(**kw)


def _whole_vmem():
    return pl.BlockSpec(memory_space=pltpu.VMEM)


def _sigmoid(v):
    return 1.0 / (1.0 + jnp.exp(-v))


def _fold8(v):
    n, c = v.shape
    return v.reshape(n // SUB, SUB, c).sum(axis=0)


def _dot_nt(a, b):
    return lax.dot_general(a, b, (((1,), (1,)), ((), ())), preferred_element_type=F32)


def _dot_tn(a, b):
    return lax.dot_general(a, b, (((0,), (0,)), ((), ())), preferred_element_type=F32)


def _mesh_pos():
    return lax.axis_index("x"), lax.axis_index("y"), lax.axis_index("c")


def _gather_weights(w_in, w_out, conv_w):
    hin = w_in.shape[0] // 2
    hout = w_out.shape[0] // 2

    def body(win_ref, wout_ref, cw_ref, wing_ref, woutg_ref, cwg_ref, send_sems, recv_sems):
        x, y, c = _mesh_pos()
        me = 2 * x + y
        chips = [(1 - x, y), (x, 1 - y), (1 - x, 1 - y)]
        sibling = (x, y, 1 - c)

        wing_ref[me] = win_ref[...].astype(BF16)
        woutg_ref[me] = wout_ref[...].astype(BF16)
        cwg_ref[me] = cw_ref[...]

        def copy(ref, blk, half, n, sem, dev):
            part = ref.at[blk, pl.ds(half * n, n)] if n else ref.at[blk]
            return pltpu.make_async_remote_copy(
                src_ref=part, dst_ref=part, send_sem=send_sems.at[sem], recv_sem=recv_sems.at[sem],
                device_id=dev, device_id_type=MESH)

        tensors = [(wing_ref, hin, 0), (woutg_ref, hout, 6)]
        started = []
        for ref, n, s0 in tensors:
            for j, (cx, cy) in enumerate(chips):
                cp = copy(ref, me, c, n, s0 + j, (cx, cy, c))
                cp.start()
                started.append(cp)
        for j, (cx, cy) in enumerate(chips):
            cp = copy(cwg_ref, me, 0, 0, 12 + j, (cx, cy, c))
            cp.start()
            started.append(cp)
        for ref, n, s0 in tensors:
            for j, (cx, cy) in enumerate(chips):
                blk = 2 * cx + cy
                copy(ref, blk, c, n, s0 + j, (cx, cy, c)).wait_recv()
                cp = copy(ref, blk, c, n, s0 + 3 + j, sibling)
                cp.start()
                started.append(cp)
        for ref, n, s0 in tensors:
            for j, (cx, cy) in enumerate(chips):
                copy(ref, 2 * cx + cy, 1 - c, n, s0 + 3 + j, sibling).wait_recv()
        for j, (cx, cy) in enumerate(chips):
            copy(cwg_ref, 2 * cx + cy, 0, 0, 12 + j, (cx, cy, c)).wait_recv()
        for cp in started:
            cp.wait_send()

    return pl.pallas_call(
        body, name="gather_weights",
        out_shape=(jax.ShapeDtypeStruct((N_SHARD,) + w_in.shape, BF16),
                   jax.ShapeDtypeStruct((N_SHARD,) + w_out.shape, BF16),
                   jax.ShapeDtypeStruct((N_SHARD,) + conv_w.shape, F32)),
        in_specs=[_whole_vmem()] * 3,
        out_specs=(_whole_vmem(),) * 3,
        scratch_shapes=[pltpu.SemaphoreType.DMA((15,)), pltpu.SemaphoreType.DMA((15,))],
        compiler_params=_cparams(),
    )(w_in, w_out, conv_w)


def _rms_proj(x, norm_g, wing):
    s = x.shape[0]
    ts = min(TS_PROJ, s)

    def body(x_ref, g_ref, w_ref, h_ref, proj_ref):
        xt = x_ref[...]
        r = lax.rsqrt(jnp.mean(xt * xt, axis=-1, keepdims=True) + EPS)
        h = (xt * r * g_ref[...]).astype(BF16)
        h_ref[...] = h
        for k in range(N_SHARD):
            proj_ref[:, k * W_IN_COLS:(k + 1) * W_IN_COLS] = jnp.dot(
                h, w_ref[k], preferred_element_type=F32).astype(BF16)

    return pl.pallas_call(
        body, name="rms_proj",
        grid=(s // ts,),
        out_shape=(jax.ShapeDtypeStruct((s, D_MODEL), BF16), jax.ShapeDtypeStruct((s, D_IN), BF16)),
        in_specs=[pl.BlockSpec((ts, D_MODEL), lambda i: (i, 0)),
                  pl.BlockSpec((1, D_MODEL), lambda i: (0, 0)),
                  _whole_vmem()],
        out_specs=(pl.BlockSpec((ts, D_MODEL), lambda i: (i, 0)),
                   pl.BlockSpec((ts, D_IN), lambda i: (i, 0))),
        compiler_params=_cparams(("parallel",)),
    )(x, norm_g, wing)


def _halo_specs(ts, s, width, col_block):
    per = ts // HALO
    last = s // HALO - 1
    prev = pl.BlockSpec((HALO, width), lambda i: (jnp.maximum(i * per - 1, 0), col_block))
    nxt = pl.BlockSpec((HALO, width), lambda i: (jnp.minimum((i + 1) * per, last), col_block))
    return prev, nxt


def _layer_norm_stats(v):
    mu = jnp.mean(v, axis=-1, keepdims=True)
    vc = v - mu
    var = jnp.mean(vc * vc, axis=-1, keepdims=True)
    rstd = lax.rsqrt(var + EPS)
    return vc * rstd, rstd


def _fill_shifted(sh_ref, ext):
    n = ext.shape[0]
    sh_ref[0] = ext
    for r in range(1, SUB):
        sh_ref[r] = pltpu.roll(ext, n - r, axis=0)


def _branch_fwd(proj, cwg, conv_b, conv_ln_g, conv_ln_b, sgu_ln_g, sgu_ln_b, w_s, bs_t):
    s = proj.shape[0]
    ts = min(TS_FWD, s)
    nt = s // ts
    ra = min(ROWS_A, ts)

    def body(pm_ref, pp_ref, pn_ref, cw_ref, cb_ref, clg_ref, clb_ref, slg_ref, slb_ref, ws_ref, bst_ref,
             y_ref, cv_ref, sh_ref):
        i = pl.program_id(0)
        keep_prev = (i > 0).astype(F32)
        keep_next = (i < nt - 1).astype(F32)

        for lb in range(C_BR // LANE):
            lanes = slice(lb * LANE, (lb + 1) * LANE)
            gates = slice(C_BR + lb * LANE, C_BR + (lb + 1) * LANE)

            def glu(ref):
                return ref[:, lanes].astype(F32) * _sigmoid(ref[:, gates].astype(F32))

            ext = jnp.concatenate([glu(pp_ref) * keep_prev, glu(pm_ref), glu(pn_ref) * keep_next], axis=0)
            _fill_shifted(sh_ref, ext)
            shard, off = divmod(lb * LANE, CONV_W_COLS)
            bias = cb_ref[:, lanes]

            def chunk(jc, carry):
                base = pl.multiple_of(jc * ra, ra)
                acc = jnp.zeros((ra, LANE), F32) + bias
                for k in range(KW):
                    o = k + 1
                    acc = acc + sh_ref[o % SUB, pl.ds(base + SUB * (o // SUB), ra), :] * cw_ref[
                        shard, k:k + 1, off:off + LANE]
                cv_ref[pl.ds(base, ra), lanes] = acc
                return carry

            lax.fori_loop(0, ts // ra, chunk, 0)

        lnh, _ = _layer_norm_stats(cv_ref[...])
        ln = lnh * clg_ref[...] + clb_ref[...]
        gc = pm_ref[:, 2 * C_BR:3 * C_BR].astype(F32)
        y_ref[:, :C_BR] = (ln * _sigmoid(ln) * gc * _sigmoid(gc)).astype(BF16)

        vh, _ = _layer_norm_stats(pm_ref[:, 4 * C_BR:5 * C_BR].astype(F32))
        vn = (vh * slg_ref[...] + slb_ref[...]).astype(BF16)
        for hd in range(HEADS):
            w_h = ws_ref[hd].astype(BF16)
            b_h = bst_ref[:, hd:hd + 1]
            cols = slice(hd * HEAD_DIM, (hd + 1) * HEAD_DIM)
            for ch in range(ts // CHUNK):
                rows = slice(ch * CHUNK, (ch + 1) * CHUNK)
                mixed = jnp.dot(w_h, vn[rows, cols], preferred_element_type=F32) + b_h
                u = pm_ref[rows, 3 * C_BR + hd * HEAD_DIM:3 * C_BR + (hd + 1) * HEAD_DIM].astype(F32)
                gs = pm_ref[rows, 5 * C_BR + hd * HEAD_DIM:5 * C_BR + (hd + 1) * HEAD_DIM].astype(F32)
                y_ref[rows, C_BR + hd * HEAD_DIM:C_BR + (hd + 1) * HEAD_DIM] = (
                    u * mixed * gs * _sigmoid(gs)).astype(BF16)

    prev, nxt = _halo_specs(ts, s, 2 * C_BR, 0)
    row = pl.BlockSpec((1, C_BR), lambda i: (0, 0))
    return pl.pallas_call(
        body, name="branch_fwd",
        grid=(nt,),
        out_shape=(jax.ShapeDtypeStruct((s, 2 * C_BR), BF16), jax.ShapeDtypeStruct((s, C_BR), F32)),
        in_specs=[pl.BlockSpec((ts, D_IN), lambda i: (i, 0)), prev, nxt,
                  _whole_vmem(), row, row, row, row, row, _whole_vmem(), _whole_vmem()],
        out_specs=(pl.BlockSpec((ts, 2 * C_BR), lambda i: (i, 0)),
                   pl.BlockSpec((ts, C_BR), lambda i: (i, 0))),
        scratch_shapes=[pltpu.VMEM((SUB, ts + 2 * HALO, LANE), F32)],
        compiler_params=_cparams(("parallel",)),
    )(proj, proj, proj, cwg, conv_b, conv_ln_g, conv_ln_b, sgu_ln_g, sgu_ln_b, w_s, bs_t)


def _out_proj(y, x, target, woutg, final_g):
    s = x.shape[0]
    ts = min(TS_OUT, s)
    nt = s // ts

    def body(y_ref, x_ref, t_ref, w_ref, g_ref, dx2_ref, dy_ref, gw_ref, gf_ref, se_ref):
        i = pl.program_id(0)

        @pl.when(i == 0)
        def _():
            gw_ref[...] = jnp.zeros_like(gw_ref)
            gf_ref[...] = jnp.zeros_like(gf_ref)
            se_ref[...] = jnp.zeros_like(se_ref)

        yt = y_ref[...]
        x2 = x_ref[...]
        for k in range(N_SHARD):
            x2 = x2 + jnp.dot(yt[:, k * W_OUT_ROWS:(k + 1) * W_OUT_ROWS], w_ref[k], preferred_element_type=F32)
        r2 = lax.rsqrt(jnp.mean(x2 * x2, axis=-1, keepdims=True) + EPS)
        n2 = x2 * r2
        g = g_ref[...]
        diff = n2 * g - t_ref[...]
        se_ref[...] += _fold8(diff * diff)
        dout = diff * (1.0 / D_MODEL)
        gf_ref[...] += _fold8(dout * n2)
        dn = dout * g
        dx2 = r2 * (dn - n2 * jnp.mean(dn * n2, axis=-1, keepdims=True))
        dx2_ref[...] = dx2
        dxb = dx2.astype(BF16)
        for k in range(N_SHARD):
            rows = slice(k * W_OUT_ROWS, (k + 1) * W_OUT_ROWS)
            dy_ref[:, rows] = _dot_nt(dxb, w_ref[k]).astype(BF16)
            gw_ref[rows, :] += _dot_tn(yt[:, rows], dxb)

        @pl.when(i == nt - 1)
        def _():
            gf_ref[...] = jnp.broadcast_to(jnp.sum(gf_ref[...], axis=0, keepdims=True), gf_ref.shape)

    tile = pl.BlockSpec((ts, D_MODEL), lambda i: (i, 0))
    wide = pl.BlockSpec((ts, 2 * C_BR), lambda i: (i, 0))
    acc8 = pl.BlockSpec((SUB, D_MODEL), lambda i: (0, 0))
    return pl.pallas_call(
        body, name="out_proj",
        grid=(nt,),
        out_shape=(jax.ShapeDtypeStruct((s, D_MODEL), F32), jax.ShapeDtypeStruct((s, 2 * C_BR), BF16),
                   jax.ShapeDtypeStruct((2 * C_BR, D_MODEL), F32),
                   jax.ShapeDtypeStruct((SUB, D_MODEL), F32), jax.ShapeDtypeStruct((SUB, D_MODEL), F32)),
        in_specs=[wide, tile, tile, _whole_vmem(), pl.BlockSpec((1, D_MODEL), lambda i: (0, 0))],
        out_specs=(tile, wide, pl.BlockSpec((2 * C_BR, D_MODEL), lambda i: (0, 0)), acc8, acc8),
        compiler_params=_cparams(("arbitrary",)),
    )(y, x, target, woutg, final_g)


def _dsilu(v, sg):
    return sg * (1.0 + v * (1.0 - sg))


def _branch_bwd(proj, dy, cv, cwg, conv_ln_g, conv_ln_b, sgu_ln_g, sgu_ln_b, w_s, ws_t, bs_t):
    s = proj.shape[0]
    ts = min(TS_BWD, s)
    nt = s // ts
    ra = min(ROWS_A, ts)
    rb = min(ROWS_B, ts)
    te = ts + 2 * HALO

    def body(pm_ref, dym_ref, cvm_ref, gcp_ref, gcn_ref, dyp_ref, dyn_ref, cvp_ref, cvn_ref,
             cw_ref, clg_ref, clb_ref, slg_ref, slb_ref, ws_ref, wst_ref, bst_ref,
             dp_ref, small_ref, gws_ref,
             sh_ref, glu_ref, dgl_ref, dcv_ref, acc_ref, gcw_ref, gbs_ref):
        i = pl.program_id(0)

        @pl.when(i == 0)
        def _():
            acc_ref[...] = jnp.zeros_like(acc_ref)
            gcw_ref[...] = jnp.zeros_like(gcw_ref)
            gbs_ref[...] = jnp.zeros_like(gbs_ref)
            gws_ref[...] = jnp.zeros_like(gws_ref)

        def ext(prev_ref, main, next_ref):
            return jnp.concatenate([prev_ref[...].astype(F32), main, next_ref[...].astype(F32)], axis=0)

        main = slice(HALO, HALO + ts)

        cv_e = ext(cvp_ref, cvm_ref[...], cvn_ref)
        gc_e = ext(gcp_ref, pm_ref[:, 2 * C_BR:3 * C_BR].astype(F32), gcn_ref)
        dyc_e = ext(dyp_ref, dym_ref[:, :C_BR].astype(F32), dyn_ref)
        lnh, rstd = _layer_norm_stats(cv_e)
        clg = clg_ref[...]
        ln = lnh * clg + clb_ref[...]
        sg_ln = _sigmoid(ln)
        sg_gc = _sigmoid(gc_e)
        d_ln = dyc_e * gc_e * sg_gc * _dsilu(ln, sg_ln)
        dp_ref[:, 2 * C_BR:3 * C_BR] = (
            dyc_e[main] * ln[main] * sg_ln[main] * _dsilu(gc_e[main], sg_gc[main])).astype(BF16)
        dlnh = d_ln * clg
        d_cv = rstd * (dlnh - jnp.mean(dlnh, axis=-1, keepdims=True)
                       - lnh * jnp.mean(dlnh * lnh, axis=-1, keepdims=True))
        row = lax.broadcasted_iota(jnp.int32, (te, 1), 0)
        valid = jnp.logical_and(jnp.logical_or(row >= HALO, i > 0),
                                jnp.logical_or(row < HALO + ts, i < nt - 1))
        d_cv = jnp.where(valid, d_cv, 0.0)
        dcv_ref[...] = d_cv
        acc_ref[0:8, :] += _fold8(d_cv[main])
        acc_ref[8:16, :] += _fold8(d_ln[main] * lnh[main])
        acc_ref[16:24, :] += _fold8(d_ln[main])

        for lb in range(C_BR // LANE):
            lanes = slice(lb * LANE, (lb + 1) * LANE)
            gates = slice(C_BR + lb * LANE, C_BR + (lb + 1) * LANE)
            shard, off = divmod(lb * LANE, CONV_W_COLS)
            av = pm_ref[:, lanes].astype(F32)
            sg = _sigmoid(pm_ref[:, gates].astype(F32))
            glu_ref[...] = av * sg
            _fill_shifted(sh_ref, dcv_ref[:, lanes])

            def chunk_a(jc, carry):
                base = pl.multiple_of(jc * ra, ra)
                acc = jnp.zeros((ra, LANE), F32)
                for j in range(KW):
                    o = j + 1
                    acc = acc + sh_ref[o % SUB, pl.ds(base + SUB * (o // SUB), ra), :] * cw_ref[
                        shard, KW - 1 - j:KW - j, off:off + LANE]
                dgl_ref[pl.ds(base, ra), :] = acc
                return carry

            lax.fori_loop(0, ts // ra, chunk_a, 0)

            def chunk_b(jc, accs):
                base = pl.multiple_of(jc * rb, rb)
                g = glu_ref[pl.ds(base, rb), :]
                out = []
                for j in range(KW):
                    o = j + 1
                    d = sh_ref[o % SUB, pl.ds(base + SUB * (o // SUB), rb), :]
                    out.append(accs[j] + _fold8(g * d))
                return tuple(out)

            accs = lax.fori_loop(0, ts // rb, chunk_b, tuple(jnp.zeros((SUB, LANE), F32) for _ in range(KW)))
            for j in range(KW):
                gcw_ref[j * SUB:(j + 1) * SUB, lanes] += accs[j]

            dglu = dgl_ref[...]
            dp_ref[:, lanes] = (dglu * sg).astype(BF16)
            dp_ref[:, gates] = (dglu * av * sg * (1.0 - sg)).astype(BF16)

        vh, vrstd = _layer_norm_stats(pm_ref[:, 4 * C_BR:5 * C_BR].astype(F32))
        slg = slg_ref[...]
        vn = (vh * slg + slb_ref[...]).astype(BF16)
        for hd in range(HEADS):
            w_h = ws_ref[hd].astype(BF16)
            wt_h = wst_ref[hd].astype(BF16)
            b_h = bst_ref[:, hd:hd + 1]
            cols = slice(hd * HEAD_DIM, (hd + 1) * HEAD_DIM)
            gws_h = jnp.zeros((CHUNK, CHUNK), F32)
            gbs_h = jnp.zeros((CHUNK, HEAD_DIM), F32)
            for ch in range(ts // CHUNK):
                rows = slice(ch * CHUNK, (ch + 1) * CHUNK)
                vn_b = vn[rows, cols]
                mixed = jnp.dot(w_h, vn_b, preferred_element_type=F32) + b_h
                u = pm_ref[rows, 3 * C_BR + hd * HEAD_DIM:3 * C_BR + (hd + 1) * HEAD_DIM].astype(F32)
                gs = pm_ref[rows, 5 * C_BR + hd * HEAD_DIM:5 * C_BR + (hd + 1) * HEAD_DIM].astype(F32)
                dys = dym_ref[rows, C_BR + hd * HEAD_DIM:C_BR + (hd + 1) * HEAD_DIM].astype(F32)
                sg_gs = _sigmoid(gs)
                silu_gs = gs * sg_gs
                dp_ref[rows, 3 * C_BR + hd * HEAD_DIM:3 * C_BR + (hd + 1) * HEAD_DIM] = (
                    dys * mixed * silu_gs).astype(BF16)
                dp_ref[rows, 5 * C_BR + hd * HEAD_DIM:5 * C_BR + (hd + 1) * HEAD_DIM] = (
                    dys * u * mixed * _dsilu(gs, sg_gs)).astype(BF16)
                d_mixed = dys * u * silu_gs
                dm_b = d_mixed.astype(BF16)
                gws_h = gws_h + _dot_nt(dm_b, vn_b)
                gbs_h = gbs_h + d_mixed
                dcv_ref[HALO + ch * CHUNK:HALO + (ch + 1) * CHUNK, cols] = jnp.dot(
                    wt_h, dm_b, preferred_element_type=F32)
            gws_ref[hd] += gws_h
            gbs_ref[:, cols] += gbs_h
        d_vn = dcv_ref[main, :]
        acc_ref[24:32, :] += _fold8(d_vn * vh)
        acc_ref[32:40, :] += _fold8(d_vn)
        dvh = d_vn * slg
        dp_ref[:, 4 * C_BR:5 * C_BR] = (vrstd * (
            dvh - jnp.mean(dvh, axis=-1, keepdims=True)
            - vh * jnp.mean(dvh * vh, axis=-1, keepdims=True))).astype(BF16)

        @pl.when(i == nt - 1)
        def _():
            small_ref[...] = jnp.zeros_like(small_ref)
            for a in range(5):
                small_ref[1 + a:2 + a, :] = jnp.sum(acc_ref[a * SUB:(a + 1) * SUB, :], axis=0, keepdims=True)
            ones = jnp.ones((SUB, HEAD_DIM), F32)
            for hd in range(HEADS):
                cols = slice(hd * HEAD_DIM, (hd + 1) * HEAD_DIM)
                rowsum = lax.dot_general(ones, gbs_ref[:, cols], (((1,), (1,)), ((), ())),
                                         precision=lax.Precision.HIGHEST, preferred_element_type=F32)
                small_ref[6:7, cols] = rowsum[0:1, :]
            for k in range(KW):
                j = KW - 1 - k
                small_ref[8 + k:9 + k, :] = jnp.sum(gcw_ref[j * SUB:(j + 1) * SUB, :], axis=0, keepdims=True)

    gc_prev, gc_next = _halo_specs(ts, s, C_BR, 2)
    lo_prev, lo_next = _halo_specs(ts, s, C_BR, 0)
    row = pl.BlockSpec((1, C_BR), lambda i: (0, 0))
    return pl.pallas_call(
        body, name="branch_bwd",
        grid=(nt,),
        out_shape=(jax.ShapeDtypeStruct((s, D_IN), BF16), jax.ShapeDtypeStruct((40, C_BR), F32),
                   jax.ShapeDtypeStruct((HEADS, CHUNK, CHUNK), F32)),
        in_specs=[pl.BlockSpec((ts, D_IN), lambda i: (i, 0)),
                  pl.BlockSpec((ts, 2 * C_BR), lambda i: (i, 0)),
                  pl.BlockSpec((ts, C_BR), lambda i: (i, 0)),
                  gc_prev, gc_next, lo_prev, lo_next, lo_prev, lo_next,
                  _whole_vmem(), row, row, row, row, _whole_vmem(), _whole_vmem(), _whole_vmem()],
        out_specs=(pl.BlockSpec((ts, D_IN), lambda i: (i, 0)),
                   pl.BlockSpec((40, C_BR), lambda i: (0, 0)),
                   pl.BlockSpec((HEADS, CHUNK, CHUNK), lambda i: (0, 0, 0))),
        scratch_shapes=[pltpu.VMEM((SUB, te, LANE), F32),
                        pltpu.VMEM((ts, LANE), F32),
                        pltpu.VMEM((ts, LANE), F32),
                        pltpu.VMEM((te, C_BR), F32),
                        pltpu.VMEM((5 * SUB, C_BR), F32),
                        pltpu.VMEM((KW * SUB, C_BR), F32),
                        pltpu.VMEM((CHUNK, C_BR), F32)],
        compiler_params=_cparams(("arbitrary",)),
    )(proj, dy, cv, proj, proj, dy, dy, cv, cv,
      cwg, conv_ln_g, conv_ln_b, sgu_ln_g, sgu_ln_b, w_s, ws_t, bs_t)


def _in_bwd(dproj, x, dx2, norm_g, wing):
    s = x.shape[0]
    ts = min(TS_INB, s)
    nt = s // ts

    def body(dp_ref, x_ref, dx2_ref, g_ref, w_ref, gx_ref, gn_ref, acc_ref):
        i = pl.program_id(0)

        @pl.when(i == 0)
        def _():
            acc_ref[...] = jnp.zeros_like(acc_ref)

        dh = jnp.zeros((ts, D_MODEL), F32)
        for k in range(N_SHARD):
            dh = dh + _dot_nt(dp_ref[:, k * W_IN_COLS:(k + 1) * W_IN_COLS], w_ref[k])
        xt = x_ref[...]
        r = lax.rsqrt(jnp.mean(xt * xt, axis=-1, keepdims=True) + EPS)
        n = xt * r
        acc_ref[...] += _fold8(dh * n)
        dn = dh * g_ref[...]
        gx_ref[...] = dx2_ref[...] + r * (dn - n * jnp.mean(dn * n, axis=-1, keepdims=True))

        @pl.when(i == nt - 1)
        def _():
            gn_ref[...] = jnp.broadcast_to(jnp.sum(acc_ref[...], axis=0, keepdims=True), gn_ref.shape)

    tile = pl.BlockSpec((ts, D_MODEL), lambda i: (i, 0))
    return pl.pallas_call(
        body, name="in_bwd",
        grid=(nt,),
        out_shape=(jax.ShapeDtypeStruct((s, D_MODEL), F32), jax.ShapeDtypeStruct((SUB, D_MODEL), F32)),
        in_specs=[pl.BlockSpec((ts, D_IN), lambda i: (i, 0)), tile, tile,
                  pl.BlockSpec((1, D_MODEL), lambda i: (0, 0)), _whole_vmem()],
        out_specs=(tile, pl.BlockSpec((SUB, D_MODEL), lambda i: (0, 0))),
        scratch_shapes=[pltpu.VMEM((SUB, D_MODEL), F32)],
        compiler_params=_cparams(("arbitrary",)),
    )(dproj, x, dx2, norm_g, wing)


def _grad_w_in(h, dproj):
    s = h.shape[0]
    tk = min(TK_GW, s)
    half = D_MODEL // 2

    def body(h_ref, dp_ref, o_ref):
        @pl.when(pl.program_id(1) == 0)
        def _():
            o_ref[...] = jnp.zeros_like(o_ref)

        o_ref[0] += _dot_tn(h_ref[...], dp_ref[...]).reshape(2, half, W_IN_COLS)

    return pl.pallas_call(
        body, name="grad_w_in",
        grid=(N_SHARD, s // tk),
        out_shape=jax.ShapeDtypeStruct((N_SHARD, 2, half, W_IN_COLS), F32),
        in_specs=[pl.BlockSpec((tk, D_MODEL), lambda k, t: (t, 0)),
                  pl.BlockSpec((tk, W_IN_COLS), lambda k, t: (t, k))],
        out_specs=pl.BlockSpec((1, 2, half, W_IN_COLS), lambda k, t: (k, 0, 0, 0)),
        compiler_params=_cparams(("parallel", "arbitrary")),
    )(h, dproj)


def _any_spec():
    return pl.BlockSpec(memory_space=pl.ANY)


def _reduce_pair_exchange(g_in, g_out):
    def body(gi_ref, go_ref, ri_ref, ro_ref, send_sems, recv_sems):
        x, y, c = _mesh_pos()
        copies = []
        for t, (g_ref, r_ref) in enumerate(((gi_ref, ri_ref), (go_ref, ro_ref))):
            for k in range(N_SHARD):
                cp = pltpu.make_async_remote_copy(
                    src_ref=g_ref.at[k, 1 - c], dst_ref=r_ref.at[k],
                    send_sem=send_sems.at[t * N_SHARD + k], recv_sem=recv_sems.at[t * N_SHARD + k],
                    device_id=(x, y, 1 - c), device_id_type=MESH)
                cp.start()
                copies.append(cp)
        for cp in copies:
            cp.wait()

    return pl.pallas_call(
        body, name="reduce_pair_exchange",
        out_shape=(jax.ShapeDtypeStruct((N_SHARD,) + g_in.shape[2:], F32),
                   jax.ShapeDtypeStruct((N_SHARD,) + g_out.shape[2:], F32)),
        in_specs=[_any_spec()] * 2, out_specs=(_any_spec(),) * 2,
        scratch_shapes=[pltpu.SemaphoreType.DMA((2 * N_SHARD,)), pltpu.SemaphoreType.DMA((2 * N_SHARD,))],
    )(g_in, g_out)


def _add_half(g, r, c_idx, name):
    _, _, rows, cols = g.shape
    tr = min(256, rows)

    def body(c_ref, g_ref, r_ref, o_ref):
        o_ref[...] = g_ref[0] + r_ref[...]

    return pl.pallas_call(
        body, name=name,
        grid_spec=pltpu.PrefetchScalarGridSpec(
            num_scalar_prefetch=1, grid=(N_SHARD, rows // tr),
            in_specs=[pl.BlockSpec((1, 1, tr, cols), lambda k, t, c_ref: (k, c_ref[0], t, 0)),
                      pl.BlockSpec((1, tr, cols), lambda k, t, c_ref: (k, t, 0))],
            out_specs=pl.BlockSpec((1, tr, cols), lambda k, t, c_ref: (k, t, 0))),
        out_shape=jax.ShapeDtypeStruct((N_SHARD, rows, cols), F32),
        compiler_params=_cparams(("parallel", "parallel")),
    )(c_idx, g, r)


def _reduce_chip_exchange(a_in, a_out):
    def body(ai_ref, ao_ref, ri_ref, ro_ref, send_sems, recv_sems):
        x, y, c = _mesh_pos()
        chips = [(1 - x, y), (x, 1 - y), (1 - x, 1 - y)]
        copies = []
        for t, (a_ref, r_ref) in enumerate(((ai_ref, ri_ref), (ao_ref, ro_ref))):
            for j, (cx, cy) in enumerate(chips):
                cp = pltpu.make_async_remote_copy(
                    src_ref=a_ref.at[2 * cx + cy], dst_ref=r_ref.at[j],
                    send_sem=send_sems.at[3 * t + j], recv_sem=recv_sems.at[3 * t + j],
                    device_id=(cx, cy, c), device_id_type=MESH)
                cp.start()
                copies.append(cp)
        for cp in copies:
            cp.wait()

    return pl.pallas_call(
        body, name="reduce_chip_exchange",
        out_shape=(jax.ShapeDtypeStruct((3,) + a_in.shape[1:], F32),
                   jax.ShapeDtypeStruct((3,) + a_out.shape[1:], F32)),
        in_specs=[_any_spec()] * 2, out_specs=(_any_spec(),) * 2,
        scratch_shapes=[pltpu.SemaphoreType.DMA((6,)), pltpu.SemaphoreType.DMA((6,))],
    )(a_in, a_out)


def _add_chips(a, r, me_idx, name):
    _, rows, cols = a.shape
    tr = min(256, rows)

    def body(me_ref, a_ref, r_ref, o_ref):
        o_ref[...] = ((a_ref[0] + r_ref[0]) + r_ref[1]) + r_ref[2]

    return pl.pallas_call(
        body, name=name,
        grid_spec=pltpu.PrefetchScalarGridSpec(
            num_scalar_prefetch=1, grid=(rows // tr,),
            in_specs=[pl.BlockSpec((1, tr, cols), lambda t, me_ref: (me_ref[0], t, 0)),
                      pl.BlockSpec((3, tr, cols), lambda t, me_ref: (0, t, 0))],
            out_specs=pl.BlockSpec((tr, cols), lambda t, me_ref: (t, 0))),
        out_shape=jax.ShapeDtypeStruct((rows, cols), F32),
        compiler_params=_cparams(("parallel",)),
    )(me_idx, a, r)


def _reduce_pair_gather(h_in, h_out):
    def body(hi_ref, ho_ref, fi_ref, fo_ref, send_sems, recv_sems, local_sems):
        x, y, c = _mesh_pos()
        copies = []
        for t, (h_ref, f_ref) in enumerate(((hi_ref, fi_ref), (ho_ref, fo_ref))):
            mine = pltpu.make_async_copy(h_ref, f_ref.at[c], local_sems.at[t])
            mine.start()
            cp = pltpu.make_async_remote_copy(
                src_ref=h_ref, dst_ref=f_ref.at[c], send_sem=send_sems.at[t], recv_sem=recv_sems.at[t],
                device_id=(x, y, 1 - c), device_id_type=MESH)
            cp.start()
            copies += [mine, cp]
        for cp in copies:
            cp.wait()

    return pl.pallas_call(
        body, name="reduce_pair_gather",
        out_shape=(jax.ShapeDtypeStruct((2,) + h_in.shape, F32), jax.ShapeDtypeStruct((2,) + h_out.shape, F32)),
        in_specs=[_any_spec()] * 2, out_specs=(_any_spec(),) * 2,
        scratch_shapes=[pltpu.SemaphoreType.DMA((2,)), pltpu.SemaphoreType.DMA((2,)),
                        pltpu.SemaphoreType.DMA((2,))],
    )(h_in, h_out)


def _allreduce_small(gn8, gf8, small, gws):
    def body(gn_ref, gf_ref, sm_ref, gws_ref, p_ref, q_ref, rp_ref, rq_ref, send_sems, recv_sems):
        x, y, c = _mesh_pos()
        p_ref[0:8, :] = gn_ref[...]
        p_ref[8:16, :] = gf_ref[...]
        p_ref[16:56, :] = sm_ref[...]
        q_ref[...] = gws_ref[...]
        for st, partner in enumerate([(x, y, 1 - c), (1 - x, y, c), (x, 1 - y, c)]):
            cps = [pltpu.make_async_remote_copy(
                src_ref=src, dst_ref=dst.at[st], send_sem=send_sems.at[2 * st + t],
                recv_sem=recv_sems.at[2 * st + t], device_id=partner, device_id_type=MESH)
                for t, (src, dst) in enumerate(((p_ref, rp_ref), (q_ref, rq_ref)))]
            for cp in cps:
                cp.start()
            for cp in cps:
                cp.wait()
            p_ref[...] = p_ref[...] + rp_ref[st]
            q_ref[...] = q_ref[...] + rq_ref[st]

    return pl.pallas_call(
        body, name="allreduce_small",
        out_shape=(jax.ShapeDtypeStruct((56, C_BR), F32), jax.ShapeDtypeStruct(gws.shape, F32)),
        in_specs=[_whole_vmem()] * 4, out_specs=(_whole_vmem(),) * 2,
        scratch_shapes=[pltpu.VMEM((3, 56, C_BR), F32), pltpu.VMEM((3,) + gws.shape, F32),
                        pltpu.SemaphoreType.DMA((6,)), pltpu.SemaphoreType.DMA((6,))],
        compiler_params=_cparams(),
    )(gn8, gf8, small, gws)


def _adamw_math(w, g, m, v):
    m = ADAM_B1 * m + (1.0 - ADAM_B1) * g
    v = ADAM_B2 * v + (1.0 - ADAM_B2) * (g * g)
    m_hat = m / (1.0 - ADAM_B1 ** ADAM_STEP)
    v_hat = v / (1.0 - ADAM_B2 ** ADAM_STEP)
    delta = -ADAM_LR * (m_hat / (jnp.sqrt(v_hat) + ADAM_EPS) + ADAM_WD * w)
    return delta, m, v


def _adamw_large(w, g, m, v, name):
    rows, cols = w.shape
    tr = min(256, rows)

    def body(w_ref, g_ref, m_ref, v_ref, d_ref, nm_ref, nv_ref):
        d_ref[...], nm_ref[...], nv_ref[...] = _adamw_math(w_ref[...], g_ref[...], m_ref[...], v_ref[...])

    tile = pl.BlockSpec((tr, cols), lambda t: (t, 0))
    return pl.pallas_call(
        body, name=name,
        grid=(rows // tr,),
        out_shape=(jax.ShapeDtypeStruct(w.shape, F32),) * 3,
        in_specs=[tile] * 4, out_specs=(tile,) * 3,
        compiler_params=_cparams(("parallel",)),
    )(w, g, m, v)


_ROW_OF = {"norm_g": 0, "final_g": 8, "conv_b": 17, "conv_ln_g": 18, "conv_ln_b": 19,
           "sgu_ln_g": 20, "sgu_ln_b": 21, "b_s": 22}
_CONV_W_ROW = 24
_VECTORS = ("norm_g", "conv_b", "conv_ln_g", "conv_ln_b", "sgu_ln_g", "sgu_ln_b", "b_s", "final_g")


def _adamw_small(p, q, me_idx, vectors, conv_w, w_s):
    names = list(_VECTORS)

    def body(me_ref, p_ref, q_ref, *refs):
        n_in = 3 * (len(names) + 2)
        ins, outs = refs[:n_in], refs[n_in:]
        me = me_ref[0]
        for a, name in enumerate(names + ["conv_w", "w_s"]):
            w_ref, m_ref, v_ref = ins[3 * a:3 * a + 3]
            if name == "conv_w":
                g = jnp.zeros((KW, CONV_W_COLS), F32)
                for k in range(N_SHARD):
                    blk = p_ref[_CONV_W_ROW:_CONV_W_ROW + KW, k * CONV_W_COLS:(k + 1) * CONV_W_COLS]
                    g = jnp.where(me == k, blk, g)
            elif name == "w_s":
                g = q_ref[...]
            else:
                g = p_ref[_ROW_OF[name]:_ROW_OF[name] + 1, :]
            delta, nm, nv = _adamw_math(w_ref[...], g, m_ref[...], v_ref[...])
            for o_ref, val in zip(outs[4 * a:4 * a + 4], (g, delta, nm, nv)):
                o_ref[...] = val

    operands, shapes = [], []
    for name in names:
        operands += list(vectors[name])
        shapes += [jax.ShapeDtypeStruct((1, C_BR), F32)] * 4
    operands += list(conv_w)
    shapes += [jax.ShapeDtypeStruct((KW, CONV_W_COLS), F32)] * 4
    operands += list(w_s)
    shapes += [jax.ShapeDtypeStruct(q.shape, F32)] * 4
    outs = pl.pallas_call(
        body, name="adamw_small",
        grid_spec=pltpu.PrefetchScalarGridSpec(
            num_scalar_prefetch=1, grid=(1,),
            in_specs=[_whole_vmem()] * (2 + len(operands)),
            out_specs=[_whole_vmem()] * len(shapes)),
        out_shape=shapes,
        compiler_params=_cparams(("arbitrary",)),
    )(me_idx, p, q, *operands)
    return {name: tuple(outs[4 * a:4 * a + 4]) for a, name in enumerate(names + ["conv_w", "w_s"])}


def kernel(x, norm_g, w_in, conv_w, conv_b, conv_ln_g, conv_ln_b, sgu_ln_g, sgu_ln_b, w_s, b_s, w_out, final_g, loss_target, m_norm_g, m_w_in, m_conv_w, m_conv_b, m_conv_ln_g, m_conv_ln_b, m_sgu_ln_g, m_sgu_ln_b, m_w_s, m_b_s, m_w_out, m_final_g, v_norm_g, v_w_in, v_conv_w, v_conv_b, v_conv_ln_g, v_conv_ln_b, v_sgu_ln_g, v_sgu_ln_b, v_w_s, v_b_s, v_w_out, v_final_g):
    xi, yi, ci = _mesh_pos()
    c_idx = jnp.reshape(ci, (1,)).astype(jnp.int32)
    me_idx = jnp.reshape(2 * xi + yi, (1,)).astype(jnp.int32)

    x2d = x[0]
    tgt = loss_target[0]
    fg = final_g.reshape(1, D_MODEL)
    ws3 = w_s[0]
    ws_t = jnp.swapaxes(ws3, 1, 2)
    bs_t = jnp.transpose(b_s[0])

    wing, woutg, cwg = _gather_weights(w_in[0], w_out[0], conv_w[0])

    h, proj = _rms_proj(x2d, norm_g, wing)
    y, cv = _branch_fwd(proj, cwg, conv_b, conv_ln_g, conv_ln_b, sgu_ln_g, sgu_ln_b, ws3, bs_t)
    dx2, dy, gwout, gf8, se8 = _out_proj(y, x2d, tgt, woutg, fg)
    dproj, small, gws = _branch_bwd(proj, dy, cv, cwg, conv_ln_g, conv_ln_b, sgu_ln_g, sgu_ln_b, ws3, ws_t, bs_t)
    grad_x, gn8 = _in_bwd(dproj, x2d, dx2, norm_g, wing)
    gwin = _grad_w_in(h, dproj)

    loss = lax.psum((0.5 / D_MODEL) * jnp.sum(se8), ("x", "y", "c"))

    gwout4 = gwout.reshape(N_SHARD, 2, W_OUT_ROWS // 2, D_MODEL)
    r1_in, r1_out = _reduce_pair_exchange(gwin, gwout4)
    a_in = _add_half(gwin, r1_in, c_idx, "add_pair_w_in")
    a_out = _add_half(gwout4, r1_out, c_idx, "add_pair_w_out")
    r2_in, r2_out = _reduce_chip_exchange(a_in, a_out)
    h_in = _add_chips(a_in, r2_in, me_idx, "add_chips_w_in")
    h_out = _add_chips(a_out, r2_out, me_idx, "add_chips_w_out")
    f_in, f_out = _reduce_pair_gather(h_in, h_out)
    g_w_in = f_in.reshape(D_MODEL, W_IN_COLS)
    g_w_out = f_out.reshape(W_OUT_ROWS, D_MODEL)
    d_w_in, nm_w_in, nv_w_in = _adamw_large(w_in[0], g_w_in, m_w_in[0], v_w_in[0], "adamw_w_in")
    d_w_out, nm_w_out, nv_w_out = _adamw_large(w_out[0], g_w_out, m_w_out[0], v_w_out[0], "adamw_w_out")

    p, q = _allreduce_small(gn8, gf8, small, gws.reshape(HEADS * CHUNK, CHUNK))
    flat = lambda a: a.reshape(1, C_BR)
    vectors = {
        "norm_g": (norm_g, m_norm_g, v_norm_g),
        "conv_b": (conv_b, m_conv_b, v_conv_b),
        "conv_ln_g": (conv_ln_g, m_conv_ln_g, v_conv_ln_g),
        "conv_ln_b": (conv_ln_b, m_conv_ln_b, v_conv_ln_b),
        "sgu_ln_g": (sgu_ln_g, m_sgu_ln_g, v_sgu_ln_g),
        "sgu_ln_b": (sgu_ln_b, m_sgu_ln_b, v_sgu_ln_b),
        "b_s": (flat(b_s), flat(m_b_s), flat(v_b_s)),
        "final_g": (flat(final_g), flat(m_final_g), flat(v_final_g)),
    }
    flat_ws = lambda a: a.reshape(HEADS * CHUNK, CHUNK)
    res = _adamw_small(p, q, me_idx, vectors, (conv_w[0], m_conv_w[0], v_conv_w[0]),
                       (flat_ws(w_s), flat_ws(m_w_s), flat_ws(v_w_s)))
    res["w_in"] = tuple(a[None] for a in (g_w_in, d_w_in, nm_w_in, nv_w_in))
    res["w_out"] = tuple(a[None] for a in (g_w_out, d_w_out, nm_w_out, nv_w_out))
    res["conv_w"] = tuple(a[None] for a in res["conv_w"])
    res["w_s"] = tuple(a.reshape(w_s.shape) for a in res["w_s"])
    res["b_s"] = tuple(a.reshape(b_s.shape) for a in res["b_s"])
    res["final_g"] = tuple(a.reshape(final_g.shape) for a in res["final_g"])

    order = ("norm_g", "w_in", "conv_w", "conv_b", "conv_ln_g", "conv_ln_b", "sgu_ln_g", "sgu_ln_b",
             "w_s", "b_s", "w_out", "final_g")
    out = [loss, grad_x[None]]
    for part in range(4):
        out += [res[name][part] for name in order]
    return tuple(out)
```

```python
import functools

import jax
import jax.numpy as jnp
from jax import lax
from jax.experimental import pallas as pl
from jax.experimental.pallas import tpu as pltpu

F32 = jnp.float32
BF16 = jnp.bfloat16
MESH = pl.DeviceIdType.MESH

EPS = 1e-6
D_MODEL = 1024
C_BR = 1024
D_IN = 6 * C_BR
N_SHARD = 4
W_IN_COLS = D_IN // N_SHARD
W_OUT_ROWS = 2 * C_BR // N_SHARD
CONV_W_COLS = C_BR // N_SHARD
KW = 31
HALO = 16
HEADS = 8
HEAD_DIM = 128
CHUNK = 128
LANE = 128
SUB = 8

ADAM_LR = 0.001
ADAM_B1 = 0.9
ADAM_B2 = 0.999
ADAM_EPS = 1e-08
ADAM_WD = 0.01
ADAM_STEP = 10

TS_PROJ = 512
TS_FWD = 256
TS_OUT = 256
TS_BWD = 256
TS_INB = 256
TK_GW = 1024
ROWS_A = 64
ROWS_B = 32
VMEM_LIMIT = 56 * 1024 * 1024


def _cparams(sem=None, vmem=VMEM_LIMIT):
    kw = dict(vmem_limit_bytes=vmem)
    if sem is not None:
        kw["dimension_semantics"] = sem
    return pltpu.CompilerParams(**kw)


def _whole_vmem():
    return pl.BlockSpec(memory_space=pltpu.VMEM)


def _sigmoid(v):
    return 1.0 / (1.0 + jnp.exp(-v))


def _fold8(v):
    n, c = v.shape
    return v.reshape(n // SUB, SUB, c).sum(axis=0)


def _dot_nt(a, b):
    return lax.dot_general(a, b, (((1,), (1,)), ((), ())), preferred_element_type=F32)


def _dot_tn(a, b):
    return lax.dot_general(a, b, (((0,), (0,)), ((), ())), preferred_element_type=F32)


def _mesh_pos():
    return lax.axis_index("x"), lax.axis_index("y"), lax.axis_index("c")


def _gather_weights(w_in, w_out, conv_w):
    hin = w_in.shape[0] // 2
    hout = w_out.shape[0] // 2

    def body(win_ref, wout_ref, cw_ref, wing_ref, woutg_ref, cwg_ref, send_sems, recv_sems):
        x, y, c = _mesh_pos()
        me = 2 * x + y
        chips = [(1 - x, y), (x, 1 - y), (1 - x, 1 - y)]
        sibling = (x, y, 1 - c)

        wing_ref[me] = win_ref[...].astype(BF16)
        woutg_ref[me] = wout_ref[...].astype(BF16)
        cwg_ref[me] = cw_ref[...]

        def copy(ref, blk, half, n, sem, dev):
            part = ref.at[blk, pl.ds(half * n, n)] if n else ref.at[blk]
            return pltpu.make_async_remote_copy(
                src_ref=part, dst_ref=part, send_sem=send_sems.at[sem], recv_sem=recv_sems.at[sem],
                device_id=dev, device_id_type=MESH)

        tensors = [(wing_ref, hin, 0), (woutg_ref, hout, 6)]
        started = []
        for ref, n, s0 in tensors:
            for j, (cx, cy) in enumerate(chips):
                cp = copy(ref, me, c, n, s0 + j, (cx, cy, c))
                cp.start()
                started.append(cp)
        for j, (cx, cy) in enumerate(chips):
            cp = copy(cwg_ref, me, 0, 0, 12 + j, (cx, cy, c))
            cp.start()
            started.append(cp)
        for ref, n, s0 in tensors:
            for j, (cx, cy) in enumerate(chips):
                blk = 2 * cx + cy
                copy(ref, blk, c, n, s0 + j, (cx, cy, c)).wait_recv()
                cp = copy(ref, blk, c, n, s0 + 3 + j, sibling)
                cp.start()
                started.append(cp)
        for ref, n, s0 in tensors:
            for j, (cx, cy) in enumerate(chips):
                copy(ref, 2 * cx + cy, 1 - c, n, s0 + 3 + j, sibling).wait_recv()
        for j, (cx, cy) in enumerate(chips):
            copy(cwg_ref, 2 * cx + cy, 0, 0, 12 + j, (cx, cy, c)).wait_recv()
        for cp in started:
            cp.wait_send()

    return pl.pallas_call(
        body, name="gather_weights",
        out_shape=(jax.ShapeDtypeStruct((N_SHARD,) + w_in.shape, BF16),
                   jax.ShapeDtypeStruct((N_SHARD,) + w_out.shape, BF16),
                   jax.ShapeDtypeStruct((N_SHARD,) + conv_w.shape, F32)),
        in_specs=[_whole_vmem()] * 3,
        out_specs=(_whole_vmem(),) * 3,
        scratch_shapes=[pltpu.SemaphoreType.DMA((15,)), pltpu.SemaphoreType.DMA((15,))],
        compiler_params=_cparams(),
    )(w_in, w_out, conv_w)


def _rms_proj(x, norm_g, wing):
    s = x.shape[0]
    ts = min(TS_PROJ, s)

    def body(x_ref, g_ref, w_ref, h_ref, proj_ref):
        xt = x_ref[...]
        r = lax.rsqrt(jnp.mean(xt * xt, axis=-1, keepdims=True) + EPS)
        h = (xt * r * g_ref[...]).astype(BF16)
        h_ref[...] = h
        for k in range(N_SHARD):
            proj_ref[:, k * W_IN_COLS:(k + 1) * W_IN_COLS] = jnp.dot(
                h, w_ref[k], preferred_element_type=F32).astype(BF16)

    return pl.pallas_call(
        body, name="rms_proj",
        grid=(s // ts,),
        out_shape=(jax.ShapeDtypeStruct((s, D_MODEL), BF16), jax.ShapeDtypeStruct((s, D_IN), BF16)),
        in_specs=[pl.BlockSpec((ts, D_MODEL), lambda i: (i, 0)),
                  pl.BlockSpec((1, D_MODEL), lambda i: (0, 0)),
                  _whole_vmem()],
        out_specs=(pl.BlockSpec((ts, D_MODEL), lambda i: (i, 0)),
                   pl.BlockSpec((ts, D_IN), lambda i: (i, 0))),
        compiler_params=_cparams(("parallel",)),
    )(x, norm_g, wing)


def _halo_specs(ts, s, width, col_block):
    per = ts // HALO
    last = s // HALO - 1
    prev = pl.BlockSpec((HALO, width), lambda i: (jnp.maximum(i * per - 1, 0), col_block))
    nxt = pl.BlockSpec((HALO, width), lambda i: (jnp.minimum((i + 1) * per, last), col_block))
    return prev, nxt


def _layer_norm_stats(v):
    mu = jnp.mean(v, axis=-1, keepdims=True)
    vc = v - mu
    var = jnp.mean(vc * vc, axis=-1, keepdims=True)
    rstd = lax.rsqrt(var + EPS)
    return vc * rstd, rstd


def _fill_shifted(sh_ref, ext):
    n = ext.shape[0]
    sh_ref[0] = ext
    for r in range(1, SUB):
        sh_ref[r] = pltpu.roll(ext, n - r, axis=0)


def _branch_fwd(proj, cwg, conv_b, conv_ln_g, conv_ln_b, sgu_ln_g, sgu_ln_b, w_s, bs_t):
    s = proj.shape[0]
    ts = min(TS_FWD, s)
    nt = s // ts
    ra = min(ROWS_A, ts)

    def body(pm_ref, pp_ref, pn_ref, cw_ref, cb_ref, clg_ref, clb_ref, slg_ref, slb_ref, ws_ref, bst_ref,
             y_ref, cv_ref, sh_ref):
        i = pl.program_id(0)
        keep_prev = (i > 0).astype(F32)
        keep_next = (i < nt - 1).astype(F32)

        for lb in range(C_BR // LANE):
            lanes = slice(lb * LANE, (lb + 1) * LANE)
            gates = slice(C_BR + lb * LANE, C_BR + (lb + 1) * LANE)

            def glu(ref):
                return ref[:, lanes].astype(F32) * _sigmoid(ref[:, gates].astype(F32))

            ext = jnp.concatenate([glu(pp_ref) * keep_prev, glu(pm_ref), glu(pn_ref) * keep_next], axis=0)
            _fill_shifted(sh_ref, ext)
            shard, off = divmod(lb * LANE, CONV_W_COLS)
            bias = cb_ref[:, lanes]

            def chunk(jc, carry):
                base = pl.multiple_of(jc * ra, ra)
                acc = jnp.zeros((ra, LANE), F32) + bias
                for k in range(KW):
                    o = k + 1
                    acc = acc + sh_ref[o % SUB, pl.ds(base + SUB * (o // SUB), ra), :] * cw_ref[
                        shard, k:k + 1, off:off + LANE]
                cv_ref[pl.ds(base, ra), lanes] = acc
                return carry

            lax.fori_loop(0, ts // ra, chunk, 0)

        lnh, _ = _layer_norm_stats(cv_ref[...])
        ln = lnh * clg_ref[...] + clb_ref[...]
        gc = pm_ref[:, 2 * C_BR:3 * C_BR].astype(F32)
        y_ref[:, :C_BR] = (ln * _sigmoid(ln) * gc * _sigmoid(gc)).astype(BF16)

        vh, _ = _layer_norm_stats(pm_ref[:, 4 * C_BR:5 * C_BR].astype(F32))
        vn = (vh * slg_ref[...] + slb_ref[...]).astype(BF16)
        for hd in range(HEADS):
            w_h = ws_ref[hd].astype(BF16)
            b_h = bst_ref[:, hd:hd + 1]
            cols = slice(hd * HEAD_DIM, (hd + 1) * HEAD_DIM)
            for ch in range(ts // CHUNK):
                rows = slice(ch * CHUNK, (ch + 1) * CHUNK)
                mixed = jnp.dot(w_h, vn[rows, cols], preferred_element_type=F32) + b_h
                u = pm_ref[rows, 3 * C_BR + hd * HEAD_DIM:3 * C_BR + (hd + 1) * HEAD_DIM].astype(F32)
                gs = pm_ref[rows, 5 * C_BR + hd * HEAD_DIM:5 * C_BR + (hd + 1) * HEAD_DIM].astype(F32)
                y_ref[rows, C_BR + hd * HEAD_DIM:C_BR + (hd + 1) * HEAD_DIM] = (
                    u * mixed * gs * _sigmoid(gs)).astype(BF16)

    prev, nxt = _halo_specs(ts, s, 2 * C_BR, 0)
    row = pl.BlockSpec((1, C_BR), lambda i: (0, 0))
    return pl.pallas_call(
        body, name="branch_fwd",
        grid=(nt,),
        out_shape=(jax.ShapeDtypeStruct((s, 2 * C_BR), BF16), jax.ShapeDtypeStruct((s, C_BR), F32)),
        in_specs=[pl.BlockSpec((ts, D_IN), lambda i: (i, 0)), prev, nxt,
                  _whole_vmem(), row, row, row, row, row, _whole_vmem(), _whole_vmem()],
        out_specs=(pl.BlockSpec((ts, 2 * C_BR), lambda i: (i, 0)),
                   pl.BlockSpec((ts, C_BR), lambda i: (i, 0))),
        scratch_shapes=[pltpu.VMEM((SUB, ts + 2 * HALO, LANE), F32)],
        compiler_params=_cparams(("parallel",)),
    )(proj, proj, proj, cwg, conv_b, conv_ln_g, conv_ln_b, sgu_ln_g, sgu_ln_b, w_s, bs_t)


def _out_proj(y, x, target, woutg, final_g):
    s = x.shape[0]
    ts = min(TS_OUT, s)
    nt = s // ts

    def body(y_ref, x_ref, t_ref, w_ref, g_ref, dx2_ref, dy_ref, gw_ref, gf_ref, se_ref):
        i = pl.program_id(0)

        @pl.when(i == 0)
        def _():
            gw_ref[...] = jnp.zeros_like(gw_ref)
            gf_ref[...] = jnp.zeros_like(gf_ref)
            se_ref[...] = jnp.zeros_like(se_ref)

        yt = y_ref[...]
        x2 = x_ref[...]
        for k in range(N_SHARD):
            x2 = x2 + jnp.dot(yt[:, k * W_OUT_ROWS:(k + 1) * W_OUT_ROWS], w_ref[k], preferred_element_type=F32)
        r2 = lax.rsqrt(jnp.mean(x2 * x2, axis=-1, keepdims=True) + EPS)
        n2 = x2 * r2
        g = g_ref[...]
        diff = n2 * g - t_ref[...]
        se_ref[...] += _fold8(diff * diff)
        dout = diff * (1.0 / D_MODEL)
        gf_ref[...] += _fold8(dout * n2)
        dn = dout * g
        dx2 = r2 * (dn - n2 * jnp.mean(dn * n2, axis=-1, keepdims=True))
        dx2_ref[...] = dx2
        dxb = dx2.astype(BF16)
        for k in range(N_SHARD):
            rows = slice(k * W_OUT_ROWS, (k + 1) * W_OUT_ROWS)
            dy_ref[:, rows] = _dot_nt(dxb, w_ref[k]).astype(BF16)
            gw_ref[rows, :] += _dot_tn(yt[:, rows], dxb)

        @pl.when(i == nt - 1)
        def _():
            gf_ref[...] = jnp.broadcast_to(jnp.sum(gf_ref[...], axis=0, keepdims=True), gf_ref.shape)

    tile = pl.BlockSpec((ts, D_MODEL), lambda i: (i, 0))
    wide = pl.BlockSpec((ts, 2 * C_BR), lambda i: (i, 0))
    acc8 = pl.BlockSpec((SUB, D_MODEL), lambda i: (0, 0))
    return pl.pallas_call(
        body, name="out_proj",
        grid=(nt,),
        out_shape=(jax.ShapeDtypeStruct((s, D_MODEL), F32), jax.ShapeDtypeStruct((s, 2 * C_BR), BF16),
                   jax.ShapeDtypeStruct((2 * C_BR, D_MODEL), F32),
                   jax.ShapeDtypeStruct((SUB, D_MODEL), F32), jax.ShapeDtypeStruct((SUB, D_MODEL), F32)),
        in_specs=[wide, tile, tile, _whole_vmem(), pl.BlockSpec((1, D_MODEL), lambda i: (0, 0))],
        out_specs=(tile, wide, pl.BlockSpec((2 * C_BR, D_MODEL), lambda i: (0, 0)), acc8, acc8),
        compiler_params=_cparams(("arbitrary",)),
    )(y, x, target, woutg, final_g)


def _dsilu(v, sg):
    return sg * (1.0 + v * (1.0 - sg))


def _branch_bwd(proj, dy, cv, cwg, conv_ln_g, conv_ln_b, sgu_ln_g, sgu_ln_b, w_s, ws_t, bs_t, gf8, dep):
    s = proj.shape[0]
    ts = min(TS_BWD, s)
    nt = s // ts
    ra = min(ROWS_A, ts)
    rb = min(ROWS_B, ts)
    te = ts + 2 * HALO

    def body(pm_ref, dym_ref, cvm_ref, gcp_ref, gcn_ref, dyp_ref, dyn_ref, cvp_ref, cvn_ref,
             cw_ref, clg_ref, clb_ref, slg_ref, slb_ref, ws_ref, wst_ref, bst_ref, gf_ref, dep_ref,
             dp_ref, small_ref, gws_ref,
             sh_ref, glu_ref, dgl_ref, dcv_ref, acc_ref, gcw_ref, gbs_ref):
        i = pl.program_id(0)

        @pl.when(i == 0)
        def _():
            acc_ref[...] = jnp.zeros_like(acc_ref)
            gcw_ref[...] = jnp.zeros_like(gcw_ref)
            gbs_ref[...] = jnp.zeros_like(gbs_ref)
            gws_ref[...] = jnp.zeros_like(gws_ref)

        def ext(prev_ref, main, next_ref):
            return jnp.concatenate([prev_ref[...].astype(F32), main, next_ref[...].astype(F32)], axis=0)

        main = slice(HALO, HALO + ts)

        cv_e = ext(cvp_ref, cvm_ref[...], cvn_ref)
        gc_e = ext(gcp_ref, pm_ref[:, 2 * C_BR:3 * C_BR].astype(F32), gcn_ref)
        dyc_e = ext(dyp_ref, dym_ref[:, :C_BR].astype(F32), dyn_ref)
        lnh, rstd = _layer_norm_stats(cv_e)
        clg = clg_ref[...]
        ln = lnh * clg + clb_ref[...]
        sg_ln = _sigmoid(ln)
        sg_gc = _sigmoid(gc_e)
        d_ln = dyc_e * gc_e * sg_gc * _dsilu(ln, sg_ln)
        dp_ref[:, 2 * C_BR:3 * C_BR] = (
            dyc_e[main] * ln[main] * sg_ln[main] * _dsilu(gc_e[main], sg_gc[main])).astype(BF16)
        dlnh = d_ln * clg
        d_cv = rstd * (dlnh - jnp.mean(dlnh, axis=-1, keepdims=True)
                       - lnh * jnp.mean(dlnh * lnh, axis=-1, keepdims=True))
        row = lax.broadcasted_iota(jnp.int32, (te, 1), 0)
        valid = jnp.logical_and(jnp.logical_or(row >= HALO, i > 0),
                                jnp.logical_or(row < HALO + ts, i < nt - 1))
        d_cv = jnp.where(valid, d_cv, 0.0)
        dcv_ref[...] = d_cv
        acc_ref[0:8, :] += _fold8(d_cv[main])
        acc_ref[8:16, :] += _fold8(d_ln[main] * lnh[main])
        acc_ref[16:24, :] += _fold8(d_ln[main])

        for lb in range(C_BR // LANE):
            lanes = slice(lb * LANE, (lb + 1) * LANE)
            gates = slice(C_BR + lb * LANE, C_BR + (lb + 1) * LANE)
            shard, off = divmod(lb * LANE, CONV_W_COLS)
            av = pm_ref[:, lanes].astype(F32)
            sg = _sigmoid(pm_ref[:, gates].astype(F32))
            glu_ref[...] = av * sg
            _fill_shifted(sh_ref, dcv_ref[:, lanes])

            def chunk_a(jc, carry):
                base = pl.multiple_of(jc * ra, ra)
                acc = jnp.zeros((ra, LANE), F32)
                for j in range(KW):
                    o = j + 1
                    acc = acc + sh_ref[o % SUB, pl.ds(base + SUB * (o // SUB), ra), :] * cw_ref[
                        shard, KW - 1 - j:KW - j, off:off + LANE]
                dgl_ref[pl.ds(base, ra), :] = acc
                return carry

            lax.fori_loop(0, ts // ra, chunk_a, 0)

            def chunk_b(jc, accs):
                base = pl.multiple_of(jc * rb, rb)
                g = glu_ref[pl.ds(base, rb), :]
                out = []
                for j in range(KW):
                    o = j + 1
                    d = sh_ref[o % SUB, pl.ds(base + SUB * (o // SUB), rb), :]
                    out.append(accs[j] + _fold8(g * d))
                return tuple(out)

            accs = lax.fori_loop(0, ts // rb, chunk_b, tuple(jnp.zeros((SUB, LANE), F32) for _ in range(KW)))
            for j in range(KW):
                gcw_ref[j * SUB:(j + 1) * SUB, lanes] += accs[j]

            dglu = dgl_ref[...]
            dp_ref[:, lanes] = (dglu * sg).astype(BF16)
            dp_ref[:, gates] = (dglu * av * sg * (1.0 - sg)).astype(BF16)

        vh, vrstd = _layer_norm_stats(pm_ref[:, 4 * C_BR:5 * C_BR].astype(F32))
        slg = slg_ref[...]
        vn = (vh * slg + slb_ref[...]).astype(BF16)
        for hd in range(HEADS):
            w_h = ws_ref[hd].astype(BF16)
            wt_h = wst_ref[hd].astype(BF16)
            b_h = bst_ref[:, hd:hd + 1]
            cols = slice(hd * HEAD_DIM, (hd + 1) * HEAD_DIM)
            gws_h = jnp.zeros((CHUNK, CHUNK), F32)
            gbs_h = jnp.zeros((CHUNK, HEAD_DIM), F32)
            for ch in range(ts // CHUNK):
                rows = slice(ch * CHUNK, (ch + 1) * CHUNK)
                vn_b = vn[rows, cols]
                mixed = jnp.dot(w_h, vn_b, preferred_element_type=F32) + b_h
                u = pm_ref[rows, 3 * C_BR + hd * HEAD_DIM:3 * C_BR + (hd + 1) * HEAD_DIM].astype(F32)
                gs = pm_ref[rows, 5 * C_BR + hd * HEAD_DIM:5 * C_BR + (hd + 1) * HEAD_DIM].astype(F32)
                dys = dym_ref[rows, C_BR + hd * HEAD_DIM:C_BR + (hd + 1) * HEAD_DIM].astype(F32)
                sg_gs = _sigmoid(gs)
                silu_gs = gs * sg_gs
                dp_ref[rows, 3 * C_BR + hd * HEAD_DIM:3 * C_BR + (hd + 1) * HEAD_DIM] = (
                    dys * mixed * silu_gs).astype(BF16)
                dp_ref[rows, 5 * C_BR + hd * HEAD_DIM:5 * C_BR + (hd + 1) * HEAD_DIM] = (
                    dys * u * mixed * _dsilu(gs, sg_gs)).astype(BF16)
                d_mixed = dys * u * silu_gs
                dm_b = d_mixed.astype(BF16)
                gws_h = gws_h + _dot_nt(dm_b, vn_b)
                gbs_h = gbs_h + d_mixed
                dcv_ref[HALO + ch * CHUNK:HALO + (ch + 1) * CHUNK, cols] = jnp.dot(
                    wt_h, dm_b, preferred_element_type=F32)
            gws_ref[hd] += gws_h
            gbs_ref[:, cols] += gbs_h
        d_vn = dcv_ref[main, :]
        acc_ref[24:32, :] += _fold8(d_vn * vh)
        acc_ref[32:40, :] += _fold8(d_vn)
        dvh = d_vn * slg
        dp_ref[:, 4 * C_BR:5 * C_BR] = (vrstd * (
            dvh - jnp.mean(dvh, axis=-1, keepdims=True)
            - vh * jnp.mean(dvh * vh, axis=-1, keepdims=True))).astype(BF16)

        @pl.when(i == nt - 1)
        def _():
            small_ref[...] = jnp.zeros_like(small_ref)
            for a in range(5):
                small_ref[1 + a:2 + a, :] = jnp.sum(acc_ref[a * SUB:(a + 1) * SUB, :], axis=0, keepdims=True)
            ones = jnp.ones((SUB, HEAD_DIM), F32)
            for hd in range(HEADS):
                cols = slice(hd * HEAD_DIM, (hd + 1) * HEAD_DIM)
                rowsum = lax.dot_general(ones, gbs_ref[:, cols], (((1,), (1,)), ((), ())),
                                         precision=lax.Precision.HIGHEST, preferred_element_type=F32)
                small_ref[6:7, cols] = rowsum[0:1, :]
            small_ref[7:8, :] = gf_ref[0:1, :]
            for k in range(KW):
                j = KW - 1 - k
                small_ref[8 + k:9 + k, :] = jnp.sum(gcw_ref[j * SUB:(j + 1) * SUB, :], axis=0, keepdims=True)

    gc_prev, gc_next = _halo_specs(ts, s, C_BR, 2)
    lo_prev, lo_next = _halo_specs(ts, s, C_BR, 0)
    row = pl.BlockSpec((1, C_BR), lambda i: (0, 0))
    return pl.pallas_call(
        body, name="branch_bwd",
        grid=(nt,),
        out_shape=(jax.ShapeDtypeStruct((s, D_IN), BF16), jax.ShapeDtypeStruct((40, C_BR), F32),
                   jax.ShapeDtypeStruct((HEADS, CHUNK, CHUNK), F32)),
        in_specs=[pl.BlockSpec((ts, D_IN), lambda i: (i, 0)),
                  pl.BlockSpec((ts, 2 * C_BR), lambda i: (i, 0)),
                  pl.BlockSpec((ts, C_BR), lambda i: (i, 0)),
                  gc_prev, gc_next, lo_prev, lo_next, lo_prev, lo_next,
                  _whole_vmem(), row, row, row, row, _whole_vmem(), _whole_vmem(), _whole_vmem(),
                  _whole_vmem(), _whole_vmem()],
        out_specs=(pl.BlockSpec((ts, D_IN), lambda i: (i, 0)),
                   pl.BlockSpec((40, C_BR), lambda i: (0, 0)),
                   pl.BlockSpec((HEADS, CHUNK, CHUNK), lambda i: (0, 0, 0))),
        scratch_shapes=[pltpu.VMEM((SUB, te, LANE), F32),
                        pltpu.VMEM((ts, LANE), F32),
                        pltpu.VMEM((ts, LANE), F32),
                        pltpu.VMEM((te, C_BR), F32),
                        pltpu.VMEM((5 * SUB, C_BR), F32),
                        pltpu.VMEM((KW * SUB, C_BR), F32),
                        pltpu.VMEM((CHUNK, C_BR), F32)],
        compiler_params=_cparams(("arbitrary",)),
    )(proj, dy, cv, proj, proj, dy, dy, cv, cv,
      cwg, conv_ln_g, conv_ln_b, sgu_ln_g, sgu_ln_b, w_s, ws_t, bs_t, gf8, dep)


def _in_bwd(dproj, x, dx2, norm_g, wing, dep, first_tile, n_tiles, gx_prev=None, gn_prev=None):
    s = x.shape[0]
    ts = min(TS_INB, s)

    def body(*refs):
        dp_ref, x_ref, dx2_ref, g_ref, w_ref = refs[:5]
        gx_ref, gn_ref, acc_ref = refs[-3:]
        i = pl.program_id(0)

        @pl.when(i == 0)
        def _():
            acc_ref[...] = jnp.zeros_like(acc_ref)

        dh = jnp.zeros((ts, D_MODEL), F32)
        for k in range(N_SHARD):
            dh = dh + _dot_nt(dp_ref[:, k * W_IN_COLS:(k + 1) * W_IN_COLS], w_ref[k])
        xt = x_ref[...]
        r = lax.rsqrt(jnp.mean(xt * xt, axis=-1, keepdims=True) + EPS)
        n = xt * r
        acc_ref[...] += _fold8(dh * n)
        dn = dh * g_ref[...]
        gx_ref[...] = dx2_ref[...] + r * (dn - n * jnp.mean(dn * n, axis=-1, keepdims=True))

        @pl.when(i == n_tiles - 1)
        def _():
            total = jnp.broadcast_to(jnp.sum(acc_ref[...], axis=0, keepdims=True), gn_ref.shape)
            if gn_prev is not None:
                total = total + refs[7][...]
            gn_ref[...] = total

    tile = pl.BlockSpec((ts, D_MODEL), lambda i: (i + first_tile, 0))
    in_specs = [pl.BlockSpec((ts, D_IN), lambda i: (i + first_tile, 0)), tile, tile,
                pl.BlockSpec((1, D_MODEL), lambda i: (0, 0)), _whole_vmem(), _whole_vmem()]
    operands = [dproj, x, dx2, norm_g, wing, dep]
    aliases = {}
    if gx_prev is not None:
        in_specs += [pl.BlockSpec(memory_space=pl.ANY), _whole_vmem()]
        operands += [gx_prev, gn_prev]
        aliases = {6: 0}
    return pl.pallas_call(
        body, name="in_bwd_%d" % first_tile,
        grid=(n_tiles,),
        out_shape=(jax.ShapeDtypeStruct((s, D_MODEL), F32), jax.ShapeDtypeStruct((SUB, D_MODEL), F32)),
        in_specs=in_specs,
        out_specs=(tile, pl.BlockSpec((SUB, D_MODEL), lambda i: (0, 0))),
        scratch_shapes=[pltpu.VMEM((SUB, D_MODEL), F32)],
        input_output_aliases=aliases,
        compiler_params=_cparams(("arbitrary",)),
    )(*operands)


def _grad_w_in(h, dproj, dep):
    s = h.shape[0]
    tk = min(TK_GW, s)
    half = D_MODEL // 2

    def body(h_ref, dp_ref, dep_ref, o_ref):
        @pl.when(pl.program_id(1) == 0)
        def _():
            o_ref[...] = jnp.zeros_like(o_ref)

        o_ref[0] += _dot_tn(h_ref[...], dp_ref[...]).reshape(2, half, W_IN_COLS)

    return pl.pallas_call(
        body, name="grad_w_in",
        grid=(N_SHARD, s // tk),
        out_shape=jax.ShapeDtypeStruct((N_SHARD, 2, half, W_IN_COLS), F32),
        in_specs=[pl.BlockSpec((tk, D_MODEL), lambda k, t: (t, 0)),
                  pl.BlockSpec((tk, W_IN_COLS), lambda k, t: (t, k)), _whole_vmem()],
        out_specs=pl.BlockSpec((1, 2, half, W_IN_COLS), lambda k, t: (k, 0, 0, 0)),
        compiler_params=_cparams(("parallel", "arbitrary")),
    )(h, dproj, dep)


HBM_SPEC = pl.BlockSpec(memory_space=pltpu.HBM)
SEM_SPEC = pl.BlockSpec(memory_space=pltpu.SEMAPHORE)
SIDE_EFFECT = pltpu.SideEffectType.DATAFLOW_SIDE_EFFECTING


def _remote_copies(plan, bufs, send_sems, recv_sems):
    x, y, c = _mesh_pos()
    return [pltpu.make_async_remote_copy(src_ref=src, dst_ref=dst, send_sem=send_sems.at[k],
                                         recv_sem=recv_sems.at[k], device_id=dev, device_id_type=MESH)
            for k, (src, dst, dev) in enumerate(plan(x, y, c, *bufs))]


def _start_copies(name, bufs, n_copies, plan):
    n = len(bufs)

    def body(*refs):
        for cp in _remote_copies(plan, refs[:n], refs[n], refs[n + 1]):
            cp.start()
        refs[-1][...] = jnp.zeros_like(refs[-1])

    outs = pl.pallas_call(
        body, name=name,
        out_shape=(pltpu.SemaphoreType.DMA((n_copies,)), pltpu.SemaphoreType.DMA((n_copies,)),
                   *[pltpu.HBM(b.shape, b.dtype) for b in bufs], jax.ShapeDtypeStruct((SUB, LANE), F32)),
        in_specs=[HBM_SPEC] * n,
        out_specs=(SEM_SPEC, SEM_SPEC, *[HBM_SPEC] * n, _whole_vmem()),
        input_output_aliases={i: 2 + i for i in range(n)},
        compiler_params=pltpu.CompilerParams(has_side_effects=SIDE_EFFECT),
    )(*[pltpu.with_memory_space_constraint(b, pltpu.HBM) for b in bufs])
    return outs[0], outs[1], list(outs[2:2 + n]), outs[-1]


def _wait_copies(name, send_sems, recv_sems, bufs, plan, after):
    n = len(bufs)

    def body(*refs):
        for cp in _remote_copies(plan, refs[:n], refs[n], refs[n + 1]):
            cp.wait_send()
            cp.wait_recv()

    outs = pl.pallas_call(
        body, name=name,
        out_shape=tuple(pltpu.HBM(b.shape, b.dtype) for b in bufs),
        in_specs=[HBM_SPEC] * n + [SEM_SPEC, SEM_SPEC, pl.BlockSpec(memory_space=pl.ANY)],
        out_specs=(HBM_SPEC,) * n,
        input_output_aliases={i: i for i in range(n)},
        compiler_params=pltpu.CompilerParams(has_side_effects=SIDE_EFFECT),
    )(*bufs, send_sems, recv_sems, after)
    return list(outs)


def _landing(shape, dtype):
    return lax.empty(shape, dtype)


def _plan_pair_exchange(x, y, c, g, r):
    return [(g.at[k, 1 - c], r.at[k], (x, y, 1 - c)) for k in range(N_SHARD)]


def _plan_chip_exchange(x, y, c, a, r):
    chips = [(1 - x, y), (x, 1 - y), (1 - x, 1 - y)]
    return [(a.at[2 * cx + cy], r.at[j], (cx, cy, c)) for j, (cx, cy) in enumerate(chips)]


def _plan_pair_gather(x, y, c, f):
    return [(f.at[c], f.at[c], (x, y, 1 - c))]


def _plan_all_gather(x, y, c, own, land):
    me = 4 * x + 2 * y + c
    flip = lambda v, bit: 1 - v if bit else v
    return [(own, land.at[me], (flip(x, m >> 2 & 1), flip(y, m >> 1 & 1), flip(c, m & 1))) for m in range(1, 8)]


def _add_pair(g, r, pos, name):
    _, _, rows, cols = g.shape
    tr = min(256, rows)

    def body(pos_ref, g_ref, r_ref, o_ref, ob_ref):
        v = g_ref[0] + r_ref[...]
        o_ref[...] = v
        ob_ref[...] = v.astype(BF16)

    blk = pl.BlockSpec((1, tr, cols), lambda k, t, pos_ref: (k, t, 0))
    return pl.pallas_call(
        body, name=name,
        grid_spec=pltpu.PrefetchScalarGridSpec(
            num_scalar_prefetch=1, grid=(N_SHARD, rows // tr),
            in_specs=[pl.BlockSpec((1, 1, tr, cols), lambda k, t, pos_ref: (k, pos_ref[1], t, 0)), blk],
            out_specs=(blk, blk)),
        out_shape=(jax.ShapeDtypeStruct((N_SHARD, rows, cols), F32), jax.ShapeDtypeStruct((N_SHARD, rows, cols), BF16)),
        compiler_params=_cparams(("parallel", "parallel")),
    )(pos, g, r)


def _add_chips(a, r, pos, name):
    _, rows, cols = a.shape
    tr = min(256, rows)

    def body(pos_ref, a_ref, r_ref, o_ref):
        o_ref[0] = ((a_ref[0] + r_ref[0].astype(F32)) + r_ref[1].astype(F32)) + r_ref[2].astype(F32)

    return pl.pallas_call(
        body, name=name,
        grid_spec=pltpu.PrefetchScalarGridSpec(
            num_scalar_prefetch=1, grid=(rows // tr,),
            in_specs=[pl.BlockSpec((1, tr, cols), lambda t, pos_ref: (pos_ref[0], t, 0)),
                      pl.BlockSpec((3, tr, cols), lambda t, pos_ref: (0, t, 0))],
            out_specs=pl.BlockSpec((1, tr, cols), lambda t, pos_ref: (pos_ref[1], t, 0))),
        out_shape=jax.ShapeDtypeStruct((2, rows, cols), F32),
        compiler_params=_cparams(("parallel",)),
    )(pos, a, r)


def _sum_slots(pos_ref, own, land_ref, rows):
    me = pos_ref[2]
    total = None
    for d in range(8):
        term = jnp.where(me == d, own, land_ref[d] if rows is None else land_ref[d, rows, :])
        total = term if total is None else total + term
    return total


def _sum_small(pos, small, small_land, gws, gws_land):
    def body(pos_ref, sm_ref, sml_ref, gw_ref, gwl_ref, o_sm, o_gw):
        o_sm[...] = _sum_slots(pos_ref, sm_ref[...], sml_ref, None)
        o_gw[...] = _sum_slots(pos_ref, gw_ref[...], gwl_ref, None)

    return pl.pallas_call(
        body, name="sum_small",
        grid_spec=pltpu.PrefetchScalarGridSpec(
            num_scalar_prefetch=1, grid=(1,),
            in_specs=[_whole_vmem()] * 4, out_specs=[_whole_vmem()] * 2),
        out_shape=[jax.ShapeDtypeStruct(small.shape, F32), jax.ShapeDtypeStruct(gws.shape, F32)],
        compiler_params=_cparams(("arbitrary",)),
    )(pos, small, small_land, gws, gws_land)


def _adamw_math(w, g, m, v):
    m = ADAM_B1 * m + (1.0 - ADAM_B1) * g
    v = ADAM_B2 * v + (1.0 - ADAM_B2) * (g * g)
    m_hat = m / (1.0 - ADAM_B1 ** ADAM_STEP)
    v_hat = v / (1.0 - ADAM_B2 ** ADAM_STEP)
    delta = -ADAM_LR * (m_hat / (jnp.sqrt(v_hat) + ADAM_EPS) + ADAM_WD * w)
    return delta, m, v


def _adamw_large(w, g, m, v, dep, name):
    rows, cols = w.shape
    tr = min(256, rows)

    def body(w_ref, g_ref, m_ref, v_ref, dep_ref, d_ref, nm_ref, nv_ref):
        d_ref[...], nm_ref[...], nv_ref[...] = _adamw_math(w_ref[...], g_ref[...], m_ref[...], v_ref[...])

    tile = pl.BlockSpec((tr, cols), lambda t: (t, 0))
    return pl.pallas_call(
        body, name=name,
        grid=(rows // tr,),
        out_shape=(jax.ShapeDtypeStruct(w.shape, F32),) * 3,
        in_specs=[tile] * 4 + [_whole_vmem()], out_specs=(tile,) * 3,
        compiler_params=_cparams(("parallel",)),
    )(w, g, m, v, dep)


_ROW_OF = {"conv_b": 1, "conv_ln_g": 2, "conv_ln_b": 3, "sgu_ln_g": 4, "sgu_ln_b": 5, "b_s": 6, "final_g": 7}
_CONV_W_ROW = 8
_VECTORS = ("norm_g", "conv_b", "conv_ln_g", "conv_ln_b", "sgu_ln_g", "sgu_ln_b", "b_s", "final_g")


def _adamw_small(p, q, gn8, gn_land, pos, vectors, conv_w, w_s):
    names = list(_VECTORS)

    def body(pos_ref, p_ref, q_ref, gn_ref, gnl_ref, *refs):
        n_in = 3 * (len(names) + 2)
        ins, outs = refs[:n_in], refs[n_in:]
        me = pos_ref[0]
        for a, name in enumerate(names + ["conv_w", "w_s"]):
            w_ref, m_ref, v_ref = ins[3 * a:3 * a + 3]
            if name == "conv_w":
                g = jnp.zeros((KW, CONV_W_COLS), F32)
                for k in range(N_SHARD):
                    blk = p_ref[_CONV_W_ROW:_CONV_W_ROW + KW, k * CONV_W_COLS:(k + 1) * CONV_W_COLS]
                    g = jnp.where(me == k, blk, g)
            elif name == "w_s":
                g = q_ref[...]
            elif name == "norm_g":
                g = _sum_slots(pos_ref, gn_ref[0:1, :], gnl_ref, slice(0, 1))
            else:
                g = p_ref[_ROW_OF[name]:_ROW_OF[name] + 1, :]
            delta, nm, nv = _adamw_math(w_ref[...], g, m_ref[...], v_ref[...])
            for o_ref, val in zip(outs[4 * a:4 * a + 4], (g, delta, nm, nv)):
                o_ref[...] = val

    operands, shapes = [], []
    for name in names:
        operands += list(vectors[name])
        shapes += [jax.ShapeDtypeStruct((1, C_BR), F32)] * 4
    operands += list(conv_w)
    shapes += [jax.ShapeDtypeStruct((KW, CONV_W_COLS), F32)] * 4
    operands += list(w_s)
    shapes += [jax.ShapeDtypeStruct(q.shape, F32)] * 4
    outs = pl.pallas_call(
        body, name="adamw_small",
        grid_spec=pltpu.PrefetchScalarGridSpec(
            num_scalar_prefetch=1, grid=(1,),
            in_specs=[_whole_vmem()] * (4 + len(operands)),
            out_specs=[_whole_vmem()] * len(shapes)),
        out_shape=shapes,
        compiler_params=_cparams(("arbitrary",)),
    )(pos, p, q, gn8, gn_land, *operands)
    return {name: tuple(outs[4 * a:4 * a + 4]) for a, name in enumerate(names + ["conv_w", "w_s"])}


def kernel(x, norm_g, w_in, conv_w, conv_b, conv_ln_g, conv_ln_b, sgu_ln_g, sgu_ln_b, w_s, b_s, w_out, final_g, loss_target, m_norm_g, m_w_in, m_conv_w, m_conv_b, m_conv_ln_g, m_conv_ln_b, m_sgu_ln_g, m_sgu_ln_b, m_w_s, m_b_s, m_w_out, m_final_g, v_norm_g, v_w_in, v_conv_w, v_conv_b, v_conv_ln_g, v_conv_ln_b, v_sgu_ln_g, v_sgu_ln_b, v_w_s, v_b_s, v_w_out, v_final_g):
    xi, yi, ci = _mesh_pos()
    pos = jnp.stack([2 * xi + yi, ci, 4 * xi + 2 * yi + ci]).astype(jnp.int32)

    x2d = x[0]
    tgt = loss_target[0]
    fg = final_g.reshape(1, D_MODEL)
    ws3 = w_s[0]
    ws_t = jnp.swapaxes(ws3, 1, 2)
    bs_t = jnp.transpose(b_s[0])

    wing, woutg, cwg = _gather_weights(w_in[0], w_out[0], conv_w[0])

    h, proj = _rms_proj(x2d, norm_g, wing)
    y, cv = _branch_fwd(proj, cwg, conv_b, conv_ln_g, conv_ln_b, sgu_ln_g, sgu_ln_b, ws3, bs_t)
    dx2, dy, gwout, gf8, se8 = _out_proj(y, x2d, tgt, woutg, fg)
    loss = lax.psum((0.5 / D_MODEL) * jnp.sum(se8), ("x", "y", "c"))

    in_rows, out_rows = D_MODEL // 2, W_OUT_ROWS // 2
    gwout4 = gwout.reshape(N_SHARD, 2, out_rows, D_MODEL)
    ss, rs, (gwout4, r1_out), tok = _start_copies(
        "start_pair_exchange_w_out", [gwout4, _landing((N_SHARD, out_rows, D_MODEL), F32)], N_SHARD,
        _plan_pair_exchange)
    dproj, small, gws3 = _branch_bwd(proj, dy, cv, cwg, conv_ln_g, conv_ln_b, sgu_ln_g, sgu_ln_b, ws3, ws_t, bs_t,
                                     gf8, tok)
    gws = gws3.reshape(HEADS * CHUNK, CHUNK)
    gwout4, r1_out = _wait_copies("wait_pair_exchange_w_out", ss, rs, [gwout4, r1_out], _plan_pair_exchange, dproj)
    a_out, a_out_bf = _add_pair(gwout4, r1_out, pos, "add_pair_w_out")

    def plan_b(x, y, c, a, r, sm, sml, gw, gwl):
        return (_plan_chip_exchange(x, y, c, a, r) + _plan_all_gather(x, y, c, sm, sml)
                + _plan_all_gather(x, y, c, gw, gwl))

    ss, rs, bufs_b, tok = _start_copies(
        "start_chip_exchange_w_out",
        [a_out_bf, _landing((3, out_rows, D_MODEL), BF16), small, _landing((8,) + small.shape, F32),
         gws, _landing((8,) + gws.shape, F32)], 3 + 7 + 7, plan_b)
    gwin = _grad_w_in(h, dproj, tok)
    ss_c, rs_c, (gwin, r1_in), tok = _start_copies(
        "start_pair_exchange_w_in", [gwin, _landing((N_SHARD, in_rows, W_IN_COLS), F32)], N_SHARD,
        _plan_pair_exchange)
    nt = x2d.shape[0] // min(TS_INB, x2d.shape[0])
    gx_a, gn_a = _in_bwd(dproj, x2d, dx2, norm_g, wing, tok, 0, nt // 2)

    gwin, r1_in = _wait_copies("wait_pair_exchange_w_in", ss_c, rs_c, [gwin, r1_in], _plan_pair_exchange, gx_a)
    a_in, a_in_bf = _add_pair(gwin, r1_in, pos, "add_pair_w_in")
    a_out_bf, r2_out, small, small_land, gws, gws_land = _wait_copies(
        "wait_chip_exchange_w_out", ss, rs, bufs_b, plan_b, gx_a)
    f_out = _add_chips(a_out, r2_out, pos, "add_chips_w_out")
    p, q = _sum_small(pos, small, small_land, gws, gws_land)

    def plan_d(x, y, c, a, r, f):
        return _plan_chip_exchange(x, y, c, a, r) + _plan_pair_gather(x, y, c, f)

    ss, rs, bufs_d, tok = _start_copies(
        "start_chip_exchange_w_in", [a_in_bf, _landing((3, in_rows, W_IN_COLS), BF16), f_out], 3 + 1, plan_d)
    grad_x, gn8 = _in_bwd(dproj, x2d, dx2, norm_g, wing, tok, nt // 2, nt - nt // 2, gx_a, gn_a)
    a_in_bf, r2_in, f_out = _wait_copies("wait_chip_exchange_w_in", ss, rs, bufs_d, plan_d, grad_x)
    f_in = _add_chips(a_in, r2_in, pos, "add_chips_w_in")

    def plan_e(x, y, c, f, gn, gnl):
        return _plan_pair_gather(x, y, c, f) + _plan_all_gather(x, y, c, gn, gnl)

    ss, rs, bufs_e, tok = _start_copies(
        "start_pair_gather_w_in", [f_in, gn8, _landing((8,) + gn8.shape, F32)], 1 + 7, plan_e)
    g_w_out = f_out.reshape(W_OUT_ROWS, D_MODEL)
    d_w_out, nm_w_out, nv_w_out = _adamw_large(w_out[0], g_w_out, m_w_out[0], v_w_out[0], tok, "adamw_w_out")
    f_in, gn8, gn_land = _wait_copies("wait_pair_gather_w_in", ss, rs, bufs_e, plan_e, d_w_out)
    g_w_in = f_in.reshape(D_MODEL, W_IN_COLS)
    d_w_in, nm_w_in, nv_w_in = _adamw_large(w_in[0], g_w_in, m_w_in[0], v_w_in[0], tok, "adamw_w_in")

    flat = lambda a: a.reshape(1, C_BR)
    vectors = {
        "norm_g": (norm_g, m_norm_g, v_norm_g),
        "conv_b": (conv_b, m_conv_b, v_conv_b),
        "conv_ln_g": (conv_ln_g, m_conv_ln_g, v_conv_ln_g),
        "conv_ln_b": (conv_ln_b, m_conv_ln_b, v_conv_ln_b),
        "sgu_ln_g": (sgu_ln_g, m_sgu_ln_g, v_sgu_ln_g),
        "sgu_ln_b": (sgu_ln_b, m_sgu_ln_b, v_sgu_ln_b),
        "b_s": (flat(b_s), flat(m_b_s), flat(v_b_s)),
        "final_g": (flat(final_g), flat(m_final_g), flat(v_final_g)),
    }
    flat_ws = lambda a: a.reshape(HEADS * CHUNK, CHUNK)
    res = _adamw_small(p, q, gn8, gn_land, pos, vectors, (conv_w[0], m_conv_w[0], v_conv_w[0]),
                       (flat_ws(w_s), flat_ws(m_w_s), flat_ws(v_w_s)))
    res["w_in"] = tuple(a[None] for a in (g_w_in, d_w_in, nm_w_in, nv_w_in))
    res["w_out"] = tuple(a[None] for a in (g_w_out, d_w_out, nm_w_out, nv_w_out))
    res["conv_w"] = tuple(a[None] for a in res["conv_w"])
    res["w_s"] = tuple(a.reshape(w_s.shape) for a in res["w_s"])
    res["b_s"] = tuple(a.reshape(b_s.shape) for a in res["b_s"])
    res["final_g"] = tuple(a.reshape(final_g.shape) for a in res["final_g"])

    order = ("norm_g", "w_in", "conv_w", "conv_b", "conv_ln_g", "conv_ln_b", "sgu_ln_g", "sgu_ln_b",
             "w_s", "b_s", "w_out", "final_g")
    out = [loss, grad_x[None]]
    for part in range(4):
        out += [res[name][part] for name in order]
    return tuple(out)
```

```python
import functools

import jax
import jax.numpy as jnp
from jax import lax
from jax.experimental import pallas as pl
from jax.experimental.pallas import tpu as pltpu

F32 = jnp.float32
BF16 = jnp.bfloat16
MESH = pl.DeviceIdType.MESH

EPS = 1e-6
D_MODEL = 1024
C_BR = 1024
D_IN = 6 * C_BR
N_SHARD = 4
W_IN_COLS = D_IN // N_SHARD
W_OUT_ROWS = 2 * C_BR // N_SHARD
CONV_W_COLS = C_BR // N_SHARD
KW = 31
HALO = 16
HEADS = 8
HEAD_DIM = 128
CHUNK = 128
LANE = 128
SUB = 8

ADAM_LR = 0.001
ADAM_B1 = 0.9
ADAM_B2 = 0.999
ADAM_EPS = 1e-08
ADAM_WD = 0.01
ADAM_STEP = 10

TS_PROJ = 512
TS_FWD = 256
TS_OUT = 256
TS_BWD = 256
TS_INB = 256
TK_GW = 1024
ROWS_A = 64
ROWS_B = 32
VMEM_LIMIT = 56 * 1024 * 1024


def _cparams(sem=None, vmem=VMEM_LIMIT):
    kw = dict(vmem_limit_bytes=vmem)
    if sem is not None:
        kw["dimension_semantics"] = sem
    return pltpu.CompilerParams(**kw)


def _whole_vmem():
    return pl.BlockSpec(memory_space=pltpu.VMEM)


def _sigmoid(v):
    return 1.0 / (1.0 + jnp.exp(-v))


def _fold8(v):
    n, c = v.shape
    return v.reshape(n // SUB, SUB, c).sum(axis=0)


def _dot_nt(a, b):
    return lax.dot_general(a, b, (((1,), (1,)), ((), ())), preferred_element_type=F32)


def _dot_tn(a, b):
    return lax.dot_general(a, b, (((0,), (0,)), ((), ())), preferred_element_type=F32)


def _mesh_pos():
    return lax.axis_index("x"), lax.axis_index("y"), lax.axis_index("c")


def _rms_proj_gather(x, norm_g, w_in, w_out, conv_w, order):
    s = x.shape[0]
    ts = min(TS_PROJ, s)
    nt = s // ts
    hin = w_in.shape[0] // 2
    hout = w_out.shape[0] // 2

    def body(order_ref, x_ref, g_ref, win_ref, wout_ref, cw_ref,
             h_ref, proj_ref, wing_ref, woutg_ref, cwg_ref, wg_ref, wob_ref, send_sems, recv_sems, local_sems):
        p = pl.program_id(0)
        t = pl.program_id(1)
        mx, my, c = _mesh_pos()
        me = 2 * mx + my
        chips = [(1 - mx, my), (mx, 1 - my), (1 - mx, 1 - my)]
        sibling = (mx, my, 1 - c)

        def remote(src, dst, sem, dev):
            return pltpu.make_async_remote_copy(
                src_ref=src, dst_ref=dst, send_sem=send_sems.at[sem], recv_sem=recv_sems.at[sem],
                device_id=dev, device_id_type=MESH)

        def w_in_part(blk, half):
            return wg_ref.at[blk, pl.ds(half * hin, hin)]

        def keep(blk, k):
            return pltpu.make_async_copy(wg_ref.at[blk], wing_ref.at[blk], local_sems.at[2 + k])

        def w_out_part(blk, half):
            return woutg_ref.at[blk, pl.ds(half * hout, hout)]

        def sends():
            out = []
            for j, (cx, cy) in enumerate(chips):
                blk = 2 * cx + cy
                out.append(remote(w_in_part(me, c), w_in_part(me, c), j, (cx, cy, c)))
                out.append(remote(w_in_part(blk, c), w_in_part(blk, c), 3 + j, sibling))
                out.append(remote(wob_ref.at[pl.ds(c * hout, hout)], w_out_part(me, c), 6 + j, (cx, cy, c)))
                out.append(remote(w_out_part(blk, c), w_out_part(blk, c), 9 + j, sibling))
                out.append(remote(cw_ref, cwg_ref.at[me], 12 + j, (cx, cy, c)))
            return out

        @pl.when(jnp.logical_and(p == 0, t == 0))
        def _():
            wg_ref[me] = win_ref[...].astype(BF16)
            wob_ref[...] = wout_ref[...].astype(BF16)
            keep(me, 0).start()
            mine = [pltpu.make_async_copy(wob_ref, woutg_ref.at[me], local_sems.at[0]),
                    pltpu.make_async_copy(cw_ref, cwg_ref.at[me], local_sems.at[1])]
            for cp in mine:
                cp.start()
            for k, cp in enumerate(sends()):
                if k % 5 in (0, 2, 4):
                    cp.start()
            for cp in mine:
                cp.wait()

        for j, (cx, cy) in enumerate(chips):
            blk = 2 * cx + cy

            @pl.when(jnp.logical_and(p == j + 1, t == 0))
            def _():
                remote(w_in_part(blk, c), w_in_part(blk, c), j, (cx, cy, c)).wait_recv()
                remote(w_in_part(blk, c), w_in_part(blk, c), 3 + j, sibling).start()
                remote(w_in_part(blk, 1 - c), w_in_part(blk, 1 - c), 3 + j, sibling).wait_recv()
                keep(blk, j + 1).start()

        @pl.when(jnp.logical_and(p == N_SHARD - 1, t == 0))
        def _():
            for j, (cx, cy) in enumerate(chips):
                blk = 2 * cx + cy
                remote(w_out_part(blk, c), w_out_part(blk, c), 6 + j, (cx, cy, c)).wait_recv()
                remote(w_out_part(blk, c), w_out_part(blk, c), 9 + j, sibling).start()

        xt = x_ref[...]
        r = lax.rsqrt(jnp.mean(xt * xt, axis=-1, keepdims=True) + EPS)
        hb = (xt * r * g_ref[...]).astype(BF16)

        @pl.when(p == 0)
        def _():
            h_ref[...] = hb

        proj_ref[...] = jnp.dot(hb, wg_ref[order_ref[p]], preferred_element_type=F32).astype(BF16)

        @pl.when(jnp.logical_and(p == N_SHARD - 1, t == nt - 1))
        def _():
            for j, (cx, cy) in enumerate(chips):
                blk = 2 * cx + cy
                remote(w_out_part(blk, 1 - c), w_out_part(blk, 1 - c), 9 + j, sibling).wait_recv()
                remote(cw_ref, cwg_ref.at[blk], 12 + j, (cx, cy, c)).wait_recv()
            for cp in sends():
                cp.wait_send()
            keep(me, 0).wait()
            for j, (cx, cy) in enumerate(chips):
                keep(2 * cx + cy, j + 1).wait()

    hbm = pl.BlockSpec(memory_space=pl.ANY)
    return pl.pallas_call(
        body, name="rms_proj_gather",
        grid_spec=pltpu.PrefetchScalarGridSpec(
            num_scalar_prefetch=1, grid=(N_SHARD, nt),
            in_specs=[pl.BlockSpec((ts, D_MODEL), lambda p, t, o: (t, 0)),
                      pl.BlockSpec((1, D_MODEL), lambda p, t, o: (0, 0)),
                      _whole_vmem(), _whole_vmem(), _whole_vmem()],
            out_specs=(pl.BlockSpec((ts, D_MODEL), lambda p, t, o: (jnp.where(p == 0, t, nt - 1), 0)),
                       pl.BlockSpec((ts, W_IN_COLS), lambda p, t, o: (t, o[p])),
                       hbm, hbm, hbm),
            scratch_shapes=[pltpu.VMEM((N_SHARD,) + w_in.shape, BF16), pltpu.VMEM(w_out.shape, BF16),
                            pltpu.SemaphoreType.DMA((15,)), pltpu.SemaphoreType.DMA((15,)),
                            pltpu.SemaphoreType.DMA((6,))]),
        out_shape=(jax.ShapeDtypeStruct((s, D_MODEL), BF16), jax.ShapeDtypeStruct((s, D_IN), BF16),
                   jax.ShapeDtypeStruct((N_SHARD,) + w_in.shape, BF16),
                   jax.ShapeDtypeStruct((N_SHARD,) + w_out.shape, BF16),
                   jax.ShapeDtypeStruct((N_SHARD,) + conv_w.shape, F32)),
        compiler_params=_cparams(("arbitrary", "arbitrary")),
    )(order, x, norm_g, w_in, w_out, conv_w)


def _halo_specs(ts, s, width, col_block):
    per = ts // HALO
    last = s // HALO - 1
    prev = pl.BlockSpec((HALO, width), lambda i: (jnp.maximum(i * per - 1, 0), col_block))
    nxt = pl.BlockSpec((HALO, width), lambda i: (jnp.minimum((i + 1) * per, last), col_block))
    return prev, nxt


def _layer_norm_stats(v):
    mu = jnp.mean(v, axis=-1, keepdims=True)
    vc = v - mu
    var = jnp.mean(vc * vc, axis=-1, keepdims=True)
    rstd = lax.rsqrt(var + EPS)
    return vc * rstd, rstd


def _fill_taps(wb_ref, cw_ref):
    for shard in range(N_SHARD):
        cols = slice(shard * CONV_W_COLS, (shard + 1) * CONV_W_COLS)
        for k in range(KW):
            wb_ref[k, :, cols] = jnp.broadcast_to(cw_ref[shard, k:k + 1, :], (SUB, CONV_W_COLS))


def _fill_shifted(sh_ref, ext):
    n = ext.shape[0]
    sh_ref[0] = ext
    for r in range(1, SUB):
        sh_ref[r] = pltpu.roll(ext, n - r, axis=0)


def _branch_fwd(proj, cwg, conv_b, conv_ln_g, conv_ln_b, sgu_ln_g, sgu_ln_b, w_s, bs_t):
    s = proj.shape[0]
    ts = min(TS_FWD, s)
    nt = s // ts
    ra = min(ROWS_A, ts)

    def body(pm_ref, pp_ref, pn_ref, cw_ref, cb_ref, clg_ref, clb_ref, slg_ref, slb_ref, ws_ref, bst_ref,
             y_ref, cv_ref, sh_ref, wb_ref):
        i = pl.program_id(0)

        @pl.when(i == 0)
        def _():
            _fill_taps(wb_ref, cw_ref)

        keep_prev = (i > 0).astype(F32)
        keep_next = (i < nt - 1).astype(F32)

        for lb in range(C_BR // LANE):
            lanes = slice(lb * LANE, (lb + 1) * LANE)
            gates = slice(C_BR + lb * LANE, C_BR + (lb + 1) * LANE)

            def glu(ref):
                return ref[:, lanes].astype(F32) * _sigmoid(ref[:, gates].astype(F32))

            ext = jnp.concatenate([glu(pp_ref) * keep_prev, glu(pm_ref), glu(pn_ref) * keep_next], axis=0)
            _fill_shifted(sh_ref, ext)
            shard, off = divmod(lb * LANE, CONV_W_COLS)
            bias = cb_ref[:, lanes]

            def chunk(jc, carry):
                base = pl.multiple_of(jc * ra, ra)
                acc = jnp.zeros((ra // SUB, SUB, LANE), F32) + bias
                for k in range(KW):
                    o = k + 1
                    d = sh_ref[o % SUB, pl.ds(base + SUB * (o // SUB), ra), :]
                    acc = acc + d.reshape(ra // SUB, SUB, LANE) * wb_ref[k, :, lanes]
                cv_ref[pl.ds(base, ra), lanes] = acc.reshape(ra, LANE)
                return carry

            lax.fori_loop(0, ts // ra, chunk, 0)

        lnh, _ = _layer_norm_stats(cv_ref[...])
        ln = lnh * clg_ref[...] + clb_ref[...]
        gc = pm_ref[:, 2 * C_BR:3 * C_BR].astype(F32)
        y_ref[:, :C_BR] = (ln * _sigmoid(ln) * gc * _sigmoid(gc)).astype(BF16)

        vh, _ = _layer_norm_stats(pm_ref[:, 4 * C_BR:5 * C_BR].astype(F32))
        vn = (vh * slg_ref[...] + slb_ref[...]).astype(BF16)
        for hd in range(HEADS):
            w_h = ws_ref[hd].astype(BF16)
            b_h = bst_ref[:, hd:hd + 1]
            cols = slice(hd * HEAD_DIM, (hd + 1) * HEAD_DIM)
            for ch in range(ts // CHUNK):
                rows = slice(ch * CHUNK, (ch + 1) * CHUNK)
                mixed = jnp.dot(w_h, vn[rows, cols], preferred_element_type=F32) + b_h
                u = pm_ref[rows, 3 * C_BR + hd * HEAD_DIM:3 * C_BR + (hd + 1) * HEAD_DIM].astype(F32)
                gs = pm_ref[rows, 5 * C_BR + hd * HEAD_DIM:5 * C_BR + (hd + 1) * HEAD_DIM].astype(F32)
                y_ref[rows, C_BR + hd * HEAD_DIM:C_BR + (hd + 1) * HEAD_DIM] = (
                    u * mixed * gs * _sigmoid(gs)).astype(BF16)

    prev, nxt = _halo_specs(ts, s, 2 * C_BR, 0)
    row = pl.BlockSpec((1, C_BR), lambda i: (0, 0))
    return pl.pallas_call(
        body, name="branch_fwd",
        grid=(nt,),
        out_shape=(jax.ShapeDtypeStruct((s, 2 * C_BR), BF16), jax.ShapeDtypeStruct((s, C_BR), F32)),
        in_specs=[pl.BlockSpec((ts, D_IN), lambda i: (i, 0)), prev, nxt,
                  _whole_vmem(), row, row, row, row, row, _whole_vmem(), _whole_vmem()],
        out_specs=(pl.BlockSpec((ts, 2 * C_BR), lambda i: (i, 0)),
                   pl.BlockSpec((ts, C_BR), lambda i: (i, 0))),
        scratch_shapes=[pltpu.VMEM((SUB, ts + 2 * HALO, LANE), F32), pltpu.VMEM((KW, SUB, C_BR), F32)],
        compiler_params=_cparams(("arbitrary",)),
    )(proj, proj, proj, cwg, conv_b, conv_ln_g, conv_ln_b, sgu_ln_g, sgu_ln_b, w_s, bs_t)


def _out_proj(y, x, target, woutg, final_g):
    s = x.shape[0]
    ts = min(TS_OUT, s)
    nt = s // ts

    def body(y_ref, x_ref, t_ref, w_ref, g_ref, dx2_ref, dy_ref, gw_ref, gf_ref, se_ref):
        i = pl.program_id(0)

        @pl.when(i == 0)
        def _():
            gw_ref[...] = jnp.zeros_like(gw_ref)
            gf_ref[...] = jnp.zeros_like(gf_ref)
            se_ref[...] = jnp.zeros_like(se_ref)

        yt = y_ref[...]
        x2 = x_ref[...]
        for k in range(N_SHARD):
            x2 = x2 + jnp.dot(yt[:, k * W_OUT_ROWS:(k + 1) * W_OUT_ROWS], w_ref[k], preferred_element_type=F32)
        r2 = lax.rsqrt(jnp.mean(x2 * x2, axis=-1, keepdims=True) + EPS)
        n2 = x2 * r2
        g = g_ref[...]
        diff = n2 * g - t_ref[...]
        se_ref[...] += _fold8(diff * diff)
        dout = diff * (1.0 / D_MODEL)
        gf_ref[...] += _fold8(dout * n2)
        dn = dout * g
        dx2 = r2 * (dn - n2 * jnp.mean(dn * n2, axis=-1, keepdims=True))
        dx2_ref[...] = dx2
        dxb = dx2.astype(BF16)
        for k in range(N_SHARD):
            rows = slice(k * W_OUT_ROWS, (k + 1) * W_OUT_ROWS)
            dy_ref[:, rows] = _dot_nt(dxb, w_ref[k]).astype(BF16)
            gw_ref[rows, :] += _dot_tn(yt[:, rows], dxb)

        @pl.when(i == nt - 1)
        def _():
            gf_ref[...] = jnp.broadcast_to(jnp.sum(gf_ref[...], axis=0, keepdims=True), gf_ref.shape)

    tile = pl.BlockSpec((ts, D_MODEL), lambda i: (i, 0))
    wide = pl.BlockSpec((ts, 2 * C_BR), lambda i: (i, 0))
    acc8 = pl.BlockSpec((SUB, D_MODEL), lambda i: (0, 0))
    return pl.pallas_call(
        body, name="out_proj",
        grid=(nt,),
        out_shape=(jax.ShapeDtypeStruct((s, D_MODEL), F32), jax.ShapeDtypeStruct((s, 2 * C_BR), BF16),
                   jax.ShapeDtypeStruct((2 * C_BR, D_MODEL), F32),
                   jax.ShapeDtypeStruct((SUB, D_MODEL), F32), jax.ShapeDtypeStruct((SUB, D_MODEL), F32)),
        in_specs=[wide, tile, tile, _whole_vmem(), pl.BlockSpec((1, D_MODEL), lambda i: (0, 0))],
        out_specs=(tile, wide, pl.BlockSpec((2 * C_BR, D_MODEL), lambda i: (0, 0)), acc8, acc8),
        compiler_params=_cparams(("arbitrary",)),
    )(y, x, target, woutg, final_g)


def _dsilu(v, sg):
    return sg * (1.0 + v * (1.0 - sg))


def _branch_bwd(proj, dy, cv, cwg, conv_ln_g, conv_ln_b, sgu_ln_g, sgu_ln_b, w_s, ws_t, bs_t, gf8, dep):
    s = proj.shape[0]
    ts = min(TS_BWD, s)
    nt = s // ts
    ra = min(ROWS_A, ts)
    rb = min(ROWS_B, ts)
    te = ts + 2 * HALO

    def body(pm_ref, dym_ref, cvm_ref, gcp_ref, gcn_ref, dyp_ref, dyn_ref, cvp_ref, cvn_ref,
             cw_ref, clg_ref, clb_ref, slg_ref, slb_ref, ws_ref, wst_ref, bst_ref, gf_ref, dep_ref,
             dp_ref, small_ref, gws_ref,
             sh_ref, glu_ref, dgl_ref, dcv_ref, acc_ref, gcw_ref, gbs_ref, wb_ref):
        i = pl.program_id(0)

        @pl.when(i == 0)
        def _():
            _fill_taps(wb_ref, cw_ref)
            acc_ref[...] = jnp.zeros_like(acc_ref)
            gcw_ref[...] = jnp.zeros_like(gcw_ref)
            gbs_ref[...] = jnp.zeros_like(gbs_ref)
            gws_ref[...] = jnp.zeros_like(gws_ref)

        def ext(prev_ref, main, next_ref):
            return jnp.concatenate([prev_ref[...].astype(F32), main, next_ref[...].astype(F32)], axis=0)

        main = slice(HALO, HALO + ts)

        cv_e = ext(cvp_ref, cvm_ref[...], cvn_ref)
        gc_e = ext(gcp_ref, pm_ref[:, 2 * C_BR:3 * C_BR].astype(F32), gcn_ref)
        dyc_e = ext(dyp_ref, dym_ref[:, :C_BR].astype(F32), dyn_ref)
        lnh, rstd = _layer_norm_stats(cv_e)
        clg = clg_ref[...]
        ln = lnh * clg + clb_ref[...]
        sg_ln = _sigmoid(ln)
        sg_gc = _sigmoid(gc_e)
        d_ln = dyc_e * gc_e * sg_gc * _dsilu(ln, sg_ln)
        dp_ref[:, 2 * C_BR:3 * C_BR] = (
            dyc_e[main] * ln[main] * sg_ln[main] * _dsilu(gc_e[main], sg_gc[main])).astype(BF16)
        dlnh = d_ln * clg
        d_cv = rstd * (dlnh - jnp.mean(dlnh, axis=-1, keepdims=True)
                       - lnh * jnp.mean(dlnh * lnh, axis=-1, keepdims=True))
        row = lax.broadcasted_iota(jnp.int32, (te, 1), 0)
        valid = jnp.logical_and(jnp.logical_or(row >= HALO, i > 0),
                                jnp.logical_or(row < HALO + ts, i < nt - 1))
        d_cv = jnp.where(valid, d_cv, 0.0)
        dcv_ref[...] = d_cv
        acc_ref[0:8, :] += _fold8(d_cv[main])
        acc_ref[8:16, :] += _fold8(d_ln[main] * lnh[main])
        acc_ref[16:24, :] += _fold8(d_ln[main])

        for lb in range(C_BR // LANE):
            lanes = slice(lb * LANE, (lb + 1) * LANE)
            gates = slice(C_BR + lb * LANE, C_BR + (lb + 1) * LANE)
            shard, off = divmod(lb * LANE, CONV_W_COLS)
            av = pm_ref[:, lanes].astype(F32)
            sg = _sigmoid(pm_ref[:, gates].astype(F32))
            glu_ref[...] = av * sg
            _fill_shifted(sh_ref, dcv_ref[:, lanes])

            def chunk_a(jc, carry):
                base = pl.multiple_of(jc * ra, ra)
                acc = jnp.zeros((ra // SUB, SUB, LANE), F32)
                for j in range(KW):
                    o = j + 1
                    d = sh_ref[o % SUB, pl.ds(base + SUB * (o // SUB), ra), :]
                    acc = acc + d.reshape(ra // SUB, SUB, LANE) * wb_ref[KW - 1 - j, :, lanes]
                dgl_ref[pl.ds(base, ra), :] = acc.reshape(ra, LANE)
                return carry

            lax.fori_loop(0, ts // ra, chunk_a, 0)

            def chunk_b(jc, accs):
                base = pl.multiple_of(jc * rb, rb)
                g = glu_ref[pl.ds(base, rb), :]
                out = []
                for j in range(KW):
                    o = j + 1
                    d = sh_ref[o % SUB, pl.ds(base + SUB * (o // SUB), rb), :]
                    out.append(accs[j] + _fold8(g * d))
                return tuple(out)

            accs = lax.fori_loop(0, ts // rb, chunk_b, tuple(jnp.zeros((SUB, LANE), F32) for _ in range(KW)))
            for j in range(KW):
                gcw_ref[j * SUB:(j + 1) * SUB, lanes] += accs[j]

            dglu = dgl_ref[...]
            dp_ref[:, lanes] = (dglu * sg).astype(BF16)
            dp_ref[:, gates] = (dglu * av * sg * (1.0 - sg)).astype(BF16)

        vh, vrstd = _layer_norm_stats(pm_ref[:, 4 * C_BR:5 * C_BR].astype(F32))
        slg = slg_ref[...]
        vn = (vh * slg + slb_ref[...]).astype(BF16)
        for hd in range(HEADS):
            w_h = ws_ref[hd].astype(BF16)
            wt_h = wst_ref[hd].astype(BF16)
            b_h = bst_ref[:, hd:hd + 1]
            cols = slice(hd * HEAD_DIM, (hd + 1) * HEAD_DIM)
            gws_h = jnp.zeros((CHUNK, CHUNK), F32)
            gbs_h = jnp.zeros((CHUNK, HEAD_DIM), F32)
            for ch in range(ts // CHUNK):
                rows = slice(ch * CHUNK, (ch + 1) * CHUNK)
                vn_b = vn[rows, cols]
                mixed = jnp.dot(w_h, vn_b, preferred_element_type=F32) + b_h
                u = pm_ref[rows, 3 * C_BR + hd * HEAD_DIM:3 * C_BR + (hd + 1) * HEAD_DIM].astype(F32)
                gs = pm_ref[rows, 5 * C_BR + hd * HEAD_DIM:5 * C_BR + (hd + 1) * HEAD_DIM].astype(F32)
                dys = dym_ref[rows, C_BR + hd * HEAD_DIM:C_BR + (hd + 1) * HEAD_DIM].astype(F32)
                sg_gs = _sigmoid(gs)
                silu_gs = gs * sg_gs
                dp_ref[rows, 3 * C_BR + hd * HEAD_DIM:3 * C_BR + (hd + 1) * HEAD_DIM] = (
                    dys * mixed * silu_gs).astype(BF16)
                dp_ref[rows, 5 * C_BR + hd * HEAD_DIM:5 * C_BR + (hd + 1) * HEAD_DIM] = (
                    dys * u * mixed * _dsilu(gs, sg_gs)).astype(BF16)
                d_mixed = dys * u * silu_gs
                dm_b = d_mixed.astype(BF16)
                gws_h = gws_h + _dot_nt(dm_b, vn_b)
                gbs_h = gbs_h + d_mixed
                dcv_ref[HALO + ch * CHUNK:HALO + (ch + 1) * CHUNK, cols] = jnp.dot(
                    wt_h, dm_b, preferred_element_type=F32)
            gws_ref[hd] += gws_h
            gbs_ref[:, cols] += gbs_h
        d_vn = dcv_ref[main, :]
        acc_ref[24:32, :] += _fold8(d_vn * vh)
        acc_ref[32:40, :] += _fold8(d_vn)
        dvh = d_vn * slg
        dp_ref[:, 4 * C_BR:5 * C_BR] = (vrstd * (
            dvh - jnp.mean(dvh, axis=-1, keepdims=True)
            - vh * jnp.mean(dvh * vh, axis=-1, keepdims=True))).astype(BF16)

        @pl.when(i == nt - 1)
        def _():
            small_ref[...] = jnp.zeros_like(small_ref)
            for a in range(5):
                small_ref[1 + a:2 + a, :] = jnp.sum(acc_ref[a * SUB:(a + 1) * SUB, :], axis=0, keepdims=True)
            ones = jnp.ones((SUB, HEAD_DIM), F32)
            for hd in range(HEADS):
                cols = slice(hd * HEAD_DIM, (hd + 1) * HEAD_DIM)
                rowsum = lax.dot_general(ones, gbs_ref[:, cols], (((1,), (1,)), ((), ())),
                                         precision=lax.Precision.HIGHEST, preferred_element_type=F32)
                small_ref[6:7, cols] = rowsum[0:1, :]
            small_ref[7:8, :] = gf_ref[0:1, :]
            for k in range(KW):
                j = KW - 1 - k
                small_ref[8 + k:9 + k, :] = jnp.sum(gcw_ref[j * SUB:(j + 1) * SUB, :], axis=0, keepdims=True)

    gc_prev, gc_next = _halo_specs(ts, s, C_BR, 2)
    lo_prev, lo_next = _halo_specs(ts, s, C_BR, 0)
    row = pl.BlockSpec((1, C_BR), lambda i: (0, 0))
    return pl.pallas_call(
        body, name="branch_bwd",
        grid=(nt,),
        out_shape=(jax.ShapeDtypeStruct((s, D_IN), BF16), jax.ShapeDtypeStruct((40, C_BR), F32),
                   jax.ShapeDtypeStruct((HEADS, CHUNK, CHUNK), F32)),
        in_specs=[pl.BlockSpec((ts, D_IN), lambda i: (i, 0)),
                  pl.BlockSpec((ts, 2 * C_BR), lambda i: (i, 0)),
                  pl.BlockSpec((ts, C_BR), lambda i: (i, 0)),
                  gc_prev, gc_next, lo_prev, lo_next, lo_prev, lo_next,
                  _whole_vmem(), row, row, row, row, _whole_vmem(), _whole_vmem(), _whole_vmem(),
                  _whole_vmem(), _whole_vmem()],
        out_specs=(pl.BlockSpec((ts, D_IN), lambda i: (i, 0)),
                   pl.BlockSpec((40, C_BR), lambda i: (0, 0)),
                   pl.BlockSpec((HEADS, CHUNK, CHUNK), lambda i: (0, 0, 0))),
        scratch_shapes=[pltpu.VMEM((SUB, te, LANE), F32),
                        pltpu.VMEM((ts, LANE), F32),
                        pltpu.VMEM((ts, LANE), F32),
                        pltpu.VMEM((te, C_BR), F32),
                        pltpu.VMEM((5 * SUB, C_BR), F32),
                        pltpu.VMEM((KW * SUB, C_BR), F32),
                        pltpu.VMEM((CHUNK, C_BR), F32),
                        pltpu.VMEM((KW, SUB, C_BR), F32)],
        compiler_params=_cparams(("arbitrary",)),
    )(proj, dy, cv, proj, proj, dy, dy, cv, cv,
      cwg, conv_ln_g, conv_ln_b, sgu_ln_g, sgu_ln_b, w_s, ws_t, bs_t, gf8, dep)


def _in_bwd(dproj, x, dx2, norm_g, wing, dep, first_tile, n_tiles, gx_prev=None, gn_prev=None):
    s = x.shape[0]
    ts = min(TS_INB, s)

    def body(*refs):
        dp_ref, x_ref, dx2_ref, g_ref, w_ref = refs[:5]
        gx_ref, gn_ref, acc_ref = refs[-3:]
        i = pl.program_id(0)

        @pl.when(i == 0)
        def _():
            acc_ref[...] = jnp.zeros_like(acc_ref)

        dh = jnp.zeros((ts, D_MODEL), F32)
        for k in range(N_SHARD):
            dh = dh + _dot_nt(dp_ref[:, k * W_IN_COLS:(k + 1) * W_IN_COLS], w_ref[k])
        xt = x_ref[...]
        r = lax.rsqrt(jnp.mean(xt * xt, axis=-1, keepdims=True) + EPS)
        n = xt * r
        acc_ref[...] += _fold8(dh * n)
        dn = dh * g_ref[...]
        gx_ref[...] = dx2_ref[...] + r * (dn - n * jnp.mean(dn * n, axis=-1, keepdims=True))

        @pl.when(i == n_tiles - 1)
        def _():
            total = jnp.broadcast_to(jnp.sum(acc_ref[...], axis=0, keepdims=True), gn_ref.shape)
            if gn_prev is not None:
                total = total + refs[7][...]
            gn_ref[...] = total

    tile = pl.BlockSpec((ts, D_MODEL), lambda i: (i + first_tile, 0))
    in_specs = [pl.BlockSpec((ts, D_IN), lambda i: (i + first_tile, 0)), tile, tile,
                pl.BlockSpec((1, D_MODEL), lambda i: (0, 0)), _whole_vmem(), _whole_vmem()]
    operands = [dproj, x, dx2, norm_g, wing, dep]
    aliases = {}
    if gx_prev is not None:
        in_specs += [pl.BlockSpec(memory_space=pl.ANY), _whole_vmem()]
        operands += [gx_prev, gn_prev]
        aliases = {6: 0}
    return pl.pallas_call(
        body, name="in_bwd_%d" % first_tile,
        grid=(n_tiles,),
        out_shape=(jax.ShapeDtypeStruct((s, D_MODEL), F32), jax.ShapeDtypeStruct((SUB, D_MODEL), F32)),
        in_specs=in_specs,
        out_specs=(tile, pl.BlockSpec((SUB, D_MODEL), lambda i: (0, 0))),
        scratch_shapes=[pltpu.VMEM((SUB, D_MODEL), F32)],
        input_output_aliases=aliases,
        compiler_params=_cparams(("arbitrary",)),
    )(*operands)


def _grad_w_in(h, dproj, dep):
    s = h.shape[0]
    tk = min(TK_GW, s)
    half = D_MODEL // 2

    def body(h_ref, dp_ref, dep_ref, o_ref):
        @pl.when(pl.program_id(1) == 0)
        def _():
            o_ref[...] = jnp.zeros_like(o_ref)

        o_ref[0] += _dot_tn(h_ref[...], dp_ref[...]).reshape(2, half, W_IN_COLS)

    return pl.pallas_call(
        body, name="grad_w_in",
        grid=(N_SHARD, s // tk),
        out_shape=jax.ShapeDtypeStruct((N_SHARD, 2, half, W_IN_COLS), F32),
        in_specs=[pl.BlockSpec((tk, D_MODEL), lambda k, t: (t, 0)),
                  pl.BlockSpec((tk, W_IN_COLS), lambda k, t: (t, k)), _whole_vmem()],
        out_specs=pl.BlockSpec((1, 2, half, W_IN_COLS), lambda k, t: (k, 0, 0, 0)),
        compiler_params=_cparams(("parallel", "arbitrary")),
    )(h, dproj, dep)


HBM_SPEC = pl.BlockSpec(memory_space=pltpu.HBM)
SEM_SPEC = pl.BlockSpec(memory_space=pltpu.SEMAPHORE)
SIDE_EFFECT = pltpu.SideEffectType.DATAFLOW_SIDE_EFFECTING


def _remote_copies(plan, bufs, send_sems, recv_sems):
    x, y, c = _mesh_pos()
    return [pltpu.make_async_remote_copy(src_ref=src, dst_ref=dst, send_sem=send_sems.at[k],
                                         recv_sem=recv_sems.at[k], device_id=dev, device_id_type=MESH)
            for k, (src, dst, dev) in enumerate(plan(x, y, c, *bufs))]


def _start_copies(name, bufs, n_copies, plan):
    n = len(bufs)

    def body(*refs):
        for cp in _remote_copies(plan, refs[:n], refs[n], refs[n + 1]):
            cp.start()
        refs[-1][...] = jnp.zeros_like(refs[-1])

    outs = pl.pallas_call(
        body, name=name,
        out_shape=(pltpu.SemaphoreType.DMA((n_copies,)), pltpu.SemaphoreType.DMA((n_copies,)),
                   *[pltpu.HBM(b.shape, b.dtype) for b in bufs], jax.ShapeDtypeStruct((SUB, LANE), F32)),
        in_specs=[HBM_SPEC] * n,
        out_specs=(SEM_SPEC, SEM_SPEC, *[HBM_SPEC] * n, _whole_vmem()),
        input_output_aliases={i: 2 + i for i in range(n)},
        compiler_params=pltpu.CompilerParams(has_side_effects=SIDE_EFFECT),
    )(*[pltpu.with_memory_space_constraint(b, pltpu.HBM) for b in bufs])
    return outs[0], outs[1], list(outs[2:2 + n]), outs[-1]


def _wait_copies(name, send_sems, recv_sems, bufs, plan, after):
    n = len(bufs)

    def body(*refs):
        for cp in _remote_copies(plan, refs[:n], refs[n], refs[n + 1]):
            cp.wait_send()
            cp.wait_recv()

    outs = pl.pallas_call(
        body, name=name,
        out_shape=tuple(pltpu.HBM(b.shape, b.dtype) for b in bufs),
        in_specs=[HBM_SPEC] * n + [SEM_SPEC, SEM_SPEC, pl.BlockSpec(memory_space=pl.ANY)],
        out_specs=(HBM_SPEC,) * n,
        input_output_aliases={i: i for i in range(n)},
        compiler_params=pltpu.CompilerParams(has_side_effects=SIDE_EFFECT),
    )(*bufs, send_sems, recv_sems, after)
    return list(outs)


def _landing(shape, dtype):
    return lax.empty(shape, dtype)


def _plan_pair_exchange(x, y, c, g, r):
    return [(g.at[k, 1 - c], r.at[k], (x, y, 1 - c)) for k in range(N_SHARD)]


def _plan_chip_exchange(x, y, c, a, r):
    chips = [(1 - x, y), (x, 1 - y), (1 - x, 1 - y)]
    return [(a.at[2 * cx + cy], r.at[j], (cx, cy, c)) for j, (cx, cy) in enumerate(chips)]


def _plan_pair_gather(x, y, c, f):
    return [(f.at[c], f.at[c], (x, y, 1 - c))]


def _plan_all_gather(x, y, c, own, land):
    me = 4 * x + 2 * y + c
    flip = lambda v, bit: 1 - v if bit else v
    return [(own, land.at[me], (flip(x, m >> 2 & 1), flip(y, m >> 1 & 1), flip(c, m & 1))) for m in range(1, 8)]


def _add_pair(g, r, pos, name):
    _, _, rows, cols = g.shape
    tr = min(256, rows)

    def body(pos_ref, g_ref, r_ref, o_ref, ob_ref):
        v = g_ref[0] + r_ref[...]
        o_ref[...] = v
        ob_ref[...] = v.astype(BF16)

    blk = pl.BlockSpec((1, tr, cols), lambda k, t, pos_ref: (k, t, 0))
    return pl.pallas_call(
        body, name=name,
        grid_spec=pltpu.PrefetchScalarGridSpec(
            num_scalar_prefetch=1, grid=(N_SHARD, rows // tr),
            in_specs=[pl.BlockSpec((1, 1, tr, cols), lambda k, t, pos_ref: (k, pos_ref[1], t, 0)), blk],
            out_specs=(blk, blk)),
        out_shape=(jax.ShapeDtypeStruct((N_SHARD, rows, cols), F32), jax.ShapeDtypeStruct((N_SHARD, rows, cols), BF16)),
        compiler_params=_cparams(("parallel", "parallel")),
    )(pos, g, r)


def _add_chips(a, r, pos, name):
    _, rows, cols = a.shape
    tr = min(256, rows)

    def body(pos_ref, a_ref, r_ref, o_ref):
        o_ref[0] = ((a_ref[0] + r_ref[0].astype(F32)) + r_ref[1].astype(F32)) + r_ref[2].astype(F32)

    return pl.pallas_call(
        body, name=name,
        grid_spec=pltpu.PrefetchScalarGridSpec(
            num_scalar_prefetch=1, grid=(rows // tr,),
            in_specs=[pl.BlockSpec((1, tr, cols), lambda t, pos_ref: (pos_ref[0], t, 0)),
                      pl.BlockSpec((3, tr, cols), lambda t, pos_ref: (0, t, 0))],
            out_specs=pl.BlockSpec((1, tr, cols), lambda t, pos_ref: (pos_ref[1], t, 0))),
        out_shape=jax.ShapeDtypeStruct((2, rows, cols), F32),
        compiler_params=_cparams(("parallel",)),
    )(pos, a, r)


def _sum_slots(pos_ref, own, land_ref, rows):
    me = pos_ref[2]
    total = None
    for d in range(8):
        term = jnp.where(me == d, own, land_ref[d] if rows is None else land_ref[d, rows, :])
        total = term if total is None else total + term
    return total


def _sum_small(pos, small, small_land, gws, gws_land):
    def body(pos_ref, sm_ref, sml_ref, gw_ref, gwl_ref, o_sm, o_gw):
        o_sm[...] = _sum_slots(pos_ref, sm_ref[...], sml_ref, None)
        o_gw[...] = _sum_slots(pos_ref, gw_ref[...], gwl_ref, None)

    return pl.pallas_call(
        body, name="sum_small",
        grid_spec=pltpu.PrefetchScalarGridSpec(
            num_scalar_prefetch=1, grid=(1,),
            in_specs=[_whole_vmem()] * 4, out_specs=[_whole_vmem()] * 2),
        out_shape=[jax.ShapeDtypeStruct(small.shape, F32), jax.ShapeDtypeStruct(gws.shape, F32)],
        compiler_params=_cparams(("arbitrary",)),
    )(pos, small, small_land, gws, gws_land)


def _adamw_math(w, g, m, v):
    m = ADAM_B1 * m + (1.0 - ADAM_B1) * g
    v = ADAM_B2 * v + (1.0 - ADAM_B2) * (g * g)
    m_hat = m / (1.0 - ADAM_B1 ** ADAM_STEP)
    v_hat = v / (1.0 - ADAM_B2 ** ADAM_STEP)
    delta = -ADAM_LR * (m_hat / (jnp.sqrt(v_hat) + ADAM_EPS) + ADAM_WD * w)
    return delta, m, v


def _adamw_large(w, g, m, v, dep, name):
    rows, cols = w.shape
    tr = min(256, rows)

    def body(w_ref, g_ref, m_ref, v_ref, dep_ref, d_ref, nm_ref, nv_ref):
        d_ref[...], nm_ref[...], nv_ref[...] = _adamw_math(w_ref[...], g_ref[...], m_ref[...], v_ref[...])

    tile = pl.BlockSpec((tr, cols), lambda t: (t, 0))
    return pl.pallas_call(
        body, name=name,
        grid=(rows // tr,),
        out_shape=(jax.ShapeDtypeStruct(w.shape, F32),) * 3,
        in_specs=[tile] * 4 + [_whole_vmem()], out_specs=(tile,) * 3,
        compiler_params=_cparams(("parallel",)),
    )(w, g, m, v, dep)


_ROW_OF = {"conv_b": 1, "conv_ln_g": 2, "conv_ln_b": 3, "sgu_ln_g": 4, "sgu_ln_b": 5, "b_s": 6, "final_g": 7}
_CONV_W_ROW = 8
_VECTORS = ("norm_g", "conv_b", "conv_ln_g", "conv_ln_b", "sgu_ln_g", "sgu_ln_b", "b_s", "final_g")


def _adamw_small(p, q, gn8, gn_land, pos, vectors, conv_w, w_s):
    names = list(_VECTORS)

    def body(pos_ref, p_ref, q_ref, gn_ref, gnl_ref, *refs):
        n_in = 3 * (len(names) + 2)
        ins, outs = refs[:n_in], refs[n_in:]
        me = pos_ref[0]
        for a, name in enumerate(names + ["conv_w", "w_s"]):
            w_ref, m_ref, v_ref = ins[3 * a:3 * a + 3]
            if name == "conv_w":
                g = jnp.zeros((KW, CONV_W_COLS), F32)
                for k in range(N_SHARD):
                    blk = p_ref[_CONV_W_ROW:_CONV_W_ROW + KW, k * CONV_W_COLS:(k + 1) * CONV_W_COLS]
                    g = jnp.where(me == k, blk, g)
            elif name == "w_s":
                g = q_ref[...]
            elif name == "norm_g":
                g = _sum_slots(pos_ref, gn_ref[0:1, :], gnl_ref, slice(0, 1))
            else:
                g = p_ref[_ROW_OF[name]:_ROW_OF[name] + 1, :]
            delta, nm, nv = _adamw_math(w_ref[...], g, m_ref[...], v_ref[...])
            for o_ref, val in zip(outs[4 * a:4 * a + 4], (g, delta, nm, nv)):
                o_ref[...] = val

    operands, shapes = [], []
    for name in names:
        operands += list(vectors[name])
        shapes += [jax.ShapeDtypeStruct((1, C_BR), F32)] * 4
    operands += list(conv_w)
    shapes += [jax.ShapeDtypeStruct((KW, CONV_W_COLS), F32)] * 4
    operands += list(w_s)
    shapes += [jax.ShapeDtypeStruct(q.shape, F32)] * 4
    outs = pl.pallas_call(
        body, name="adamw_small",
        grid_spec=pltpu.PrefetchScalarGridSpec(
            num_scalar_prefetch=1, grid=(1,),
            in_specs=[_whole_vmem()] * (4 + len(operands)),
            out_specs=[_whole_vmem()] * len(shapes)),
        out_shape=shapes,
        compiler_params=_cparams(("arbitrary",)),
    )(pos, p, q, gn8, gn_land, *operands)
    return {name: tuple(outs[4 * a:4 * a + 4]) for a, name in enumerate(names + ["conv_w", "w_s"])}


def kernel(x, norm_g, w_in, conv_w, conv_b, conv_ln_g, conv_ln_b, sgu_ln_g, sgu_ln_b, w_s, b_s, w_out, final_g, loss_target, m_norm_g, m_w_in, m_conv_w, m_conv_b, m_conv_ln_g, m_conv_ln_b, m_sgu_ln_g, m_sgu_ln_b, m_w_s, m_b_s, m_w_out, m_final_g, v_norm_g, v_w_in, v_conv_w, v_conv_b, v_conv_ln_g, v_conv_ln_b, v_sgu_ln_g, v_sgu_ln_b, v_w_s, v_b_s, v_w_out, v_final_g):
    xi, yi, ci = _mesh_pos()
    pos = jnp.stack([2 * xi + yi, ci, 4 * xi + 2 * yi + ci]).astype(jnp.int32)

    x2d = x[0]
    tgt = loss_target[0]
    fg = final_g.reshape(1, D_MODEL)
    ws3 = w_s[0]
    ws_t = jnp.swapaxes(ws3, 1, 2)
    bs_t = jnp.transpose(b_s[0])

    chip = 2 * xi + yi
    order = jnp.stack([chip, 2 * (1 - xi) + yi, 2 * xi + 1 - yi, 2 * (1 - xi) + 1 - yi]).astype(jnp.int32)
    h, proj, wing, woutg, cwg = _rms_proj_gather(x2d, norm_g, w_in[0], w_out[0], conv_w[0], order)
    y, cv = _branch_fwd(proj, cwg, conv_b, conv_ln_g, conv_ln_b, sgu_ln_g, sgu_ln_b, ws3, bs_t)
    dx2, dy, gwout, gf8, se8 = _out_proj(y, x2d, tgt, woutg, fg)
    loss = lax.psum((0.5 / D_MODEL) * jnp.sum(se8), ("x", "y", "c"))

    in_rows, out_rows = D_MODEL // 2, W_OUT_ROWS // 2
    gwout4 = gwout.reshape(N_SHARD, 2, out_rows, D_MODEL)
    ss, rs, (gwout4, r1_out), tok = _start_copies(
        "start_pair_exchange_w_out", [gwout4, _landing((N_SHARD, out_rows, D_MODEL), F32)], N_SHARD,
        _plan_pair_exchange)
    dproj, small, gws3 = _branch_bwd(proj, dy, cv, cwg, conv_ln_g, conv_ln_b, sgu_ln_g, sgu_ln_b, ws3, ws_t, bs_t,
                                     gf8, tok)
    gws = gws3.reshape(HEADS * CHUNK, CHUNK)
    gwout4, r1_out = _wait_copies("wait_pair_exchange_w_out", ss, rs, [gwout4, r1_out], _plan_pair_exchange, dproj)
    a_out, a_out_bf = _add_pair(gwout4, r1_out, pos, "add_pair_w_out")

    def plan_b(x, y, c, a, r, sm, sml, gw, gwl):
        return (_plan_chip_exchange(x, y, c, a, r) + _plan_all_gather(x, y, c, sm, sml)
                + _plan_all_gather(x, y, c, gw, gwl))

    ss, rs, bufs_b, tok = _start_copies(
        "start_chip_exchange_w_out",
        [a_out_bf, _landing((3, out_rows, D_MODEL), BF16), small, _landing((8,) + small.shape, F32),
         gws, _landing((8,) + gws.shape, F32)], 3 + 7 + 7, plan_b)
    gwin = _grad_w_in(h, dproj, tok)
    ss_c, rs_c, (gwin, r1_in), tok = _start_copies(
        "start_pair_exchange_w_in", [gwin, _landing((N_SHARD, in_rows, W_IN_COLS), F32)], N_SHARD,
        _plan_pair_exchange)
    nt = x2d.shape[0] // min(TS_INB, x2d.shape[0])
    gx_a, gn_a = _in_bwd(dproj, x2d, dx2, norm_g, wing, tok, 0, nt // 2)

    gwin, r1_in = _wait_copies("wait_pair_exchange_w_in", ss_c, rs_c, [gwin, r1_in], _plan_pair_exchange, gx_a)
    a_in, a_in_bf = _add_pair(gwin, r1_in, pos, "add_pair_w_in")
    a_out_bf, r2_out, small, small_land, gws, gws_land = _wait_copies(
        "wait_chip_exchange_w_out", ss, rs, bufs_b, plan_b, gx_a)
    f_out = _add_chips(a_out, r2_out, pos, "add_chips_w_out")
    p, q = _sum_small(pos, small, small_land, gws, gws_land)

    def plan_d(x, y, c, a, r, f):
        return _plan_chip_exchange(x, y, c, a, r) + _plan_pair_gather(x, y, c, f)

    ss, rs, bufs_d, tok = _start_copies(
        "start_chip_exchange_w_in", [a_in_bf, _landing((3, in_rows, W_IN_COLS), BF16), f_out], 3 + 1, plan_d)
    grad_x, gn8 = _in_bwd(dproj, x2d, dx2, norm_g, wing, tok, nt // 2, nt - nt // 2, gx_a, gn_a)
    a_in_bf, r2_in, f_out = _wait_copies("wait_chip_exchange_w_in", ss, rs, bufs_d, plan_d, grad_x)
    f_in = _add_chips(a_in, r2_in, pos, "add_chips_w_in")

    def plan_e(x, y, c, f, gn, gnl):
        return _plan_pair_gather(x, y, c, f) + _plan_all_gather(x, y, c, gn, gnl)

    ss, rs, bufs_e, tok = _start_copies(
        "start_pair_gather_w_in", [f_in, gn8, _landing((8,) + gn8.shape, F32)], 1 + 7, plan_e)
    g_w_out = f_out.reshape(W_OUT_ROWS, D_MODEL)
    d_w_out, nm_w_out, nv_w_out = _adamw_large(w_out[0], g_w_out, m_w_out[0], v_w_out[0], tok, "adamw_w_out")
    f_in, gn8, gn_land = _wait_copies("wait_pair_gather_w_in", ss, rs, bufs_e, plan_e, d_w_out)
    g_w_in = f_in.reshape(D_MODEL, W_IN_COLS)
    d_w_in, nm_w_in, nv_w_in = _adamw_large(w_in[0], g_w_in, m_w_in[0], v_w_in[0], tok, "adamw_w_in")

    flat = lambda a: a.reshape(1, C_BR)
    vectors = {
        "norm_g": (norm_g, m_norm_g, v_norm_g),
        "conv_b": (conv_b, m_conv_b, v_conv_b),
        "conv_ln_g": (conv_ln_g, m_conv_ln_g, v_conv_ln_g),
        "conv_ln_b": (conv_ln_b, m_conv_ln_b, v_conv_ln_b),
        "sgu_ln_g": (sgu_ln_g, m_sgu_ln_g, v_sgu_ln_g),
        "sgu_ln_b": (sgu_ln_b, m_sgu_ln_b, v_sgu_ln_b),
        "b_s": (flat(b_s), flat(m_b_s), flat(v_b_s)),
        "final_g": (flat(final_g), flat(m_final_g), flat(v_final_g)),
    }
    flat_ws = lambda a: a.reshape(HEADS * CHUNK, CHUNK)
    res = _adamw_small(p, q, gn8, gn_land, pos, vectors, (conv_w[0], m_conv_w[0], v_conv_w[0]),
                       (flat_ws(w_s), flat_ws(m_w_s), flat_ws(v_w_s)))
    res["w_in"] = tuple(a[None] for a in (g_w_in, d_w_in, nm_w_in, nv_w_in))
    res["w_out"] = tuple(a[None] for a in (g_w_out, d_w_out, nm_w_out, nv_w_out))
    res["conv_w"] = tuple(a[None] for a in res["conv_w"])
    res["w_s"] = tuple(a.reshape(w_s.shape) for a in res["w_s"])
    res["b_s"] = tuple(a.reshape(b_s.shape) for a in res["b_s"])
    res["final_g"] = tuple(a.reshape(final_g.shape) for a in res["final_g"])

    order = ("norm_g", "w_in", "conv_w", "conv_b", "conv_ln_g", "conv_ln_b", "sgu_ln_g", "sgu_ln_b",
             "w_s", "b_s", "w_out", "final_g")
    out = [loss, grad_x[None]]
    for part in range(4):
        out += [res[name][part] for name in order]
    return tuple(out)
```

```python
import functools

import jax
import jax.numpy as jnp
from jax import lax
from jax.experimental import pallas as pl
from jax.experimental.pallas import tpu as pltpu

F32 = jnp.float32
BF16 = jnp.bfloat16
MESH = pl.DeviceIdType.MESH

EPS = 1e-6
D_MODEL = 1024
C_BR = 1024
D_IN = 6 * C_BR
N_SHARD = 4
W_IN_COLS = D_IN // N_SHARD
W_OUT_ROWS = 2 * C_BR // N_SHARD
CONV_W_COLS = C_BR // N_SHARD
KW = 31
HALO = 16
HEADS = 8
HEAD_DIM = 128
CHUNK = 128
LANE = 128
SUB = 8

ADAM_LR = 0.001
ADAM_B1 = 0.9
ADAM_B2 = 0.999
ADAM_EPS = 1e-08
ADAM_WD = 0.01
ADAM_STEP = 10

TS_PROJ = 512
TS_FWD = 256
TS_OUT = 512
TS_BWD = 256
TS_INB = 512
TK_GW = 1024
ROWS_A = 128
ROWS_B = 64
VMEM_LIMIT = 56 * 1024 * 1024


def _cparams(sem=None, vmem=VMEM_LIMIT):
    kw = dict(vmem_limit_bytes=vmem)
    if sem is not None:
        kw["dimension_semantics"] = sem
    return pltpu.CompilerParams(**kw)


def _whole_vmem():
    return pl.BlockSpec(memory_space=pltpu.VMEM)


def _sigmoid(v):
    return 1.0 / (1.0 + jnp.exp(-v))


def _fold8(v):
    n, c = v.shape
    return v.reshape(n // SUB, SUB, c).sum(axis=0)


def _dot_nt(a, b):
    return lax.dot_general(a, b, (((1,), (1,)), ((), ())), preferred_element_type=F32)


def _dot_tn(a, b):
    return lax.dot_general(a, b, (((0,), (0,)), ((), ())), preferred_element_type=F32)


def _mesh_pos():
    return lax.axis_index("x"), lax.axis_index("y"), lax.axis_index("c")


def _rms_proj_gather(x, norm_g, w_in, w_out, conv_w, order):
    s = x.shape[0]
    ts = min(TS_PROJ, s)
    nt = s // ts
    hin = w_in.shape[0] // 2
    hout = w_out.shape[0] // 2

    def body(order_ref, x_ref, g_ref, win_ref, wout_ref, cw_ref,
             h_ref, proj_ref, wing_ref, woutg_ref, cwg_ref, wg_ref, wob_ref, send_sems, recv_sems, local_sems):
        p = pl.program_id(0)
        t = pl.program_id(1)
        mx, my, c = _mesh_pos()
        me = 2 * mx + my
        chips = [(1 - mx, my), (mx, 1 - my), (1 - mx, 1 - my)]
        sibling = (mx, my, 1 - c)

        def remote(src, dst, sem, dev):
            return pltpu.make_async_remote_copy(
                src_ref=src, dst_ref=dst, send_sem=send_sems.at[sem], recv_sem=recv_sems.at[sem],
                device_id=dev, device_id_type=MESH)

        def w_in_part(blk, half):
            return wg_ref.at[blk, pl.ds(half * hin, hin)]

        def keep(blk, k):
            return pltpu.make_async_copy(wg_ref.at[blk], wing_ref.at[blk], local_sems.at[2 + k])

        def w_out_part(blk, half):
            return woutg_ref.at[blk, pl.ds(half * hout, hout)]

        def sends():
            out = []
            for j, (cx, cy) in enumerate(chips):
                blk = 2 * cx + cy
                out.append(remote(w_in_part(me, c), w_in_part(me, c), j, (cx, cy, c)))
                out.append(remote(w_in_part(blk, c), w_in_part(blk, c), 3 + j, sibling))
                out.append(remote(wob_ref.at[pl.ds(c * hout, hout)], w_out_part(me, c), 6 + j, (cx, cy, c)))
                out.append(remote(w_out_part(blk, c), w_out_part(blk, c), 9 + j, sibling))
                out.append(remote(cw_ref, cwg_ref.at[me], 12 + j, (cx, cy, c)))
            return out

        @pl.when(jnp.logical_and(p == 0, t == 0))
        def _():
            wg_ref[me] = win_ref[...].astype(BF16)
            wob_ref[...] = wout_ref[...].astype(BF16)
            keep(me, 0).start()
            mine = [pltpu.make_async_copy(wob_ref, woutg_ref.at[me], local_sems.at[0]),
                    pltpu.make_async_copy(cw_ref, cwg_ref.at[me], local_sems.at[1])]
            for cp in mine:
                cp.start()
            for k, cp in enumerate(sends()):
                if k % 5 == 4 or (k % 5 == 0 and k // 5 < 2):
                    cp.start()
            for cp in mine:
                cp.wait()

        for j, (cx, cy) in enumerate(chips):
            blk = 2 * cx + cy

            @pl.when(jnp.logical_and(p == j + 1, t == 0))
            def _():
                remote(w_in_part(blk, c), w_in_part(blk, c), j, (cx, cy, c)).wait_recv()
                remote(w_in_part(blk, c), w_in_part(blk, c), 3 + j, sibling).start()
                remote(w_in_part(blk, 1 - c), w_in_part(blk, 1 - c), 3 + j, sibling).wait_recv()
                keep(blk, j + 1).start()
                if j == 0:
                    sends()[5 * 2].start()
                if j == 1:
                    for jj in range(3):
                        sends()[5 * jj + 2].start()

        xt = x_ref[...]
        r = lax.rsqrt(jnp.mean(xt * xt, axis=-1, keepdims=True) + EPS)
        hb = (xt * r * g_ref[...]).astype(BF16)

        @pl.when(p == 0)
        def _():
            h_ref[...] = hb

        proj_ref[...] = jnp.dot(hb, wg_ref[order_ref[p]], preferred_element_type=F32).astype(BF16)

        @pl.when(jnp.logical_and(p == N_SHARD - 1, t == nt - 1))
        def _():
            for j, (cx, cy) in enumerate(chips):
                blk = 2 * cx + cy
                remote(w_out_part(blk, c), w_out_part(blk, c), 6 + j, (cx, cy, c)).wait_recv()
                remote(w_out_part(blk, c), w_out_part(blk, c), 9 + j, sibling).start()
            for j, (cx, cy) in enumerate(chips):
                blk = 2 * cx + cy
                remote(w_out_part(blk, 1 - c), w_out_part(blk, 1 - c), 9 + j, sibling).wait_recv()
                remote(cw_ref, cwg_ref.at[blk], 12 + j, (cx, cy, c)).wait_recv()
            for cp in sends():
                cp.wait_send()
            keep(me, 0).wait()
            for j, (cx, cy) in enumerate(chips):
                keep(2 * cx + cy, j + 1).wait()

    hbm = pl.BlockSpec(memory_space=pl.ANY)
    return pl.pallas_call(
        body, name="rms_proj_gather",
        grid_spec=pltpu.PrefetchScalarGridSpec(
            num_scalar_prefetch=1, grid=(N_SHARD, nt),
            in_specs=[pl.BlockSpec((ts, D_MODEL), lambda p, t, o: (t, 0)),
                      pl.BlockSpec((1, D_MODEL), lambda p, t, o: (0, 0)),
                      _whole_vmem(), _whole_vmem(), _whole_vmem()],
            out_specs=(pl.BlockSpec((ts, D_MODEL), lambda p, t, o: (jnp.where(p == 0, t, nt - 1), 0)),
                       pl.BlockSpec((ts, W_IN_COLS), lambda p, t, o: (t, o[p])),
                       hbm, hbm, hbm),
            scratch_shapes=[pltpu.VMEM((N_SHARD,) + w_in.shape, BF16), pltpu.VMEM(w_out.shape, BF16),
                            pltpu.SemaphoreType.DMA((15,)), pltpu.SemaphoreType.DMA((15,)),
                            pltpu.SemaphoreType.DMA((6,))]),
        out_shape=(jax.ShapeDtypeStruct((s, D_MODEL), BF16), jax.ShapeDtypeStruct((s, D_IN), BF16),
                   jax.ShapeDtypeStruct((N_SHARD,) + w_in.shape, BF16),
                   jax.ShapeDtypeStruct((N_SHARD,) + w_out.shape, BF16),
                   jax.ShapeDtypeStruct((N_SHARD,) + conv_w.shape, F32)),
        compiler_params=_cparams(("arbitrary", "arbitrary")),
    )(order, x, norm_g, w_in, w_out, conv_w)


def _halo_specs(ts, s, width, col_block):
    per = ts // HALO
    last = s // HALO - 1
    prev = pl.BlockSpec((HALO, width), lambda i: (jnp.maximum(i * per - 1, 0), col_block))
    nxt = pl.BlockSpec((HALO, width), lambda i: (jnp.minimum((i + 1) * per, last), col_block))
    return prev, nxt


def _layer_norm_stats(v):
    mu = jnp.mean(v, axis=-1, keepdims=True)
    vc = v - mu
    var = jnp.mean(vc * vc, axis=-1, keepdims=True)
    rstd = lax.rsqrt(var + EPS)
    return vc * rstd, rstd


def _fill_shifted(sh_ref, ext):
    n = ext.shape[0]
    sh_ref[0] = ext
    for r in range(1, SUB):
        sh_ref[r] = pltpu.roll(ext, n - r, axis=0)


def _branch_fwd(proj, cwg, conv_b, conv_ln_g, conv_ln_b, sgu_ln_g, sgu_ln_b, w_s, bs_t):
    s = proj.shape[0]
    ts = min(TS_FWD, s)
    nt = s // ts
    ra = min(ROWS_A, ts)

    def body(pm_ref, pp_ref, pn_ref, cw_ref, cb_ref, clg_ref, clb_ref, slg_ref, slb_ref, ws_ref, bst_ref,
             y_ref, cv_ref, sh_ref):
        i = pl.program_id(0)
        keep_prev = (i > 0).astype(F32)
        keep_next = (i < nt - 1).astype(F32)

        for lb in range(C_BR // LANE):
            lanes = slice(lb * LANE, (lb + 1) * LANE)
            gates = slice(C_BR + lb * LANE, C_BR + (lb + 1) * LANE)

            def glu(ref):
                return ref[:, lanes].astype(F32) * _sigmoid(ref[:, gates].astype(F32))

            ext = jnp.concatenate([glu(pp_ref) * keep_prev, glu(pm_ref), glu(pn_ref) * keep_next], axis=0)
            _fill_shifted(sh_ref, ext)
            shard, off = divmod(lb * LANE, CONV_W_COLS)
            bias = cb_ref[:, lanes]

            def chunk(jc, carry):
                base = pl.multiple_of(jc * ra, ra)
                acc = jnp.zeros((ra, LANE), F32) + bias
                for k in range(KW):
                    o = k + 1
                    acc = acc + sh_ref[o % SUB, pl.ds(base + SUB * (o // SUB), ra), :] * cw_ref[
                        shard, k:k + 1, off:off + LANE]
                cv_ref[pl.ds(base, ra), lanes] = acc
                return carry

            lax.fori_loop(0, ts // ra, chunk, 0)

        lnh, _ = _layer_norm_stats(cv_ref[...])
        ln = lnh * clg_ref[...] + clb_ref[...]
        gc = pm_ref[:, 2 * C_BR:3 * C_BR].astype(F32)
        y_ref[:, :C_BR] = (ln * _sigmoid(ln) * gc * _sigmoid(gc)).astype(BF16)

        vh, _ = _layer_norm_stats(pm_ref[:, 4 * C_BR:5 * C_BR].astype(F32))
        vn = (vh * slg_ref[...] + slb_ref[...]).astype(BF16)
        for hd in range(HEADS):
            w_h = ws_ref[hd].astype(BF16)
            b_h = bst_ref[:, hd:hd + 1]
            cols = slice(hd * HEAD_DIM, (hd + 1) * HEAD_DIM)
            for ch in range(ts // CHUNK):
                rows = slice(ch * CHUNK, (ch + 1) * CHUNK)
                mixed = jnp.dot(w_h, vn[rows, cols], preferred_element_type=F32) + b_h
                u = pm_ref[rows, 3 * C_BR + hd * HEAD_DIM:3 * C_BR + (hd + 1) * HEAD_DIM].astype(F32)
                gs = pm_ref[rows, 5 * C_BR + hd * HEAD_DIM:5 * C_BR + (hd + 1) * HEAD_DIM].astype(F32)
                y_ref[rows, C_BR + hd * HEAD_DIM:C_BR + (hd + 1) * HEAD_DIM] = (
                    u * mixed * gs * _sigmoid(gs)).astype(BF16)

    prev, nxt = _halo_specs(ts, s, 2 * C_BR, 0)
    row = pl.BlockSpec((1, C_BR), lambda i: (0, 0))
    return pl.pallas_call(
        body, name="branch_fwd",
        grid=(nt,),
        out_shape=(jax.ShapeDtypeStruct((s, 2 * C_BR), BF16), jax.ShapeDtypeStruct((s, C_BR), F32)),
        in_specs=[pl.BlockSpec((ts, D_IN), lambda i: (i, 0)), prev, nxt,
                  _whole_vmem(), row, row, row, row, row, _whole_vmem(), _whole_vmem()],
        out_specs=(pl.BlockSpec((ts, 2 * C_BR), lambda i: (i, 0)),
                   pl.BlockSpec((ts, C_BR), lambda i: (i, 0))),
        scratch_shapes=[pltpu.VMEM((SUB, ts + 2 * HALO, LANE), F32)],
        compiler_params=_cparams(("parallel",)),
    )(proj, proj, proj, cwg, conv_b, conv_ln_g, conv_ln_b, sgu_ln_g, sgu_ln_b, w_s, bs_t)


def _out_proj(y, x, target, woutg, final_g):
    s = x.shape[0]
    ts = min(TS_OUT, s)
    nt = s // ts

    def body(y_ref, x_ref, t_ref, w_ref, g_ref, dx2_ref, dy_ref, gw_ref, gf_ref, se_ref):
        i = pl.program_id(0)

        @pl.when(i == 0)
        def _():
            gw_ref[...] = jnp.zeros_like(gw_ref)
            gf_ref[...] = jnp.zeros_like(gf_ref)
            se_ref[...] = jnp.zeros_like(se_ref)

        yt = y_ref[...]
        x2 = x_ref[...]
        for k in range(N_SHARD):
            x2 = x2 + jnp.dot(yt[:, k * W_OUT_ROWS:(k + 1) * W_OUT_ROWS], w_ref[k], preferred_element_type=F32)
        r2 = lax.rsqrt(jnp.mean(x2 * x2, axis=-1, keepdims=True) + EPS)
        n2 = x2 * r2
        g = g_ref[...]
        diff = n2 * g - t_ref[...]
        se_ref[...] += _fold8(diff * diff)
        dout = diff * (1.0 / D_MODEL)
        gf_ref[...] += _fold8(dout * n2)
        dn = dout * g
        dx2 = r2 * (dn - n2 * jnp.mean(dn * n2, axis=-1, keepdims=True))
        dx2_ref[...] = dx2
        dxb = dx2.astype(BF16)
        for k in range(N_SHARD):
            rows = slice(k * W_OUT_ROWS, (k + 1) * W_OUT_ROWS)
            dy_ref[:, rows] = _dot_nt(dxb, w_ref[k]).astype(BF16)
            gw_ref[rows, :] += _dot_tn(yt[:, rows], dxb)

        @pl.when(i == nt - 1)
        def _():
            gf_ref[...] = jnp.broadcast_to(jnp.sum(gf_ref[...], axis=0, keepdims=True), gf_ref.shape)

    tile = pl.BlockSpec((ts, D_MODEL), lambda i: (i, 0))
    wide = pl.BlockSpec((ts, 2 * C_BR), lambda i: (i, 0))
    acc8 = pl.BlockSpec((SUB, D_MODEL), lambda i: (0, 0))
    return pl.pallas_call(
        body, name="out_proj",
        grid=(nt,),
        out_shape=(jax.ShapeDtypeStruct((s, D_MODEL), F32), jax.ShapeDtypeStruct((s, 2 * C_BR), BF16),
                   jax.ShapeDtypeStruct((2 * C_BR, D_MODEL), F32),
                   jax.ShapeDtypeStruct((SUB, D_MODEL), F32), jax.ShapeDtypeStruct((SUB, D_MODEL), F32)),
        in_specs=[wide, tile, tile, _whole_vmem(), pl.BlockSpec((1, D_MODEL), lambda i: (0, 0))],
        out_specs=(tile, wide, pl.BlockSpec((2 * C_BR, D_MODEL), lambda i: (0, 0)), acc8, acc8),
        compiler_params=_cparams(("arbitrary",)),
    )(y, x, target, woutg, final_g)


def _dsilu(v, sg):
    return sg * (1.0 + v * (1.0 - sg))


def _branch_bwd(proj, dy, cv, cwg, conv_ln_g, conv_ln_b, sgu_ln_g, sgu_ln_b, w_s, ws_t, bs_t, gf8, se8, dep):
    s = proj.shape[0]
    ts = min(TS_BWD, s)
    nt = s // ts
    ra = min(ROWS_A, ts)
    rb = min(ROWS_B, ts)
    te = ts + 2 * HALO

    def body(pm_ref, dym_ref, cvm_ref, gcp_ref, gcn_ref, dyp_ref, dyn_ref, cvp_ref, cvn_ref,
             cw_ref, clg_ref, clb_ref, slg_ref, slb_ref, ws_ref, wst_ref, bst_ref, gf_ref, se_ref, dep_ref,
             dp_ref, small_ref, gws_ref,
             sh_ref, glu_ref, dgl_ref, dcv_ref, acc_ref, gcw_ref, gbs_ref):
        i = pl.program_id(0)

        @pl.when(i == 0)
        def _():
            acc_ref[...] = jnp.zeros_like(acc_ref)
            gcw_ref[...] = jnp.zeros_like(gcw_ref)
            gbs_ref[...] = jnp.zeros_like(gbs_ref)
            gws_ref[...] = jnp.zeros_like(gws_ref)

        def ext(prev_ref, main, next_ref):
            return jnp.concatenate([prev_ref[...].astype(F32), main, next_ref[...].astype(F32)], axis=0)

        main = slice(HALO, HALO + ts)

        cv_e = ext(cvp_ref, cvm_ref[...], cvn_ref)
        gc_e = ext(gcp_ref, pm_ref[:, 2 * C_BR:3 * C_BR].astype(F32), gcn_ref)
        dyc_e = ext(dyp_ref, dym_ref[:, :C_BR].astype(F32), dyn_ref)
        lnh, rstd = _layer_norm_stats(cv_e)
        clg = clg_ref[...]
        ln = lnh * clg + clb_ref[...]
        sg_ln = _sigmoid(ln)
        sg_gc = _sigmoid(gc_e)
        d_ln = dyc_e * gc_e * sg_gc * _dsilu(ln, sg_ln)
        dp_ref[:, 2 * C_BR:3 * C_BR] = (
            dyc_e[main] * ln[main] * sg_ln[main] * _dsilu(gc_e[main], sg_gc[main])).astype(BF16)
        dlnh = d_ln * clg
        d_cv = rstd * (dlnh - jnp.mean(dlnh, axis=-1, keepdims=True)
                       - lnh * jnp.mean(dlnh * lnh, axis=-1, keepdims=True))
        row = lax.broadcasted_iota(jnp.int32, (te, 1), 0)
        valid = jnp.logical_and(jnp.logical_or(row >= HALO, i > 0),
                                jnp.logical_or(row < HALO + ts, i < nt - 1))
        d_cv = jnp.where(valid, d_cv, 0.0)
        dcv_ref[...] = d_cv
        acc_ref[0:8, :] += _fold8(d_cv[main])
        acc_ref[8:16, :] += _fold8(d_ln[main] * lnh[main])
        acc_ref[16:24, :] += _fold8(d_ln[main])

        for lb in range(C_BR // LANE):
            lanes = slice(lb * LANE, (lb + 1) * LANE)
            gates = slice(C_BR + lb * LANE, C_BR + (lb + 1) * LANE)
            shard, off = divmod(lb * LANE, CONV_W_COLS)
            av = pm_ref[:, lanes].astype(F32)
            sg = _sigmoid(pm_ref[:, gates].astype(F32))
            glu_ref[...] = av * sg
            _fill_shifted(sh_ref, dcv_ref[:, lanes])

            def chunk_a(jc, carry):
                base = pl.multiple_of(jc * ra, ra)
                acc = jnp.zeros((ra, LANE), F32)
                for j in range(KW):
                    o = j + 1
                    acc = acc + sh_ref[o % SUB, pl.ds(base + SUB * (o // SUB), ra), :] * cw_ref[
                        shard, KW - 1 - j:KW - j, off:off + LANE]
                dgl_ref[pl.ds(base, ra), :] = acc
                return carry

            lax.fori_loop(0, ts // ra, chunk_a, 0)

            def chunk_b(jc, accs):
                base = pl.multiple_of(jc * rb, rb)
                g = glu_ref[pl.ds(base, rb), :]
                out = []
                for j in range(KW):
                    o = j + 1
                    d = sh_ref[o % SUB, pl.ds(base + SUB * (o // SUB), rb), :]
                    out.append(accs[j] + _fold8(g * d))
                return tuple(out)

            accs = lax.fori_loop(0, ts // rb, chunk_b, tuple(jnp.zeros((SUB, LANE), F32) for _ in range(KW)))
            for j in range(KW):
                gcw_ref[j * SUB:(j + 1) * SUB, lanes] += accs[j]

            dglu = dgl_ref[...]
            dp_ref[:, lanes] = (dglu * sg).astype(BF16)
            dp_ref[:, gates] = (dglu * av * sg * (1.0 - sg)).astype(BF16)

        vh, vrstd = _layer_norm_stats(pm_ref[:, 4 * C_BR:5 * C_BR].astype(F32))
        slg = slg_ref[...]
        vn = (vh * slg + slb_ref[...]).astype(BF16)
        for hd in range(HEADS):
            w_h = ws_ref[hd].astype(BF16)
            wt_h = wst_ref[hd].astype(BF16)
            b_h = bst_ref[:, hd:hd + 1]
            cols = slice(hd * HEAD_DIM, (hd + 1) * HEAD_DIM)
            gws_h = jnp.zeros((CHUNK, CHUNK), F32)
            gbs_h = jnp.zeros((CHUNK, HEAD_DIM), F32)
            for ch in range(ts // CHUNK):
                rows = slice(ch * CHUNK, (ch + 1) * CHUNK)
                vn_b = vn[rows, cols]
                mixed = jnp.dot(w_h, vn_b, preferred_element_type=F32) + b_h
                u = pm_ref[rows, 3 * C_BR + hd * HEAD_DIM:3 * C_BR + (hd + 1) * HEAD_DIM].astype(F32)
                gs = pm_ref[rows, 5 * C_BR + hd * HEAD_DIM:5 * C_BR + (hd + 1) * HEAD_DIM].astype(F32)
                dys = dym_ref[rows, C_BR + hd * HEAD_DIM:C_BR + (hd + 1) * HEAD_DIM].astype(F32)
                sg_gs = _sigmoid(gs)
                silu_gs = gs * sg_gs
                dp_ref[rows, 3 * C_BR + hd * HEAD_DIM:3 * C_BR + (hd + 1) * HEAD_DIM] = (
                    dys * mixed * silu_gs).astype(BF16)
                dp_ref[rows, 5 * C_BR + hd * HEAD_DIM:5 * C_BR + (hd + 1) * HEAD_DIM] = (
                    dys * u * mixed * _dsilu(gs, sg_gs)).astype(BF16)
                d_mixed = dys * u * silu_gs
                dm_b = d_mixed.astype(BF16)
                gws_h = gws_h + _dot_nt(dm_b, vn_b)
                gbs_h = gbs_h + d_mixed
                dcv_ref[HALO + ch * CHUNK:HALO + (ch + 1) * CHUNK, cols] = jnp.dot(
                    wt_h, dm_b, preferred_element_type=F32)
            gws_ref[hd] += gws_h
            gbs_ref[:, cols] += gbs_h
        d_vn = dcv_ref[main, :]
        acc_ref[24:32, :] += _fold8(d_vn * vh)
        acc_ref[32:40, :] += _fold8(d_vn)
        dvh = d_vn * slg
        dp_ref[:, 4 * C_BR:5 * C_BR] = (vrstd * (
            dvh - jnp.mean(dvh, axis=-1, keepdims=True)
            - vh * jnp.mean(dvh * vh, axis=-1, keepdims=True))).astype(BF16)

        @pl.when(i == nt - 1)
        def _():
            small_ref[...] = jnp.zeros_like(small_ref)
            for a in range(5):
                small_ref[1 + a:2 + a, :] = jnp.sum(acc_ref[a * SUB:(a + 1) * SUB, :], axis=0, keepdims=True)
            ones = jnp.ones((SUB, HEAD_DIM), F32)
            for hd in range(HEADS):
                cols = slice(hd * HEAD_DIM, (hd + 1) * HEAD_DIM)
                rowsum = lax.dot_general(ones, gbs_ref[:, cols], (((1,), (1,)), ((), ())),
                                         precision=lax.Precision.HIGHEST, preferred_element_type=F32)
                small_ref[6:7, cols] = rowsum[0:1, :]
            small_ref[7:8, :] = gf_ref[0:1, :]
            small_ref[0:1, :] = jnp.sum(se_ref[...], axis=0, keepdims=True)
            for k in range(KW):
                j = KW - 1 - k
                small_ref[8 + k:9 + k, :] = jnp.sum(gcw_ref[j * SUB:(j + 1) * SUB, :], axis=0, keepdims=True)

    gc_prev, gc_next = _halo_specs(ts, s, C_BR, 2)
    lo_prev, lo_next = _halo_specs(ts, s, C_BR, 0)
    row = pl.BlockSpec((1, C_BR), lambda i: (0, 0))
    return pl.pallas_call(
        body, name="branch_bwd",
        grid=(nt,),
        out_shape=(jax.ShapeDtypeStruct((s, D_IN), BF16), jax.ShapeDtypeStruct((40, C_BR), F32),
                   jax.ShapeDtypeStruct((HEADS, CHUNK, CHUNK), F32)),
        in_specs=[pl.BlockSpec((ts, D_IN), lambda i: (i, 0)),
                  pl.BlockSpec((ts, 2 * C_BR), lambda i: (i, 0)),
                  pl.BlockSpec((ts, C_BR), lambda i: (i, 0)),
                  gc_prev, gc_next, lo_prev, lo_next, lo_prev, lo_next,
                  _whole_vmem(), row, row, row, row, _whole_vmem(), _whole_vmem(), _whole_vmem(),
                  _whole_vmem(), _whole_vmem(), _whole_vmem()],
        out_specs=(pl.BlockSpec((ts, D_IN), lambda i: (i, 0)),
                   pl.BlockSpec((40, C_BR), lambda i: (0, 0)),
                   pl.BlockSpec((HEADS, CHUNK, CHUNK), lambda i: (0, 0, 0))),
        scratch_shapes=[pltpu.VMEM((SUB, te, LANE), F32),
                        pltpu.VMEM((ts, LANE), F32),
                        pltpu.VMEM((ts, LANE), F32),
                        pltpu.VMEM((te, C_BR), F32),
                        pltpu.VMEM((5 * SUB, C_BR), F32),
                        pltpu.VMEM((KW * SUB, C_BR), F32),
                        pltpu.VMEM((CHUNK, C_BR), F32)],
        compiler_params=_cparams(("arbitrary",)),
    )(proj, dy, cv, proj, proj, dy, dy, cv, cv,
      cwg, conv_ln_g, conv_ln_b, sgu_ln_g, sgu_ln_b, w_s, ws_t, bs_t, gf8, se8, dep)


def _in_bwd(dproj, x, dx2, norm_g, wing, dep, first_tile, n_tiles, gx_prev=None, gn_prev=None):
    s = x.shape[0]
    ts = min(TS_INB, s)

    def body(*refs):
        dp_ref, x_ref, dx2_ref, g_ref, w_ref = refs[:5]
        gx_ref, gn_ref, acc_ref = refs[-3:]
        i = pl.program_id(0)

        @pl.when(i == 0)
        def _():
            acc_ref[...] = jnp.zeros_like(acc_ref)

        dh = jnp.zeros((ts, D_MODEL), F32)
        for k in range(N_SHARD):
            dh = dh + _dot_nt(dp_ref[:, k * W_IN_COLS:(k + 1) * W_IN_COLS], w_ref[k])
        xt = x_ref[...]
        r = lax.rsqrt(jnp.mean(xt * xt, axis=-1, keepdims=True) + EPS)
        n = xt * r
        acc_ref[...] += _fold8(dh * n)
        dn = dh * g_ref[...]
        gx_ref[...] = dx2_ref[...] + r * (dn - n * jnp.mean(dn * n, axis=-1, keepdims=True))

        @pl.when(i == n_tiles - 1)
        def _():
            total = jnp.broadcast_to(jnp.sum(acc_ref[...], axis=0, keepdims=True), gn_ref.shape)
            if gn_prev is not None:
                total = total + refs[7][...]
            gn_ref[...] = total

    tile = pl.BlockSpec((ts, D_MODEL), lambda i: (i + first_tile, 0))
    in_specs = [pl.BlockSpec((ts, D_IN), lambda i: (i + first_tile, 0)), tile, tile,
                pl.BlockSpec((1, D_MODEL), lambda i: (0, 0)), _whole_vmem(), _whole_vmem()]
    operands = [dproj, x, dx2, norm_g, wing, dep]
    aliases = {}
    if gx_prev is not None:
        in_specs += [pl.BlockSpec(memory_space=pl.ANY), _whole_vmem()]
        operands += [gx_prev, gn_prev]
        aliases = {6: 0}
    return pl.pallas_call(
        body, name="in_bwd_%d" % first_tile,
        grid=(n_tiles,),
        out_shape=(jax.ShapeDtypeStruct((s, D_MODEL), F32), jax.ShapeDtypeStruct((SUB, D_MODEL), F32)),
        in_specs=in_specs,
        out_specs=(tile, pl.BlockSpec((SUB, D_MODEL), lambda i: (0, 0))),
        scratch_shapes=[pltpu.VMEM((SUB, D_MODEL), F32)],
        input_output_aliases=aliases,
        compiler_params=_cparams(("arbitrary",)),
    )(*operands)


def _grad_w_in(h, dproj, dep):
    s = h.shape[0]
    tk = min(TK_GW, s)
    half = D_MODEL // 2

    def body(h_ref, dp_ref, dep_ref, o_ref):
        @pl.when(pl.program_id(1) == 0)
        def _():
            o_ref[...] = jnp.zeros_like(o_ref)

        o_ref[0] += _dot_tn(h_ref[...], dp_ref[...]).reshape(2, half, W_IN_COLS)

    return pl.pallas_call(
        body, name="grad_w_in",
        grid=(N_SHARD, s // tk),
        out_shape=jax.ShapeDtypeStruct((N_SHARD, 2, half, W_IN_COLS), F32),
        in_specs=[pl.BlockSpec((tk, D_MODEL), lambda k, t: (t, 0)),
                  pl.BlockSpec((tk, W_IN_COLS), lambda k, t: (t, k)), _whole_vmem()],
        out_specs=pl.BlockSpec((1, 2, half, W_IN_COLS), lambda k, t: (k, 0, 0, 0)),
        compiler_params=_cparams(("parallel", "arbitrary")),
    )(h, dproj, dep)


HBM_SPEC = pl.BlockSpec(memory_space=pltpu.HBM)
SEM_SPEC = pl.BlockSpec(memory_space=pltpu.SEMAPHORE)
SIDE_EFFECT = pltpu.SideEffectType.DATAFLOW_SIDE_EFFECTING


def _remote_copies(plan, bufs, send_sems, recv_sems):
    x, y, c = _mesh_pos()
    return [pltpu.make_async_remote_copy(src_ref=src, dst_ref=dst, send_sem=send_sems.at[k],
                                         recv_sem=recv_sems.at[k], device_id=dev, device_id_type=MESH)
            for k, (src, dst, dev) in enumerate(plan(x, y, c, *bufs))]


def _start_copies(name, bufs, n_copies, plan):
    n = len(bufs)

    def body(*refs):
        for cp in _remote_copies(plan, refs[:n], refs[n], refs[n + 1]):
            cp.start()
        refs[-1][...] = jnp.zeros_like(refs[-1])

    outs = pl.pallas_call(
        body, name=name,
        out_shape=(pltpu.SemaphoreType.DMA((n_copies,)), pltpu.SemaphoreType.DMA((n_copies,)),
                   *[pltpu.HBM(b.shape, b.dtype) for b in bufs], jax.ShapeDtypeStruct((SUB, LANE), F32)),
        in_specs=[HBM_SPEC] * n,
        out_specs=(SEM_SPEC, SEM_SPEC, *[HBM_SPEC] * n, _whole_vmem()),
        input_output_aliases={i: 2 + i for i in range(n)},
        compiler_params=pltpu.CompilerParams(has_side_effects=SIDE_EFFECT),
    )(*[pltpu.with_memory_space_constraint(b, pltpu.HBM) for b in bufs])
    return outs[0], outs[1], list(outs[2:2 + n]), outs[-1]


def _wait_copies(name, send_sems, recv_sems, bufs, plan, after):
    n = len(bufs)

    def body(*refs):
        for cp in _remote_copies(plan, refs[:n], refs[n], refs[n + 1]):
            cp.wait_send()
            cp.wait_recv()

    outs = pl.pallas_call(
        body, name=name,
        out_shape=tuple(pltpu.HBM(b.shape, b.dtype) for b in bufs),
        in_specs=[HBM_SPEC] * n + [SEM_SPEC, SEM_SPEC, pl.BlockSpec(memory_space=pl.ANY)],
        out_specs=(HBM_SPEC,) * n,
        input_output_aliases={i: i for i in range(n)},
        compiler_params=pltpu.CompilerParams(has_side_effects=SIDE_EFFECT),
    )(*bufs, send_sems, recv_sems, after)
    return list(outs)


def _landing(shape, dtype):
    return lax.empty(shape, dtype)


def _plan_pair_exchange(x, y, c, g, r):
    return [(g.at[k, 1 - c], r.at[k], (x, y, 1 - c)) for k in range(N_SHARD)]


def _plan_chip_exchange(x, y, c, a, r):
    chips = [(1 - x, y), (x, 1 - y), (1 - x, 1 - y)]
    return [(a.at[2 * cx + cy], r.at[j], (cx, cy, c)) for j, (cx, cy) in enumerate(chips)]


def _plan_pair_gather(x, y, c, f):
    return [(f.at[c], f.at[c], (x, y, 1 - c))]


def _plan_all_gather(x, y, c, own, land):
    me = 4 * x + 2 * y + c
    flip = lambda v, bit: 1 - v if bit else v
    return [(own, land.at[me], (flip(x, m >> 2 & 1), flip(y, m >> 1 & 1), flip(c, m & 1))) for m in range(1, 8)]


def _add_pair(g, r, pos, name):
    _, _, rows, cols = g.shape
    tr = min(256, rows)

    def body(pos_ref, g_ref, r_ref, o_ref, ob_ref):
        v = g_ref[0] + r_ref[...]
        o_ref[...] = v
        ob_ref[...] = v.astype(BF16)

    blk = pl.BlockSpec((1, tr, cols), lambda k, t, pos_ref: (k, t, 0))
    return pl.pallas_call(
        body, name=name,
        grid_spec=pltpu.PrefetchScalarGridSpec(
            num_scalar_prefetch=1, grid=(N_SHARD, rows // tr),
            in_specs=[pl.BlockSpec((1, 1, tr, cols), lambda k, t, pos_ref: (k, pos_ref[1], t, 0)), blk],
            out_specs=(blk, blk)),
        out_shape=(jax.ShapeDtypeStruct((N_SHARD, rows, cols), F32), jax.ShapeDtypeStruct((N_SHARD, rows, cols), BF16)),
        compiler_params=_cparams(("parallel", "parallel")),
    )(pos, g, r)


def _add_chips(a, r, pos, name):
    _, rows, cols = a.shape
    tr = min(256, rows)

    def body(pos_ref, a_ref, r_ref, o_ref):
        o_ref[0] = ((a_ref[0] + r_ref[0].astype(F32)) + r_ref[1].astype(F32)) + r_ref[2].astype(F32)

    return pl.pallas_call(
        body, name=name,
        grid_spec=pltpu.PrefetchScalarGridSpec(
            num_scalar_prefetch=1, grid=(rows // tr,),
            in_specs=[pl.BlockSpec((1, tr, cols), lambda t, pos_ref: (pos_ref[0], t, 0)),
                      pl.BlockSpec((3, tr, cols), lambda t, pos_ref: (0, t, 0))],
            out_specs=pl.BlockSpec((1, tr, cols), lambda t, pos_ref: (pos_ref[1], t, 0))),
        out_shape=jax.ShapeDtypeStruct((2, rows, cols), F32),
        compiler_params=_cparams(("parallel",)),
    )(pos, a, r)


def _sum_slots(pos_ref, own, land_ref, rows):
    me = pos_ref[2]
    total = None
    for d in range(8):
        term = jnp.where(me == d, own, land_ref[d] if rows is None else land_ref[d, rows, :])
        total = term if total is None else total + term
    return total


def _sum_small(pos, small, small_land, gws, gws_land):
    def body(pos_ref, sm_ref, sml_ref, gw_ref, gwl_ref, o_sm, o_gw):
        o_sm[...] = _sum_slots(pos_ref, sm_ref[...], sml_ref, None)
        o_gw[...] = _sum_slots(pos_ref, gw_ref[...], gwl_ref, None)

    return pl.pallas_call(
        body, name="sum_small",
        grid_spec=pltpu.PrefetchScalarGridSpec(
            num_scalar_prefetch=1, grid=(1,),
            in_specs=[_whole_vmem()] * 4, out_specs=[_whole_vmem()] * 2),
        out_shape=[jax.ShapeDtypeStruct(small.shape, F32), jax.ShapeDtypeStruct(gws.shape, F32)],
        compiler_params=_cparams(("arbitrary",)),
    )(pos, small, small_land, gws, gws_land)


def _adamw_math(w, g, m, v):
    m = ADAM_B1 * m + (1.0 - ADAM_B1) * g
    v = ADAM_B2 * v + (1.0 - ADAM_B2) * (g * g)
    m_hat = m / (1.0 - ADAM_B1 ** ADAM_STEP)
    v_hat = v / (1.0 - ADAM_B2 ** ADAM_STEP)
    delta = -ADAM_LR * (m_hat / (jnp.sqrt(v_hat) + ADAM_EPS) + ADAM_WD * w)
    return delta, m, v


def _adamw_large(w, g, m, v, dep, name):
    rows, cols = w.shape
    tr = min(256, rows)

    def body(w_ref, g_ref, m_ref, v_ref, dep_ref, d_ref, nm_ref, nv_ref):
        d_ref[...], nm_ref[...], nv_ref[...] = _adamw_math(w_ref[...], g_ref[...], m_ref[...], v_ref[...])

    tile = pl.BlockSpec((tr, cols), lambda t: (t, 0))
    return pl.pallas_call(
        body, name=name,
        grid=(rows // tr,),
        out_shape=(jax.ShapeDtypeStruct(w.shape, F32),) * 3,
        in_specs=[tile] * 4 + [_whole_vmem()], out_specs=(tile,) * 3,
        compiler_params=_cparams(("parallel",)),
    )(w, g, m, v, dep)


_ROW_OF = {"conv_b": 1, "conv_ln_g": 2, "conv_ln_b": 3, "sgu_ln_g": 4, "sgu_ln_b": 5, "b_s": 6, "final_g": 7}
_CONV_W_ROW = 8
_VECTORS = ("norm_g", "conv_b", "conv_ln_g", "conv_ln_b", "sgu_ln_g", "sgu_ln_b", "b_s", "final_g")


def _adamw_small(p, q, gn8, gn_land, pos, vectors, conv_w, w_s):
    names = list(_VECTORS)

    def body(pos_ref, p_ref, q_ref, gn_ref, gnl_ref, *refs):
        n_in = 3 * (len(names) + 2)
        ins, outs = refs[:n_in], refs[n_in:]
        me = pos_ref[0]
        for a, name in enumerate(names + ["conv_w", "w_s"]):
            w_ref, m_ref, v_ref = ins[3 * a:3 * a + 3]
            if name == "conv_w":
                g = jnp.zeros((KW, CONV_W_COLS), F32)
                for k in range(N_SHARD):
                    blk = p_ref[_CONV_W_ROW:_CONV_W_ROW + KW, k * CONV_W_COLS:(k + 1) * CONV_W_COLS]
                    g = jnp.where(me == k, blk, g)
            elif name == "w_s":
                g = q_ref[...]
            elif name == "norm_g":
                g = _sum_slots(pos_ref, gn_ref[0:1, :], gnl_ref, slice(0, 1))
            else:
                g = p_ref[_ROW_OF[name]:_ROW_OF[name] + 1, :]
            delta, nm, nv = _adamw_math(w_ref[...], g, m_ref[...], v_ref[...])
            for o_ref, val in zip(outs[4 * a:4 * a + 4], (g, delta, nm, nv)):
                o_ref[...] = val

    operands, shapes = [], []
    for name in names:
        operands += list(vectors[name])
        shapes += [jax.ShapeDtypeStruct((1, C_BR), F32)] * 4
    operands += list(conv_w)
    shapes += [jax.ShapeDtypeStruct((KW, CONV_W_COLS), F32)] * 4
    operands += list(w_s)
    shapes += [jax.ShapeDtypeStruct(q.shape, F32)] * 4
    outs = pl.pallas_call(
        body, name="adamw_small",
        grid_spec=pltpu.PrefetchScalarGridSpec(
            num_scalar_prefetch=1, grid=(1,),
            in_specs=[_whole_vmem()] * (4 + len(operands)),
            out_specs=[_whole_vmem()] * len(shapes)),
        out_shape=shapes,
        compiler_params=_cparams(("arbitrary",)),
    )(pos, p, q, gn8, gn_land, *operands)
    return {name: tuple(outs[4 * a:4 * a + 4]) for a, name in enumerate(names + ["conv_w", "w_s"])}


def kernel(x, norm_g, w_in, conv_w, conv_b, conv_ln_g, conv_ln_b, sgu_ln_g, sgu_ln_b, w_s, b_s, w_out, final_g, loss_target, m_norm_g, m_w_in, m_conv_w, m_conv_b, m_conv_ln_g, m_conv_ln_b, m_sgu_ln_g, m_sgu_ln_b, m_w_s, m_b_s, m_w_out, m_final_g, v_norm_g, v_w_in, v_conv_w, v_conv_b, v_conv_ln_g, v_conv_ln_b, v_sgu_ln_g, v_sgu_ln_b, v_w_s, v_b_s, v_w_out, v_final_g):
    xi, yi, ci = _mesh_pos()
    pos = jnp.stack([2 * xi + yi, ci, 4 * xi + 2 * yi + ci]).astype(jnp.int32)

    x2d = x[0]
    tgt = loss_target[0]
    fg = final_g.reshape(1, D_MODEL)
    ws3 = w_s[0]
    ws_t = jnp.swapaxes(ws3, 1, 2)
    bs_t = jnp.transpose(b_s[0])

    chip = 2 * xi + yi
    order = jnp.stack([chip, 2 * (1 - xi) + yi, 2 * xi + 1 - yi, 2 * (1 - xi) + 1 - yi]).astype(jnp.int32)
    h, proj, wing, woutg, cwg = _rms_proj_gather(x2d, norm_g, w_in[0], w_out[0], conv_w[0], order)
    y, cv = _branch_fwd(proj, cwg, conv_b, conv_ln_g, conv_ln_b, sgu_ln_g, sgu_ln_b, ws3, bs_t)
    dx2, dy, gwout, gf8, se8 = _out_proj(y, x2d, tgt, woutg, fg)
    in_rows, out_rows = D_MODEL // 2, W_OUT_ROWS // 2
    gwout4 = gwout.reshape(N_SHARD, 2, out_rows, D_MODEL)
    ss, rs, (gwout4, r1_out), tok = _start_copies(
        "start_pair_exchange_w_out", [gwout4, _landing((N_SHARD, out_rows, D_MODEL), F32)], N_SHARD,
        _plan_pair_exchange)
    dproj, small, gws3 = _branch_bwd(proj, dy, cv, cwg, conv_ln_g, conv_ln_b, sgu_ln_g, sgu_ln_b, ws3, ws_t, bs_t,
                                     gf8, se8, tok)
    gws = gws3.reshape(HEADS * CHUNK, CHUNK)
    gwout4, r1_out = _wait_copies("wait_pair_exchange_w_out", ss, rs, [gwout4, r1_out], _plan_pair_exchange, dproj)
    a_out, a_out_bf = _add_pair(gwout4, r1_out, pos, "add_pair_w_out")

    def plan_b(x, y, c, a, r, sm, sml, gw, gwl):
        return (_plan_chip_exchange(x, y, c, a, r) + _plan_all_gather(x, y, c, sm, sml)
                + _plan_all_gather(x, y, c, gw, gwl))

    ss, rs, bufs_b, tok = _start_copies(
        "start_chip_exchange_w_out",
        [a_out_bf, _landing((3, out_rows, D_MODEL), BF16), small, _landing((8,) + small.shape, F32),
         gws, _landing((8,) + gws.shape, F32)], 3 + 7 + 7, plan_b)
    gwin = _grad_w_in(h, dproj, tok)
    ss_c, rs_c, (gwin, r1_in), tok = _start_copies(
        "start_pair_exchange_w_in", [gwin, _landing((N_SHARD, in_rows, W_IN_COLS), F32)], N_SHARD,
        _plan_pair_exchange)
    nt = x2d.shape[0] // min(TS_INB, x2d.shape[0])
    gx_a, gn_a = _in_bwd(dproj, x2d, dx2, norm_g, wing, tok, 0, nt // 2)

    gwin, r1_in = _wait_copies("wait_pair_exchange_w_in", ss_c, rs_c, [gwin, r1_in], _plan_pair_exchange, gx_a)
    a_in, a_in_bf = _add_pair(gwin, r1_in, pos, "add_pair_w_in")
    a_out_bf, r2_out, small, small_land, gws, gws_land = _wait_copies(
        "wait_chip_exchange_w_out", ss, rs, bufs_b, plan_b, gx_a)
    f_out = _add_chips(a_out, r2_out, pos, "add_chips_w_out")
    p, q = _sum_small(pos, small, small_land, gws, gws_land)
    loss = (0.5 / D_MODEL) * jnp.sum(p[0])

    def plan_d(x, y, c, a, r, f):
        return _plan_chip_exchange(x, y, c, a, r) + _plan_pair_gather(x, y, c, f)

    ss, rs, bufs_d, tok = _start_copies(
        "start_chip_exchange_w_in", [a_in_bf, _landing((3, in_rows, W_IN_COLS), BF16), f_out], 3 + 1, plan_d)
    grad_x, gn8 = _in_bwd(dproj, x2d, dx2, norm_g, wing, tok, nt // 2, nt - nt // 2, gx_a, gn_a)
    a_in_bf, r2_in, f_out = _wait_copies("wait_chip_exchange_w_in", ss, rs, bufs_d, plan_d, grad_x)
    f_in = _add_chips(a_in, r2_in, pos, "add_chips_w_in")

    def plan_e(x, y, c, f, gn, gnl):
        return _plan_pair_gather(x, y, c, f) + _plan_all_gather(x, y, c, gn, gnl)

    ss, rs, bufs_e, tok = _start_copies(
        "start_pair_gather_w_in", [f_in, gn8, _landing((8,) + gn8.shape, F32)], 1 + 7, plan_e)
    g_w_out = f_out.reshape(W_OUT_ROWS, D_MODEL)
    d_w_out, nm_w_out, nv_w_out = _adamw_large(w_out[0], g_w_out, m_w_out[0], v_w_out[0], tok, "adamw_w_out")
    f_in, gn8, gn_land = _wait_copies("wait_pair_gather_w_in", ss, rs, bufs_e, plan_e, d_w_out)
    g_w_in = f_in.reshape(D_MODEL, W_IN_COLS)
    d_w_in, nm_w_in, nv_w_in = _adamw_large(w_in[0], g_w_in, m_w_in[0], v_w_in[0], tok, "adamw_w_in")

    flat = lambda a: a.reshape(1, C_BR)
    vectors = {
        "norm_g": (norm_g, m_norm_g, v_norm_g),
        "conv_b": (conv_b, m_conv_b, v_conv_b),
        "conv_ln_g": (conv_ln_g, m_conv_ln_g, v_conv_ln_g),
        "conv_ln_b": (conv_ln_b, m_conv_ln_b, v_conv_ln_b),
        "sgu_ln_g": (sgu_ln_g, m_sgu_ln_g, v_sgu_ln_g),
        "sgu_ln_b": (sgu_ln_b, m_sgu_ln_b, v_sgu_ln_b),
        "b_s": (flat(b_s), flat(m_b_s), flat(v_b_s)),
        "final_g": (flat(final_g), flat(m_final_g), flat(v_final_g)),
    }
    flat_ws = lambda a: a.reshape(HEADS * CHUNK, CHUNK)
    res = _adamw_small(p, q, gn8, gn_land, pos, vectors, (conv_w[0], m_conv_w[0], v_conv_w[0]),
                       (flat_ws(w_s), flat_ws(m_w_s), flat_ws(v_w_s)))
    res["w_in"] = tuple(a[None] for a in (g_w_in, d_w_in, nm_w_in, nv_w_in))
    res["w_out"] = tuple(a[None] for a in (g_w_out, d_w_out, nm_w_out, nv_w_out))
    res["conv_w"] = tuple(a[None] for a in res["conv_w"])
    res["w_s"] = tuple(a.reshape(w_s.shape) for a in res["w_s"])
    res["b_s"] = tuple(a.reshape(b_s.shape) for a in res["b_s"])
    res["final_g"] = tuple(a.reshape(final_g.shape) for a in res["final_g"])

    order = ("norm_g", "w_in", "conv_w", "conv_b", "conv_ln_g", "conv_ln_b", "sgu_ln_g", "sgu_ln_b",
             "w_s", "b_s", "w_out", "final_g")
    out = [loss, grad_x[None]]
    for part in range(4):
        out += [res[name][part] for name in order]
    return tuple(out)
```

```python
import functools

import jax
import jax.numpy as jnp
from jax import lax
from jax.experimental import pallas as pl
from jax.experimental.pallas import tpu as pltpu

F32 = jnp.float32
BF16 = jnp.bfloat16
MESH = pl.DeviceIdType.MESH

EPS = 1e-6
D_MODEL = 1024
C_BR = 1024
D_IN = 6 * C_BR
N_SHARD = 4
W_IN_COLS = D_IN // N_SHARD
W_OUT_ROWS = 2 * C_BR // N_SHARD
CONV_W_COLS = C_BR // N_SHARD
KW = 31
HALO = 16
HEADS = 8
HEAD_DIM = 128
CHUNK = 128
LANE = 128
SUB = 8

ADAM_LR = 0.001
ADAM_B1 = 0.9
ADAM_B2 = 0.999
ADAM_EPS = 1e-08
ADAM_WD = 0.01
ADAM_STEP = 10

TS_PROJ = 512
TS_FWD = 256
TS_BWD = 256
TS_INB = 512
TK_GW = 1024
ROWS_A = 128
ROWS_B = 64
VMEM_LIMIT = 56 * 1024 * 1024


def _cparams(sem=None, vmem=VMEM_LIMIT):
    kw = dict(vmem_limit_bytes=vmem)
    if sem is not None:
        kw["dimension_semantics"] = sem
    return pltpu.CompilerParams(**kw)


def _whole_vmem():
    return pl.BlockSpec(memory_space=pltpu.VMEM)


def _sigmoid(v):
    return 1.0 / (1.0 + jnp.exp(-v))


def _fold8(v):
    n, c = v.shape
    return v.reshape(n // SUB, SUB, c).sum(axis=0)


def _dot_nt(a, b):
    return lax.dot_general(a, b, (((1,), (1,)), ((), ())), preferred_element_type=F32)


def _dot_tn(a, b):
    return lax.dot_general(a, b, (((0,), (0,)), ((), ())), preferred_element_type=F32)


def _mesh_pos():
    return lax.axis_index("x"), lax.axis_index("y"), lax.axis_index("c")


def _rms_proj_gather(x, norm_g, w_in, w_out, conv_w, order):
    s = x.shape[0]
    ts = min(TS_PROJ, s)
    nt = s // ts
    hin = w_in.shape[0] // 2
    hout = w_out.shape[0] // 2

    def body(order_ref, x_ref, g_ref, win_ref, wout_ref, cw_ref,
             h_ref, proj_ref, wing_ref, woutg_ref, cwg_ref, wg_ref, wob_ref, hs_ref, send_sems, recv_sems, local_sems):
        p = pl.program_id(0)
        t = pl.program_id(1)
        mx, my, c = _mesh_pos()
        me = 2 * mx + my
        chips = [(1 - mx, my), (mx, 1 - my), (1 - mx, 1 - my)]
        sibling = (mx, my, 1 - c)

        def remote(src, dst, sem, dev):
            return pltpu.make_async_remote_copy(
                src_ref=src, dst_ref=dst, send_sem=send_sems.at[sem], recv_sem=recv_sems.at[sem],
                device_id=dev, device_id_type=MESH)

        def w_in_part(blk, half):
            return wg_ref.at[blk, pl.ds(half * hin, hin)]

        def keep(blk, k):
            return pltpu.make_async_copy(wg_ref.at[blk], wing_ref.at[blk], local_sems.at[2 + k])

        def w_out_part(blk, half):
            return woutg_ref.at[blk, pl.ds(half * hout, hout)]

        def sends():
            out = []
            for j, (cx, cy) in enumerate(chips):
                blk = 2 * cx + cy
                out.append(remote(w_in_part(me, c), w_in_part(me, c), j, (cx, cy, c)))
                out.append(remote(w_in_part(blk, c), w_in_part(blk, c), 3 + j, sibling))
                out.append(remote(wob_ref.at[pl.ds(c * hout, hout)], w_out_part(me, c), 6 + j, (cx, cy, c)))
                out.append(remote(w_out_part(blk, c), w_out_part(blk, c), 9 + j, sibling))
                out.append(remote(cw_ref, cwg_ref.at[me], 12 + j, (cx, cy, c)))
            return out

        @pl.when(jnp.logical_and(p == 0, t == 0))
        def _():
            wg_ref[me] = win_ref[...].astype(BF16)
            wob_ref[...] = wout_ref[...].astype(BF16)
            keep(me, 0).start()
            mine = [pltpu.make_async_copy(wob_ref, woutg_ref.at[me], local_sems.at[0]),
                    pltpu.make_async_copy(cw_ref, cwg_ref.at[me], local_sems.at[1])]
            for cp in mine:
                cp.start()
            for k, cp in enumerate(sends()):
                if k % 5 == 4 or (k % 5 == 0 and k // 5 < 2):
                    cp.start()
            for cp in mine:
                cp.wait()

        for j, (cx, cy) in enumerate(chips):
            blk = 2 * cx + cy

            @pl.when(jnp.logical_and(p == j + 1, t == 0))
            def _():
                remote(w_in_part(blk, c), w_in_part(blk, c), j, (cx, cy, c)).wait_recv()
                remote(w_in_part(blk, c), w_in_part(blk, c), 3 + j, sibling).start()
                remote(w_in_part(blk, 1 - c), w_in_part(blk, 1 - c), 3 + j, sibling).wait_recv()
                keep(blk, j + 1).start()
                if j == 0:
                    sends()[5 * 2].start()
                if j == 1:
                    for jj in range(3):
                        sends()[5 * jj + 2].start()

        rows = pl.ds(pl.multiple_of(t * ts, ts), ts)

        @pl.when(p == 0)
        def _():
            xt = x_ref[...]
            r = lax.rsqrt(jnp.mean(xt * xt, axis=-1, keepdims=True) + EPS)
            hb = (xt * r * g_ref[...]).astype(BF16)
            h_ref[...] = hb
            hs_ref[rows, :] = hb

        proj_ref[...] = jnp.dot(hs_ref[rows, :], wg_ref[order_ref[p]], preferred_element_type=F32).astype(BF16)

        @pl.when(jnp.logical_and(p == N_SHARD - 1, t == nt - 1))
        def _():
            for j, (cx, cy) in enumerate(chips):
                blk = 2 * cx + cy
                remote(w_out_part(blk, c), w_out_part(blk, c), 6 + j, (cx, cy, c)).wait_recv()
                remote(w_out_part(blk, c), w_out_part(blk, c), 9 + j, sibling).start()
            for j, (cx, cy) in enumerate(chips):
                blk = 2 * cx + cy
                remote(w_out_part(blk, 1 - c), w_out_part(blk, 1 - c), 9 + j, sibling).wait_recv()
                remote(cw_ref, cwg_ref.at[blk], 12 + j, (cx, cy, c)).wait_recv()
            for cp in sends():
                cp.wait_send()
            keep(me, 0).wait()
            for j, (cx, cy) in enumerate(chips):
                keep(2 * cx + cy, j + 1).wait()

    hbm = pl.BlockSpec(memory_space=pl.ANY)
    return pl.pallas_call(
        body, name="rms_proj_gather",
        grid_spec=pltpu.PrefetchScalarGridSpec(
            num_scalar_prefetch=1, grid=(N_SHARD, nt),
            in_specs=[pl.BlockSpec((ts, D_MODEL), lambda p, t, o: (jnp.where(p == 0, t, nt - 1), 0)),
                      pl.BlockSpec((1, D_MODEL), lambda p, t, o: (0, 0)),
                      _whole_vmem(), _whole_vmem(), _whole_vmem()],
            out_specs=(pl.BlockSpec((ts, D_MODEL), lambda p, t, o: (jnp.where(p == 0, t, nt - 1), 0)),
                       pl.BlockSpec((ts, W_IN_COLS), lambda p, t, o: (t, o[p])),
                       hbm, hbm, hbm),
            scratch_shapes=[pltpu.VMEM((N_SHARD,) + w_in.shape, BF16), pltpu.VMEM(w_out.shape, BF16),
                            pltpu.VMEM((s, D_MODEL), BF16),
                            pltpu.SemaphoreType.DMA((15,)), pltpu.SemaphoreType.DMA((15,)),
                            pltpu.SemaphoreType.DMA((6,))]),
        out_shape=(jax.ShapeDtypeStruct((s, D_MODEL), BF16), jax.ShapeDtypeStruct((s, D_IN), BF16),
                   jax.ShapeDtypeStruct((N_SHARD,) + w_in.shape, BF16),
                   jax.ShapeDtypeStruct((N_SHARD,) + w_out.shape, BF16),
                   jax.ShapeDtypeStruct((N_SHARD,) + conv_w.shape, F32)),
        compiler_params=_cparams(("arbitrary", "arbitrary")),
    )(order, x, norm_g, w_in, w_out, conv_w)


def _halo_specs(ts, s, width, col_block):
    per = ts // HALO
    last = s // HALO - 1
    prev = pl.BlockSpec((HALO, width), lambda i: (jnp.maximum(i * per - 1, 0), col_block))
    nxt = pl.BlockSpec((HALO, width), lambda i: (jnp.minimum((i + 1) * per, last), col_block))
    return prev, nxt


def _layer_norm_stats(v):
    mu = jnp.mean(v, axis=-1, keepdims=True)
    vc = v - mu
    var = jnp.mean(vc * vc, axis=-1, keepdims=True)
    rstd = lax.rsqrt(var + EPS)
    return vc * rstd, rstd


def _fill_shifted(sh_ref, ext):
    n = ext.shape[0]
    sh_ref[0] = ext
    for r in range(1, SUB):
        sh_ref[r] = pltpu.roll(ext, n - r, axis=0)


def _fwd_fused(proj, x, target, cwg, conv_b, conv_ln_g, conv_ln_b, sgu_ln_g, sgu_ln_b, w_s, bs_t, woutg, final_g):
    s = proj.shape[0]
    ts = min(TS_FWD, s)
    nt = s // ts
    ra = min(ROWS_A, ts)

    def body(pm_ref, pp_ref, pn_ref, x_ref, t_ref, cw_ref, cb_ref, clg_ref, clb_ref, slg_ref, slb_ref, ws_ref,
             bst_ref, wo_ref, fg_ref,
             cv_ref, dx2_ref, dy_ref, gw_ref, gf_ref, se_ref,
             sh_ref, y_scr, gw_acc, sem):
        i = pl.program_id(0)
        tile = jnp.minimum(i, nt - 1)
        keep_prev = (tile > 0).astype(F32)
        keep_next = (tile < nt - 1).astype(F32)
        y_new = y_scr.at[i % 2]
        y_old = y_scr.at[(i + 1) % 2]

        @pl.when(i == 0)
        def _():
            y_scr[...] = jnp.zeros_like(y_scr)
            gw_acc[...] = jnp.zeros_like(gw_acc)
            gf_ref[...] = jnp.zeros_like(gf_ref)
            se_ref[...] = jnp.zeros_like(se_ref)

        for lb in range(C_BR // LANE):
            lanes = slice(lb * LANE, (lb + 1) * LANE)
            gates = slice(C_BR + lb * LANE, C_BR + (lb + 1) * LANE)

            def glu(ref):
                return ref[:, lanes].astype(F32) * _sigmoid(ref[:, gates].astype(F32))

            ext = jnp.concatenate([glu(pp_ref) * keep_prev, glu(pm_ref), glu(pn_ref) * keep_next], axis=0)
            _fill_shifted(sh_ref, ext)
            shard, off = divmod(lb * LANE, CONV_W_COLS)
            bias = cb_ref[:, lanes]

            def chunk(jc, carry):
                base = pl.multiple_of(jc * ra, ra)
                acc = jnp.zeros((ra, LANE), F32) + bias
                for k in range(KW):
                    o = k + 1
                    acc = acc + sh_ref[o % SUB, pl.ds(base + SUB * (o // SUB), ra), :] * cw_ref[
                        shard, k:k + 1, off:off + LANE]
                cv_ref[pl.ds(base, ra), lanes] = acc
                return carry

            lax.fori_loop(0, ts // ra, chunk, 0)

        live = (i > 0).astype(F32)
        yt = y_old[...]
        x2 = x_ref[...]
        for k in range(N_SHARD):
            x2 = x2 + jnp.dot(yt[:, k * W_OUT_ROWS:(k + 1) * W_OUT_ROWS], wo_ref[k], preferred_element_type=F32)
        r2 = lax.rsqrt(jnp.mean(x2 * x2, axis=-1, keepdims=True) + EPS)
        n2 = x2 * r2
        g = fg_ref[...]
        diff = n2 * g - t_ref[...]
        se_ref[...] += live * _fold8(diff * diff)
        dout = diff * (1.0 / D_MODEL)
        gf_ref[...] += live * _fold8(dout * n2)
        dn = dout * g
        dx2 = r2 * (dn - n2 * jnp.mean(dn * n2, axis=-1, keepdims=True))
        dx2_ref[...] = dx2
        dxb = dx2.astype(BF16)
        for k in range(N_SHARD):
            rows = slice(k * W_OUT_ROWS, (k + 1) * W_OUT_ROWS)
            dy_ref[:, rows] = _dot_nt(dxb, wo_ref[k]).astype(BF16)
            gw_acc[rows, :] += _dot_tn(yt[:, rows], dxb)

        lnh, _ = _layer_norm_stats(cv_ref[...])
        ln = lnh * clg_ref[...] + clb_ref[...]
        gc = pm_ref[:, 2 * C_BR:3 * C_BR].astype(F32)
        y_new[:, :C_BR] = (ln * _sigmoid(ln) * gc * _sigmoid(gc)).astype(BF16)

        vh, _ = _layer_norm_stats(pm_ref[:, 4 * C_BR:5 * C_BR].astype(F32))
        vn = (vh * slg_ref[...] + slb_ref[...]).astype(BF16)
        for hd in range(HEADS):
            w_h = ws_ref[hd].astype(BF16)
            b_h = bst_ref[:, hd:hd + 1]
            cols = slice(hd * HEAD_DIM, (hd + 1) * HEAD_DIM)
            for ch in range(ts // CHUNK):
                rows = slice(ch * CHUNK, (ch + 1) * CHUNK)
                mixed = jnp.dot(w_h, vn[rows, cols], preferred_element_type=F32) + b_h
                u = pm_ref[rows, 3 * C_BR + hd * HEAD_DIM:3 * C_BR + (hd + 1) * HEAD_DIM].astype(F32)
                gs = pm_ref[rows, 5 * C_BR + hd * HEAD_DIM:5 * C_BR + (hd + 1) * HEAD_DIM].astype(F32)
                y_new[rows, C_BR + hd * HEAD_DIM:C_BR + (hd + 1) * HEAD_DIM] = (
                    u * mixed * gs * _sigmoid(gs)).astype(BF16)

        @pl.when(i == nt)
        def _():
            gf_ref[...] = jnp.broadcast_to(jnp.sum(gf_ref[...], axis=0, keepdims=True), gf_ref.shape)
            out = pltpu.make_async_copy(gw_acc, gw_ref, sem)
            out.start()
            out.wait()

    per = ts // HALO
    last = s // HALO - 1
    cur = lambda i: jnp.minimum(i, nt - 1)
    before = lambda i: jnp.maximum(i - 1, 0)
    prev = pl.BlockSpec((HALO, 2 * C_BR), lambda i: (jnp.maximum(cur(i) * per - 1, 0), 0))
    nxt = pl.BlockSpec((HALO, 2 * C_BR), lambda i: (jnp.minimum((cur(i) + 1) * per, last), 0))
    row = pl.BlockSpec((1, C_BR), lambda i: (0, 0))
    old_tile = pl.BlockSpec((ts, D_MODEL), lambda i: (before(i), 0))
    acc8 = pl.BlockSpec((SUB, D_MODEL), lambda i: (0, 0))
    return pl.pallas_call(
        body, name="fwd_fused",
        grid=(nt + 1,),
        out_shape=(jax.ShapeDtypeStruct((s, C_BR), F32), jax.ShapeDtypeStruct((s, D_MODEL), F32),
                   jax.ShapeDtypeStruct((s, 2 * C_BR), BF16), jax.ShapeDtypeStruct((2 * C_BR, D_MODEL), F32),
                   jax.ShapeDtypeStruct((SUB, D_MODEL), F32), jax.ShapeDtypeStruct((SUB, D_MODEL), F32)),
        in_specs=[pl.BlockSpec((ts, D_IN), lambda i: (cur(i), 0)), prev, nxt, old_tile, old_tile,
                  _whole_vmem(), row, row, row, row, row, _whole_vmem(), _whole_vmem(), _whole_vmem(), row],
        out_specs=(pl.BlockSpec((ts, C_BR), lambda i: (cur(i), 0)), old_tile,
                   pl.BlockSpec((ts, 2 * C_BR), lambda i: (before(i), 0)),
                   pl.BlockSpec(memory_space=pl.ANY), acc8, acc8),
        scratch_shapes=[pltpu.VMEM((SUB, ts + 2 * HALO, LANE), F32),
                        pltpu.VMEM((2, ts, 2 * C_BR), BF16),
                        pltpu.VMEM((2 * C_BR, D_MODEL), F32),
                        pltpu.SemaphoreType.DMA(())],
        compiler_params=_cparams(("arbitrary",), 60 * 1024 * 1024),
    )(proj, proj, proj, x, target, cwg, conv_b, conv_ln_g, conv_ln_b, sgu_ln_g, sgu_ln_b, w_s, bs_t, woutg, final_g)


def _dsilu(v, sg):
    return sg * (1.0 + v * (1.0 - sg))


def _branch_bwd(proj, dy, cv, cwg, conv_ln_g, conv_ln_b, sgu_ln_g, sgu_ln_b, w_s, ws_t, bs_t, gf8, se8, dep):
    s = proj.shape[0]
    ts = min(TS_BWD, s)
    nt = s // ts
    ra = min(ROWS_A, ts)
    rb = min(ROWS_B, ts)
    te = ts + 2 * HALO

    def body(pm_ref, dym_ref, cvm_ref, gcp_ref, gcn_ref, dyp_ref, dyn_ref, cvp_ref, cvn_ref,
             cw_ref, clg_ref, clb_ref, slg_ref, slb_ref, ws_ref, wst_ref, bst_ref, gf_ref, se_ref, dep_ref,
             dp_ref, small_ref, gws_ref,
             sh_ref, glu_ref, dgl_ref, dcv_ref, acc_ref, gcw_ref, gbs_ref):
        i = pl.program_id(0)

        @pl.when(i == 0)
        def _():
            acc_ref[...] = jnp.zeros_like(acc_ref)
            gcw_ref[...] = jnp.zeros_like(gcw_ref)
            gbs_ref[...] = jnp.zeros_like(gbs_ref)
            gws_ref[...] = jnp.zeros_like(gws_ref)

        def ext(prev_ref, main, next_ref):
            return jnp.concatenate([prev_ref[...].astype(F32), main, next_ref[...].astype(F32)], axis=0)

        main = slice(HALO, HALO + ts)

        cv_e = ext(cvp_ref, cvm_ref[...], cvn_ref)
        gc_e = ext(gcp_ref, pm_ref[:, 2 * C_BR:3 * C_BR].astype(F32), gcn_ref)
        dyc_e = ext(dyp_ref, dym_ref[:, :C_BR].astype(F32), dyn_ref)
        lnh, rstd = _layer_norm_stats(cv_e)
        clg = clg_ref[...]
        ln = lnh * clg + clb_ref[...]
        sg_ln = _sigmoid(ln)
        sg_gc = _sigmoid(gc_e)
        d_ln = dyc_e * gc_e * sg_gc * _dsilu(ln, sg_ln)
        dp_ref[:, 2 * C_BR:3 * C_BR] = (
            dyc_e[main] * ln[main] * sg_ln[main] * _dsilu(gc_e[main], sg_gc[main])).astype(BF16)
        dlnh = d_ln * clg
        d_cv = rstd * (dlnh - jnp.mean(dlnh, axis=-1, keepdims=True)
                       - lnh * jnp.mean(dlnh * lnh, axis=-1, keepdims=True))
        row = lax.broadcasted_iota(jnp.int32, (te, 1), 0)
        valid = jnp.logical_and(jnp.logical_or(row >= HALO, i > 0),
                                jnp.logical_or(row < HALO + ts, i < nt - 1))
        d_cv = jnp.where(valid, d_cv, 0.0)
        dcv_ref[...] = d_cv
        acc_ref[0:8, :] += _fold8(d_cv[main])
        acc_ref[8:16, :] += _fold8(d_ln[main] * lnh[main])
        acc_ref[16:24, :] += _fold8(d_ln[main])

        for lb in range(C_BR // LANE):
            lanes = slice(lb * LANE, (lb + 1) * LANE)
            gates = slice(C_BR + lb * LANE, C_BR + (lb + 1) * LANE)
            shard, off = divmod(lb * LANE, CONV_W_COLS)
            av = pm_ref[:, lanes].astype(F32)
            sg = _sigmoid(pm_ref[:, gates].astype(F32))
            glu_ref[...] = av * sg
            _fill_shifted(sh_ref, dcv_ref[:, lanes])

            def chunk_a(jc, carry):
                base = pl.multiple_of(jc * ra, ra)
                acc = jnp.zeros((ra, LANE), F32)
                for j in range(KW):
                    o = j + 1
                    acc = acc + sh_ref[o % SUB, pl.ds(base + SUB * (o // SUB), ra), :] * cw_ref[
                        shard, KW - 1 - j:KW - j, off:off + LANE]
                dgl_ref[pl.ds(base, ra), :] = acc
                return carry

            lax.fori_loop(0, ts // ra, chunk_a, 0)

            def chunk_b(jc, accs):
                base = pl.multiple_of(jc * rb, rb)
                g = glu_ref[pl.ds(base, rb), :]
                out = []
                for j in range(KW):
                    o = j + 1
                    d = sh_ref[o % SUB, pl.ds(base + SUB * (o // SUB), rb), :]
                    out.append(accs[j] + _fold8(g * d))
                return tuple(out)

            accs = lax.fori_loop(0, ts // rb, chunk_b, tuple(jnp.zeros((SUB, LANE), F32) for _ in range(KW)))
            for j in range(KW):
                gcw_ref[j * SUB:(j + 1) * SUB, lanes] += accs[j]

            dglu = dgl_ref[...]
            dp_ref[:, lanes] = (dglu * sg).astype(BF16)
            dp_ref[:, gates] = (dglu * av * sg * (1.0 - sg)).astype(BF16)

        vh, vrstd = _layer_norm_stats(pm_ref[:, 4 * C_BR:5 * C_BR].astype(F32))
        slg = slg_ref[...]
        vn = (vh * slg + slb_ref[...]).astype(BF16)
        for hd in range(HEADS):
            w_h = ws_ref[hd].astype(BF16)
            wt_h = wst_ref[hd].astype(BF16)
            b_h = bst_ref[:, hd:hd + 1]
            cols = slice(hd * HEAD_DIM, (hd + 1) * HEAD_DIM)
            gws_h = jnp.zeros((CHUNK, CHUNK), F32)
            gbs_h = jnp.zeros((CHUNK, HEAD_DIM), F32)
            for ch in range(ts // CHUNK):
                rows = slice(ch * CHUNK, (ch + 1) * CHUNK)
                vn_b = vn[rows, cols]
                mixed = jnp.dot(w_h, vn_b, preferred_element_type=F32) + b_h
                u = pm_ref[rows, 3 * C_BR + hd * HEAD_DIM:3 * C_BR + (hd + 1) * HEAD_DIM].astype(F32)
                gs = pm_ref[rows, 5 * C_BR + hd * HEAD_DIM:5 * C_BR + (hd + 1) * HEAD_DIM].astype(F32)
                dys = dym_ref[rows, C_BR + hd * HEAD_DIM:C_BR + (hd + 1) * HEAD_DIM].astype(F32)
                sg_gs = _sigmoid(gs)
                silu_gs = gs * sg_gs
                dp_ref[rows, 3 * C_BR + hd * HEAD_DIM:3 * C_BR + (hd + 1) * HEAD_DIM] = (
                    dys * mixed * silu_gs).astype(BF16)
                dp_ref[rows, 5 * C_BR + hd * HEAD_DIM:5 * C_BR + (hd + 1) * HEAD_DIM] = (
                    dys * u * mixed * _dsilu(gs, sg_gs)).astype(BF16)
                d_mixed = dys * u * silu_gs
                dm_b = d_mixed.astype(BF16)
                gws_h = gws_h + _dot_nt(dm_b, vn_b)
                gbs_h = gbs_h + d_mixed
                dcv_ref[HALO + ch * CHUNK:HALO + (ch + 1) * CHUNK, cols] = jnp.dot(
                    wt_h, dm_b, preferred_element_type=F32)
            gws_ref[hd] += gws_h
            gbs_ref[:, cols] += gbs_h
        d_vn = dcv_ref[main, :]
        acc_ref[24:32, :] += _fold8(d_vn * vh)
        acc_ref[32:40, :] += _fold8(d_vn)
        dvh = d_vn * slg
        dp_ref[:, 4 * C_BR:5 * C_BR] = (vrstd * (
            dvh - jnp.mean(dvh, axis=-1, keepdims=True)
            - vh * jnp.mean(dvh * vh, axis=-1, keepdims=True))).astype(BF16)

        @pl.when(i == nt - 1)
        def _():
            small_ref[...] = jnp.zeros_like(small_ref)
            for a in range(5):
                small_ref[1 + a:2 + a, :] = jnp.sum(acc_ref[a * SUB:(a + 1) * SUB, :], axis=0, keepdims=True)
            ones = jnp.ones((SUB, HEAD_DIM), F32)
            for hd in range(HEADS):
                cols = slice(hd * HEAD_DIM, (hd + 1) * HEAD_DIM)
                rowsum = lax.dot_general(ones, gbs_ref[:, cols], (((1,), (1,)), ((), ())),
                                         precision=lax.Precision.HIGHEST, preferred_element_type=F32)
                small_ref[6:7, cols] = rowsum[0:1, :]
            small_ref[7:8, :] = gf_ref[0:1, :]
            small_ref[0:1, :] = jnp.sum(se_ref[...], axis=0, keepdims=True)
            for k in range(KW):
                j = KW - 1 - k
                small_ref[8 + k:9 + k, :] = jnp.sum(gcw_ref[j * SUB:(j + 1) * SUB, :], axis=0, keepdims=True)

    gc_prev, gc_next = _halo_specs(ts, s, C_BR, 2)
    lo_prev, lo_next = _halo_specs(ts, s, C_BR, 0)
    row = pl.BlockSpec((1, C_BR), lambda i: (0, 0))
    return pl.pallas_call(
        body, name="branch_bwd",
        grid=(nt,),
        out_shape=(jax.ShapeDtypeStruct((s, D_IN), BF16), jax.ShapeDtypeStruct((40, C_BR), F32),
                   jax.ShapeDtypeStruct((HEADS, CHUNK, CHUNK), F32)),
        in_specs=[pl.BlockSpec((ts, D_IN), lambda i: (i, 0)),
                  pl.BlockSpec((ts, 2 * C_BR), lambda i: (i, 0)),
                  pl.BlockSpec((ts, C_BR), lambda i: (i, 0)),
                  gc_prev, gc_next, lo_prev, lo_next, lo_prev, lo_next,
                  _whole_vmem(), row, row, row, row, _whole_vmem(), _whole_vmem(), _whole_vmem(),
                  _whole_vmem(), _whole_vmem(), _whole_vmem()],
        out_specs=(pl.BlockSpec((ts, D_IN), lambda i: (i, 0)),
                   pl.BlockSpec((40, C_BR), lambda i: (0, 0)),
                   pl.BlockSpec((HEADS, CHUNK, CHUNK), lambda i: (0, 0, 0))),
        scratch_shapes=[pltpu.VMEM((SUB, te, LANE), F32),
                        pltpu.VMEM((ts, LANE), F32),
                        pltpu.VMEM((ts, LANE), F32),
                        pltpu.VMEM((te, C_BR), F32),
                        pltpu.VMEM((5 * SUB, C_BR), F32),
                        pltpu.VMEM((KW * SUB, C_BR), F32),
                        pltpu.VMEM((CHUNK, C_BR), F32)],
        compiler_params=_cparams(("arbitrary",)),
    )(proj, dy, cv, proj, proj, dy, dy, cv, cv,
      cwg, conv_ln_g, conv_ln_b, sgu_ln_g, sgu_ln_b, w_s, ws_t, bs_t, gf8, se8, dep)


def _in_bwd(dproj, x, dx2, norm_g, wing, dep, first_tile, n_tiles, gx_prev=None, gn_prev=None):
    s = x.shape[0]
    ts = min(TS_INB, s)

    def body(*refs):
        dp_ref, x_ref, dx2_ref, g_ref, w_ref = refs[:5]
        gx_ref, gn_ref, acc_ref = refs[-3:]
        i = pl.program_id(0)

        @pl.when(i == 0)
        def _():
            acc_ref[...] = jnp.zeros_like(acc_ref)

        dh = jnp.zeros((ts, D_MODEL), F32)
        for k in range(N_SHARD):
            dh = dh + _dot_nt(dp_ref[:, k * W_IN_COLS:(k + 1) * W_IN_COLS], w_ref[k])
        xt = x_ref[...]
        r = lax.rsqrt(jnp.mean(xt * xt, axis=-1, keepdims=True) + EPS)
        n = xt * r
        acc_ref[...] += _fold8(dh * n)
        dn = dh * g_ref[...]
        gx_ref[...] = dx2_ref[...] + r * (dn - n * jnp.mean(dn * n, axis=-1, keepdims=True))

        @pl.when(i == n_tiles - 1)
        def _():
            total = jnp.broadcast_to(jnp.sum(acc_ref[...], axis=0, keepdims=True), gn_ref.shape)
            if gn_prev is not None:
                total = total + refs[7][...]
            gn_ref[...] = total

    tile = pl.BlockSpec((ts, D_MODEL), lambda i: (i + first_tile, 0))
    in_specs = [pl.BlockSpec((ts, D_IN), lambda i: (i + first_tile, 0)), tile, tile,
                pl.BlockSpec((1, D_MODEL), lambda i: (0, 0)), _whole_vmem(), _whole_vmem()]
    operands = [dproj, x, dx2, norm_g, wing, dep]
    aliases = {}
    if gx_prev is not None:
        in_specs += [pl.BlockSpec(memory_space=pl.ANY), _whole_vmem()]
        operands += [gx_prev, gn_prev]
        aliases = {6: 0}
    return pl.pallas_call(
        body, name="in_bwd_%d" % first_tile,
        grid=(n_tiles,),
        out_shape=(jax.ShapeDtypeStruct((s, D_MODEL), F32), jax.ShapeDtypeStruct((SUB, D_MODEL), F32)),
        in_specs=in_specs,
        out_specs=(tile, pl.BlockSpec((SUB, D_MODEL), lambda i: (0, 0))),
        scratch_shapes=[pltpu.VMEM((SUB, D_MODEL), F32)],
        input_output_aliases=aliases,
        compiler_params=_cparams(("arbitrary",)),
    )(*operands)


def _grad_w_in(h, dproj, dep):
    s = h.shape[0]
    tk = min(TK_GW, s)
    half = D_MODEL // 2

    def body(h_ref, dp_ref, dep_ref, o_ref):
        @pl.when(pl.program_id(1) == 0)
        def _():
            o_ref[...] = jnp.zeros_like(o_ref)

        o_ref[0] += _dot_tn(h_ref[...], dp_ref[...]).reshape(2, half, W_IN_COLS)

    return pl.pallas_call(
        body, name="grad_w_in",
        grid=(N_SHARD, s // tk),
        out_shape=jax.ShapeDtypeStruct((N_SHARD, 2, half, W_IN_COLS), F32),
        in_specs=[pl.BlockSpec((tk, D_MODEL), lambda k, t: (t, 0)),
                  pl.BlockSpec((tk, W_IN_COLS), lambda k, t: (t, k)), _whole_vmem()],
        out_specs=pl.BlockSpec((1, 2, half, W_IN_COLS), lambda k, t: (k, 0, 0, 0)),
        compiler_params=_cparams(("parallel", "arbitrary")),
    )(h, dproj, dep)


HBM_SPEC = pl.BlockSpec(memory_space=pltpu.HBM)
SEM_SPEC = pl.BlockSpec(memory_space=pltpu.SEMAPHORE)
SIDE_EFFECT = pltpu.SideEffectType.DATAFLOW_SIDE_EFFECTING


def _remote_copies(plan, bufs, send_sems, recv_sems):
    x, y, c = _mesh_pos()
    return [pltpu.make_async_remote_copy(src_ref=src, dst_ref=dst, send_sem=send_sems.at[k],
                                         recv_sem=recv_sems.at[k], device_id=dev, device_id_type=MESH)
            for k, (src, dst, dev) in enumerate(plan(x, y, c, *bufs))]


def _start_copies(name, bufs, n_copies, plan):
    n = len(bufs)

    def body(*refs):
        for cp in _remote_copies(plan, refs[:n], refs[n], refs[n + 1]):
            cp.start()
        refs[-1][...] = jnp.zeros_like(refs[-1])

    outs = pl.pallas_call(
        body, name=name,
        out_shape=(pltpu.SemaphoreType.DMA((n_copies,)), pltpu.SemaphoreType.DMA((n_copies,)),
                   *[pltpu.HBM(b.shape, b.dtype) for b in bufs], jax.ShapeDtypeStruct((SUB, LANE), F32)),
        in_specs=[HBM_SPEC] * n,
        out_specs=(SEM_SPEC, SEM_SPEC, *[HBM_SPEC] * n, _whole_vmem()),
        input_output_aliases={i: 2 + i for i in range(n)},
        compiler_params=pltpu.CompilerParams(has_side_effects=SIDE_EFFECT),
    )(*[pltpu.with_memory_space_constraint(b, pltpu.HBM) for b in bufs])
    return outs[0], outs[1], list(outs[2:2 + n]), outs[-1]


def _wait_copies(name, send_sems, recv_sems, bufs, plan, after):
    n = len(bufs)

    def body(*refs):
        for cp in _remote_copies(plan, refs[:n], refs[n], refs[n + 1]):
            cp.wait_send()
            cp.wait_recv()

    outs = pl.pallas_call(
        body, name=name,
        out_shape=tuple(pltpu.HBM(b.shape, b.dtype) for b in bufs),
        in_specs=[HBM_SPEC] * n + [SEM_SPEC, SEM_SPEC, pl.BlockSpec(memory_space=pl.ANY)],
        out_specs=(HBM_SPEC,) * n,
        input_output_aliases={i: i for i in range(n)},
        compiler_params=pltpu.CompilerParams(has_side_effects=SIDE_EFFECT),
    )(*bufs, send_sems, recv_sems, after)
    return list(outs)


def _landing(shape, dtype):
    return lax.empty(shape, dtype)


def _plan_pair_exchange(x, y, c, g, r):
    return [(g.at[k, 1 - c], r.at[k], (x, y, 1 - c)) for k in range(N_SHARD)]


def _plan_chip_exchange(x, y, c, a, r):
    chips = [(1 - x, y), (x, 1 - y), (1 - x, 1 - y)]
    return [(a.at[2 * cx + cy], r.at[j], (cx, cy, c)) for j, (cx, cy) in enumerate(chips)]


def _plan_pair_gather(x, y, c, f):
    return [(f.at[c], f.at[c], (x, y, 1 - c))]


def _plan_all_gather(x, y, c, own, land):
    me = 4 * x + 2 * y + c
    flip = lambda v, bit: 1 - v if bit else v
    return [(own, land.at[me], (flip(x, m >> 2 & 1), flip(y, m >> 1 & 1), flip(c, m & 1))) for m in range(1, 8)]


def _add_pair(g, r, pos, name):
    _, _, rows, cols = g.shape
    tr = min(256, rows)

    def body(pos_ref, g_ref, r_ref, o_ref, ob_ref):
        v = g_ref[0] + r_ref[...]
        o_ref[...] = v
        ob_ref[...] = v.astype(BF16)

    blk = pl.BlockSpec((1, tr, cols), lambda k, t, pos_ref: (k, t, 0))
    return pl.pallas_call(
        body, name=name,
        grid_spec=pltpu.PrefetchScalarGridSpec(
            num_scalar_prefetch=1, grid=(N_SHARD, rows // tr),
            in_specs=[pl.BlockSpec((1, 1, tr, cols), lambda k, t, pos_ref: (k, pos_ref[1], t, 0)), blk],
            out_specs=(blk, blk)),
        out_shape=(jax.ShapeDtypeStruct((N_SHARD, rows, cols), F32), jax.ShapeDtypeStruct((N_SHARD, rows, cols), BF16)),
        compiler_params=_cparams(("parallel", "parallel")),
    )(pos, g, r)


def _add_chips(a, r, pos, name):
    _, rows, cols = a.shape
    tr = min(256, rows)

    def body(pos_ref, a_ref, r_ref, o_ref):
        o_ref[0] = ((a_ref[0] + r_ref[0].astype(F32)) + r_ref[1].astype(F32)) + r_ref[2].astype(F32)

    return pl.pallas_call(
        body, name=name,
        grid_spec=pltpu.PrefetchScalarGridSpec(
            num_scalar_prefetch=1, grid=(rows // tr,),
            in_specs=[pl.BlockSpec((1, tr, cols), lambda t, pos_ref: (pos_ref[0], t, 0)),
                      pl.BlockSpec((3, tr, cols), lambda t, pos_ref: (0, t, 0))],
            out_specs=pl.BlockSpec((1, tr, cols), lambda t, pos_ref: (pos_ref[1], t, 0))),
        out_shape=jax.ShapeDtypeStruct((2, rows, cols), F32),
        compiler_params=_cparams(("parallel",)),
    )(pos, a, r)


def _sum_slots(pos_ref, own, land_ref, rows):
    me = pos_ref[2]
    total = None
    for d in range(8):
        term = jnp.where(me == d, own, land_ref[d] if rows is None else land_ref[d, rows, :])
        total = term if total is None else total + term
    return total


def _sum_small(pos, small, small_land, gws, gws_land):
    def body(pos_ref, sm_ref, sml_ref, gw_ref, gwl_ref, o_sm, o_gw):
        o_sm[...] = _sum_slots(pos_ref, sm_ref[...], sml_ref, None)
        o_gw[...] = _sum_slots(pos_ref, gw_ref[...], gwl_ref, None)

    return pl.pallas_call(
        body, name="sum_small",
        grid_spec=pltpu.PrefetchScalarGridSpec(
            num_scalar_prefetch=1, grid=(1,),
            in_specs=[_whole_vmem()] * 4, out_specs=[_whole_vmem()] * 2),
        out_shape=[jax.ShapeDtypeStruct(small.shape, F32), jax.ShapeDtypeStruct(gws.shape, F32)],
        compiler_params=_cparams(("arbitrary",)),
    )(pos, small, small_land, gws, gws_land)


def _adamw_math(w, g, m, v):
    m = ADAM_B1 * m + (1.0 - ADAM_B1) * g
    v = ADAM_B2 * v + (1.0 - ADAM_B2) * (g * g)
    m_hat = m / (1.0 - ADAM_B1 ** ADAM_STEP)
    v_hat = v / (1.0 - ADAM_B2 ** ADAM_STEP)
    delta = -ADAM_LR * (m_hat / (jnp.sqrt(v_hat) + ADAM_EPS) + ADAM_WD * w)
    return delta, m, v


def _adamw_large(w, g, m, v, dep, name):
    rows, cols = w.shape
    tr = min(256, rows)

    def body(w_ref, g_ref, m_ref, v_ref, dep_ref, d_ref, nm_ref, nv_ref):
        d_ref[...], nm_ref[...], nv_ref[...] = _adamw_math(w_ref[...], g_ref[...], m_ref[...], v_ref[...])

    tile = pl.BlockSpec((tr, cols), lambda t: (t, 0))
    return pl.pallas_call(
        body, name=name,
        grid=(rows // tr,),
        out_shape=(jax.ShapeDtypeStruct(w.shape, F32),) * 3,
        in_specs=[tile] * 4 + [_whole_vmem()], out_specs=(tile,) * 3,
        compiler_params=_cparams(("parallel",)),
    )(w, g, m, v, dep)


_ROW_OF = {"conv_b": 1, "conv_ln_g": 2, "conv_ln_b": 3, "sgu_ln_g": 4, "sgu_ln_b": 5, "b_s": 6, "final_g": 7}
_CONV_W_ROW = 8
_VECTORS = ("norm_g", "conv_b", "conv_ln_g", "conv_ln_b", "sgu_ln_g", "sgu_ln_b", "b_s", "final_g")


def _adamw_small(p, q, gn8, gn_land, pos, vectors, conv_w, w_s):
    names = list(_VECTORS)

    def body(pos_ref, p_ref, q_ref, gn_ref, gnl_ref, *refs):
        n_in = 3 * (len(names) + 2)
        ins, outs = refs[:n_in], refs[n_in:]
        me = pos_ref[0]
        for a, name in enumerate(names + ["conv_w", "w_s"]):
            w_ref, m_ref, v_ref = ins[3 * a:3 * a + 3]
            if name == "conv_w":
                g = jnp.zeros((KW, CONV_W_COLS), F32)
                for k in range(N_SHARD):
                    blk = p_ref[_CONV_W_ROW:_CONV_W_ROW + KW, k * CONV_W_COLS:(k + 1) * CONV_W_COLS]
                    g = jnp.where(me == k, blk, g)
            elif name == "w_s":
                g = q_ref[...]
            elif name == "norm_g":
                g = _sum_slots(pos_ref, gn_ref[0:1, :], gnl_ref, slice(0, 1))
            else:
                g = p_ref[_ROW_OF[name]:_ROW_OF[name] + 1, :]
            delta, nm, nv = _adamw_math(w_ref[...], g, m_ref[...], v_ref[...])
            for o_ref, val in zip(outs[4 * a:4 * a + 4], (g, delta, nm, nv)):
                o_ref[...] = val

    operands, shapes = [], []
    for name in names:
        operands += list(vectors[name])
        shapes += [jax.ShapeDtypeStruct((1, C_BR), F32)] * 4
    operands += list(conv_w)
    shapes += [jax.ShapeDtypeStruct((KW, CONV_W_COLS), F32)] * 4
    operands += list(w_s)
    shapes += [jax.ShapeDtypeStruct(q.shape, F32)] * 4
    outs = pl.pallas_call(
        body, name="adamw_small",
        grid_spec=pltpu.PrefetchScalarGridSpec(
            num_scalar_prefetch=1, grid=(1,),
            in_specs=[_whole_vmem()] * (4 + len(operands)),
            out_specs=[_whole_vmem()] * len(shapes)),
        out_shape=shapes,
        compiler_params=_cparams(("arbitrary",)),
    )(pos, p, q, gn8, gn_land, *operands)
    return {name: tuple(outs[4 * a:4 * a + 4]) for a, name in enumerate(names + ["conv_w", "w_s"])}


def kernel(x, norm_g, w_in, conv_w, conv_b, conv_ln_g, conv_ln_b, sgu_ln_g, sgu_ln_b, w_s, b_s, w_out, final_g, loss_target, m_norm_g, m_w_in, m_conv_w, m_conv_b, m_conv_ln_g, m_conv_ln_b, m_sgu_ln_g, m_sgu_ln_b, m_w_s, m_b_s, m_w_out, m_final_g, v_norm_g, v_w_in, v_conv_w, v_conv_b, v_conv_ln_g, v_conv_ln_b, v_sgu_ln_g, v_sgu_ln_b, v_w_s, v_b_s, v_w_out, v_final_g):
    xi, yi, ci = _mesh_pos()
    pos = jnp.stack([2 * xi + yi, ci, 4 * xi + 2 * yi + ci]).astype(jnp.int32)

    x2d = x[0]
    tgt = loss_target[0]
    fg = final_g.reshape(1, D_MODEL)
    ws3 = w_s[0]
    ws_t = jnp.swapaxes(ws3, 1, 2)
    bs_t = jnp.transpose(b_s[0])

    chip = 2 * xi + yi
    order = jnp.stack([chip, 2 * (1 - xi) + yi, 2 * xi + 1 - yi, 2 * (1 - xi) + 1 - yi]).astype(jnp.int32)
    h, proj, wing, woutg, cwg = _rms_proj_gather(x2d, norm_g, w_in[0], w_out[0], conv_w[0], order)
    cv, dx2, dy, gwout, gf8, se8 = _fwd_fused(proj, x2d, tgt, cwg, conv_b, conv_ln_g, conv_ln_b, sgu_ln_g,
                                              sgu_ln_b, ws3, bs_t, woutg, fg)
    in_rows, out_rows = D_MODEL // 2, W_OUT_ROWS // 2
    gwout4 = gwout.reshape(N_SHARD, 2, out_rows, D_MODEL)
    ss, rs, (gwout4, r1_out), tok = _start_copies(
        "start_pair_exchange_w_out", [gwout4, _landing((N_SHARD, out_rows, D_MODEL), F32)], N_SHARD,
        _plan_pair_exchange)
    dproj, small, gws3 = _branch_bwd(proj, dy, cv, cwg, conv_ln_g, conv_ln_b, sgu_ln_g, sgu_ln_b, ws3, ws_t, bs_t,
                                     gf8, se8, tok)
    gws = gws3.reshape(HEADS * CHUNK, CHUNK)
    gwout4, r1_out = _wait_copies("wait_pair_exchange_w_out", ss, rs, [gwout4, r1_out], _plan_pair_exchange, dproj)
    a_out, a_out_bf = _add_pair(gwout4, r1_out, pos, "add_pair_w_out")

    def plan_b(x, y, c, a, r, sm, sml, gw, gwl):
        return (_plan_chip_exchange(x, y, c, a, r) + _plan_all_gather(x, y, c, sm, sml)
                + _plan_all_gather(x, y, c, gw, gwl))

    ss, rs, bufs_b, tok = _start_copies(
        "start_chip_exchange_w_out",
        [a_out_bf, _landing((3, out_rows, D_MODEL), BF16), small, _landing((8,) + small.shape, F32),
         gws, _landing((8,) + gws.shape, F32)], 3 + 7 + 7, plan_b)
    gwin = _grad_w_in(h, dproj, tok)
    ss_c, rs_c, (gwin, r1_in), tok = _start_copies(
        "start_pair_exchange_w_in", [gwin, _landing((N_SHARD, in_rows, W_IN_COLS), F32)], N_SHARD,
        _plan_pair_exchange)
    nt = x2d.shape[0] // min(TS_INB, x2d.shape[0])
    gx_a, gn_a = _in_bwd(dproj, x2d, dx2, norm_g, wing, tok, 0, nt // 2)

    gwin, r1_in = _wait_copies("wait_pair_exchange_w_in", ss_c, rs_c, [gwin, r1_in], _plan_pair_exchange, gx_a)
    a_in, a_in_bf = _add_pair(gwin, r1_in, pos, "add_pair_w_in")
    a_out_bf, r2_out, small, small_land, gws, gws_land = _wait_copies(
        "wait_chip_exchange_w_out", ss, rs, bufs_b, plan_b, gx_a)
    f_out = _add_chips(a_out, r2_out, pos, "add_chips_w_out")
    p, q = _sum_small(pos, small, small_land, gws, gws_land)
    loss = (0.5 / D_MODEL) * jnp.sum(p[0])

    def plan_d(x, y, c, a, r, f):
        return _plan_chip_exchange(x, y, c, a, r) + _plan_pair_gather(x, y, c, f)

    ss, rs, bufs_d, tok = _start_copies(
        "start_chip_exchange_w_in", [a_in_bf, _landing((3, in_rows, W_IN_COLS), BF16), f_out], 3 + 1, plan_d)
    grad_x, gn8 = _in_bwd(dproj, x2d, dx2, norm_g, wing, tok, nt // 2, nt - nt // 2, gx_a, gn_a)
    a_in_bf, r2_in, f_out = _wait_copies("wait_chip_exchange_w_in", ss, rs, bufs_d, plan_d, grad_x)
    f_in = _add_chips(a_in, r2_in, pos, "add_chips_w_in")

    def plan_e(x, y, c, f, gn, gnl):
        return _plan_pair_gather(x, y, c, f) + _plan_all_gather(x, y, c, gn, gnl)

    ss, rs, bufs_e, tok = _start_copies(
        "start_pair_gather_w_in", [f_in, gn8, _landing((8,) + gn8.shape, F32)], 1 + 7, plan_e)
    g_w_out = f_out.reshape(W_OUT_ROWS, D_MODEL)
    d_w_out, nm_w_out, nv_w_out = _adamw_large(w_out[0], g_w_out, m_w_out[0], v_w_out[0], tok, "adamw_w_out")
    f_in, gn8, gn_land = _wait_copies("wait_pair_gather_w_in", ss, rs, bufs_e, plan_e, d_w_out)
    g_w_in = f_in.reshape(D_MODEL, W_IN_COLS)
    d_w_in, nm_w_in, nv_w_in = _adamw_large(w_in[0], g_w_in, m_w_in[0], v_w_in[0], tok, "adamw_w_in")

    flat = lambda a: a.reshape(1, C_BR)
    vectors = {
        "norm_g": (norm_g, m_norm_g, v_norm_g),
        "conv_b": (conv_b, m_conv_b, v_conv_b),
        "conv_ln_g": (conv_ln_g, m_conv_ln_g, v_conv_ln_g),
        "conv_ln_b": (conv_ln_b, m_conv_ln_b, v_conv_ln_b),
        "sgu_ln_g": (sgu_ln_g, m_sgu_ln_g, v_sgu_ln_g),
        "sgu_ln_b": (sgu_ln_b, m_sgu_ln_b, v_sgu_ln_b),
        "b_s": (flat(b_s), flat(m_b_s), flat(v_b_s)),
        "final_g": (flat(final_g), flat(m_final_g), flat(v_final_g)),
    }
    flat_ws = lambda a: a.reshape(HEADS * CHUNK, CHUNK)
    res = _adamw_small(p, q, gn8, gn_land, pos, vectors, (conv_w[0], m_conv_w[0], v_conv_w[0]),
                       (flat_ws(w_s), flat_ws(m_w_s), flat_ws(v_w_s)))
    res["w_in"] = tuple(a[None] for a in (g_w_in, d_w_in, nm_w_in, nv_w_in))
    res["w_out"] = tuple(a[None] for a in (g_w_out, d_w_out, nm_w_out, nv_w_out))
    res["conv_w"] = tuple(a[None] for a in res["conv_w"])
    res["w_s"] = tuple(a.reshape(w_s.shape) for a in res["w_s"])
    res["b_s"] = tuple(a.reshape(b_s.shape) for a in res["b_s"])
    res["final_g"] = tuple(a.reshape(final_g.shape) for a in res["final_g"])

    order = ("norm_g", "w_in", "conv_w", "conv_b", "conv_ln_g", "conv_ln_b", "sgu_ln_g", "sgu_ln_b",
             "w_s", "b_s", "w_out", "final_g")
    out = [loss, grad_x[None]]
    for part in range(4):
        out += [res[name][part] for name in order]
    return tuple(out)
```

```python
import functools

import jax
import jax.numpy as jnp
from jax import lax
from jax.experimental import pallas as pl
from jax.experimental.pallas import tpu as pltpu

F32 = jnp.float32
BF16 = jnp.bfloat16
MESH = pl.DeviceIdType.MESH

EPS = 1e-6
D_MODEL = 1024
C_BR = 1024
D_IN = 6 * C_BR
N_SHARD = 4
W_IN_COLS = D_IN // N_SHARD
W_OUT_ROWS = 2 * C_BR // N_SHARD
CONV_W_COLS = C_BR // N_SHARD
KW = 31
HALO = 16
HEADS = 8
HEAD_DIM = 128
CHUNK = 128
LANE = 128
SUB = 8

ADAM_LR = 0.001
ADAM_B1 = 0.9
ADAM_B2 = 0.999
ADAM_EPS = 1e-08
ADAM_WD = 0.01
ADAM_STEP = 10

TS_PROJ = 512
TS_FWD = 256
TS_BWD = 256
TS_INB = 512
TK_GW = 2048
ROWS_A = 128
ROWS_B = 64
VMEM_LIMIT = 56 * 1024 * 1024


def _cparams(sem=None, vmem=VMEM_LIMIT):
    kw = dict(vmem_limit_bytes=vmem)
    if sem is not None:
        kw["dimension_semantics"] = sem
    return pltpu.CompilerParams(**kw)


def _whole_vmem():
    return pl.BlockSpec(memory_space=pltpu.VMEM)


def _sigmoid(v):
    return 1.0 / (1.0 + jnp.exp(-v))


def _fold8(v):
    n, c = v.shape
    return v.reshape(n // SUB, SUB, c).sum(axis=0)


def _dot_nt(a, b):
    return lax.dot_general(a, b, (((1,), (1,)), ((), ())), preferred_element_type=F32)


def _dot_tn(a, b):
    return lax.dot_general(a, b, (((0,), (0,)), ((), ())), preferred_element_type=F32)


def _mesh_pos():
    return lax.axis_index("x"), lax.axis_index("y"), lax.axis_index("c")


def _rms_proj_gather(x, norm_g, w_in, w_out, conv_w, order):
    s = x.shape[0]
    ts = min(TS_PROJ, s)
    nt = s // ts
    hin = w_in.shape[0] // 2
    hout = w_out.shape[0] // 2

    def body(order_ref, x_ref, g_ref, win_ref, wout_ref, cw_ref,
             h_ref, proj_ref, wing_ref, woutg_ref, cwg_ref, wg_ref, wob_ref, hs_ref, send_sems, recv_sems, local_sems):
        p = pl.program_id(0)
        t = pl.program_id(1)
        mx, my, c = _mesh_pos()
        me = 2 * mx + my
        chips = [(1 - mx, my), (mx, 1 - my), (1 - mx, 1 - my)]
        sibling = (mx, my, 1 - c)

        def remote(src, dst, sem, dev):
            return pltpu.make_async_remote_copy(
                src_ref=src, dst_ref=dst, send_sem=send_sems.at[sem], recv_sem=recv_sems.at[sem],
                device_id=dev, device_id_type=MESH)

        def w_in_part(blk, half):
            return wg_ref.at[blk, pl.ds(half * hin, hin)]

        def keep(blk, k):
            return pltpu.make_async_copy(wg_ref.at[blk], wing_ref.at[blk], local_sems.at[2 + k])

        def w_out_part(blk, half):
            return woutg_ref.at[blk, pl.ds(half * hout, hout)]

        def sends():
            out = []
            for j, (cx, cy) in enumerate(chips):
                blk = 2 * cx + cy
                out.append(remote(w_in_part(me, c), w_in_part(me, c), j, (cx, cy, c)))
                out.append(remote(w_in_part(blk, c), w_in_part(blk, c), 3 + j, sibling))
                out.append(remote(wob_ref.at[pl.ds(c * hout, hout)], w_out_part(me, c), 6 + j, (cx, cy, c)))
                out.append(remote(w_out_part(blk, c), w_out_part(blk, c), 9 + j, sibling))
                out.append(remote(cw_ref, cwg_ref.at[me], 12 + j, (cx, cy, c)))
            return out

        @pl.when(jnp.logical_and(p == 0, t == 0))
        def _():
            wg_ref[me] = win_ref[...].astype(BF16)
            wob_ref[...] = wout_ref[...].astype(BF16)
            keep(me, 0).start()
            mine = [pltpu.make_async_copy(wob_ref, woutg_ref.at[me], local_sems.at[0]),
                    pltpu.make_async_copy(cw_ref, cwg_ref.at[me], local_sems.at[1])]
            for cp in mine:
                cp.start()
            for k, cp in enumerate(sends()):
                if k % 5 == 4 or (k % 5 == 0 and k // 5 < 2):
                    cp.start()
            for cp in mine:
                cp.wait()

        for j, (cx, cy) in enumerate(chips):
            blk = 2 * cx + cy

            @pl.when(jnp.logical_and(p == j + 1, t == 0))
            def _():
                remote(w_in_part(blk, c), w_in_part(blk, c), j, (cx, cy, c)).wait_recv()
                remote(w_in_part(blk, c), w_in_part(blk, c), 3 + j, sibling).start()
                remote(w_in_part(blk, 1 - c), w_in_part(blk, 1 - c), 3 + j, sibling).wait_recv()
                keep(blk, j + 1).start()
                if j == 0:
                    sends()[5 * 2].start()
                if j == 1:
                    for jj in range(3):
                        sends()[5 * jj + 2].start()

        rows = pl.ds(pl.multiple_of(t * ts, ts), ts)

        @pl.when(p == 0)
        def _():
            xt = x_ref[...]
            r = lax.rsqrt(jnp.mean(xt * xt, axis=-1, keepdims=True) + EPS)
            hb = (xt * r * g_ref[...]).astype(BF16)
            h_ref[...] = hb
            hs_ref[rows, :] = hb

        proj_ref[...] = jnp.dot(hs_ref[rows, :], wg_ref[order_ref[p]], preferred_element_type=F32).astype(BF16)

        @pl.when(jnp.logical_and(p == N_SHARD - 1, t == nt - 1))
        def _():
            for j, (cx, cy) in enumerate(chips):
                blk = 2 * cx + cy
                remote(w_out_part(blk, c), w_out_part(blk, c), 6 + j, (cx, cy, c)).wait_recv()
                remote(w_out_part(blk, c), w_out_part(blk, c), 9 + j, sibling).start()
            for j, (cx, cy) in enumerate(chips):
                blk = 2 * cx + cy
                remote(w_out_part(blk, 1 - c), w_out_part(blk, 1 - c), 9 + j, sibling).wait_recv()
                remote(cw_ref, cwg_ref.at[blk], 12 + j, (cx, cy, c)).wait_recv()
            for cp in sends():
                cp.wait_send()
            keep(me, 0).wait()
            for j, (cx, cy) in enumerate(chips):
                keep(2 * cx + cy, j + 1).wait()

    hbm = pl.BlockSpec(memory_space=pl.ANY)
    return pl.pallas_call(
        body, name="rms_proj_gather",
        grid_spec=pltpu.PrefetchScalarGridSpec(
            num_scalar_prefetch=1, grid=(N_SHARD, nt),
            in_specs=[pl.BlockSpec((ts, D_MODEL), lambda p, t, o: (jnp.where(p == 0, t, nt - 1), 0)),
                      pl.BlockSpec((1, D_MODEL), lambda p, t, o: (0, 0)),
                      _whole_vmem(), _whole_vmem(), _whole_vmem()],
            out_specs=(pl.BlockSpec((ts, D_MODEL), lambda p, t, o: (jnp.where(p == 0, t, nt - 1), 0)),
                       pl.BlockSpec((ts, W_IN_COLS), lambda p, t, o: (t, o[p])),
                       hbm, hbm, hbm),
            scratch_shapes=[pltpu.VMEM((N_SHARD,) + w_in.shape, BF16), pltpu.VMEM(w_out.shape, BF16),
                            pltpu.VMEM((s, D_MODEL), BF16),
                            pltpu.SemaphoreType.DMA((15,)), pltpu.SemaphoreType.DMA((15,)),
                            pltpu.SemaphoreType.DMA((6,))]),
        out_shape=(jax.ShapeDtypeStruct((s, D_MODEL), BF16), jax.ShapeDtypeStruct((s, D_IN), BF16),
                   jax.ShapeDtypeStruct((N_SHARD,) + w_in.shape, BF16),
                   jax.ShapeDtypeStruct((N_SHARD,) + w_out.shape, BF16),
                   jax.ShapeDtypeStruct((N_SHARD,) + conv_w.shape, F32)),
        compiler_params=_cparams(("arbitrary", "arbitrary")),
    )(order, x, norm_g, w_in, w_out, conv_w)


def _halo_specs(ts, s, width, col_block):
    per = ts // HALO
    last = s // HALO - 1
    prev = pl.BlockSpec((HALO, width), lambda i: (jnp.maximum(i * per - 1, 0), col_block))
    nxt = pl.BlockSpec((HALO, width), lambda i: (jnp.minimum((i + 1) * per, last), col_block))
    return prev, nxt


def _layer_norm_stats(v):
    mu = jnp.mean(v, axis=-1, keepdims=True)
    vc = v - mu
    var = jnp.mean(vc * vc, axis=-1, keepdims=True)
    rstd = lax.rsqrt(var + EPS)
    return vc * rstd, rstd


def _fill_shifted(sh_ref, ext):
    n = ext.shape[0]
    sh_ref[0] = ext
    for r in range(1, SUB):
        sh_ref[r] = pltpu.roll(ext, n - r, axis=0)


def _fwd_fused(proj, x, target, cwg, conv_b, conv_ln_g, conv_ln_b, sgu_ln_g, sgu_ln_b, w_s, bs_t, woutg, final_g):
    s = proj.shape[0]
    ts = min(TS_FWD, s)
    nt = s // ts
    ra = min(ROWS_A, ts)

    def body(pm_ref, pp_ref, pn_ref, x_ref, t_ref, cw_ref, cb_ref, clg_ref, clb_ref, slg_ref, slb_ref, ws_ref,
             bst_ref, wo_ref, fg_ref,
             cv_ref, dx2_ref, dy_ref, gw_ref, gf_ref, se_ref,
             sh_ref, y_new, y_old, gw_acc, sem):
        i = pl.program_id(0)
        tile = jnp.minimum(i, nt - 1)
        keep_prev = (tile > 0).astype(F32)
        keep_next = (tile < nt - 1).astype(F32)

        @pl.when(i == 0)
        def _():
            y_old[...] = jnp.zeros_like(y_old)
            gw_acc[...] = jnp.zeros_like(gw_acc)
            gf_ref[...] = jnp.zeros_like(gf_ref)
            se_ref[...] = jnp.zeros_like(se_ref)

        for lb in range(C_BR // LANE):
            lanes = slice(lb * LANE, (lb + 1) * LANE)
            gates = slice(C_BR + lb * LANE, C_BR + (lb + 1) * LANE)

            def glu(ref):
                return ref[:, lanes].astype(F32) * _sigmoid(ref[:, gates].astype(F32))

            ext = jnp.concatenate([glu(pp_ref) * keep_prev, glu(pm_ref), glu(pn_ref) * keep_next], axis=0)
            _fill_shifted(sh_ref, ext)
            shard, off = divmod(lb * LANE, CONV_W_COLS)
            bias = cb_ref[:, lanes]

            def chunk(jc, carry):
                base = pl.multiple_of(jc * ra, ra)
                acc = jnp.zeros((ra, LANE), F32) + bias
                for k in range(KW):
                    o = k + 1
                    acc = acc + sh_ref[o % SUB, pl.ds(base + SUB * (o // SUB), ra), :] * cw_ref[
                        shard, k:k + 1, off:off + LANE]
                cv_ref[pl.ds(base, ra), lanes] = acc
                return carry

            lax.fori_loop(0, ts // ra, chunk, 0)

        n_ch = ts // CHUNK

        def conv_tail(ch):
            rows = slice(ch * CHUNK, (ch + 1) * CHUNK)
            lnh, _ = _layer_norm_stats(cv_ref[rows, :])
            ln = lnh * clg_ref[...] + clb_ref[...]
            gc = pm_ref[rows, 2 * C_BR:3 * C_BR].astype(F32)
            y_new[rows, :C_BR] = (ln * _sigmoid(ln) * gc * _sigmoid(gc)).astype(BF16)

        def gating(ch, heads):
            rows = slice(ch * CHUNK, (ch + 1) * CHUNK)
            vh, _ = _layer_norm_stats(pm_ref[rows, 4 * C_BR:5 * C_BR].astype(F32))
            vn = (vh * slg_ref[...] + slb_ref[...]).astype(BF16)
            for hd in heads:
                cols = slice(hd * HEAD_DIM, (hd + 1) * HEAD_DIM)
                mixed = jnp.dot(ws_ref[hd].astype(BF16), vn[:, cols], preferred_element_type=F32) + bst_ref[
                    :, hd:hd + 1]
                u = pm_ref[rows, 3 * C_BR + hd * HEAD_DIM:3 * C_BR + (hd + 1) * HEAD_DIM].astype(F32)
                gs = pm_ref[rows, 5 * C_BR + hd * HEAD_DIM:5 * C_BR + (hd + 1) * HEAD_DIM].astype(F32)
                y_new[rows, C_BR + hd * HEAD_DIM:C_BR + (hd + 1) * HEAD_DIM] = (
                    u * mixed * gs * _sigmoid(gs)).astype(BF16)

        pieces = [functools.partial(conv_tail, ch) for ch in range(n_ch)]
        pieces += [functools.partial(gating, ch, range(HEADS)) for ch in range(n_ch)]
        first = pieces[:len(pieces) // 2]
        second = pieces[len(pieces) // 2:]

        def interleave(matmuls, work):
            per = -(-len(work) // max(len(matmuls), 1))
            for m, f in enumerate(matmuls):
                f()
                for w in work[m * per:(m + 1) * per]:
                    w()
            for w in work[len(matmuls) * per:]:
                w()

        live = (i > 0).astype(F32)
        yt = y_old[...]
        acc = [x_ref[...]]

        def x2_dot(k):
            acc[0] = acc[0] + jnp.dot(yt[:, k * W_OUT_ROWS:(k + 1) * W_OUT_ROWS], wo_ref[k],
                                      preferred_element_type=F32)

        interleave([functools.partial(x2_dot, k) for k in range(N_SHARD)], first)
        x2 = acc[0]
        r2 = lax.rsqrt(jnp.mean(x2 * x2, axis=-1, keepdims=True) + EPS)
        n2 = x2 * r2
        g = fg_ref[...]
        diff = n2 * g - t_ref[...]
        se_ref[...] += live * _fold8(diff * diff)
        dout = diff * (1.0 / D_MODEL)
        gf_ref[...] += live * _fold8(dout * n2)
        dn = dout * g
        dx2 = r2 * (dn - n2 * jnp.mean(dn * n2, axis=-1, keepdims=True))
        dx2_ref[...] = dx2
        dxb = dx2.astype(BF16)

        def back_dots(k):
            rows = slice(k * W_OUT_ROWS, (k + 1) * W_OUT_ROWS)
            dy_ref[:, rows] = _dot_nt(dxb, wo_ref[k]).astype(BF16)
            gw_acc[rows, :] += _dot_tn(yt[:, rows], dxb)

        interleave([functools.partial(back_dots, k) for k in range(N_SHARD)], second)

        y_old[...] = y_new[...]

        @pl.when(i == nt)
        def _():
            gf_ref[...] = jnp.broadcast_to(jnp.sum(gf_ref[...], axis=0, keepdims=True), gf_ref.shape)
            out = pltpu.make_async_copy(gw_acc, gw_ref, sem)
            out.start()
            out.wait()

    per = ts // HALO
    last = s // HALO - 1
    cur = lambda i: jnp.minimum(i, nt - 1)
    before = lambda i: jnp.maximum(i - 1, 0)
    prev = pl.BlockSpec((HALO, 2 * C_BR), lambda i: (jnp.maximum(cur(i) * per - 1, 0), 0))
    nxt = pl.BlockSpec((HALO, 2 * C_BR), lambda i: (jnp.minimum((cur(i) + 1) * per, last), 0))
    row = pl.BlockSpec((1, C_BR), lambda i: (0, 0))
    old_tile = pl.BlockSpec((ts, D_MODEL), lambda i: (before(i), 0))
    acc8 = pl.BlockSpec((SUB, D_MODEL), lambda i: (0, 0))
    return pl.pallas_call(
        body, name="fwd_fused",
        grid=(nt + 1,),
        out_shape=(jax.ShapeDtypeStruct((s, C_BR), F32), jax.ShapeDtypeStruct((s, D_MODEL), F32),
                   jax.ShapeDtypeStruct((s, 2 * C_BR), BF16), jax.ShapeDtypeStruct((2 * C_BR, D_MODEL), F32),
                   jax.ShapeDtypeStruct((SUB, D_MODEL), F32), jax.ShapeDtypeStruct((SUB, D_MODEL), F32)),
        in_specs=[pl.BlockSpec((ts, D_IN), lambda i: (cur(i), 0)), prev, nxt, old_tile, old_tile,
                  _whole_vmem(), row, row, row, row, row, _whole_vmem(), _whole_vmem(), _whole_vmem(), row],
        out_specs=(pl.BlockSpec((ts, C_BR), lambda i: (cur(i), 0)), old_tile,
                   pl.BlockSpec((ts, 2 * C_BR), lambda i: (before(i), 0)),
                   pl.BlockSpec(memory_space=pl.ANY), acc8, acc8),
        scratch_shapes=[pltpu.VMEM((SUB, ts + 2 * HALO, LANE), F32),
                        pltpu.VMEM((ts, 2 * C_BR), BF16),
                        pltpu.VMEM((ts, 2 * C_BR), BF16),
                        pltpu.VMEM((2 * C_BR, D_MODEL), F32),
                        pltpu.SemaphoreType.DMA(())],
        compiler_params=_cparams(("arbitrary",), 60 * 1024 * 1024),
    )(proj, proj, proj, x, target, cwg, conv_b, conv_ln_g, conv_ln_b, sgu_ln_g, sgu_ln_b, w_s, bs_t, woutg, final_g)


def _dsilu(v, sg):
    return sg * (1.0 + v * (1.0 - sg))


def _branch_bwd(proj, dy, cv, cwg, conv_ln_g, conv_ln_b, sgu_ln_g, sgu_ln_b, w_s, ws_t, bs_t, gf8, se8, dep):
    s = proj.shape[0]
    ts = min(TS_BWD, s)
    nt = s // ts
    ra = min(ROWS_A, ts)
    rb = min(ROWS_B, ts)
    te = ts + 2 * HALO

    def body(pm_ref, dym_ref, cvm_ref, gcp_ref, gcn_ref, dyp_ref, dyn_ref, cvp_ref, cvn_ref,
             cw_ref, clg_ref, clb_ref, slg_ref, slb_ref, ws_ref, wst_ref, bst_ref, gf_ref, se_ref, dep_ref,
             dp_ref, small_ref, gws_ref,
             sh_ref, glu_ref, dgl_ref, dcv_ref, acc_ref, gcw_ref, gbs_ref):
        i = pl.program_id(0)

        @pl.when(i == 0)
        def _():
            acc_ref[...] = jnp.zeros_like(acc_ref)
            gcw_ref[...] = jnp.zeros_like(gcw_ref)
            gbs_ref[...] = jnp.zeros_like(gbs_ref)
            gws_ref[...] = jnp.zeros_like(gws_ref)

        def ext(prev_ref, main, next_ref):
            return jnp.concatenate([prev_ref[...].astype(F32), main, next_ref[...].astype(F32)], axis=0)

        main = slice(HALO, HALO + ts)

        cv_e = ext(cvp_ref, cvm_ref[...], cvn_ref)
        gc_e = ext(gcp_ref, pm_ref[:, 2 * C_BR:3 * C_BR].astype(F32), gcn_ref)
        dyc_e = ext(dyp_ref, dym_ref[:, :C_BR].astype(F32), dyn_ref)
        lnh, rstd = _layer_norm_stats(cv_e)
        clg = clg_ref[...]
        ln = lnh * clg + clb_ref[...]
        sg_ln = _sigmoid(ln)
        sg_gc = _sigmoid(gc_e)
        d_ln = dyc_e * gc_e * sg_gc * _dsilu(ln, sg_ln)
        dp_ref[:, 2 * C_BR:3 * C_BR] = (
            dyc_e[main] * ln[main] * sg_ln[main] * _dsilu(gc_e[main], sg_gc[main])).astype(BF16)
        dlnh = d_ln * clg
        d_cv = rstd * (dlnh - jnp.mean(dlnh, axis=-1, keepdims=True)
                       - lnh * jnp.mean(dlnh * lnh, axis=-1, keepdims=True))
        row = lax.broadcasted_iota(jnp.int32, (te, 1), 0)
        valid = jnp.logical_and(jnp.logical_or(row >= HALO, i > 0),
                                jnp.logical_or(row < HALO + ts, i < nt - 1))
        d_cv = jnp.where(valid, d_cv, 0.0)
        dcv_ref[...] = d_cv
        acc_ref[0:8, :] += _fold8(d_cv[main])
        acc_ref[8:16, :] += _fold8(d_ln[main] * lnh[main])
        acc_ref[16:24, :] += _fold8(d_ln[main])

        for lb in range(C_BR // LANE):
            lanes = slice(lb * LANE, (lb + 1) * LANE)
            gates = slice(C_BR + lb * LANE, C_BR + (lb + 1) * LANE)
            shard, off = divmod(lb * LANE, CONV_W_COLS)
            av = pm_ref[:, lanes].astype(F32)
            sg = _sigmoid(pm_ref[:, gates].astype(F32))
            glu_ref[...] = av * sg
            _fill_shifted(sh_ref, dcv_ref[:, lanes])

            def chunk_a(jc, carry):
                base = pl.multiple_of(jc * ra, ra)
                acc = jnp.zeros((ra, LANE), F32)
                for j in range(KW):
                    o = j + 1
                    acc = acc + sh_ref[o % SUB, pl.ds(base + SUB * (o // SUB), ra), :] * cw_ref[
                        shard, KW - 1 - j:KW - j, off:off + LANE]
                dgl_ref[pl.ds(base, ra), :] = acc
                return carry

            lax.fori_loop(0, ts // ra, chunk_a, 0)

            def chunk_b(jc, accs):
                base = pl.multiple_of(jc * rb, rb)
                g = glu_ref[pl.ds(base, rb), :]
                out = []
                for j in range(KW):
                    o = j + 1
                    d = sh_ref[o % SUB, pl.ds(base + SUB * (o // SUB), rb), :]
                    out.append(accs[j] + _fold8(g * d))
                return tuple(out)

            accs = lax.fori_loop(0, ts // rb, chunk_b, tuple(jnp.zeros((SUB, LANE), F32) for _ in range(KW)))
            for j in range(KW):
                gcw_ref[j * SUB:(j + 1) * SUB, lanes] += accs[j]

            dglu = dgl_ref[...]
            dp_ref[:, lanes] = (dglu * sg).astype(BF16)
            dp_ref[:, gates] = (dglu * av * sg * (1.0 - sg)).astype(BF16)

        vh, vrstd = _layer_norm_stats(pm_ref[:, 4 * C_BR:5 * C_BR].astype(F32))
        slg = slg_ref[...]
        vn = (vh * slg + slb_ref[...]).astype(BF16)
        for hd in range(HEADS):
            w_h = ws_ref[hd].astype(BF16)
            wt_h = wst_ref[hd].astype(BF16)
            b_h = bst_ref[:, hd:hd + 1]
            cols = slice(hd * HEAD_DIM, (hd + 1) * HEAD_DIM)
            gws_h = jnp.zeros((CHUNK, CHUNK), F32)
            gbs_h = jnp.zeros((CHUNK, HEAD_DIM), F32)
            for ch in range(ts // CHUNK):
                rows = slice(ch * CHUNK, (ch + 1) * CHUNK)
                vn_b = vn[rows, cols]
                mixed = jnp.dot(w_h, vn_b, preferred_element_type=F32) + b_h
                u = pm_ref[rows, 3 * C_BR + hd * HEAD_DIM:3 * C_BR + (hd + 1) * HEAD_DIM].astype(F32)
                gs = pm_ref[rows, 5 * C_BR + hd * HEAD_DIM:5 * C_BR + (hd + 1) * HEAD_DIM].astype(F32)
                dys = dym_ref[rows, C_BR + hd * HEAD_DIM:C_BR + (hd + 1) * HEAD_DIM].astype(F32)
                sg_gs = _sigmoid(gs)
                silu_gs = gs * sg_gs
                dp_ref[rows, 3 * C_BR + hd * HEAD_DIM:3 * C_BR + (hd + 1) * HEAD_DIM] = (
                    dys * mixed * silu_gs).astype(BF16)
                dp_ref[rows, 5 * C_BR + hd * HEAD_DIM:5 * C_BR + (hd + 1) * HEAD_DIM] = (
                    dys * u * mixed * _dsilu(gs, sg_gs)).astype(BF16)
                d_mixed = dys * u * silu_gs
                dm_b = d_mixed.astype(BF16)
                gws_h = gws_h + _dot_nt(dm_b, vn_b)
                gbs_h = gbs_h + d_mixed
                dcv_ref[HALO + ch * CHUNK:HALO + (ch + 1) * CHUNK, cols] = jnp.dot(
                    wt_h, dm_b, preferred_element_type=F32)
            gws_ref[hd] += gws_h
            gbs_ref[:, cols] += gbs_h
        d_vn = dcv_ref[main, :]
        acc_ref[24:32, :] += _fold8(d_vn * vh)
        acc_ref[32:40, :] += _fold8(d_vn)
        dvh = d_vn * slg
        dp_ref[:, 4 * C_BR:5 * C_BR] = (vrstd * (
            dvh - jnp.mean(dvh, axis=-1, keepdims=True)
            - vh * jnp.mean(dvh * vh, axis=-1, keepdims=True))).astype(BF16)

        @pl.when(i == nt - 1)
        def _():
            small_ref[...] = jnp.zeros_like(small_ref)
            for a in range(5):
                small_ref[1 + a:2 + a, :] = jnp.sum(acc_ref[a * SUB:(a + 1) * SUB, :], axis=0, keepdims=True)
            ones = jnp.ones((SUB, HEAD_DIM), F32)
            for hd in range(HEADS):
                cols = slice(hd * HEAD_DIM, (hd + 1) * HEAD_DIM)
                rowsum = lax.dot_general(ones, gbs_ref[:, cols], (((1,), (1,)), ((), ())),
                                         precision=lax.Precision.HIGHEST, preferred_element_type=F32)
                small_ref[6:7, cols] = rowsum[0:1, :]
            small_ref[7:8, :] = gf_ref[0:1, :]
            small_ref[0:1, :] = jnp.sum(se_ref[...], axis=0, keepdims=True)
            for k in range(KW):
                j = KW - 1 - k
                small_ref[8 + k:9 + k, :] = jnp.sum(gcw_ref[j * SUB:(j + 1) * SUB, :], axis=0, keepdims=True)

    gc_prev, gc_next = _halo_specs(ts, s, C_BR, 2)
    lo_prev, lo_next = _halo_specs(ts, s, C_BR, 0)
    row = pl.BlockSpec((1, C_BR), lambda i: (0, 0))
    return pl.pallas_call(
        body, name="branch_bwd",
        grid=(nt,),
        out_shape=(jax.ShapeDtypeStruct((s, D_IN), BF16), jax.ShapeDtypeStruct((40, C_BR), F32),
                   jax.ShapeDtypeStruct((HEADS, CHUNK, CHUNK), F32)),
        in_specs=[pl.BlockSpec((ts, D_IN), lambda i: (i, 0)),
                  pl.BlockSpec((ts, 2 * C_BR), lambda i: (i, 0)),
                  pl.BlockSpec((ts, C_BR), lambda i: (i, 0)),
                  gc_prev, gc_next, lo_prev, lo_next, lo_prev, lo_next,
                  _whole_vmem(), row, row, row, row, _whole_vmem(), _whole_vmem(), _whole_vmem(),
                  _whole_vmem(), _whole_vmem(), _whole_vmem()],
        out_specs=(pl.BlockSpec((ts, D_IN), lambda i: (i, 0)),
                   pl.BlockSpec((40, C_BR), lambda i: (0, 0)),
                   pl.BlockSpec((HEADS, CHUNK, CHUNK), lambda i: (0, 0, 0))),
        scratch_shapes=[pltpu.VMEM((SUB, te, LANE), F32),
                        pltpu.VMEM((ts, LANE), F32),
                        pltpu.VMEM((ts, LANE), F32),
                        pltpu.VMEM((te, C_BR), F32),
                        pltpu.VMEM((5 * SUB, C_BR), F32),
                        pltpu.VMEM((KW * SUB, C_BR), F32),
                        pltpu.VMEM((CHUNK, C_BR), F32)],
        compiler_params=_cparams(("arbitrary",)),
    )(proj, dy, cv, proj, proj, dy, dy, cv, cv,
      cwg, conv_ln_g, conv_ln_b, sgu_ln_g, sgu_ln_b, w_s, ws_t, bs_t, gf8, se8, dep)


def _in_bwd(dproj, x, dx2, norm_g, wing, dep, first_tile, n_tiles, gx_prev=None, gn_prev=None):
    s = x.shape[0]
    ts = min(TS_INB, s)

    def body(*refs):
        dp_ref, x_ref, dx2_ref, g_ref, w_ref = refs[:5]
        gx_ref, gn_ref, acc_ref = refs[-3:]
        i = pl.program_id(0)

        @pl.when(i == 0)
        def _():
            acc_ref[...] = jnp.zeros_like(acc_ref)

        dh = jnp.zeros((ts, D_MODEL), F32)
        for k in range(N_SHARD):
            dh = dh + _dot_nt(dp_ref[:, k * W_IN_COLS:(k + 1) * W_IN_COLS], w_ref[k])
        xt = x_ref[...]
        r = lax.rsqrt(jnp.mean(xt * xt, axis=-1, keepdims=True) + EPS)
        n = xt * r
        acc_ref[...] += _fold8(dh * n)
        dn = dh * g_ref[...]
        gx_ref[...] = dx2_ref[...] + r * (dn - n * jnp.mean(dn * n, axis=-1, keepdims=True))

        @pl.when(i == n_tiles - 1)
        def _():
            total = jnp.broadcast_to(jnp.sum(acc_ref[...], axis=0, keepdims=True), gn_ref.shape)
            if gn_prev is not None:
                total = total + refs[7][...]
            gn_ref[...] = total

    tile = pl.BlockSpec((ts, D_MODEL), lambda i: (i + first_tile, 0))
    in_specs = [pl.BlockSpec((ts, D_IN), lambda i: (i + first_tile, 0)), tile, tile,
                pl.BlockSpec((1, D_MODEL), lambda i: (0, 0)), _whole_vmem(), _whole_vmem()]
    operands = [dproj, x, dx2, norm_g, wing, dep]
    aliases = {}
    if gx_prev is not None:
        in_specs += [pl.BlockSpec(memory_space=pl.ANY), _whole_vmem()]
        operands += [gx_prev, gn_prev]
        aliases = {6: 0}
    return pl.pallas_call(
        body, name="in_bwd_%d" % first_tile,
        grid=(n_tiles,),
        out_shape=(jax.ShapeDtypeStruct((s, D_MODEL), F32), jax.ShapeDtypeStruct((SUB, D_MODEL), F32)),
        in_specs=in_specs,
        out_specs=(tile, pl.BlockSpec((SUB, D_MODEL), lambda i: (0, 0))),
        scratch_shapes=[pltpu.VMEM((SUB, D_MODEL), F32)],
        input_output_aliases=aliases,
        compiler_params=_cparams(("arbitrary",)),
    )(*operands)


def _grad_w_in(h, dproj, dep):
    s = h.shape[0]
    tk = min(TK_GW, s)
    half = D_MODEL // 2

    def body(h_ref, dp_ref, dep_ref, o_ref):
        @pl.when(pl.program_id(1) == 0)
        def _():
            o_ref[...] = jnp.zeros_like(o_ref)

        o_ref[0] += _dot_tn(h_ref[...], dp_ref[...]).reshape(2, half, W_IN_COLS)

    return pl.pallas_call(
        body, name="grad_w_in",
        grid=(N_SHARD, s // tk),
        out_shape=jax.ShapeDtypeStruct((N_SHARD, 2, half, W_IN_COLS), F32),
        in_specs=[pl.BlockSpec((tk, D_MODEL), lambda k, t: (t, 0)),
                  pl.BlockSpec((tk, W_IN_COLS), lambda k, t: (t, k)), _whole_vmem()],
        out_specs=pl.BlockSpec((1, 2, half, W_IN_COLS), lambda k, t: (k, 0, 0, 0)),
        compiler_params=_cparams(("parallel", "arbitrary")),
    )(h, dproj, dep)


HBM_SPEC = pl.BlockSpec(memory_space=pltpu.HBM)
SEM_SPEC = pl.BlockSpec(memory_space=pltpu.SEMAPHORE)
SIDE_EFFECT = pltpu.SideEffectType.DATAFLOW_SIDE_EFFECTING


def _remote_copies(plan, bufs, send_sems, recv_sems):
    x, y, c = _mesh_pos()
    return [pltpu.make_async_remote_copy(src_ref=src, dst_ref=dst, send_sem=send_sems.at[k],
                                         recv_sem=recv_sems.at[k], device_id=dev, device_id_type=MESH)
            for k, (src, dst, dev) in enumerate(plan(x, y, c, *bufs))]


def _start_copies(name, bufs, n_copies, plan):
    n = len(bufs)

    def body(*refs):
        for cp in _remote_copies(plan, refs[:n], refs[n], refs[n + 1]):
            cp.start()
        refs[-1][...] = jnp.zeros_like(refs[-1])

    outs = pl.pallas_call(
        body, name=name,
        out_shape=(pltpu.SemaphoreType.DMA((n_copies,)), pltpu.SemaphoreType.DMA((n_copies,)),
                   *[pltpu.HBM(b.shape, b.dtype) for b in bufs], jax.ShapeDtypeStruct((SUB, LANE), F32)),
        in_specs=[HBM_SPEC] * n,
        out_specs=(SEM_SPEC, SEM_SPEC, *[HBM_SPEC] * n, _whole_vmem()),
        input_output_aliases={i: 2 + i for i in range(n)},
        compiler_params=pltpu.CompilerParams(has_side_effects=SIDE_EFFECT),
    )(*[pltpu.with_memory_space_constraint(b, pltpu.HBM) for b in bufs])
    return outs[0], outs[1], list(outs[2:2 + n]), outs[-1]


def _wait_copies(name, send_sems, recv_sems, bufs, plan, after):
    n = len(bufs)

    def body(*refs):
        for cp in _remote_copies(plan, refs[:n], refs[n], refs[n + 1]):
            cp.wait_send()
            cp.wait_recv()

    outs = pl.pallas_call(
        body, name=name,
        out_shape=tuple(pltpu.HBM(b.shape, b.dtype) for b in bufs),
        in_specs=[HBM_SPEC] * n + [SEM_SPEC, SEM_SPEC, pl.BlockSpec(memory_space=pl.ANY)],
        out_specs=(HBM_SPEC,) * n,
        input_output_aliases={i: i for i in range(n)},
        compiler_params=pltpu.CompilerParams(has_side_effects=SIDE_EFFECT),
    )(*bufs, send_sems, recv_sems, after)
    return list(outs)


def _landing(shape, dtype):
    return lax.empty(shape, dtype)


def _plan_pair_exchange(x, y, c, g, r):
    return [(g.at[k, 1 - c], r.at[k], (x, y, 1 - c)) for k in range(N_SHARD)]


def _plan_chip_exchange(x, y, c, a, r):
    chips = [(1 - x, y), (x, 1 - y), (1 - x, 1 - y)]
    return [(a.at[2 * cx + cy], r.at[j], (cx, cy, c)) for j, (cx, cy) in enumerate(chips)]


def _plan_pair_gather(x, y, c, f):
    return [(f.at[c], f.at[c], (x, y, 1 - c))]


def _plan_all_gather(x, y, c, own, land):
    me = 4 * x + 2 * y + c
    flip = lambda v, bit: 1 - v if bit else v
    return [(own, land.at[me], (flip(x, m >> 2 & 1), flip(y, m >> 1 & 1), flip(c, m & 1))) for m in range(1, 8)]


def _add_pair(g, r, pos, name):
    _, _, rows, cols = g.shape
    tr = min(256, rows)

    def body(pos_ref, g_ref, r_ref, o_ref, ob_ref):
        v = g_ref[0] + r_ref[...]
        o_ref[...] = v
        ob_ref[...] = v.astype(BF16)

    blk = pl.BlockSpec((1, tr, cols), lambda k, t, pos_ref: (k, t, 0))
    return pl.pallas_call(
        body, name=name,
        grid_spec=pltpu.PrefetchScalarGridSpec(
            num_scalar_prefetch=1, grid=(N_SHARD, rows // tr),
            in_specs=[pl.BlockSpec((1, 1, tr, cols), lambda k, t, pos_ref: (k, pos_ref[1], t, 0)), blk],
            out_specs=(blk, blk)),
        out_shape=(jax.ShapeDtypeStruct((N_SHARD, rows, cols), F32), jax.ShapeDtypeStruct((N_SHARD, rows, cols), BF16)),
        compiler_params=_cparams(("parallel", "parallel")),
    )(pos, g, r)


def _add_chips(a, r, pos, name):
    _, rows, cols = a.shape
    tr = min(256, rows)

    def body(pos_ref, a_ref, r_ref, o_ref):
        o_ref[0] = ((a_ref[0] + r_ref[0].astype(F32)) + r_ref[1].astype(F32)) + r_ref[2].astype(F32)

    return pl.pallas_call(
        body, name=name,
        grid_spec=pltpu.PrefetchScalarGridSpec(
            num_scalar_prefetch=1, grid=(rows // tr,),
            in_specs=[pl.BlockSpec((1, tr, cols), lambda t, pos_ref: (pos_ref[0], t, 0)),
                      pl.BlockSpec((3, tr, cols), lambda t, pos_ref: (0, t, 0))],
            out_specs=pl.BlockSpec((1, tr, cols), lambda t, pos_ref: (pos_ref[1], t, 0))),
        out_shape=jax.ShapeDtypeStruct((2, rows, cols), F32),
        compiler_params=_cparams(("parallel",)),
    )(pos, a, r)


def _sum_slots(pos_ref, own, land_ref, rows):
    me = pos_ref[2]
    total = None
    for d in range(8):
        term = jnp.where(me == d, own, land_ref[d] if rows is None else land_ref[d, rows, :])
        total = term if total is None else total + term
    return total


def _sum_small(pos, small, small_land, gws, gws_land):
    def body(pos_ref, sm_ref, sml_ref, gw_ref, gwl_ref, o_sm, o_gw):
        o_sm[...] = _sum_slots(pos_ref, sm_ref[...], sml_ref, None)
        o_gw[...] = _sum_slots(pos_ref, gw_ref[...], gwl_ref, None)

    return pl.pallas_call(
        body, name="sum_small",
        grid_spec=pltpu.PrefetchScalarGridSpec(
            num_scalar_prefetch=1, grid=(1,),
            in_specs=[_whole_vmem()] * 4, out_specs=[_whole_vmem()] * 2),
        out_shape=[jax.ShapeDtypeStruct(small.shape, F32), jax.ShapeDtypeStruct(gws.shape, F32)],
        compiler_params=_cparams(("arbitrary",)),
    )(pos, small, small_land, gws, gws_land)


def _adamw_math(w, g, m, v):
    m = ADAM_B1 * m + (1.0 - ADAM_B1) * g
    v = ADAM_B2 * v + (1.0 - ADAM_B2) * (g * g)
    m_hat = m / (1.0 - ADAM_B1 ** ADAM_STEP)
    v_hat = v / (1.0 - ADAM_B2 ** ADAM_STEP)
    delta = -ADAM_LR * (m_hat / (jnp.sqrt(v_hat) + ADAM_EPS) + ADAM_WD * w)
    return delta, m, v


def _adamw_large(w, g, m, v, dep, name):
    rows, cols = w.shape
    tr = min(256, rows)

    def body(w_ref, g_ref, m_ref, v_ref, dep_ref, d_ref, nm_ref, nv_ref):
        d_ref[...], nm_ref[...], nv_ref[...] = _adamw_math(w_ref[...], g_ref[...], m_ref[...], v_ref[...])

    tile = pl.BlockSpec((tr, cols), lambda t: (t, 0))
    return pl.pallas_call(
        body, name=name,
        grid=(rows // tr,),
        out_shape=(jax.ShapeDtypeStruct(w.shape, F32),) * 3,
        in_specs=[tile] * 4 + [_whole_vmem()], out_specs=(tile,) * 3,
        compiler_params=_cparams(("parallel",)),
    )(w, g, m, v, dep)


_ROW_OF = {"conv_b": 1, "conv_ln_g": 2, "conv_ln_b": 3, "sgu_ln_g": 4, "sgu_ln_b": 5, "b_s": 6, "final_g": 7}
_CONV_W_ROW = 8
_VECTORS = ("norm_g", "conv_b", "conv_ln_g", "conv_ln_b", "sgu_ln_g", "sgu_ln_b", "b_s", "final_g")


def _adamw_small(p, q, gn8, gn_land, pos, vectors, conv_w, w_s):
    names = list(_VECTORS)

    def body(pos_ref, p_ref, q_ref, gn_ref, gnl_ref, *refs):
        n_in = 3 * (len(names) + 2)
        ins, outs = refs[:n_in], refs[n_in:]
        me = pos_ref[0]
        for a, name in enumerate(names + ["conv_w", "w_s"]):
            w_ref, m_ref, v_ref = ins[3 * a:3 * a + 3]
            if name == "conv_w":
                g = jnp.zeros((KW, CONV_W_COLS), F32)
                for k in range(N_SHARD):
                    blk = p_ref[_CONV_W_ROW:_CONV_W_ROW + KW, k * CONV_W_COLS:(k + 1) * CONV_W_COLS]
                    g = jnp.where(me == k, blk, g)
            elif name == "w_s":
                g = q_ref[...]
            elif name == "norm_g":
                g = _sum_slots(pos_ref, gn_ref[0:1, :], gnl_ref, slice(0, 1))
            else:
                g = p_ref[_ROW_OF[name]:_ROW_OF[name] + 1, :]
            delta, nm, nv = _adamw_math(w_ref[...], g, m_ref[...], v_ref[...])
            for o_ref, val in zip(outs[4 * a:4 * a + 4], (g, delta, nm, nv)):
                o_ref[...] = val

    operands, shapes = [], []
    for name in names:
        operands += list(vectors[name])
        shapes += [jax.ShapeDtypeStruct((1, C_BR), F32)] * 4
    operands += list(conv_w)
    shapes += [jax.ShapeDtypeStruct((KW, CONV_W_COLS), F32)] * 4
    operands += list(w_s)
    shapes += [jax.ShapeDtypeStruct(q.shape, F32)] * 4
    outs = pl.pallas_call(
        body, name="adamw_small",
        grid_spec=pltpu.PrefetchScalarGridSpec(
            num_scalar_prefetch=1, grid=(1,),
            in_specs=[_whole_vmem()] * (4 + len(operands)),
            out_specs=[_whole_vmem()] * len(shapes)),
        out_shape=shapes,
        compiler_params=_cparams(("arbitrary",)),
    )(pos, p, q, gn8, gn_land, *operands)
    return {name: tuple(outs[4 * a:4 * a + 4]) for a, name in enumerate(names + ["conv_w", "w_s"])}


def kernel(x, norm_g, w_in, conv_w, conv_b, conv_ln_g, conv_ln_b, sgu_ln_g, sgu_ln_b, w_s, b_s, w_out, final_g, loss_target, m_norm_g, m_w_in, m_conv_w, m_conv_b, m_conv_ln_g, m_conv_ln_b, m_sgu_ln_g, m_sgu_ln_b, m_w_s, m_b_s, m_w_out, m_final_g, v_norm_g, v_w_in, v_conv_w, v_conv_b, v_conv_ln_g, v_conv_ln_b, v_sgu_ln_g, v_sgu_ln_b, v_w_s, v_b_s, v_w_out, v_final_g):
    xi, yi, ci = _mesh_pos()
    pos = jnp.stack([2 * xi + yi, ci, 4 * xi + 2 * yi + ci]).astype(jnp.int32)

    x2d = x[0]
    tgt = loss_target[0]
    fg = final_g.reshape(1, D_MODEL)
    ws3 = w_s[0]
    ws_t = jnp.swapaxes(ws3, 1, 2)
    bs_t = jnp.transpose(b_s[0])

    chip = 2 * xi + yi
    order = jnp.stack([chip, 2 * (1 - xi) + yi, 2 * xi + 1 - yi, 2 * (1 - xi) + 1 - yi]).astype(jnp.int32)
    h, proj, wing, woutg, cwg = _rms_proj_gather(x2d, norm_g, w_in[0], w_out[0], conv_w[0], order)
    cv, dx2, dy, gwout, gf8, se8 = _fwd_fused(proj, x2d, tgt, cwg, conv_b, conv_ln_g, conv_ln_b, sgu_ln_g,
                                              sgu_ln_b, ws3, bs_t, woutg, fg)
    in_rows, out_rows = D_MODEL // 2, W_OUT_ROWS // 2
    gwout4 = gwout.reshape(N_SHARD, 2, out_rows, D_MODEL)
    ss, rs, (gwout4, r1_out), tok = _start_copies(
        "start_pair_exchange_w_out", [gwout4, _landing((N_SHARD, out_rows, D_MODEL), F32)], N_SHARD,
        _plan_pair_exchange)
    dproj, small, gws3 = _branch_bwd(proj, dy, cv, cwg, conv_ln_g, conv_ln_b, sgu_ln_g, sgu_ln_b, ws3, ws_t, bs_t,
                                     gf8, se8, tok)
    gws = gws3.reshape(HEADS * CHUNK, CHUNK)
    gwout4, r1_out = _wait_copies("wait_pair_exchange_w_out", ss, rs, [gwout4, r1_out], _plan_pair_exchange, dproj)
    a_out, a_out_bf = _add_pair(gwout4, r1_out, pos, "add_pair_w_out")

    def plan_b(x, y, c, a, r, sm, sml, gw, gwl):
        return (_plan_chip_exchange(x, y, c, a, r) + _plan_all_gather(x, y, c, sm, sml)
                + _plan_all_gather(x, y, c, gw, gwl))

    ss, rs, bufs_b, tok = _start_copies(
        "start_chip_exchange_w_out",
        [a_out_bf, _landing((3, out_rows, D_MODEL), BF16), small, _landing((8,) + small.shape, F32),
         gws, _landing((8,) + gws.shape, F32)], 3 + 7 + 7, plan_b)
    gwin = _grad_w_in(h, dproj, tok)
    ss_c, rs_c, (gwin, r1_in), tok = _start_copies(
        "start_pair_exchange_w_in", [gwin, _landing((N_SHARD, in_rows, W_IN_COLS), F32)], N_SHARD,
        _plan_pair_exchange)
    nt = x2d.shape[0] // min(TS_INB, x2d.shape[0])
    gx_a, gn_a = _in_bwd(dproj, x2d, dx2, norm_g, wing, tok, 0, nt // 2)

    gwin, r1_in = _wait_copies("wait_pair_exchange_w_in", ss_c, rs_c, [gwin, r1_in], _plan_pair_exchange, gx_a)
    a_in, a_in_bf = _add_pair(gwin, r1_in, pos, "add_pair_w_in")
    a_out_bf, r2_out, small, small_land, gws, gws_land = _wait_copies(
        "wait_chip_exchange_w_out", ss, rs, bufs_b, plan_b, gx_a)
    f_out = _add_chips(a_out, r2_out, pos, "add_chips_w_out")
    p, q = _sum_small(pos, small, small_land, gws, gws_land)
    loss = (0.5 / D_MODEL) * jnp.sum(p[0])

    def plan_d(x, y, c, a, r, f):
        return _plan_chip_exchange(x, y, c, a, r) + _plan_pair_gather(x, y, c, f)

    ss, rs, bufs_d, tok = _start_copies(
        "start_chip_exchange_w_in", [a_in_bf, _landing((3, in_rows, W_IN_COLS), BF16), f_out], 3 + 1, plan_d)
    grad_x, gn8 = _in_bwd(dproj, x2d, dx2, norm_g, wing, tok, nt // 2, nt - nt // 2, gx_a, gn_a)
    a_in_bf, r2_in, f_out = _wait_copies("wait_chip_exchange_w_in", ss, rs, bufs_d, plan_d, grad_x)
    f_in = _add_chips(a_in, r2_in, pos, "add_chips_w_in")

    def plan_e(x, y, c, f, gn, gnl):
        return _plan_pair_gather(x, y, c, f) + _plan_all_gather(x, y, c, gn, gnl)

    ss, rs, bufs_e, tok = _start_copies(
        "start_pair_gather_w_in", [f_in, gn8, _landing((8,) + gn8.shape, F32)], 1 + 7, plan_e)
    g_w_out = f_out.reshape(W_OUT_ROWS, D_MODEL)
    d_w_out, nm_w_out, nv_w_out = _adamw_large(w_out[0], g_w_out, m_w_out[0], v_w_out[0], tok, "adamw_w_out")
    f_in, gn8, gn_land = _wait_copies("wait_pair_gather_w_in", ss, rs, bufs_e, plan_e, d_w_out)
    g_w_in = f_in.reshape(D_MODEL, W_IN_COLS)
    d_w_in, nm_w_in, nv_w_in = _adamw_large(w_in[0], g_w_in, m_w_in[0], v_w_in[0], tok, "adamw_w_in")

    flat = lambda a: a.reshape(1, C_BR)
    vectors = {
        "norm_g": (norm_g, m_norm_g, v_norm_g),
        "conv_b": (conv_b, m_conv_b, v_conv_b),
        "conv_ln_g": (conv_ln_g, m_conv_ln_g, v_conv_ln_g),
        "conv_ln_b": (conv_ln_b, m_conv_ln_b, v_conv_ln_b),
        "sgu_ln_g": (sgu_ln_g, m_sgu_ln_g, v_sgu_ln_g),
        "sgu_ln_b": (sgu_ln_b, m_sgu_ln_b, v_sgu_ln_b),
        "b_s": (flat(b_s), flat(m_b_s), flat(v_b_s)),
        "final_g": (flat(final_g), flat(m_final_g), flat(v_final_g)),
    }
    flat_ws = lambda a: a.reshape(HEADS * CHUNK, CHUNK)
    res = _adamw_small(p, q, gn8, gn_land, pos, vectors, (conv_w[0], m_conv_w[0], v_conv_w[0]),
                       (flat_ws(w_s), flat_ws(m_w_s), flat_ws(v_w_s)))
    res["w_in"] = tuple(a[None] for a in (g_w_in, d_w_in, nm_w_in, nv_w_in))
    res["w_out"] = tuple(a[None] for a in (g_w_out, d_w_out, nm_w_out, nv_w_out))
    res["conv_w"] = tuple(a[None] for a in res["conv_w"])
    res["w_s"] = tuple(a.reshape(w_s.shape) for a in res["w_s"])
    res["b_s"] = tuple(a.reshape(b_s.shape) for a in res["b_s"])
    res["final_g"] = tuple(a.reshape(final_g.shape) for a in res["final_g"])

    order = ("norm_g", "w_in", "conv_w", "conv_b", "conv_ln_g", "conv_ln_b", "sgu_ln_g", "sgu_ln_b",
             "w_s", "b_s", "w_out", "final_g")
    out = [loss, grad_x[None]]
    for part in range(4):
        out += [res[name][part] for name in order]
    return tuple(out)
```

```python
import functools

import jax
import jax.numpy as jnp
from jax import lax
from jax.experimental import pallas as pl
from jax.experimental.pallas import tpu as pltpu

F32 = jnp.float32
BF16 = jnp.bfloat16
MESH = pl.DeviceIdType.MESH

EPS = 1e-6
D_MODEL = 1024
C_BR = 1024
D_IN = 6 * C_BR
N_SHARD = 4
W_IN_COLS = D_IN // N_SHARD
W_OUT_ROWS = 2 * C_BR // N_SHARD
CONV_W_COLS = C_BR // N_SHARD
KW = 31
HALO = 16
HEADS = 8
HEAD_DIM = 128
CHUNK = 128
LANE = 128
SUB = 8

ADAM_LR = 0.001
ADAM_B1 = 0.9
ADAM_B2 = 0.999
ADAM_EPS = 1e-08
ADAM_WD = 0.01
ADAM_STEP = 10

TS_PROJ = 512
TS_FWD = 256
TS_BWD = 256
TS_INB = 512
TK_GW = 2048
ROWS_A = 128
ROWS_B = 64
VMEM_LIMIT = 56 * 1024 * 1024


def _cparams(sem=None, vmem=VMEM_LIMIT):
    kw = dict(vmem_limit_bytes=vmem)
    if sem is not None:
        kw["dimension_semantics"] = sem
    return pltpu.CompilerParams(**kw)


def _whole_vmem():
    return pl.BlockSpec(memory_space=pltpu.VMEM)


def _sigmoid(v):
    return 1.0 / (1.0 + jnp.exp(-v))


def _fold8(v):
    n, c = v.shape
    return v.reshape(n // SUB, SUB, c).sum(axis=0)


def _dot_nt(a, b):
    return lax.dot_general(a, b, (((1,), (1,)), ((), ())), preferred_element_type=F32)


def _dot_tn(a, b):
    return lax.dot_general(a, b, (((0,), (0,)), ((), ())), preferred_element_type=F32)


def _mesh_pos():
    return lax.axis_index("x"), lax.axis_index("y"), lax.axis_index("c")


def _rms_proj_gather(x, norm_g, w_in, w_out, conv_w, order):
    s = x.shape[0]
    ts = min(TS_PROJ, s)
    nt = s // ts
    hin = w_in.shape[0] // 2
    hout = w_out.shape[0] // 2

    def body(order_ref, x_ref, g_ref, win_ref, wout_ref, cw_ref,
             h_ref, proj_ref, wing_ref, woutg_ref, cwg_ref, wg_ref, wob_ref, hs_ref, send_sems, recv_sems, local_sems):
        p = pl.program_id(0)
        t = pl.program_id(1)
        mx, my, c = _mesh_pos()
        me = 2 * mx + my
        chips = [(1 - mx, my), (mx, 1 - my), (1 - mx, 1 - my)]
        sibling = (mx, my, 1 - c)

        def remote(src, dst, sem, dev):
            return pltpu.make_async_remote_copy(
                src_ref=src, dst_ref=dst, send_sem=send_sems.at[sem], recv_sem=recv_sems.at[sem],
                device_id=dev, device_id_type=MESH)

        def w_in_part(blk, half):
            return wg_ref.at[blk, pl.ds(half * hin, hin)]

        def keep(blk, k):
            return pltpu.make_async_copy(wg_ref.at[blk], wing_ref.at[blk], local_sems.at[2 + k])

        def w_out_part(blk, half):
            return woutg_ref.at[blk, pl.ds(half * hout, hout)]

        def sends():
            out = []
            for j, (cx, cy) in enumerate(chips):
                blk = 2 * cx + cy
                out.append(remote(w_in_part(me, c), w_in_part(me, c), j, (cx, cy, c)))
                out.append(remote(w_in_part(blk, c), w_in_part(blk, c), 3 + j, sibling))
                out.append(remote(wob_ref.at[pl.ds(c * hout, hout)], w_out_part(me, c), 6 + j, (cx, cy, c)))
                out.append(remote(w_out_part(blk, c), w_out_part(blk, c), 9 + j, sibling))
                out.append(remote(cw_ref, cwg_ref.at[me], 12 + j, (cx, cy, c)))
            return out

        @pl.when(jnp.logical_and(p == 0, t == 0))
        def _():
            wg_ref[me] = win_ref[...].astype(BF16)
            wob_ref[...] = wout_ref[...].astype(BF16)
            keep(me, 0).start()
            mine = [pltpu.make_async_copy(wob_ref, woutg_ref.at[me], local_sems.at[0]),
                    pltpu.make_async_copy(cw_ref, cwg_ref.at[me], local_sems.at[1])]
            for cp in mine:
                cp.start()
            for k, cp in enumerate(sends()):
                if k % 5 == 4 or (k % 5 == 0 and k // 5 < 2):
                    cp.start()
            for cp in mine:
                cp.wait()

        for j, (cx, cy) in enumerate(chips):
            blk = 2 * cx + cy

            @pl.when(jnp.logical_and(p == j + 1, t == 0))
            def _():
                remote(w_in_part(blk, c), w_in_part(blk, c), j, (cx, cy, c)).wait_recv()
                remote(w_in_part(blk, c), w_in_part(blk, c), 3 + j, sibling).start()
                remote(w_in_part(blk, 1 - c), w_in_part(blk, 1 - c), 3 + j, sibling).wait_recv()
                keep(blk, j + 1).start()
                if j == 0:
                    sends()[5 * 2].start()
                if j == 1:
                    for jj in range(3):
                        sends()[5 * jj + 2].start()

        rows = pl.ds(pl.multiple_of(t * ts, ts), ts)

        @pl.when(p == 0)
        def _():
            xt = x_ref[...]
            r = lax.rsqrt(jnp.mean(xt * xt, axis=-1, keepdims=True) + EPS)
            hb = (xt * r * g_ref[...]).astype(BF16)
            h_ref[...] = hb
            hs_ref[rows, :] = hb

        proj_ref[...] = jnp.dot(hs_ref[rows, :], wg_ref[order_ref[p]], preferred_element_type=F32).astype(BF16)

        @pl.when(jnp.logical_and(p == N_SHARD - 1, t == nt - 1))
        def _():
            for j, (cx, cy) in enumerate(chips):
                blk = 2 * cx + cy
                remote(w_out_part(blk, c), w_out_part(blk, c), 6 + j, (cx, cy, c)).wait_recv()
                remote(w_out_part(blk, c), w_out_part(blk, c), 9 + j, sibling).start()
            for j, (cx, cy) in enumerate(chips):
                blk = 2 * cx + cy
                remote(w_out_part(blk, 1 - c), w_out_part(blk, 1 - c), 9 + j, sibling).wait_recv()
                remote(cw_ref, cwg_ref.at[blk], 12 + j, (cx, cy, c)).wait_recv()
            for cp in sends():
                cp.wait_send()
            keep(me, 0).wait()
            for j, (cx, cy) in enumerate(chips):
                keep(2 * cx + cy, j + 1).wait()

    hbm = pl.BlockSpec(memory_space=pl.ANY)
    return pl.pallas_call(
        body, name="rms_proj_gather",
        grid_spec=pltpu.PrefetchScalarGridSpec(
            num_scalar_prefetch=1, grid=(N_SHARD, nt),
            in_specs=[pl.BlockSpec((ts, D_MODEL), lambda p, t, o: (jnp.where(p == 0, t, nt - 1), 0)),
                      pl.BlockSpec((1, D_MODEL), lambda p, t, o: (0, 0)),
                      _whole_vmem(), _whole_vmem(), _whole_vmem()],
            out_specs=(pl.BlockSpec((ts, D_MODEL), lambda p, t, o: (jnp.where(p == 0, t, nt - 1), 0)),
                       pl.BlockSpec((ts, W_IN_COLS), lambda p, t, o: (t, o[p])),
                       hbm, hbm, hbm),
            scratch_shapes=[pltpu.VMEM((N_SHARD,) + w_in.shape, BF16), pltpu.VMEM(w_out.shape, BF16),
                            pltpu.VMEM((s, D_MODEL), BF16),
                            pltpu.SemaphoreType.DMA((15,)), pltpu.SemaphoreType.DMA((15,)),
                            pltpu.SemaphoreType.DMA((6,))]),
        out_shape=(jax.ShapeDtypeStruct((s, D_MODEL), BF16), jax.ShapeDtypeStruct((s, D_IN), BF16),
                   jax.ShapeDtypeStruct((N_SHARD,) + w_in.shape, BF16),
                   jax.ShapeDtypeStruct((N_SHARD,) + w_out.shape, BF16),
                   jax.ShapeDtypeStruct((N_SHARD,) + conv_w.shape, F32)),
        compiler_params=_cparams(("arbitrary", "arbitrary")),
    )(order, x, norm_g, w_in, w_out, conv_w)


def _halo_specs(ts, s, width, col_block):
    per = ts // HALO
    last = s // HALO - 1
    prev = pl.BlockSpec((HALO, width), lambda i: (jnp.maximum(i * per - 1, 0), col_block))
    nxt = pl.BlockSpec((HALO, width), lambda i: (jnp.minimum((i + 1) * per, last), col_block))
    return prev, nxt


def _layer_norm_stats(v):
    mu = jnp.mean(v, axis=-1, keepdims=True)
    vc = v - mu
    var = jnp.mean(vc * vc, axis=-1, keepdims=True)
    rstd = lax.rsqrt(var + EPS)
    return vc * rstd, rstd


def _fill_shifted(sh_ref, ext):
    n = ext.shape[0]
    sh_ref[0] = ext
    for r in range(1, SUB):
        sh_ref[r] = pltpu.roll(ext, n - r, axis=0)


def _fwd_fused(proj, x, target, cwg, conv_b, conv_ln_g, conv_ln_b, sgu_ln_g, sgu_ln_b, w_s, bs_t, woutg, final_g):
    s = proj.shape[0]
    ts = min(TS_FWD, s)
    nt = s // ts
    ra = min(ROWS_A, ts)

    def body(pm_ref, pp_ref, pn_ref, x_ref, t_ref, cw_ref, cb_ref, clg_ref, clb_ref, slg_ref, slb_ref, ws_ref,
             bst_ref, wo_ref, fg_ref,
             cv_ref, dx2_ref, dy_ref, gw_ref, gf_ref, se_ref,
             sh_ref, y_new, y_old, gw_acc, sem):
        i = pl.program_id(0)
        tile = jnp.minimum(i, nt - 1)
        keep_prev = (tile > 0).astype(F32)
        keep_next = (tile < nt - 1).astype(F32)

        @pl.when(i == 0)
        def _():
            y_old[...] = jnp.zeros_like(y_old)
            gw_acc[...] = jnp.zeros_like(gw_acc)
            gf_ref[...] = jnp.zeros_like(gf_ref)
            se_ref[...] = jnp.zeros_like(se_ref)

        for lb in range(C_BR // LANE):
            lanes = slice(lb * LANE, (lb + 1) * LANE)
            gates = slice(C_BR + lb * LANE, C_BR + (lb + 1) * LANE)

            def glu(ref):
                return ref[:, lanes].astype(F32) * _sigmoid(ref[:, gates].astype(F32))

            ext = jnp.concatenate([glu(pp_ref) * keep_prev, glu(pm_ref), glu(pn_ref) * keep_next], axis=0)
            _fill_shifted(sh_ref, ext)
            shard, off = divmod(lb * LANE, CONV_W_COLS)
            bias = cb_ref[:, lanes]

            def chunk(jc, carry):
                base = pl.multiple_of(jc * ra, ra)
                acc = jnp.zeros((ra, LANE), F32) + bias
                for k in range(KW):
                    o = k + 1
                    acc = acc + sh_ref[o % SUB, pl.ds(base + SUB * (o // SUB), ra), :] * cw_ref[
                        shard, k:k + 1, off:off + LANE]
                cv_ref[pl.ds(base, ra), lanes] = acc
                return carry

            lax.fori_loop(0, ts // ra, chunk, 0)

        live = (i > 0).astype(F32)
        yt = y_old[...]
        x2 = x_ref[...]
        for k in range(N_SHARD):
            x2 = x2 + jnp.dot(yt[:, k * W_OUT_ROWS:(k + 1) * W_OUT_ROWS], wo_ref[k], preferred_element_type=F32)
        r2 = lax.rsqrt(jnp.mean(x2 * x2, axis=-1, keepdims=True) + EPS)
        n2 = x2 * r2
        g = fg_ref[...]
        diff = n2 * g - t_ref[...]
        se_ref[...] += live * _fold8(diff * diff)
        dout = diff * (1.0 / D_MODEL)
        gf_ref[...] += live * _fold8(dout * n2)
        dn = dout * g
        dx2 = r2 * (dn - n2 * jnp.mean(dn * n2, axis=-1, keepdims=True))
        dx2_ref[...] = dx2
        dxb = dx2.astype(BF16)
        for k in range(N_SHARD):
            rows = slice(k * W_OUT_ROWS, (k + 1) * W_OUT_ROWS)
            dy_ref[:, rows] = _dot_nt(dxb, wo_ref[k]).astype(BF16)
            gw_acc[rows, :] += _dot_tn(yt[:, rows], dxb)

        lnh, _ = _layer_norm_stats(cv_ref[...])
        ln = lnh * clg_ref[...] + clb_ref[...]
        gc = pm_ref[:, 2 * C_BR:3 * C_BR].astype(F32)
        y_new[:, :C_BR] = (ln * _sigmoid(ln) * gc * _sigmoid(gc)).astype(BF16)

        vh, _ = _layer_norm_stats(pm_ref[:, 4 * C_BR:5 * C_BR].astype(F32))
        vn = (vh * slg_ref[...] + slb_ref[...]).astype(BF16)
        for hd in range(HEADS):
            w_h = ws_ref[hd].astype(BF16)
            b_h = bst_ref[:, hd:hd + 1]
            cols = slice(hd * HEAD_DIM, (hd + 1) * HEAD_DIM)
            for ch in range(ts // CHUNK):
                rows = slice(ch * CHUNK, (ch + 1) * CHUNK)
                mixed = jnp.dot(w_h, vn[rows, cols], preferred_element_type=F32) + b_h
                u = pm_ref[rows, 3 * C_BR + hd * HEAD_DIM:3 * C_BR + (hd + 1) * HEAD_DIM].astype(F32)
                gs = pm_ref[rows, 5 * C_BR + hd * HEAD_DIM:5 * C_BR + (hd + 1) * HEAD_DIM].astype(F32)
                y_new[rows, C_BR + hd * HEAD_DIM:C_BR + (hd + 1) * HEAD_DIM] = (
                    u * mixed * gs * _sigmoid(gs)).astype(BF16)

        y_old[...] = y_new[...]

        @pl.when(i == nt)
        def _():
            gf_ref[...] = jnp.broadcast_to(jnp.sum(gf_ref[...], axis=0, keepdims=True), gf_ref.shape)
            out = pltpu.make_async_copy(gw_acc, gw_ref, sem)
            out.start()
            out.wait()

    per = ts // HALO
    last = s // HALO - 1
    cur = lambda i: jnp.minimum(i, nt - 1)
    before = lambda i: jnp.maximum(i - 1, 0)
    prev = pl.BlockSpec((HALO, 2 * C_BR), lambda i: (jnp.maximum(cur(i) * per - 1, 0), 0))
    nxt = pl.BlockSpec((HALO, 2 * C_BR), lambda i: (jnp.minimum((cur(i) + 1) * per, last), 0))
    row = pl.BlockSpec((1, C_BR), lambda i: (0, 0))
    old_tile = pl.BlockSpec((ts, D_MODEL), lambda i: (before(i), 0))
    acc8 = pl.BlockSpec((SUB, D_MODEL), lambda i: (0, 0))
    return pl.pallas_call(
        body, name="fwd_fused",
        grid=(nt + 1,),
        out_shape=(jax.ShapeDtypeStruct((s, C_BR), F32), jax.ShapeDtypeStruct((s, D_MODEL), F32),
                   jax.ShapeDtypeStruct((s, 2 * C_BR), BF16), jax.ShapeDtypeStruct((2 * C_BR, D_MODEL), F32),
                   jax.ShapeDtypeStruct((SUB, D_MODEL), F32), jax.ShapeDtypeStruct((SUB, D_MODEL), F32)),
        in_specs=[pl.BlockSpec((ts, D_IN), lambda i: (cur(i), 0)), prev, nxt, old_tile, old_tile,
                  _whole_vmem(), row, row, row, row, row, _whole_vmem(), _whole_vmem(), _whole_vmem(), row],
        out_specs=(pl.BlockSpec((ts, C_BR), lambda i: (cur(i), 0)), old_tile,
                   pl.BlockSpec((ts, 2 * C_BR), lambda i: (before(i), 0)),
                   pl.BlockSpec(memory_space=pl.ANY), acc8, acc8),
        scratch_shapes=[pltpu.VMEM((SUB, ts + 2 * HALO, LANE), F32),
                        pltpu.VMEM((ts, 2 * C_BR), BF16),
                        pltpu.VMEM((ts, 2 * C_BR), BF16),
                        pltpu.VMEM((2 * C_BR, D_MODEL), F32),
                        pltpu.SemaphoreType.DMA(())],
        compiler_params=_cparams(("arbitrary",), 60 * 1024 * 1024),
    )(proj, proj, proj, x, target, cwg, conv_b, conv_ln_g, conv_ln_b, sgu_ln_g, sgu_ln_b, w_s, bs_t, woutg, final_g)


def _dsilu(v, sg):
    return sg * (1.0 + v * (1.0 - sg))


def _branch_bwd(proj, dy, cv, cwg, conv_ln_g, conv_ln_b, sgu_ln_g, sgu_ln_b, w_s, ws_t, bs_t, gf8, se8, dep):
    s = proj.shape[0]
    ts = min(TS_BWD, s)
    nt = s // ts
    ra = min(ROWS_A, ts)
    rb = min(ROWS_B, ts)
    te = ts + 2 * HALO

    def body(pm_ref, dym_ref, cvm_ref, gcp_ref, gcn_ref, dyp_ref, dyn_ref, cvp_ref, cvn_ref,
             cw_ref, clg_ref, clb_ref, slg_ref, slb_ref, ws_ref, wst_ref, bst_ref, gf_ref, se_ref, dep_ref,
             dp_ref, small_ref, gws_ref,
             sh_ref, glu_ref, dgl_ref, dcv_ref, acc_ref, gcw_ref, gbs_ref):
        i = pl.program_id(0)

        @pl.when(i == 0)
        def _():
            acc_ref[...] = jnp.zeros_like(acc_ref)
            gcw_ref[...] = jnp.zeros_like(gcw_ref)
            gbs_ref[...] = jnp.zeros_like(gbs_ref)
            gws_ref[...] = jnp.zeros_like(gws_ref)

        def ext(prev_ref, main, next_ref):
            return jnp.concatenate([prev_ref[...].astype(F32), main, next_ref[...].astype(F32)], axis=0)

        main = slice(HALO, HALO + ts)

        cv_e = ext(cvp_ref, cvm_ref[...], cvn_ref)
        gc_e = ext(gcp_ref, pm_ref[:, 2 * C_BR:3 * C_BR].astype(F32), gcn_ref)
        dyc_e = ext(dyp_ref, dym_ref[:, :C_BR].astype(F32), dyn_ref)
        lnh, rstd = _layer_norm_stats(cv_e)
        clg = clg_ref[...]
        ln = lnh * clg + clb_ref[...]
        sg_ln = _sigmoid(ln)
        sg_gc = _sigmoid(gc_e)
        d_ln = dyc_e * gc_e * sg_gc * _dsilu(ln, sg_ln)
        dp_ref[:, 2 * C_BR:3 * C_BR] = (
            dyc_e[main] * ln[main] * sg_ln[main] * _dsilu(gc_e[main], sg_gc[main])).astype(BF16)
        dlnh = d_ln * clg
        d_cv = rstd * (dlnh - jnp.mean(dlnh, axis=-1, keepdims=True)
                       - lnh * jnp.mean(dlnh * lnh, axis=-1, keepdims=True))
        row = lax.broadcasted_iota(jnp.int32, (te, 1), 0)
        valid = jnp.logical_and(jnp.logical_or(row >= HALO, i > 0),
                                jnp.logical_or(row < HALO + ts, i < nt - 1))
        d_cv = jnp.where(valid, d_cv, 0.0)
        dcv_ref[...] = d_cv
        acc_ref[0:8, :] += _fold8(d_cv[main])
        acc_ref[8:16, :] += _fold8(d_ln[main] * lnh[main])
        acc_ref[16:24, :] += _fold8(d_ln[main])

        for lb in range(C_BR // LANE):
            lanes = slice(lb * LANE, (lb + 1) * LANE)
            gates = slice(C_BR + lb * LANE, C_BR + (lb + 1) * LANE)
            shard, off = divmod(lb * LANE, CONV_W_COLS)
            av = pm_ref[:, lanes].astype(F32)
            sg = _sigmoid(pm_ref[:, gates].astype(F32))
            glu_ref[...] = av * sg
            _fill_shifted(sh_ref, dcv_ref[:, lanes])

            def chunk_a(jc, carry):
                base = pl.multiple_of(jc * ra, ra)
                acc = jnp.zeros((ra, LANE), F32)
                for j in range(KW):
                    o = j + 1
                    acc = acc + sh_ref[o % SUB, pl.ds(base + SUB * (o // SUB), ra), :] * cw_ref[
                        shard, KW - 1 - j:KW - j, off:off + LANE]
                dgl_ref[pl.ds(base, ra), :] = acc
                return carry

            lax.fori_loop(0, ts // ra, chunk_a, 0)

            def chunk_b(jc, accs):
                base = pl.multiple_of(jc * rb, rb)
                g = glu_ref[pl.ds(base, rb), :]
                out = []
                for j in range(KW):
                    o = j + 1
                    d = sh_ref[o % SUB, pl.ds(base + SUB * (o // SUB), rb), :]
                    out.append(accs[j] + _fold8(g * d))
                return tuple(out)

            accs = lax.fori_loop(0, ts // rb, chunk_b, tuple(jnp.zeros((SUB, LANE), F32) for _ in range(KW)))
            for j in range(KW):
                gcw_ref[j * SUB:(j + 1) * SUB, lanes] += accs[j]

            dglu = dgl_ref[...]
            dp_ref[:, lanes] = (dglu * sg).astype(BF16)
            dp_ref[:, gates] = (dglu * av * sg * (1.0 - sg)).astype(BF16)

        vh, vrstd = _layer_norm_stats(pm_ref[:, 4 * C_BR:5 * C_BR].astype(F32))
        slg = slg_ref[...]
        vn = (vh * slg + slb_ref[...]).astype(BF16)
        for hd in range(HEADS):
            w_h = ws_ref[hd].astype(BF16)
            wt_h = wst_ref[hd].astype(BF16)
            b_h = bst_ref[:, hd:hd + 1]
            cols = slice(hd * HEAD_DIM, (hd + 1) * HEAD_DIM)
            gws_h = jnp.zeros((CHUNK, CHUNK), F32)
            gbs_h = jnp.zeros((CHUNK, HEAD_DIM), F32)
            for ch in range(ts // CHUNK):
                rows = slice(ch * CHUNK, (ch + 1) * CHUNK)
                vn_b = vn[rows, cols]
                mixed = jnp.dot(w_h, vn_b, preferred_element_type=F32) + b_h
                u = pm_ref[rows, 3 * C_BR + hd * HEAD_DIM:3 * C_BR + (hd + 1) * HEAD_DIM].astype(F32)
                gs = pm_ref[rows, 5 * C_BR + hd * HEAD_DIM:5 * C_BR + (hd + 1) * HEAD_DIM].astype(F32)
                dys = dym_ref[rows, C_BR + hd * HEAD_DIM:C_BR + (hd + 1) * HEAD_DIM].astype(F32)
                sg_gs = _sigmoid(gs)
                silu_gs = gs * sg_gs
                dp_ref[rows, 3 * C_BR + hd * HEAD_DIM:3 * C_BR + (hd + 1) * HEAD_DIM] = (
                    dys * mixed * silu_gs).astype(BF16)
                dp_ref[rows, 5 * C_BR + hd * HEAD_DIM:5 * C_BR + (hd + 1) * HEAD_DIM] = (
                    dys * u * mixed * _dsilu(gs, sg_gs)).astype(BF16)
                d_mixed = dys * u * silu_gs
                dm_b = d_mixed.astype(BF16)
                gws_h = gws_h + _dot_nt(dm_b, vn_b)
                gbs_h = gbs_h + d_mixed
                dcv_ref[HALO + ch * CHUNK:HALO + (ch + 1) * CHUNK, cols] = jnp.dot(
                    wt_h, dm_b, preferred_element_type=F32)
            gws_ref[hd] += gws_h
            gbs_ref[:, cols] += gbs_h
        d_vn = dcv_ref[main, :]
        acc_ref[24:32, :] += _fold8(d_vn * vh)
        acc_ref[32:40, :] += _fold8(d_vn)
        dvh = d_vn * slg
        dp_ref[:, 4 * C_BR:5 * C_BR] = (vrstd * (
            dvh - jnp.mean(dvh, axis=-1, keepdims=True)
            - vh * jnp.mean(dvh * vh, axis=-1, keepdims=True))).astype(BF16)

        @pl.when(i == nt - 1)
        def _():
            small_ref[...] = jnp.zeros_like(small_ref)
            for a in range(5):
                small_ref[1 + a:2 + a, :] = jnp.sum(acc_ref[a * SUB:(a + 1) * SUB, :], axis=0, keepdims=True)
            ones = jnp.ones((SUB, HEAD_DIM), F32)
            for hd in range(HEADS):
                cols = slice(hd * HEAD_DIM, (hd + 1) * HEAD_DIM)
                rowsum = lax.dot_general(ones, gbs_ref[:, cols], (((1,), (1,)), ((), ())),
                                         precision=lax.Precision.HIGHEST, preferred_element_type=F32)
                small_ref[6:7, cols] = rowsum[0:1, :]
            small_ref[7:8, :] = gf_ref[0:1, :]
            small_ref[0:1, :] = jnp.sum(se_ref[...], axis=0, keepdims=True)
            for k in range(KW):
                j = KW - 1 - k
                small_ref[8 + k:9 + k, :] = jnp.sum(gcw_ref[j * SUB:(j + 1) * SUB, :], axis=0, keepdims=True)

    gc_prev, gc_next = _halo_specs(ts, s, C_BR, 2)
    lo_prev, lo_next = _halo_specs(ts, s, C_BR, 0)
    row = pl.BlockSpec((1, C_BR), lambda i: (0, 0))
    return pl.pallas_call(
        body, name="branch_bwd",
        grid=(nt,),
        out_shape=(jax.ShapeDtypeStruct((s, D_IN), BF16), jax.ShapeDtypeStruct((40, C_BR), F32),
                   jax.ShapeDtypeStruct((HEADS, CHUNK, CHUNK), F32)),
        in_specs=[pl.BlockSpec((ts, D_IN), lambda i: (i, 0)),
                  pl.BlockSpec((ts, 2 * C_BR), lambda i: (i, 0)),
                  pl.BlockSpec((ts, C_BR), lambda i: (i, 0)),
                  gc_prev, gc_next, lo_prev, lo_next, lo_prev, lo_next,
                  _whole_vmem(), row, row, row, row, _whole_vmem(), _whole_vmem(), _whole_vmem(),
                  _whole_vmem(), _whole_vmem(), _whole_vmem()],
        out_specs=(pl.BlockSpec((ts, D_IN), lambda i: (i, 0)),
                   pl.BlockSpec((40, C_BR), lambda i: (0, 0)),
                   pl.BlockSpec((HEADS, CHUNK, CHUNK), lambda i: (0, 0, 0))),
        scratch_shapes=[pltpu.VMEM((SUB, te, LANE), F32),
                        pltpu.VMEM((ts, LANE), F32),
                        pltpu.VMEM((ts, LANE), F32),
                        pltpu.VMEM((te, C_BR), F32),
                        pltpu.VMEM((5 * SUB, C_BR), F32),
                        pltpu.VMEM((KW * SUB, C_BR), F32),
                        pltpu.VMEM((CHUNK, C_BR), F32)],
        compiler_params=_cparams(("arbitrary",)),
    )(proj, dy, cv, proj, proj, dy, dy, cv, cv,
      cwg, conv_ln_g, conv_ln_b, sgu_ln_g, sgu_ln_b, w_s, ws_t, bs_t, gf8, se8, dep)


def _in_bwd(dproj, x, dx2, norm_g, wing, dep, first_tile, n_tiles, gx_prev=None, gn_prev=None):
    s = x.shape[0]
    ts = min(TS_INB, s)

    def body(*refs):
        dp_ref, x_ref, dx2_ref, g_ref, w_ref = refs[:5]
        gx_ref, gn_ref, acc_ref = refs[-3:]
        i = pl.program_id(0)

        @pl.when(i == 0)
        def _():
            acc_ref[...] = jnp.zeros_like(acc_ref)

        dh = jnp.zeros((ts, D_MODEL), F32)
        for k in range(N_SHARD):
            dh = dh + _dot_nt(dp_ref[:, k * W_IN_COLS:(k + 1) * W_IN_COLS], w_ref[k])
        xt = x_ref[...]
        r = lax.rsqrt(jnp.mean(xt * xt, axis=-1, keepdims=True) + EPS)
        n = xt * r
        acc_ref[...] += _fold8(dh * n)
        dn = dh * g_ref[...]
        gx_ref[...] = dx2_ref[...] + r * (dn - n * jnp.mean(dn * n, axis=-1, keepdims=True))

        @pl.when(i == n_tiles - 1)
        def _():
            total = jnp.broadcast_to(jnp.sum(acc_ref[...], axis=0, keepdims=True), gn_ref.shape)
            if gn_prev is not None:
                total = total + refs[7][...]
            gn_ref[...] = total

    tile = pl.BlockSpec((ts, D_MODEL), lambda i: (i + first_tile, 0))
    in_specs = [pl.BlockSpec((ts, D_IN), lambda i: (i + first_tile, 0)), tile, tile,
                pl.BlockSpec((1, D_MODEL), lambda i: (0, 0)), _whole_vmem(), pl.BlockSpec(memory_space=pl.ANY)]
    operands = [dproj, x, dx2, norm_g, wing, dep]
    aliases = {}
    if gx_prev is not None:
        in_specs += [pl.BlockSpec(memory_space=pl.ANY), _whole_vmem()]
        operands += [gx_prev, gn_prev]
        aliases = {6: 0}
    return pl.pallas_call(
        body, name="in_bwd_%d" % first_tile,
        grid=(n_tiles,),
        out_shape=(jax.ShapeDtypeStruct((s, D_MODEL), F32), jax.ShapeDtypeStruct((SUB, D_MODEL), F32)),
        in_specs=in_specs,
        out_specs=(tile, pl.BlockSpec((SUB, D_MODEL), lambda i: (0, 0))),
        scratch_shapes=[pltpu.VMEM((SUB, D_MODEL), F32)],
        input_output_aliases=aliases,
        compiler_params=_cparams(("arbitrary",)),
    )(*operands)


def _grad_w_in(h, dproj, dep):
    s = h.shape[0]
    tk = min(TK_GW, s)
    half = D_MODEL // 2

    def body(h_ref, dp_ref, dep_ref, o_ref):
        @pl.when(pl.program_id(1) == 0)
        def _():
            o_ref[...] = jnp.zeros_like(o_ref)

        o_ref[0] += _dot_tn(h_ref[...], dp_ref[...]).reshape(2, half, W_IN_COLS)

    return pl.pallas_call(
        body, name="grad_w_in",
        grid=(N_SHARD, s // tk),
        out_shape=jax.ShapeDtypeStruct((N_SHARD, 2, half, W_IN_COLS), F32),
        in_specs=[pl.BlockSpec((tk, D_MODEL), lambda k, t: (t, 0)),
                  pl.BlockSpec((tk, W_IN_COLS), lambda k, t: (t, k)), _whole_vmem()],
        out_specs=pl.BlockSpec((1, 2, half, W_IN_COLS), lambda k, t: (k, 0, 0, 0)),
        compiler_params=_cparams(("parallel", "arbitrary")),
    )(h, dproj, dep)


HBM_SPEC = pl.BlockSpec(memory_space=pltpu.HBM)
SEM_SPEC = pl.BlockSpec(memory_space=pltpu.SEMAPHORE)
SIDE_EFFECT = pltpu.SideEffectType.DATAFLOW_SIDE_EFFECTING


def _remote_copies(plan, bufs, send_sems, recv_sems):
    x, y, c = _mesh_pos()
    return [pltpu.make_async_remote_copy(src_ref=src, dst_ref=dst, send_sem=send_sems.at[k],
                                         recv_sem=recv_sems.at[k], device_id=dev, device_id_type=MESH)
            for k, (src, dst, dev) in enumerate(plan(x, y, c, *bufs))]


def _start_copies(name, bufs, n_copies, plan):
    n = len(bufs)

    def body(*refs):
        for cp in _remote_copies(plan, refs[:n], refs[n], refs[n + 1]):
            cp.start()
        refs[-1][...] = jnp.zeros_like(refs[-1])

    outs = pl.pallas_call(
        body, name=name,
        out_shape=(pltpu.SemaphoreType.DMA((n_copies,)), pltpu.SemaphoreType.DMA((n_copies,)),
                   *[pltpu.HBM(b.shape, b.dtype) for b in bufs], jax.ShapeDtypeStruct((SUB, LANE), F32)),
        in_specs=[HBM_SPEC] * n,
        out_specs=(SEM_SPEC, SEM_SPEC, *[HBM_SPEC] * n, _whole_vmem()),
        input_output_aliases={i: 2 + i for i in range(n)},
        compiler_params=pltpu.CompilerParams(has_side_effects=SIDE_EFFECT),
    )(*[pltpu.with_memory_space_constraint(b, pltpu.HBM) for b in bufs])
    return outs[0], outs[1], list(outs[2:2 + n]), outs[-1]


def _wait_copies(name, send_sems, recv_sems, bufs, plan, after):
    n = len(bufs)

    def body(*refs):
        for cp in _remote_copies(plan, refs[:n], refs[n], refs[n + 1]):
            cp.wait_send()
            cp.wait_recv()

    outs = pl.pallas_call(
        body, name=name,
        out_shape=tuple(pltpu.HBM(b.shape, b.dtype) for b in bufs),
        in_specs=[HBM_SPEC] * n + [SEM_SPEC, SEM_SPEC, pl.BlockSpec(memory_space=pl.ANY)],
        out_specs=(HBM_SPEC,) * n,
        input_output_aliases={i: i for i in range(n)},
        compiler_params=pltpu.CompilerParams(has_side_effects=SIDE_EFFECT),
    )(*bufs, send_sems, recv_sems, after)
    return list(outs)


def _landing(shape, dtype):
    return lax.empty(shape, dtype)


def _plan_pair_exchange(x, y, c, g, r):
    return [(g.at[k, 1 - c], r.at[k], (x, y, 1 - c)) for k in range(N_SHARD)]


def _plan_chip_exchange(x, y, c, a, r):
    chips = [(1 - x, y), (x, 1 - y), (1 - x, 1 - y)]
    return [(a.at[2 * cx + cy], r.at[j], (cx, cy, c)) for j, (cx, cy) in enumerate(chips)]


def _plan_pair_gather(x, y, c, f):
    return [(f.at[c], f.at[c], (x, y, 1 - c))]


def _plan_all_gather(x, y, c, own, land):
    me = 4 * x + 2 * y + c
    flip = lambda v, bit: 1 - v if bit else v
    return [(own, land.at[me], (flip(x, m >> 2 & 1), flip(y, m >> 1 & 1), flip(c, m & 1))) for m in range(1, 8)]


def _add_pair(g, r, pos, name):
    _, _, rows, cols = g.shape
    tr = min(256, rows)

    def body(pos_ref, g_ref, r_ref, o_ref, ob_ref):
        v = g_ref[0] + r_ref[...]
        o_ref[...] = v
        ob_ref[...] = v.astype(BF16)

    blk = pl.BlockSpec((1, tr, cols), lambda k, t, pos_ref: (k, t, 0))
    return pl.pallas_call(
        body, name=name,
        grid_spec=pltpu.PrefetchScalarGridSpec(
            num_scalar_prefetch=1, grid=(N_SHARD, rows // tr),
            in_specs=[pl.BlockSpec((1, 1, tr, cols), lambda k, t, pos_ref: (k, pos_ref[1], t, 0)), blk],
            out_specs=(blk, blk)),
        out_shape=(jax.ShapeDtypeStruct((N_SHARD, rows, cols), F32), jax.ShapeDtypeStruct((N_SHARD, rows, cols), BF16)),
        compiler_params=_cparams(("parallel", "parallel")),
    )(pos, g, r)


def _add_chips(a, r, pos, name):
    _, rows, cols = a.shape
    tr = min(256, rows)

    def body(pos_ref, a_ref, r_ref, o_ref):
        o_ref[0] = ((a_ref[0] + r_ref[0].astype(F32)) + r_ref[1].astype(F32)) + r_ref[2].astype(F32)

    return pl.pallas_call(
        body, name=name,
        grid_spec=pltpu.PrefetchScalarGridSpec(
            num_scalar_prefetch=1, grid=(rows // tr,),
            in_specs=[pl.BlockSpec((1, tr, cols), lambda t, pos_ref: (pos_ref[0], t, 0)),
                      pl.BlockSpec((3, tr, cols), lambda t, pos_ref: (0, t, 0))],
            out_specs=pl.BlockSpec((1, tr, cols), lambda t, pos_ref: (pos_ref[1], t, 0))),
        out_shape=jax.ShapeDtypeStruct((2, rows, cols), F32),
        compiler_params=_cparams(("parallel",)),
    )(pos, a, r)


def _sum_slots(pos_ref, own, land_ref, rows):
    me = pos_ref[2]
    total = None
    for d in range(8):
        term = jnp.where(me == d, own, land_ref[d] if rows is None else land_ref[d, rows, :])
        total = term if total is None else total + term
    return total


def _sum_small(pos, small, small_land, gws, gws_land):
    def body(pos_ref, sm_ref, sml_ref, gw_ref, gwl_ref, o_sm, o_gw):
        o_sm[...] = _sum_slots(pos_ref, sm_ref[...], sml_ref, None)
        o_gw[...] = _sum_slots(pos_ref, gw_ref[...], gwl_ref, None)

    return pl.pallas_call(
        body, name="sum_small",
        grid_spec=pltpu.PrefetchScalarGridSpec(
            num_scalar_prefetch=1, grid=(1,),
            in_specs=[_whole_vmem()] * 4, out_specs=[_whole_vmem()] * 2),
        out_shape=[jax.ShapeDtypeStruct(small.shape, F32), jax.ShapeDtypeStruct(gws.shape, F32)],
        compiler_params=_cparams(("arbitrary",)),
    )(pos, small, small_land, gws, gws_land)


def _adamw_math(w, g, m, v):
    m = ADAM_B1 * m + (1.0 - ADAM_B1) * g
    v = ADAM_B2 * v + (1.0 - ADAM_B2) * (g * g)
    m_hat = m / (1.0 - ADAM_B1 ** ADAM_STEP)
    v_hat = v / (1.0 - ADAM_B2 ** ADAM_STEP)
    delta = -ADAM_LR * (m_hat / (jnp.sqrt(v_hat) + ADAM_EPS) + ADAM_WD * w)
    return delta, m, v


def _adamw_large(w, g, m, v, dep, name):
    rows, cols = w.shape
    tr = min(256, rows)

    def body(w_ref, g_ref, m_ref, v_ref, dep_ref, d_ref, nm_ref, nv_ref):
        d_ref[...], nm_ref[...], nv_ref[...] = _adamw_math(w_ref[...], g_ref[...], m_ref[...], v_ref[...])

    tile = pl.BlockSpec((tr, cols), lambda t: (t, 0))
    return pl.pallas_call(
        body, name=name,
        grid=(rows // tr,),
        out_shape=(jax.ShapeDtypeStruct(w.shape, F32),) * 3,
        in_specs=[tile] * 4 + [_whole_vmem()], out_specs=(tile,) * 3,
        compiler_params=_cparams(("parallel",)),
    )(w, g, m, v, dep)


_ROW_OF = {"conv_b": 1, "conv_ln_g": 2, "conv_ln_b": 3, "sgu_ln_g": 4, "sgu_ln_b": 5, "b_s": 6, "final_g": 7}
_CONV_W_ROW = 8
_VECTORS = ("norm_g", "conv_b", "conv_ln_g", "conv_ln_b", "sgu_ln_g", "sgu_ln_b", "b_s", "final_g")


def _adamw_small(call_name, names, grads, pos, params):
    def body(pos_ref, p_ref, q_ref, *refs):
        gn_ref, gnl_ref = p_ref, q_ref
        n_in = 3 * len(names)
        ins, outs = refs[:n_in], refs[n_in:]
        me = pos_ref[0]
        for a, name in enumerate(names):
            w_ref, m_ref, v_ref = ins[3 * a:3 * a + 3]
            if name == "conv_w":
                g = jnp.zeros((KW, CONV_W_COLS), F32)
                for k in range(N_SHARD):
                    blk = p_ref[_CONV_W_ROW:_CONV_W_ROW + KW, k * CONV_W_COLS:(k + 1) * CONV_W_COLS]
                    g = jnp.where(me == k, blk, g)
            elif name == "w_s":
                g = q_ref[...]
            elif name == "norm_g":
                g = _sum_slots(pos_ref, gn_ref[0:1, :], gnl_ref, slice(0, 1))
            else:
                g = p_ref[_ROW_OF[name]:_ROW_OF[name] + 1, :]
            delta, nm, nv = _adamw_math(w_ref[...], g, m_ref[...], v_ref[...])
            for o_ref, val in zip(outs[4 * a:4 * a + 4], (g, delta, nm, nv)):
                o_ref[...] = val

    operands, shapes = [], []
    for name in names:
        operands += list(params[name])
        shapes += [jax.ShapeDtypeStruct(params[name][0].shape, F32)] * 4
    outs = pl.pallas_call(
        body, name=call_name,
        grid_spec=pltpu.PrefetchScalarGridSpec(
            num_scalar_prefetch=1, grid=(1,),
            in_specs=[_whole_vmem()] * (2 + len(operands)),
            out_specs=[_whole_vmem()] * len(shapes)),
        out_shape=shapes,
        compiler_params=_cparams(("arbitrary",)),
    )(pos, *grads, *operands)
    return {name: tuple(outs[4 * a:4 * a + 4]) for a, name in enumerate(names)}


def kernel(x, norm_g, w_in, conv_w, conv_b, conv_ln_g, conv_ln_b, sgu_ln_g, sgu_ln_b, w_s, b_s, w_out, final_g, loss_target, m_norm_g, m_w_in, m_conv_w, m_conv_b, m_conv_ln_g, m_conv_ln_b, m_sgu_ln_g, m_sgu_ln_b, m_w_s, m_b_s, m_w_out, m_final_g, v_norm_g, v_w_in, v_conv_w, v_conv_b, v_conv_ln_g, v_conv_ln_b, v_sgu_ln_g, v_sgu_ln_b, v_w_s, v_b_s, v_w_out, v_final_g):
    xi, yi, ci = _mesh_pos()
    pos = jnp.stack([2 * xi + yi, ci, 4 * xi + 2 * yi + ci]).astype(jnp.int32)

    x2d = x[0]
    tgt = loss_target[0]
    fg = final_g.reshape(1, D_MODEL)
    ws3 = w_s[0]
    ws_t = jnp.swapaxes(ws3, 1, 2)
    bs_t = jnp.transpose(b_s[0])

    chip = 2 * xi + yi
    order = jnp.stack([chip, 2 * (1 - xi) + yi, 2 * xi + 1 - yi, 2 * (1 - xi) + 1 - yi]).astype(jnp.int32)
    h, proj, wing, woutg, cwg = _rms_proj_gather(x2d, norm_g, w_in[0], w_out[0], conv_w[0], order)
    cv, dx2, dy, gwout, gf8, se8 = _fwd_fused(proj, x2d, tgt, cwg, conv_b, conv_ln_g, conv_ln_b, sgu_ln_g,
                                              sgu_ln_b, ws3, bs_t, woutg, fg)
    in_rows, out_rows = D_MODEL // 2, W_OUT_ROWS // 2
    gwout4 = gwout.reshape(N_SHARD, 2, out_rows, D_MODEL)
    ss, rs, (gwout4, r1_out), tok = _start_copies(
        "start_pair_exchange_w_out", [gwout4, _landing((N_SHARD, out_rows, D_MODEL), F32)], N_SHARD,
        _plan_pair_exchange)
    dproj, small, gws3 = _branch_bwd(proj, dy, cv, cwg, conv_ln_g, conv_ln_b, sgu_ln_g, sgu_ln_b, ws3, ws_t, bs_t,
                                     gf8, se8, tok)
    gws = gws3.reshape(HEADS * CHUNK, CHUNK)
    gwout4, r1_out = _wait_copies("wait_pair_exchange_w_out", ss, rs, [gwout4, r1_out], _plan_pair_exchange, dproj)
    a_out, a_out_bf = _add_pair(gwout4, r1_out, pos, "add_pair_w_out")

    def plan_b(x, y, c, a, r, sm, sml, gw, gwl):
        return (_plan_chip_exchange(x, y, c, a, r) + _plan_all_gather(x, y, c, sm, sml)
                + _plan_all_gather(x, y, c, gw, gwl))

    ss, rs, bufs_b, tok = _start_copies(
        "start_chip_exchange_w_out",
        [a_out_bf, _landing((3, out_rows, D_MODEL), BF16), small, _landing((8,) + small.shape, F32),
         gws, _landing((8,) + gws.shape, F32)], 3 + 7 + 7, plan_b)
    gwin = _grad_w_in(h, dproj, tok)
    ss_c, rs_c, (gwin, r1_in), tok = _start_copies(
        "start_pair_exchange_w_in", [gwin, _landing((N_SHARD, in_rows, W_IN_COLS), F32)], N_SHARD,
        _plan_pair_exchange)
    nt = x2d.shape[0] // min(TS_INB, x2d.shape[0])
    cut1, cut2 = nt // 3, 2 * nt // 3
    gx_a, gn_a = _in_bwd(dproj, x2d, dx2, norm_g, wing, tok, 0, cut1)

    gwin, r1_in = _wait_copies("wait_pair_exchange_w_in", ss_c, rs_c, [gwin, r1_in], _plan_pair_exchange, gx_a)
    a_in, a_in_bf = _add_pair(gwin, r1_in, pos, "add_pair_w_in")
    a_out_bf, r2_out, small, small_land, gws, gws_land = _wait_copies(
        "wait_chip_exchange_w_out", ss, rs, bufs_b, plan_b, gx_a)
    f_out = _add_chips(a_out, r2_out, pos, "add_chips_w_out")
    p, q = _sum_small(pos, small, small_land, gws, gws_land)
    loss = (0.5 / D_MODEL) * jnp.sum(p[0])

    flat = lambda a: a.reshape(1, C_BR)
    flat_ws = lambda a: a.reshape(HEADS * CHUNK, CHUNK)
    params = {
        "norm_g": (norm_g, m_norm_g, v_norm_g),
        "conv_b": (conv_b, m_conv_b, v_conv_b),
        "conv_ln_g": (conv_ln_g, m_conv_ln_g, v_conv_ln_g),
        "conv_ln_b": (conv_ln_b, m_conv_ln_b, v_conv_ln_b),
        "sgu_ln_g": (sgu_ln_g, m_sgu_ln_g, v_sgu_ln_g),
        "sgu_ln_b": (sgu_ln_b, m_sgu_ln_b, v_sgu_ln_b),
        "b_s": (flat(b_s), flat(m_b_s), flat(v_b_s)),
        "final_g": (flat(final_g), flat(m_final_g), flat(v_final_g)),
        "conv_w": (conv_w[0], m_conv_w[0], v_conv_w[0]),
        "w_s": (flat_ws(w_s), flat_ws(m_w_s), flat_ws(v_w_s)),
    }
    res = _adamw_small("adamw_small", [n for n in params if n != "norm_g"], (p, q), pos, params)

    def plan_d(x, y, c, a, r, f):
        return _plan_chip_exchange(x, y, c, a, r) + _plan_pair_gather(x, y, c, f)

    ss, rs, bufs_d, tok = _start_copies(
        "start_chip_exchange_w_in", [a_in_bf, _landing((3, in_rows, W_IN_COLS), BF16), f_out], 3 + 1, plan_d)
    gx_b, gn_b = _in_bwd(dproj, x2d, dx2, norm_g, wing, tok, cut1, cut2 - cut1, gx_a, gn_a)
    a_in_bf, r2_in, f_out = _wait_copies("wait_chip_exchange_w_in", ss, rs, bufs_d, plan_d, gx_b)
    f_in = _add_chips(a_in, r2_in, pos, "add_chips_w_in")
    ss, rs, (f_in,), tok = _start_copies("start_pair_gather_w_in", [f_in], 1, _plan_pair_gather)
    g_w_out = f_out.reshape(W_OUT_ROWS, D_MODEL)
    d_w_out, nm_w_out, nv_w_out = _adamw_large(w_out[0], g_w_out, m_w_out[0], v_w_out[0], tok, "adamw_w_out")
    grad_x, gn8 = _in_bwd(dproj, x2d, dx2, norm_g, wing, d_w_out, cut2, nt - cut2, gx_b, gn_b)

    ss_f, rs_f, (gn8, gn_land), tok_f = _start_copies(
        "start_all_gather_norm_g", [gn8, _landing((8,) + gn8.shape, F32)], 7, _plan_all_gather)
    (f_in,) = _wait_copies("wait_pair_gather_w_in", ss, rs, [f_in], _plan_pair_gather, grad_x)
    g_w_in = f_in.reshape(D_MODEL, W_IN_COLS)
    d_w_in, nm_w_in, nv_w_in = _adamw_large(w_in[0], g_w_in, m_w_in[0], v_w_in[0], tok_f, "adamw_w_in")
    gn8, gn_land = _wait_copies("wait_all_gather_norm_g", ss_f, rs_f, [gn8, gn_land], _plan_all_gather, d_w_in)
    res.update(_adamw_small("adamw_norm_g", ["norm_g"], (gn8, gn_land), pos, params))
    res["w_in"] = tuple(a[None] for a in (g_w_in, d_w_in, nm_w_in, nv_w_in))
    res["w_out"] = tuple(a[None] for a in (g_w_out, d_w_out, nm_w_out, nv_w_out))
    res["conv_w"] = tuple(a[None] for a in res["conv_w"])
    res["w_s"] = tuple(a.reshape(w_s.shape) for a in res["w_s"])
    res["b_s"] = tuple(a.reshape(b_s.shape) for a in res["b_s"])
    res["final_g"] = tuple(a.reshape(final_g.shape) for a in res["final_g"])

    order = ("norm_g", "w_in", "conv_w", "conv_b", "conv_ln_g", "conv_ln_b", "sgu_ln_g", "sgu_ln_b",
             "w_s", "b_s", "w_out", "final_g")
    out = [loss, grad_x[None]]
    for part in range(4):
        out += [res[name][part] for name in order]
    return tuple(out)
```

```python
import jax
import jax.numpy as jnp
from jax import lax
from jax.experimental import pallas as pl
from jax.experimental.pallas import tpu as pltpu

F32 = jnp.float32
BF16 = jnp.bfloat16
MESH = pl.DeviceIdType.MESH

EPS = 1e-6
D_MODEL = 1024
C_BR = 1024
D_IN = 6 * C_BR
N_SHARD = 4
W_IN_COLS = D_IN // N_SHARD
W_OUT_ROWS = 2 * C_BR // N_SHARD
CONV_W_COLS = C_BR // N_SHARD
KW = 31
HALO = 16
HEADS = 8
HEAD_DIM = 128
CHUNK = 128
LANE = 128
SUB = 8

ADAM_LR = 0.001
ADAM_B1 = 0.9
ADAM_B2 = 0.999
ADAM_EPS = 1e-08
ADAM_WD = 0.01
ADAM_STEP = 10

TS_PROJ = 512
TS_FWD = 256
TS_BWD = 256
TS_INB = 512
TK_GW = 2048
ROWS_A = 128
ROWS_B = 64
VMEM_LIMIT = 56 * 1024 * 1024


def _cparams(sem=None, vmem=VMEM_LIMIT):
    kw = dict(vmem_limit_bytes=vmem)
    if sem is not None:
        kw["dimension_semantics"] = sem
    return pltpu.CompilerParams(**kw)


def _whole_vmem():
    return pl.BlockSpec(memory_space=pltpu.VMEM)


def _sigmoid(v):
    return 1.0 / (1.0 + jnp.exp(-v))


def _fold8(v):
    n, c = v.shape
    return v.reshape(n // SUB, SUB, c).sum(axis=0)


def _dot_nt(a, b):
    return lax.dot_general(a, b, (((1,), (1,)), ((), ())), preferred_element_type=F32)


def _dot_tn(a, b):
    return lax.dot_general(a, b, (((0,), (0,)), ((), ())), preferred_element_type=F32)


def _mesh_pos():
    return lax.axis_index("x"), lax.axis_index("y"), lax.axis_index("c")


def _rms_proj_gather(x, norm_g, w_in, w_out, conv_w, order):
    s = x.shape[0]
    ts = min(TS_PROJ, s)
    nt = s // ts
    hin = w_in.shape[0] // 2
    hout = w_out.shape[0] // 2

    def body(order_ref, x_ref, g_ref, win_ref, wout_ref, cw_ref,
             h_ref, proj_ref, wing_ref, woutg_ref, cwg_ref, wg_ref, wob_ref, hs_ref, send_sems, recv_sems, local_sems):
        p = pl.program_id(0)
        t = pl.program_id(1)
        mx, my, c = _mesh_pos()
        me = 2 * mx + my
        chips = [(1 - mx, my), (mx, 1 - my), (1 - mx, 1 - my)]
        sibling = (mx, my, 1 - c)

        def remote(src, dst, sem, dev):
            return pltpu.make_async_remote_copy(
                src_ref=src, dst_ref=dst, send_sem=send_sems.at[sem], recv_sem=recv_sems.at[sem],
                device_id=dev, device_id_type=MESH)

        def w_in_part(blk, half):
            return wg_ref.at[blk, pl.ds(half * hin, hin)]

        def keep(blk, k):
            return pltpu.make_async_copy(wg_ref.at[blk], wing_ref.at[blk], local_sems.at[2 + k])

        def w_out_part(blk, half):
            return woutg_ref.at[blk, pl.ds(half * hout, hout)]

        def sends():
            out = []
            for j, (cx, cy) in enumerate(chips):
                blk = 2 * cx + cy
                out.append(remote(w_in_part(me, c), w_in_part(me, c), j, (cx, cy, c)))
                out.append(remote(w_in_part(blk, c), w_in_part(blk, c), 3 + j, sibling))
                out.append(remote(wob_ref.at[pl.ds(c * hout, hout)], w_out_part(me, c), 6 + j, (cx, cy, c)))
                out.append(remote(w_out_part(blk, c), w_out_part(blk, c), 9 + j, sibling))
                out.append(remote(cw_ref, cwg_ref.at[me], 12 + j, (cx, cy, c)))
            return out

        @pl.when(jnp.logical_and(p == 0, t == 0))
        def _():
            wg_ref[me] = win_ref[...].astype(BF16)
            wob_ref[...] = wout_ref[...].astype(BF16)
            keep(me, 0).start()
            mine = [pltpu.make_async_copy(wob_ref, woutg_ref.at[me], local_sems.at[0]),
                    pltpu.make_async_copy(cw_ref, cwg_ref.at[me], local_sems.at[1])]
            for cp in mine:
                cp.start()
            for k, cp in enumerate(sends()):
                if k % 5 == 4 or (k % 5 == 0 and k // 5 < 2):
                    cp.start()
            for cp in mine:
                cp.wait()

        for j, (cx, cy) in enumerate(chips):
            blk = 2 * cx + cy

            @pl.when(jnp.logical_and(p == j + 1, t == 0))
            def _():
                remote(w_in_part(blk, c), w_in_part(blk, c), j, (cx, cy, c)).wait_recv()
                remote(w_in_part(blk, c), w_in_part(blk, c), 3 + j, sibling).start()
                remote(w_in_part(blk, 1 - c), w_in_part(blk, 1 - c), 3 + j, sibling).wait_recv()
                keep(blk, j + 1).start()
                if j == 0:
                    sends()[5 * 2].start()
                if j == 1:
                    for jj in range(3):
                        sends()[5 * jj + 2].start()

        rows = pl.ds(pl.multiple_of(t * ts, ts), ts)

        @pl.when(p == 0)
        def _():
            xt = x_ref[...]
            r = lax.rsqrt(jnp.mean(xt * xt, axis=-1, keepdims=True) + EPS)
            hb = (xt * r * g_ref[...]).astype(BF16)
            h_ref[...] = hb
            hs_ref[rows, :] = hb

        proj_ref[...] = jnp.dot(hs_ref[rows, :], wg_ref[order_ref[p]], preferred_element_type=F32).astype(BF16)

        @pl.when(jnp.logical_and(p == N_SHARD - 1, t == nt - 1))
        def _():
            for j, (cx, cy) in enumerate(chips):
                blk = 2 * cx + cy
                remote(w_out_part(blk, c), w_out_part(blk, c), 6 + j, (cx, cy, c)).wait_recv()
                remote(w_out_part(blk, c), w_out_part(blk, c), 9 + j, sibling).start()
            for j, (cx, cy) in enumerate(chips):
                blk = 2 * cx + cy
                remote(w_out_part(blk, 1 - c), w_out_part(blk, 1 - c), 9 + j, sibling).wait_recv()
                remote(cw_ref, cwg_ref.at[blk], 12 + j, (cx, cy, c)).wait_recv()
            for cp in sends():
                cp.wait_send()
            keep(me, 0).wait()
            for j, (cx, cy) in enumerate(chips):
                keep(2 * cx + cy, j + 1).wait()

    hbm = pl.BlockSpec(memory_space=pl.ANY)
    return pl.pallas_call(
        body, name="rms_proj_gather",
        grid_spec=pltpu.PrefetchScalarGridSpec(
            num_scalar_prefetch=1, grid=(N_SHARD, nt),
            in_specs=[pl.BlockSpec((ts, D_MODEL), lambda p, t, o: (jnp.where(p == 0, t, nt - 1), 0)),
                      pl.BlockSpec((1, D_MODEL), lambda p, t, o: (0, 0)),
                      _whole_vmem(), _whole_vmem(), _whole_vmem()],
            out_specs=(pl.BlockSpec((ts, D_MODEL), lambda p, t, o: (jnp.where(p == 0, t, nt - 1), 0)),
                       pl.BlockSpec((ts, W_IN_COLS), lambda p, t, o: (t, o[p])),
                       hbm, hbm, hbm),
            scratch_shapes=[pltpu.VMEM((N_SHARD,) + w_in.shape, BF16), pltpu.VMEM(w_out.shape, BF16),
                            pltpu.VMEM((s, D_MODEL), BF16),
                            pltpu.SemaphoreType.DMA((15,)), pltpu.SemaphoreType.DMA((15,)),
                            pltpu.SemaphoreType.DMA((6,))]),
        out_shape=(jax.ShapeDtypeStruct((s, D_MODEL), BF16), jax.ShapeDtypeStruct((s, D_IN), BF16),
                   jax.ShapeDtypeStruct((N_SHARD,) + w_in.shape, BF16),
                   jax.ShapeDtypeStruct((N_SHARD,) + w_out.shape, BF16),
                   jax.ShapeDtypeStruct((N_SHARD,) + conv_w.shape, F32)),
        compiler_params=_cparams(("arbitrary", "arbitrary")),
    )(order, x, norm_g, w_in, w_out, conv_w)


def _halo_specs(ts, s, width, col_block):
    per = ts // HALO
    last = s // HALO - 1
    prev = pl.BlockSpec((HALO, width), lambda i: (jnp.maximum(i * per - 1, 0), col_block))
    nxt = pl.BlockSpec((HALO, width), lambda i: (jnp.minimum((i + 1) * per, last), col_block))
    return prev, nxt


def _layer_norm_stats(v):
    mu = jnp.mean(v, axis=-1, keepdims=True)
    vc = v - mu
    var = jnp.mean(vc * vc, axis=-1, keepdims=True)
    rstd = lax.rsqrt(var + EPS)
    return vc * rstd, rstd


def _fill_shifted(sh_ref, ext):
    n = ext.shape[0]
    sh_ref[0] = ext
    for r in range(1, SUB):
        sh_ref[r] = pltpu.roll(ext, n - r, axis=0)


def _fwd_fused(proj, x, target, cwg, conv_b, conv_ln_g, conv_ln_b, sgu_ln_g, sgu_ln_b, w_s, bs_t, woutg, final_g):
    s = proj.shape[0]
    ts = min(TS_FWD, s)
    nt = s // ts
    ra = min(ROWS_A, ts)

    def body(pm_ref, pp_ref, pn_ref, x_ref, t_ref, cw_ref, cb_ref, clg_ref, clb_ref, slg_ref, slb_ref, ws_ref,
             bst_ref, wo_ref, fg_ref,
             cv_ref, dx2_ref, dy_ref, gw_ref, gf_ref, se_ref,
             sh_ref, y_new, y_old, gw_acc, sem):
        i = pl.program_id(0)
        tile = jnp.minimum(i, nt - 1)
        keep_prev = (tile > 0).astype(F32)
        keep_next = (tile < nt - 1).astype(F32)

        @pl.when(i == 0)
        def _():
            y_old[...] = jnp.zeros_like(y_old)
            gw_acc[...] = jnp.zeros_like(gw_acc)
            gf_ref[...] = jnp.zeros_like(gf_ref)
            se_ref[...] = jnp.zeros_like(se_ref)

        for lb in range(C_BR // LANE):
            lanes = slice(lb * LANE, (lb + 1) * LANE)
            gates = slice(C_BR + lb * LANE, C_BR + (lb + 1) * LANE)

            def glu(ref):
                return ref[:, lanes].astype(F32) * _sigmoid(ref[:, gates].astype(F32))

            ext = jnp.concatenate([glu(pp_ref) * keep_prev, glu(pm_ref), glu(pn_ref) * keep_next], axis=0)
            _fill_shifted(sh_ref, ext)
            shard, off = divmod(lb * LANE, CONV_W_COLS)
            bias = cb_ref[:, lanes]

            def chunk(jc, carry):
                base = pl.multiple_of(jc * ra, ra)
                acc = jnp.zeros((ra, LANE), F32) + bias
                for k in range(KW):
                    o = k + 1
                    acc = acc + sh_ref[o % SUB, pl.ds(base + SUB * (o // SUB), ra), :] * cw_ref[
                        shard, k:k + 1, off:off + LANE]
                cv_ref[pl.ds(base, ra), lanes] = acc
                return carry

            lax.fori_loop(0, ts // ra, chunk, 0)

        live = (i > 0).astype(F32)
        yt = y_old[...]
        x2 = x_ref[...]
        for k in range(N_SHARD):
            x2 = x2 + jnp.dot(yt[:, k * W_OUT_ROWS:(k + 1) * W_OUT_ROWS], wo_ref[k], preferred_element_type=F32)
        r2 = lax.rsqrt(jnp.mean(x2 * x2, axis=-1, keepdims=True) + EPS)
        n2 = x2 * r2
        g = fg_ref[...]
        diff = n2 * g - t_ref[...]
        se_ref[...] += live * _fold8(diff * diff)
        dout = diff * (1.0 / D_MODEL)
        gf_ref[...] += live * _fold8(dout * n2)
        dn = dout * g
        dx2 = r2 * (dn - n2 * jnp.mean(dn * n2, axis=-1, keepdims=True))
        dx2_ref[...] = dx2
        dxb = dx2.astype(BF16)
        for k in range(N_SHARD):
            rows = slice(k * W_OUT_ROWS, (k + 1) * W_OUT_ROWS)
            dy_ref[:, rows] = _dot_nt(dxb, wo_ref[k]).astype(BF16)
            gw_acc[rows, :] += _dot_tn(yt[:, rows], dxb)

        lnh, _ = _layer_norm_stats(cv_ref[...])
        ln = lnh * clg_ref[...] + clb_ref[...]
        gc = pm_ref[:, 2 * C_BR:3 * C_BR].astype(F32)
        y_new[:, :C_BR] = (ln * _sigmoid(ln) * gc * _sigmoid(gc)).astype(BF16)

        vh, _ = _layer_norm_stats(pm_ref[:, 4 * C_BR:5 * C_BR].astype(F32))
        vn = (vh * slg_ref[...] + slb_ref[...]).astype(BF16)
        for hd in range(HEADS):
            w_h = ws_ref[hd].astype(BF16)
            b_h = bst_ref[:, hd:hd + 1]
            cols = slice(hd * HEAD_DIM, (hd + 1) * HEAD_DIM)
            for ch in range(ts // CHUNK):
                rows = slice(ch * CHUNK, (ch + 1) * CHUNK)
                mixed = jnp.dot(w_h, vn[rows, cols], preferred_element_type=F32) + b_h
                u = pm_ref[rows, 3 * C_BR + hd * HEAD_DIM:3 * C_BR + (hd + 1) * HEAD_DIM].astype(F32)
                gs = pm_ref[rows, 5 * C_BR + hd * HEAD_DIM:5 * C_BR + (hd + 1) * HEAD_DIM].astype(F32)
                y_new[rows, C_BR + hd * HEAD_DIM:C_BR + (hd + 1) * HEAD_DIM] = (
                    u * mixed * gs * _sigmoid(gs)).astype(BF16)

        y_old[...] = y_new[...]

        @pl.when(i == nt)
        def _():
            gf_ref[...] = jnp.broadcast_to(jnp.sum(gf_ref[...], axis=0, keepdims=True), gf_ref.shape)
            out = pltpu.make_async_copy(gw_acc, gw_ref, sem)
            out.start()
            out.wait()

    per = ts // HALO
    last = s // HALO - 1
    cur = lambda i: jnp.minimum(i, nt - 1)
    before = lambda i: jnp.maximum(i - 1, 0)
    prev = pl.BlockSpec((HALO, 2 * C_BR), lambda i: (jnp.maximum(cur(i) * per - 1, 0), 0))
    nxt = pl.BlockSpec((HALO, 2 * C_BR), lambda i: (jnp.minimum((cur(i) + 1) * per, last), 0))
    row = pl.BlockSpec((1, C_BR), lambda i: (0, 0))
    old_tile = pl.BlockSpec((ts, D_MODEL), lambda i: (before(i), 0))
    acc8 = pl.BlockSpec((SUB, D_MODEL), lambda i: (0, 0))
    return pl.pallas_call(
        body, name="fwd_fused",
        grid=(nt + 1,),
        out_shape=(jax.ShapeDtypeStruct((s, C_BR), F32), jax.ShapeDtypeStruct((s, D_MODEL), F32),
                   jax.ShapeDtypeStruct((s, 2 * C_BR), BF16), jax.ShapeDtypeStruct((2 * C_BR, D_MODEL), F32),
                   jax.ShapeDtypeStruct((SUB, D_MODEL), F32), jax.ShapeDtypeStruct((SUB, D_MODEL), F32)),
        in_specs=[pl.BlockSpec((ts, D_IN), lambda i: (cur(i), 0)), prev, nxt, old_tile, old_tile,
                  _whole_vmem(), row, row, row, row, row, _whole_vmem(), _whole_vmem(), _whole_vmem(), row],
        out_specs=(pl.BlockSpec((ts, C_BR), lambda i: (cur(i), 0)), old_tile,
                   pl.BlockSpec((ts, 2 * C_BR), lambda i: (before(i), 0)),
                   pl.BlockSpec(memory_space=pl.ANY), acc8, acc8),
        scratch_shapes=[pltpu.VMEM((SUB, ts + 2 * HALO, LANE), F32),
                        pltpu.VMEM((ts, 2 * C_BR), BF16),
                        pltpu.VMEM((ts, 2 * C_BR), BF16),
                        pltpu.VMEM((2 * C_BR, D_MODEL), F32),
                        pltpu.SemaphoreType.DMA(())],
        compiler_params=_cparams(("arbitrary",), 60 * 1024 * 1024),
    )(proj, proj, proj, x, target, cwg, conv_b, conv_ln_g, conv_ln_b, sgu_ln_g, sgu_ln_b, w_s, bs_t, woutg, final_g)


def _dsilu(v, sg):
    return sg * (1.0 + v * (1.0 - sg))


def _branch_bwd(proj, dy, cv, cwg, conv_ln_g, conv_ln_b, sgu_ln_g, sgu_ln_b, w_s, ws_t, bs_t, gf8, se8, dep):
    s = proj.shape[0]
    ts = min(TS_BWD, s)
    nt = s // ts
    ra = min(ROWS_A, ts)
    rb = min(ROWS_B, ts)
    te = ts + 2 * HALO

    def body(pm_ref, dym_ref, cvm_ref, gcp_ref, gcn_ref, dyp_ref, dyn_ref, cvp_ref, cvn_ref,
             cw_ref, clg_ref, clb_ref, slg_ref, slb_ref, ws_ref, wst_ref, bst_ref, gf_ref, se_ref, dep_ref,
             dp_ref, small_ref, gws_ref,
             sh_ref, glu_ref, dgl_ref, dcv_ref, acc_ref, gcw_ref, gbs_ref):
        i = pl.program_id(0)

        @pl.when(i == 0)
        def _():
            acc_ref[...] = jnp.zeros_like(acc_ref)
            gcw_ref[...] = jnp.zeros_like(gcw_ref)
            gbs_ref[...] = jnp.zeros_like(gbs_ref)
            gws_ref[...] = jnp.zeros_like(gws_ref)

        def ext(prev_ref, main, next_ref):
            return jnp.concatenate([prev_ref[...].astype(F32), main, next_ref[...].astype(F32)], axis=0)

        main = slice(HALO, HALO + ts)

        cv_e = ext(cvp_ref, cvm_ref[...], cvn_ref)
        gc_e = ext(gcp_ref, pm_ref[:, 2 * C_BR:3 * C_BR].astype(F32), gcn_ref)
        dyc_e = ext(dyp_ref, dym_ref[:, :C_BR].astype(F32), dyn_ref)
        lnh, rstd = _layer_norm_stats(cv_e)
        clg = clg_ref[...]
        ln = lnh * clg + clb_ref[...]
        sg_ln = _sigmoid(ln)
        sg_gc = _sigmoid(gc_e)
        d_ln = dyc_e * gc_e * sg_gc * _dsilu(ln, sg_ln)
        dp_ref[:, 2 * C_BR:3 * C_BR] = (
            dyc_e[main] * ln[main] * sg_ln[main] * _dsilu(gc_e[main], sg_gc[main])).astype(BF16)
        dlnh = d_ln * clg
        d_cv = rstd * (dlnh - jnp.mean(dlnh, axis=-1, keepdims=True)
                       - lnh * jnp.mean(dlnh * lnh, axis=-1, keepdims=True))
        row = lax.broadcasted_iota(jnp.int32, (te, 1), 0)
        valid = jnp.logical_and(jnp.logical_or(row >= HALO, i > 0),
                                jnp.logical_or(row < HALO + ts, i < nt - 1))
        d_cv = jnp.where(valid, d_cv, 0.0)
        dcv_ref[...] = d_cv
        acc_ref[0:8, :] += _fold8(d_cv[main])
        acc_ref[8:16, :] += _fold8(d_ln[main] * lnh[main])
        acc_ref[16:24, :] += _fold8(d_ln[main])

        for lb in range(C_BR // LANE):
            lanes = slice(lb * LANE, (lb + 1) * LANE)
            gates = slice(C_BR + lb * LANE, C_BR + (lb + 1) * LANE)
            shard, off = divmod(lb * LANE, CONV_W_COLS)
            av = pm_ref[:, lanes].astype(F32)
            sg = _sigmoid(pm_ref[:, gates].astype(F32))
            glu_ref[...] = av * sg
            _fill_shifted(sh_ref, dcv_ref[:, lanes])

            def chunk_a(jc, carry):
                base = pl.multiple_of(jc * ra, ra)
                acc = jnp.zeros((ra, LANE), F32)
                for j in range(KW):
                    o = j + 1
                    acc = acc + sh_ref[o % SUB, pl.ds(base + SUB * (o // SUB), ra), :] * cw_ref[
                        shard, KW - 1 - j:KW - j, off:off + LANE]
                dgl_ref[pl.ds(base, ra), :] = acc
                return carry

            lax.fori_loop(0, ts // ra, chunk_a, 0)

            def chunk_b(jc, accs):
                base = pl.multiple_of(jc * rb, rb)
                g = glu_ref[pl.ds(base, rb), :]
                out = []
                for j in range(KW):
                    o = j + 1
                    d = sh_ref[o % SUB, pl.ds(base + SUB * (o // SUB), rb), :]
                    out.append(accs[j] + _fold8(g * d))
                return tuple(out)

            accs = lax.fori_loop(0, ts // rb, chunk_b, tuple(jnp.zeros((SUB, LANE), F32) for _ in range(KW)))
            for j in range(KW):
                gcw_ref[j * SUB:(j + 1) * SUB, lanes] += accs[j]

            dglu = dgl_ref[...]
            dp_ref[:, lanes] = (dglu * sg).astype(BF16)
            dp_ref[:, gates] = (dglu * av * sg * (1.0 - sg)).astype(BF16)

        vh, vrstd = _layer_norm_stats(pm_ref[:, 4 * C_BR:5 * C_BR].astype(F32))
        slg = slg_ref[...]
        vn = (vh * slg + slb_ref[...]).astype(BF16)
        for hd in range(HEADS):
            w_h = ws_ref[hd].astype(BF16)
            wt_h = wst_ref[hd].astype(BF16)
            b_h = bst_ref[:, hd:hd + 1]
            cols = slice(hd * HEAD_DIM, (hd + 1) * HEAD_DIM)
            gws_h = jnp.zeros((CHUNK, CHUNK), F32)
            gbs_h = jnp.zeros((CHUNK, HEAD_DIM), F32)
            for ch in range(ts // CHUNK):
                rows = slice(ch * CHUNK, (ch + 1) * CHUNK)
                vn_b = vn[rows, cols]
                mixed = jnp.dot(w_h, vn_b, preferred_element_type=F32) + b_h
                u = pm_ref[rows, 3 * C_BR + hd * HEAD_DIM:3 * C_BR + (hd + 1) * HEAD_DIM].astype(F32)
                gs = pm_ref[rows, 5 * C_BR + hd * HEAD_DIM:5 * C_BR + (hd + 1) * HEAD_DIM].astype(F32)
                dys = dym_ref[rows, C_BR + hd * HEAD_DIM:C_BR + (hd + 1) * HEAD_DIM].astype(F32)
                sg_gs = _sigmoid(gs)
                silu_gs = gs * sg_gs
                dp_ref[rows, 3 * C_BR + hd * HEAD_DIM:3 * C_BR + (hd + 1) * HEAD_DIM] = (
                    dys * mixed * silu_gs).astype(BF16)
                dp_ref[rows, 5 * C_BR + hd * HEAD_DIM:5 * C_BR + (hd + 1) * HEAD_DIM] = (
                    dys * u * mixed * _dsilu(gs, sg_gs)).astype(BF16)
                d_mixed = dys * u * silu_gs
                dm_b = d_mixed.astype(BF16)
                gws_h = gws_h + _dot_nt(dm_b, vn_b)
                gbs_h = gbs_h + d_mixed
                dcv_ref[HALO + ch * CHUNK:HALO + (ch + 1) * CHUNK, cols] = jnp.dot(
                    wt_h, dm_b, preferred_element_type=F32)
            gws_ref[hd] += gws_h
            gbs_ref[:, cols] += gbs_h
        d_vn = dcv_ref[main, :]
        acc_ref[24:32, :] += _fold8(d_vn * vh)
        acc_ref[32:40, :] += _fold8(d_vn)
        dvh = d_vn * slg
        dp_ref[:, 4 * C_BR:5 * C_BR] = (vrstd * (
            dvh - jnp.mean(dvh, axis=-1, keepdims=True)
            - vh * jnp.mean(dvh * vh, axis=-1, keepdims=True))).astype(BF16)

        @pl.when(i == nt - 1)
        def _():
            small_ref[...] = jnp.zeros_like(small_ref)
            for a in range(5):
                small_ref[1 + a:2 + a, :] = jnp.sum(acc_ref[a * SUB:(a + 1) * SUB, :], axis=0, keepdims=True)
            ones = jnp.ones((SUB, HEAD_DIM), F32)
            for hd in range(HEADS):
                cols = slice(hd * HEAD_DIM, (hd + 1) * HEAD_DIM)
                rowsum = lax.dot_general(ones, gbs_ref[:, cols], (((1,), (1,)), ((), ())),
                                         precision=lax.Precision.HIGHEST, preferred_element_type=F32)
                small_ref[6:7, cols] = rowsum[0:1, :]
            small_ref[7:8, :] = gf_ref[0:1, :]
            small_ref[0:1, :] = jnp.sum(se_ref[...], axis=0, keepdims=True)
            for k in range(KW):
                j = KW - 1 - k
                small_ref[8 + k:9 + k, :] = jnp.sum(gcw_ref[j * SUB:(j + 1) * SUB, :], axis=0, keepdims=True)

    gc_prev, gc_next = _halo_specs(ts, s, C_BR, 2)
    lo_prev, lo_next = _halo_specs(ts, s, C_BR, 0)
    row = pl.BlockSpec((1, C_BR), lambda i: (0, 0))
    return pl.pallas_call(
        body, name="branch_bwd",
        grid=(nt,),
        out_shape=(jax.ShapeDtypeStruct((s, D_IN), BF16), jax.ShapeDtypeStruct((40, C_BR), F32),
                   jax.ShapeDtypeStruct((HEADS, CHUNK, CHUNK), F32)),
        in_specs=[pl.BlockSpec((ts, D_IN), lambda i: (i, 0)),
                  pl.BlockSpec((ts, 2 * C_BR), lambda i: (i, 0)),
                  pl.BlockSpec((ts, C_BR), lambda i: (i, 0)),
                  gc_prev, gc_next, lo_prev, lo_next, lo_prev, lo_next,
                  _whole_vmem(), row, row, row, row, _whole_vmem(), _whole_vmem(), _whole_vmem(),
                  _whole_vmem(), _whole_vmem(), _whole_vmem()],
        out_specs=(pl.BlockSpec((ts, D_IN), lambda i: (i, 0)),
                   pl.BlockSpec((40, C_BR), lambda i: (0, 0)),
                   pl.BlockSpec((HEADS, CHUNK, CHUNK), lambda i: (0, 0, 0))),
        scratch_shapes=[pltpu.VMEM((SUB, te, LANE), F32),
                        pltpu.VMEM((ts, LANE), F32),
                        pltpu.VMEM((ts, LANE), F32),
                        pltpu.VMEM((te, C_BR), F32),
                        pltpu.VMEM((5 * SUB, C_BR), F32),
                        pltpu.VMEM((KW * SUB, C_BR), F32),
                        pltpu.VMEM((CHUNK, C_BR), F32)],
        compiler_params=_cparams(("arbitrary",)),
    )(proj, dy, cv, proj, proj, dy, dy, cv, cv,
      cwg, conv_ln_g, conv_ln_b, sgu_ln_g, sgu_ln_b, w_s, ws_t, bs_t, gf8, se8, dep)


def _in_bwd(dproj, x, dx2, norm_g, wing, dep, first_tile, n_tiles, gx_prev=None, gn_prev=None):
    s = x.shape[0]
    ts = min(TS_INB, s)

    def body(*refs):
        dp_ref, x_ref, dx2_ref, g_ref, w_ref = refs[:5]
        gx_ref, gn_ref, acc_ref = refs[-3:]
        i = pl.program_id(0)

        @pl.when(i == 0)
        def _():
            acc_ref[...] = jnp.zeros_like(acc_ref)

        dh = jnp.zeros((ts, D_MODEL), F32)
        for k in range(N_SHARD):
            dh = dh + _dot_nt(dp_ref[:, k * W_IN_COLS:(k + 1) * W_IN_COLS], w_ref[k])
        xt = x_ref[...]
        r = lax.rsqrt(jnp.mean(xt * xt, axis=-1, keepdims=True) + EPS)
        n = xt * r
        acc_ref[...] += _fold8(dh * n)
        dn = dh * g_ref[...]
        gx_ref[...] = dx2_ref[...] + r * (dn - n * jnp.mean(dn * n, axis=-1, keepdims=True))

        @pl.when(i == n_tiles - 1)
        def _():
            total = jnp.broadcast_to(jnp.sum(acc_ref[...], axis=0, keepdims=True), gn_ref.shape)
            if gn_prev is not None:
                total = total + refs[7][...]
            gn_ref[...] = total

    tile = pl.BlockSpec((ts, D_MODEL), lambda i: (i + first_tile, 0))
    in_specs = [pl.BlockSpec((ts, D_IN), lambda i: (i + first_tile, 0)), tile, tile,
                pl.BlockSpec((1, D_MODEL), lambda i: (0, 0)), _whole_vmem(), pl.BlockSpec(memory_space=pl.ANY)]
    operands = [dproj, x, dx2, norm_g, wing, dep]
    aliases = {}
    if gx_prev is not None:
        in_specs += [pl.BlockSpec(memory_space=pl.ANY), _whole_vmem()]
        operands += [gx_prev, gn_prev]
        aliases = {6: 0}
    return pl.pallas_call(
        body, name="in_bwd_%d" % first_tile,
        grid=(n_tiles,),
        out_shape=(jax.ShapeDtypeStruct((s, D_MODEL), F32), jax.ShapeDtypeStruct((SUB, D_MODEL), F32)),
        in_specs=in_specs,
        out_specs=(tile, pl.BlockSpec((SUB, D_MODEL), lambda i: (0, 0))),
        scratch_shapes=[pltpu.VMEM((SUB, D_MODEL), F32)],
        input_output_aliases=aliases,
        compiler_params=_cparams(("arbitrary",)),
    )(*operands)


def _grad_w_in(h, dproj, dep):
    s = h.shape[0]
    tk = min(TK_GW, s)
    half = D_MODEL // 2

    def body(h_ref, dp_ref, dep_ref, o_ref):
        @pl.when(pl.program_id(1) == 0)
        def _():
            o_ref[...] = jnp.zeros_like(o_ref)

        o_ref[0] += _dot_tn(h_ref[...], dp_ref[...]).reshape(2, half, W_IN_COLS)

    return pl.pallas_call(
        body, name="grad_w_in",
        grid=(N_SHARD, s // tk),
        out_shape=jax.ShapeDtypeStruct((N_SHARD, 2, half, W_IN_COLS), F32),
        in_specs=[pl.BlockSpec((tk, D_MODEL), lambda k, t: (t, 0)),
                  pl.BlockSpec((tk, W_IN_COLS), lambda k, t: (t, k)), _whole_vmem()],
        out_specs=pl.BlockSpec((1, 2, half, W_IN_COLS), lambda k, t: (k, 0, 0, 0)),
        compiler_params=_cparams(("parallel", "arbitrary")),
    )(h, dproj, dep)


HBM_SPEC = pl.BlockSpec(memory_space=pltpu.HBM)
SEM_SPEC = pl.BlockSpec(memory_space=pltpu.SEMAPHORE)
SIDE_EFFECT = pltpu.SideEffectType.DATAFLOW_SIDE_EFFECTING


def _remote_copies(plan, bufs, send_sems, recv_sems):
    x, y, c = _mesh_pos()
    return [pltpu.make_async_remote_copy(src_ref=src, dst_ref=dst, send_sem=send_sems.at[k],
                                         recv_sem=recv_sems.at[k], device_id=dev, device_id_type=MESH)
            for k, (src, dst, dev) in enumerate(plan(x, y, c, *bufs))]


def _start_copies(name, bufs, n_copies, plan):
    n = len(bufs)

    def body(*refs):
        for cp in _remote_copies(plan, refs[:n], refs[n], refs[n + 1]):
            cp.start()
        refs[-1][...] = jnp.zeros_like(refs[-1])

    outs = pl.pallas_call(
        body, name=name,
        out_shape=(pltpu.SemaphoreType.DMA((n_copies,)), pltpu.SemaphoreType.DMA((n_copies,)),
                   *[pltpu.HBM(b.shape, b.dtype) for b in bufs], jax.ShapeDtypeStruct((SUB, LANE), F32)),
        in_specs=[HBM_SPEC] * n,
        out_specs=(SEM_SPEC, SEM_SPEC, *[HBM_SPEC] * n, _whole_vmem()),
        input_output_aliases={i: 2 + i for i in range(n)},
        compiler_params=pltpu.CompilerParams(has_side_effects=SIDE_EFFECT),
    )(*[pltpu.with_memory_space_constraint(b, pltpu.HBM) for b in bufs])
    return outs[0], outs[1], list(outs[2:2 + n]), outs[-1]


def _wait_copies(name, send_sems, recv_sems, bufs, plan, after):
    n = len(bufs)

    def body(*refs):
        for cp in _remote_copies(plan, refs[:n], refs[n], refs[n + 1]):
            cp.wait_send()
            cp.wait_recv()

    outs = pl.pallas_call(
        body, name=name,
        out_shape=tuple(pltpu.HBM(b.shape, b.dtype) for b in bufs),
        in_specs=[HBM_SPEC] * n + [SEM_SPEC, SEM_SPEC, pl.BlockSpec(memory_space=pl.ANY)],
        out_specs=(HBM_SPEC,) * n,
        input_output_aliases={i: i for i in range(n)},
        compiler_params=pltpu.CompilerParams(has_side_effects=SIDE_EFFECT),
    )(*bufs, send_sems, recv_sems, after)
    return list(outs)


def _landing(shape, dtype):
    return lax.empty(shape, dtype)


def _plan_pair_exchange(x, y, c, g, r):
    return [(g.at[k, 1 - c], r.at[k], (x, y, 1 - c)) for k in range(N_SHARD)]


def _plan_chip_exchange(x, y, c, a, r):
    chips = [(1 - x, y), (x, 1 - y), (1 - x, 1 - y)]
    return [(a.at[2 * cx + cy], r.at[j], (cx, cy, c)) for j, (cx, cy) in enumerate(chips)]


def _plan_pair_gather(x, y, c, f):
    return [(f.at[c], f.at[c], (x, y, 1 - c))]


def _plan_all_gather(x, y, c, own, land):
    me = 4 * x + 2 * y + c
    flip = lambda v, bit: 1 - v if bit else v
    return [(own, land.at[me], (flip(x, m >> 2 & 1), flip(y, m >> 1 & 1), flip(c, m & 1))) for m in range(1, 8)]


def _add_pair(g, r, pos, name):
    _, _, rows, cols = g.shape
    tr = min(256, rows)

    def body(pos_ref, g_ref, r_ref, o_ref, ob_ref):
        v = g_ref[0] + r_ref[...]
        o_ref[...] = v
        ob_ref[...] = v.astype(BF16)

    blk = pl.BlockSpec((1, tr, cols), lambda k, t, pos_ref: (k, t, 0))
    return pl.pallas_call(
        body, name=name,
        grid_spec=pltpu.PrefetchScalarGridSpec(
            num_scalar_prefetch=1, grid=(N_SHARD, rows // tr),
            in_specs=[pl.BlockSpec((1, 1, tr, cols), lambda k, t, pos_ref: (k, pos_ref[1], t, 0)), blk],
            out_specs=(blk, blk)),
        out_shape=(jax.ShapeDtypeStruct((N_SHARD, rows, cols), F32), jax.ShapeDtypeStruct((N_SHARD, rows, cols), BF16)),
        compiler_params=_cparams(("parallel", "parallel")),
    )(pos, g, r)


def _add_chips(a, r, pos, name):
    _, rows, cols = a.shape
    tr = min(256, rows)

    def body(pos_ref, a_ref, r_ref, o_ref):
        o_ref[0] = ((a_ref[0] + r_ref[0].astype(F32)) + r_ref[1].astype(F32)) + r_ref[2].astype(F32)

    return pl.pallas_call(
        body, name=name,
        grid_spec=pltpu.PrefetchScalarGridSpec(
            num_scalar_prefetch=1, grid=(rows // tr,),
            in_specs=[pl.BlockSpec((1, tr, cols), lambda t, pos_ref: (pos_ref[0], t, 0)),
                      pl.BlockSpec((3, tr, cols), lambda t, pos_ref: (0, t, 0))],
            out_specs=pl.BlockSpec((1, tr, cols), lambda t, pos_ref: (pos_ref[1], t, 0))),
        out_shape=jax.ShapeDtypeStruct((2, rows, cols), F32),
        compiler_params=_cparams(("parallel",)),
    )(pos, a, r)


def _sum_slots(pos_ref, own, land_ref, rows):
    me = pos_ref[2]
    total = None
    for d in range(8):
        term = jnp.where(me == d, own, land_ref[d] if rows is None else land_ref[d, rows, :])
        total = term if total is None else total + term
    return total


def _sum_small(pos, small, small_land, gws, gws_land):
    def body(pos_ref, sm_ref, sml_ref, gw_ref, gwl_ref, o_sm, o_gw):
        o_sm[...] = _sum_slots(pos_ref, sm_ref[...], sml_ref, None)
        o_gw[...] = _sum_slots(pos_ref, gw_ref[...], gwl_ref, None)

    return pl.pallas_call(
        body, name="sum_small",
        grid_spec=pltpu.PrefetchScalarGridSpec(
            num_scalar_prefetch=1, grid=(1,),
            in_specs=[_whole_vmem()] * 4, out_specs=[_whole_vmem()] * 2),
        out_shape=[jax.ShapeDtypeStruct(small.shape, F32), jax.ShapeDtypeStruct(gws.shape, F32)],
        compiler_params=_cparams(("arbitrary",)),
    )(pos, small, small_land, gws, gws_land)


def _adamw_math(w, g, m, v):
    m = ADAM_B1 * m + (1.0 - ADAM_B1) * g
    v = ADAM_B2 * v + (1.0 - ADAM_B2) * (g * g)
    m_hat = m / (1.0 - ADAM_B1 ** ADAM_STEP)
    v_hat = v / (1.0 - ADAM_B2 ** ADAM_STEP)
    delta = -ADAM_LR * (m_hat / (jnp.sqrt(v_hat) + ADAM_EPS) + ADAM_WD * w)
    return delta, m, v


def _adamw_large(w, g, m, v, dep, name):
    rows, cols = w.shape
    tr = min(256, rows)

    def body(w_ref, g_ref, m_ref, v_ref, dep_ref, d_ref, nm_ref, nv_ref):
        d_ref[...], nm_ref[...], nv_ref[...] = _adamw_math(w_ref[...], g_ref[...], m_ref[...], v_ref[...])

    tile = pl.BlockSpec((tr, cols), lambda t: (t, 0))
    return pl.pallas_call(
        body, name=name,
        grid=(rows // tr,),
        out_shape=(jax.ShapeDtypeStruct(w.shape, F32),) * 3,
        in_specs=[tile] * 4 + [_whole_vmem()], out_specs=(tile,) * 3,
        compiler_params=_cparams(("parallel",)),
    )(w, g, m, v, dep)


_ROW_OF = {"conv_b": 1, "conv_ln_g": 2, "conv_ln_b": 3, "sgu_ln_g": 4, "sgu_ln_b": 5, "b_s": 6, "final_g": 7}
_CONV_W_ROW = 8


def _adamw_small(call_name, names, grads, pos, params):
    def body(pos_ref, p_ref, q_ref, *refs):
        gn_ref, gnl_ref = p_ref, q_ref
        n_in = 3 * len(names)
        ins, outs = refs[:n_in], refs[n_in:]
        me = pos_ref[0]
        for a, name in enumerate(names):
            w_ref, m_ref, v_ref = ins[3 * a:3 * a + 3]
            if name == "conv_w":
                g = jnp.zeros((KW, CONV_W_COLS), F32)
                for k in range(N_SHARD):
                    blk = p_ref[_CONV_W_ROW:_CONV_W_ROW + KW, k * CONV_W_COLS:(k + 1) * CONV_W_COLS]
                    g = jnp.where(me == k, blk, g)
            elif name == "w_s":
                g = q_ref[...]
            elif name == "norm_g":
                g = _sum_slots(pos_ref, gn_ref[0:1, :], gnl_ref, slice(0, 1))
            else:
                g = p_ref[_ROW_OF[name]:_ROW_OF[name] + 1, :]
            delta, nm, nv = _adamw_math(w_ref[...], g, m_ref[...], v_ref[...])
            for o_ref, val in zip(outs[4 * a:4 * a + 4], (g, delta, nm, nv)):
                o_ref[...] = val

    operands, shapes = [], []
    for name in names:
        operands += list(params[name])
        shapes += [jax.ShapeDtypeStruct(params[name][0].shape, F32)] * 4
    outs = pl.pallas_call(
        body, name=call_name,
        grid_spec=pltpu.PrefetchScalarGridSpec(
            num_scalar_prefetch=1, grid=(1,),
            in_specs=[_whole_vmem()] * (2 + len(operands)),
            out_specs=[_whole_vmem()] * len(shapes)),
        out_shape=shapes,
        compiler_params=_cparams(("arbitrary",)),
    )(pos, *grads, *operands)
    return {name: tuple(outs[4 * a:4 * a + 4]) for a, name in enumerate(names)}


def kernel(x, norm_g, w_in, conv_w, conv_b, conv_ln_g, conv_ln_b, sgu_ln_g, sgu_ln_b, w_s, b_s, w_out, final_g, loss_target, m_norm_g, m_w_in, m_conv_w, m_conv_b, m_conv_ln_g, m_conv_ln_b, m_sgu_ln_g, m_sgu_ln_b, m_w_s, m_b_s, m_w_out, m_final_g, v_norm_g, v_w_in, v_conv_w, v_conv_b, v_conv_ln_g, v_conv_ln_b, v_sgu_ln_g, v_sgu_ln_b, v_w_s, v_b_s, v_w_out, v_final_g):
    xi, yi, ci = _mesh_pos()
    pos = jnp.stack([2 * xi + yi, ci, 4 * xi + 2 * yi + ci]).astype(jnp.int32)

    x2d = x[0]
    tgt = loss_target[0]
    fg = final_g.reshape(1, D_MODEL)
    ws3 = w_s[0]
    ws_t = jnp.swapaxes(ws3, 1, 2)
    bs_t = jnp.transpose(b_s[0])

    chip = 2 * xi + yi
    order = jnp.stack([chip, 2 * (1 - xi) + yi, 2 * xi + 1 - yi, 2 * (1 - xi) + 1 - yi]).astype(jnp.int32)
    h, proj, wing, woutg, cwg = _rms_proj_gather(x2d, norm_g, w_in[0], w_out[0], conv_w[0], order)
    cv, dx2, dy, gwout, gf8, se8 = _fwd_fused(proj, x2d, tgt, cwg, conv_b, conv_ln_g, conv_ln_b, sgu_ln_g,
                                              sgu_ln_b, ws3, bs_t, woutg, fg)
    in_rows, out_rows = D_MODEL // 2, W_OUT_ROWS // 2
    gwout4 = gwout.reshape(N_SHARD, 2, out_rows, D_MODEL)
    ss, rs, (gwout4, r1_out), tok = _start_copies(
        "start_pair_exchange_w_out", [gwout4, _landing((N_SHARD, out_rows, D_MODEL), F32)], N_SHARD,
        _plan_pair_exchange)
    dproj, small, gws3 = _branch_bwd(proj, dy, cv, cwg, conv_ln_g, conv_ln_b, sgu_ln_g, sgu_ln_b, ws3, ws_t, bs_t,
                                     gf8, se8, tok)
    gws = gws3.reshape(HEADS * CHUNK, CHUNK)
    gwout4, r1_out = _wait_copies("wait_pair_exchange_w_out", ss, rs, [gwout4, r1_out], _plan_pair_exchange, dproj)
    a_out, a_out_bf = _add_pair(gwout4, r1_out, pos, "add_pair_w_out")

    def plan_b(x, y, c, a, r, sm, sml, gw, gwl):
        return (_plan_chip_exchange(x, y, c, a, r) + _plan_all_gather(x, y, c, sm, sml)
                + _plan_all_gather(x, y, c, gw, gwl))

    ss, rs, bufs_b, tok = _start_copies(
        "start_chip_exchange_w_out",
        [a_out_bf, _landing((3, out_rows, D_MODEL), BF16), small, _landing((8,) + small.shape, F32),
         gws, _landing((8,) + gws.shape, F32)], 3 + 7 + 7, plan_b)
    gwin = _grad_w_in(h, dproj, tok)
    ss_c, rs_c, (gwin, r1_in), tok = _start_copies(
        "start_pair_exchange_w_in", [gwin, _landing((N_SHARD, in_rows, W_IN_COLS), F32)], N_SHARD,
        _plan_pair_exchange)
    nt = x2d.shape[0] // min(TS_INB, x2d.shape[0])
    cut = nt // 2
    gx_a, gn_a = _in_bwd(dproj, x2d, dx2, norm_g, wing, tok, 0, cut)

    gwin, r1_in = _wait_copies("wait_pair_exchange_w_in", ss_c, rs_c, [gwin, r1_in], _plan_pair_exchange, gx_a)
    a_in, a_in_bf = _add_pair(gwin, r1_in, pos, "add_pair_w_in")
    a_out_bf, r2_out, small, small_land, gws, gws_land = _wait_copies(
        "wait_chip_exchange_w_out", ss, rs, bufs_b, plan_b, gx_a)
    f_out = _add_chips(a_out, r2_out, pos, "add_chips_w_out")
    p, q = _sum_small(pos, small, small_land, gws, gws_land)
    loss = (0.5 / D_MODEL) * jnp.sum(p[0])

    flat = lambda a: a.reshape(1, C_BR)
    flat_ws = lambda a: a.reshape(HEADS * CHUNK, CHUNK)
    params = {
        "norm_g": (norm_g, m_norm_g, v_norm_g),
        "conv_b": (conv_b, m_conv_b, v_conv_b),
        "conv_ln_g": (conv_ln_g, m_conv_ln_g, v_conv_ln_g),
        "conv_ln_b": (conv_ln_b, m_conv_ln_b, v_conv_ln_b),
        "sgu_ln_g": (sgu_ln_g, m_sgu_ln_g, v_sgu_ln_g),
        "sgu_ln_b": (sgu_ln_b, m_sgu_ln_b, v_sgu_ln_b),
        "b_s": (flat(b_s), flat(m_b_s), flat(v_b_s)),
        "final_g": (flat(final_g), flat(m_final_g), flat(v_final_g)),
        "conv_w": (conv_w[0], m_conv_w[0], v_conv_w[0]),
        "w_s": (flat_ws(w_s), flat_ws(m_w_s), flat_ws(v_w_s)),
    }
    res = _adamw_small("adamw_small", [n for n in params if n != "norm_g"], (p, q), pos, params)

    def plan_d(x, y, c, a, r, f):
        return _plan_chip_exchange(x, y, c, a, r) + _plan_pair_gather(x, y, c, f)

    ss, rs, bufs_d, tok = _start_copies(
        "start_chip_exchange_w_in", [a_in_bf, _landing((3, in_rows, W_IN_COLS), BF16), f_out], 3 + 1, plan_d)
    grad_x, gn8 = _in_bwd(dproj, x2d, dx2, norm_g, wing, tok, cut, nt - cut, gx_a, gn_a)
    a_in_bf, r2_in, f_out = _wait_copies("wait_chip_exchange_w_in", ss, rs, bufs_d, plan_d, grad_x)
    f_in = _add_chips(a_in, r2_in, pos, "add_chips_w_in")

    def plan_e(x, y, c, f, gn, gnl):
        return _plan_pair_gather(x, y, c, f) + _plan_all_gather(x, y, c, gn, gnl)

    ss, rs, bufs_e, tok = _start_copies(
        "start_pair_gather_w_in", [f_in, gn8, _landing((8,) + gn8.shape, F32)], 1 + 7, plan_e)
    g_w_out = f_out.reshape(W_OUT_ROWS, D_MODEL)
    d_w_out, nm_w_out, nv_w_out = _adamw_large(w_out[0], g_w_out, m_w_out[0], v_w_out[0], tok, "adamw_w_out")
    f_in, gn8, gn_land = _wait_copies("wait_pair_gather_w_in", ss, rs, bufs_e, plan_e, d_w_out)
    g_w_in = f_in.reshape(D_MODEL, W_IN_COLS)
    d_w_in, nm_w_in, nv_w_in = _adamw_large(w_in[0], g_w_in, m_w_in[0], v_w_in[0], tok, "adamw_w_in")
    res.update(_adamw_small("adamw_norm_g", ["norm_g"], (gn8, gn_land), pos, params))
    res["w_in"] = tuple(a[None] for a in (g_w_in, d_w_in, nm_w_in, nv_w_in))
    res["w_out"] = tuple(a[None] for a in (g_w_out, d_w_out, nm_w_out, nv_w_out))
    res["conv_w"] = tuple(a[None] for a in res["conv_w"])
    res["w_s"] = tuple(a.reshape(w_s.shape) for a in res["w_s"])
    res["b_s"] = tuple(a.reshape(b_s.shape) for a in res["b_s"])
    res["final_g"] = tuple(a.reshape(final_g.shape) for a in res["final_g"])

    order = ("norm_g", "w_in", "conv_w", "conv_b", "conv_ln_g", "conv_ln_b", "sgu_ln_g", "sgu_ln_b",
             "w_s", "b_s", "w_out", "final_g")
    out = [loss, grad_x[None]]
    for part in range(4):
        out += [res[name][part] for name in order]
    return tuple(out)
```

```python
import jax
import jax.numpy as jnp
from jax import lax
from jax.experimental import pallas as pl
from jax.experimental.pallas import tpu as pltpu

F32 = jnp.float32
BF16 = jnp.bfloat16
MESH = pl.DeviceIdType.MESH

EPS = 1e-6
D_MODEL = 1024
C_BR = 1024
D_IN = 6 * C_BR
N_SHARD = 4
W_IN_COLS = D_IN // N_SHARD
W_OUT_ROWS = 2 * C_BR // N_SHARD
CONV_W_COLS = C_BR // N_SHARD
KW = 31
HALO = 16
HEADS = 8
HEAD_DIM = 128
CHUNK = 128
LANE = 128
SUB = 8

ADAM_LR = 0.001
ADAM_B1 = 0.9
ADAM_B2 = 0.999
ADAM_EPS = 1e-08
ADAM_WD = 0.01
ADAM_STEP = 10

TS_PROJ = 512
TS_FWD = 256
TS_OUT = 512
TS_BWD = 256
TS_INB = 512
TK_GW = 2048
ROWS_A = 128
ROWS_B = 64
VMEM_LIMIT = 56 * 1024 * 1024


def _cparams(sem=None, vmem=VMEM_LIMIT):
    kw = dict(vmem_limit_bytes=vmem)
    if sem is not None:
        kw["dimension_semantics"] = sem
    return pltpu.CompilerParams(**kw)


def _whole_vmem():
    return pl.BlockSpec(memory_space=pltpu.VMEM)


def _sigmoid(v):
    return 1.0 / (1.0 + jnp.exp(-v))


def _fold8(v):
    n, c = v.shape
    return v.reshape(n // SUB, SUB, c).sum(axis=0)


def _dot_nt(a, b):
    return lax.dot_general(a, b, (((1,), (1,)), ((), ())), preferred_element_type=F32)


def _dot_tn(a, b):
    return lax.dot_general(a, b, (((0,), (0,)), ((), ())), preferred_element_type=F32)


def _mesh_pos():
    return lax.axis_index("x"), lax.axis_index("y"), lax.axis_index("c")


def _rms_proj_gather(x, norm_g, w_in, w_out, conv_w, order):
    s = x.shape[0]
    ts = min(TS_PROJ, s)
    nt = s // ts
    hin = w_in.shape[0] // 2
    hout = w_out.shape[0] // 2

    def body(order_ref, x_ref, g_ref, win_ref, wout_ref, cw_ref,
             h_ref, proj_ref, wing_ref, woutg_ref, cwg_ref, wg_ref, wob_ref, hs_ref, send_sems, recv_sems, local_sems):
        p = pl.program_id(0)
        t = pl.program_id(1)
        mx, my, c = _mesh_pos()
        me = 2 * mx + my
        chips = [(1 - mx, my), (mx, 1 - my), (1 - mx, 1 - my)]
        sibling = (mx, my, 1 - c)

        def remote(src, dst, sem, dev):
            return pltpu.make_async_remote_copy(
                src_ref=src, dst_ref=dst, send_sem=send_sems.at[sem], recv_sem=recv_sems.at[sem],
                device_id=dev, device_id_type=MESH)

        def w_in_part(blk, half):
            return wg_ref.at[blk, pl.ds(half * hin, hin)]

        def keep(blk, k):
            return pltpu.make_async_copy(wg_ref.at[blk], wing_ref.at[blk], local_sems.at[2 + k])

        def w_out_part(blk, half):
            return woutg_ref.at[blk, pl.ds(half * hout, hout)]

        def sends():
            out = []
            for j, (cx, cy) in enumerate(chips):
                blk = 2 * cx + cy
                out.append(remote(w_in_part(me, c), w_in_part(me, c), j, (cx, cy, c)))
                out.append(remote(w_in_part(blk, c), w_in_part(blk, c), 3 + j, sibling))
                out.append(remote(wob_ref.at[pl.ds(c * hout, hout)], w_out_part(me, c), 6 + j, (cx, cy, c)))
                out.append(remote(w_out_part(blk, c), w_out_part(blk, c), 9 + j, sibling))
                out.append(remote(cw_ref, cwg_ref.at[me], 12 + j, (cx, cy, c)))
            return out

        @pl.when(jnp.logical_and(p == 0, t == 0))
        def _():
            wg_ref[me] = win_ref[...].astype(BF16)
            wob_ref[...] = wout_ref[...].astype(BF16)
            keep(me, 0).start()
            mine = [pltpu.make_async_copy(wob_ref, woutg_ref.at[me], local_sems.at[0]),
                    pltpu.make_async_copy(cw_ref, cwg_ref.at[me], local_sems.at[1])]
            for cp in mine:
                cp.start()
            for k, cp in enumerate(sends()):
                if k % 5 == 4 or (k % 5 == 0 and k // 5 < 2):
                    cp.start()
            for cp in mine:
                cp.wait()

        for j, (cx, cy) in enumerate(chips):
            blk = 2 * cx + cy

            @pl.when(jnp.logical_and(p == j + 1, t == 0))
            def _():
                remote(w_in_part(blk, c), w_in_part(blk, c), j, (cx, cy, c)).wait_recv()
                remote(w_in_part(blk, c), w_in_part(blk, c), 3 + j, sibling).start()
                remote(w_in_part(blk, 1 - c), w_in_part(blk, 1 - c), 3 + j, sibling).wait_recv()
                keep(blk, j + 1).start()
                if j == 0:
                    sends()[5 * 2].start()
                if j == 1:
                    for jj in range(3):
                        sends()[5 * jj + 2].start()

        rows = pl.ds(pl.multiple_of(t * ts, ts), ts)

        @pl.when(p == 0)
        def _():
            xt = x_ref[...]
            r = lax.rsqrt(jnp.mean(xt * xt, axis=-1, keepdims=True) + EPS)
            hb = (xt * r * g_ref[...]).astype(BF16)
            h_ref[...] = hb
            hs_ref[rows, :] = hb

        proj_ref[...] = jnp.dot(hs_ref[rows, :], wg_ref[order_ref[p]], preferred_element_type=F32).astype(BF16)

        @pl.when(jnp.logical_and(p == N_SHARD - 1, t == nt - 1))
        def _():
            for j, (cx, cy) in enumerate(chips):
                blk = 2 * cx + cy
                remote(w_out_part(blk, c), w_out_part(blk, c), 6 + j, (cx, cy, c)).wait_recv()
                remote(w_out_part(blk, c), w_out_part(blk, c), 9 + j, sibling).start()
            for j, (cx, cy) in enumerate(chips):
                blk = 2 * cx + cy
                remote(w_out_part(blk, 1 - c), w_out_part(blk, 1 - c), 9 + j, sibling).wait_recv()
                remote(cw_ref, cwg_ref.at[blk], 12 + j, (cx, cy, c)).wait_recv()
            for cp in sends():
                cp.wait_send()
            keep(me, 0).wait()
            for j, (cx, cy) in enumerate(chips):
                keep(2 * cx + cy, j + 1).wait()

    hbm = pl.BlockSpec(memory_space=pl.ANY)
    return pl.pallas_call(
        body, name="rms_proj_gather",
        grid_spec=pltpu.PrefetchScalarGridSpec(
            num_scalar_prefetch=1, grid=(N_SHARD, nt),
            in_specs=[pl.BlockSpec((ts, D_MODEL), lambda p, t, o: (jnp.where(p == 0, t, nt - 1), 0)),
                      pl.BlockSpec((1, D_MODEL), lambda p, t, o: (0, 0)),
                      _whole_vmem(), _whole_vmem(), _whole_vmem()],
            out_specs=(pl.BlockSpec((ts, D_MODEL), lambda p, t, o: (jnp.where(p == 0, t, nt - 1), 0)),
                       pl.BlockSpec((ts, W_IN_COLS), lambda p, t, o: (t, o[p])),
                       hbm, hbm, hbm),
            scratch_shapes=[pltpu.VMEM((N_SHARD,) + w_in.shape, BF16), pltpu.VMEM(w_out.shape, BF16),
                            pltpu.VMEM((s, D_MODEL), BF16),
                            pltpu.SemaphoreType.DMA((15,)), pltpu.SemaphoreType.DMA((15,)),
                            pltpu.SemaphoreType.DMA((6,))]),
        out_shape=(jax.ShapeDtypeStruct((s, D_MODEL), BF16), jax.ShapeDtypeStruct((s, D_IN), BF16),
                   jax.ShapeDtypeStruct((N_SHARD,) + w_in.shape, BF16),
                   jax.ShapeDtypeStruct((N_SHARD,) + w_out.shape, BF16),
                   jax.ShapeDtypeStruct((N_SHARD,) + conv_w.shape, F32)),
        compiler_params=_cparams(("arbitrary", "arbitrary")),
    )(order, x, norm_g, w_in, w_out, conv_w)


def _halo_specs(ts, s, width, col_block):
    per = ts // HALO
    last = s // HALO - 1
    prev = pl.BlockSpec((HALO, width), lambda i: (jnp.maximum(i * per - 1, 0), col_block))
    nxt = pl.BlockSpec((HALO, width), lambda i: (jnp.minimum((i + 1) * per, last), col_block))
    return prev, nxt


def _layer_norm_stats(v):
    mu = jnp.mean(v, axis=-1, keepdims=True)
    vc = v - mu
    var = jnp.mean(vc * vc, axis=-1, keepdims=True)
    rstd = lax.rsqrt(var + EPS)
    return vc * rstd, rstd


def _fill_shifted(sh_ref, ext):
    n = ext.shape[0]
    sh_ref[0] = ext
    for r in range(1, SUB):
        sh_ref[r] = pltpu.roll(ext, n - r, axis=0)


def _branch_fwd(proj, cwg, conv_b, conv_ln_g, conv_ln_b, sgu_ln_g, sgu_ln_b, w_s, bs_t):
    s = proj.shape[0]
    ts = min(TS_FWD, s)
    nt = s // ts
    ra = min(ROWS_A, ts)

    def body(pm_ref, pp_ref, pn_ref, cw_ref, cb_ref, clg_ref, clb_ref, slg_ref, slb_ref, ws_ref, bst_ref,
             y_new, cv_ref, sh_ref):
        i = pl.program_id(0)
        keep_prev = (i > 0).astype(F32)
        keep_next = (i < nt - 1).astype(F32)

        for lb in range(C_BR // LANE):
            lanes = slice(lb * LANE, (lb + 1) * LANE)
            gates = slice(C_BR + lb * LANE, C_BR + (lb + 1) * LANE)

            def glu(ref):
                return ref[:, lanes].astype(F32) * _sigmoid(ref[:, gates].astype(F32))

            ext = jnp.concatenate([glu(pp_ref) * keep_prev, glu(pm_ref), glu(pn_ref) * keep_next], axis=0)
            _fill_shifted(sh_ref, ext)
            shard, off = divmod(lb * LANE, CONV_W_COLS)
            bias = cb_ref[:, lanes]

            def chunk(jc, carry):
                base = pl.multiple_of(jc * ra, ra)
                acc = jnp.zeros((ra, LANE), F32) + bias
                for k in range(KW):
                    o = k + 1
                    acc = acc + sh_ref[o % SUB, pl.ds(base + SUB * (o // SUB), ra), :] * cw_ref[
                        shard, k:k + 1, off:off + LANE]
                cv_ref[pl.ds(base, ra), lanes] = acc
                return carry

            lax.fori_loop(0, ts // ra, chunk, 0)

        lnh, _ = _layer_norm_stats(cv_ref[...])
        ln = lnh * clg_ref[...] + clb_ref[...]
        gc = pm_ref[:, 2 * C_BR:3 * C_BR].astype(F32)
        y_new[:, :C_BR] = (ln * _sigmoid(ln) * gc * _sigmoid(gc)).astype(BF16)

        vh, _ = _layer_norm_stats(pm_ref[:, 4 * C_BR:5 * C_BR].astype(F32))
        vn = (vh * slg_ref[...] + slb_ref[...]).astype(BF16)
        for hd in range(HEADS):
            w_h = ws_ref[hd].astype(BF16)
            b_h = bst_ref[:, hd:hd + 1]
            cols = slice(hd * HEAD_DIM, (hd + 1) * HEAD_DIM)
            for ch in range(ts // CHUNK):
                rows = slice(ch * CHUNK, (ch + 1) * CHUNK)
                mixed = jnp.dot(w_h, vn[rows, cols], preferred_element_type=F32) + b_h
                u = pm_ref[rows, 3 * C_BR + hd * HEAD_DIM:3 * C_BR + (hd + 1) * HEAD_DIM].astype(F32)
                gs = pm_ref[rows, 5 * C_BR + hd * HEAD_DIM:5 * C_BR + (hd + 1) * HEAD_DIM].astype(F32)
                y_new[rows, C_BR + hd * HEAD_DIM:C_BR + (hd + 1) * HEAD_DIM] = (
                    u * mixed * gs * _sigmoid(gs)).astype(BF16)

    prev, nxt = _halo_specs(ts, s, 2 * C_BR, 0)
    row = pl.BlockSpec((1, C_BR), lambda i: (0, 0))
    return pl.pallas_call(
        body, name="branch_fwd",
        grid=(nt,),
        out_shape=(jax.ShapeDtypeStruct((s, 2 * C_BR), BF16), jax.ShapeDtypeStruct((s, C_BR), F32)),
        in_specs=[pl.BlockSpec((ts, D_IN), lambda i: (i, 0)), prev, nxt,
                  _whole_vmem(), row, row, row, row, row, _whole_vmem(), _whole_vmem()],
        out_specs=(pl.BlockSpec((ts, 2 * C_BR), lambda i: (i, 0)),
                   pl.BlockSpec((ts, C_BR), lambda i: (i, 0))),
        scratch_shapes=[pltpu.VMEM((SUB, ts + 2 * HALO, LANE), F32)],
        compiler_params=_cparams(("parallel",)),
    )(proj, proj, proj, cwg, conv_b, conv_ln_g, conv_ln_b, sgu_ln_g, sgu_ln_b, w_s, bs_t)


def _out_proj(y, x, target, woutg, final_g):
    s = x.shape[0]
    ts = min(TS_OUT, s)
    nt = s // ts

    def body(y_ref, x_ref, t_ref, w_ref, g_ref, dx2_ref, dy_ref, gw_ref, gf_ref, se_ref):
        i = pl.program_id(0)

        @pl.when(i == 0)
        def _():
            gw_ref[...] = jnp.zeros_like(gw_ref)
            gf_ref[...] = jnp.zeros_like(gf_ref)
            se_ref[...] = jnp.zeros_like(se_ref)

        yt = y_ref[...]
        x2 = x_ref[...]
        for k in range(N_SHARD):
            x2 = x2 + jnp.dot(yt[:, k * W_OUT_ROWS:(k + 1) * W_OUT_ROWS], w_ref[k], preferred_element_type=F32)
        r2 = lax.rsqrt(jnp.mean(x2 * x2, axis=-1, keepdims=True) + EPS)
        n2 = x2 * r2
        g = g_ref[...]
        diff = n2 * g - t_ref[...]
        se_ref[...] += _fold8(diff * diff)
        dout = diff * (1.0 / D_MODEL)
        gf_ref[...] += _fold8(dout * n2)
        dn = dout * g
        dx2 = r2 * (dn - n2 * jnp.mean(dn * n2, axis=-1, keepdims=True))
        dx2_ref[...] = dx2
        dxb = dx2.astype(BF16)
        for k in range(N_SHARD):
            rows = slice(k * W_OUT_ROWS, (k + 1) * W_OUT_ROWS)
            dy_ref[:, rows] = _dot_nt(dxb, w_ref[k]).astype(BF16)
            gw_ref[rows, :] += _dot_tn(yt[:, rows], dxb)

        @pl.when(i == nt - 1)
        def _():
            gf_ref[...] = jnp.broadcast_to(jnp.sum(gf_ref[...], axis=0, keepdims=True), gf_ref.shape)

    tile = pl.BlockSpec((ts, D_MODEL), lambda i: (i, 0))
    wide = pl.BlockSpec((ts, 2 * C_BR), lambda i: (i, 0))
    acc8 = pl.BlockSpec((SUB, D_MODEL), lambda i: (0, 0))
    return pl.pallas_call(
        body, name="out_proj",
        grid=(nt,),
        out_shape=(jax.ShapeDtypeStruct((s, D_MODEL), F32), jax.ShapeDtypeStruct((s, 2 * C_BR), BF16),
                   jax.ShapeDtypeStruct((2 * C_BR, D_MODEL), F32),
                   jax.ShapeDtypeStruct((SUB, D_MODEL), F32), jax.ShapeDtypeStruct((SUB, D_MODEL), F32)),
        in_specs=[wide, tile, tile, _whole_vmem(), pl.BlockSpec((1, D_MODEL), lambda i: (0, 0))],
        out_specs=(tile, wide, pl.BlockSpec((2 * C_BR, D_MODEL), lambda i: (0, 0)), acc8, acc8),
        compiler_params=_cparams(("arbitrary",)),
    )(y, x, target, woutg, final_g)


def _dsilu(v, sg):
    return sg * (1.0 + v * (1.0 - sg))


def _branch_bwd(proj, dy, cv, cwg, conv_ln_g, conv_ln_b, sgu_ln_g, sgu_ln_b, w_s, ws_t, bs_t, gf8, se8, dep):
    s = proj.shape[0]
    ts = min(TS_BWD, s)
    nt = s // ts
    ra = min(ROWS_A, ts)
    rb = min(ROWS_B, ts)
    te = ts + 2 * HALO

    def body(pm_ref, dym_ref, cvm_ref, gcp_ref, gcn_ref, dyp_ref, dyn_ref, cvp_ref, cvn_ref,
             cw_ref, clg_ref, clb_ref, slg_ref, slb_ref, ws_ref, wst_ref, bst_ref, gf_ref, se_ref, dep_ref,
             dp_ref, small_ref, gws_ref,
             sh_ref, glu_ref, dgl_ref, dcv_ref, acc_ref, gcw_ref, gbs_ref):
        i = pl.program_id(0)

        @pl.when(i == 0)
        def _():
            acc_ref[...] = jnp.zeros_like(acc_ref)
            gcw_ref[...] = jnp.zeros_like(gcw_ref)
            gbs_ref[...] = jnp.zeros_like(gbs_ref)
            gws_ref[...] = jnp.zeros_like(gws_ref)

        def ext(prev_ref, main, next_ref):
            return jnp.concatenate([prev_ref[...].astype(F32), main, next_ref[...].astype(F32)], axis=0)

        main = slice(HALO, HALO + ts)

        cv_e = ext(cvp_ref, cvm_ref[...], cvn_ref)
        gc_e = ext(gcp_ref, pm_ref[:, 2 * C_BR:3 * C_BR].astype(F32), gcn_ref)
        dyc_e = ext(dyp_ref, dym_ref[:, :C_BR].astype(F32), dyn_ref)
        lnh, rstd = _layer_norm_stats(cv_e)
        clg = clg_ref[...]
        ln = lnh * clg + clb_ref[...]
        sg_ln = _sigmoid(ln)
        sg_gc = _sigmoid(gc_e)
        d_ln = dyc_e * gc_e * sg_gc * _dsilu(ln, sg_ln)
        dp_ref[:, 2 * C_BR:3 * C_BR] = (
            dyc_e[main] * ln[main] * sg_ln[main] * _dsilu(gc_e[main], sg_gc[main])).astype(BF16)
        dlnh = d_ln * clg
        d_cv = rstd * (dlnh - jnp.mean(dlnh, axis=-1, keepdims=True)
                       - lnh * jnp.mean(dlnh * lnh, axis=-1, keepdims=True))
        row = lax.broadcasted_iota(jnp.int32, (te, 1), 0)
        valid = jnp.logical_and(jnp.logical_or(row >= HALO, i > 0),
                                jnp.logical_or(row < HALO + ts, i < nt - 1))
        d_cv = jnp.where(valid, d_cv, 0.0)
        dcv_ref[...] = d_cv
        acc_ref[0:8, :] += _fold8(d_cv[main])
        acc_ref[8:16, :] += _fold8(d_ln[main] * lnh[main])
        acc_ref[16:24, :] += _fold8(d_ln[main])

        for lb in range(C_BR // LANE):
            lanes = slice(lb * LANE, (lb + 1) * LANE)
            gates = slice(C_BR + lb * LANE, C_BR + (lb + 1) * LANE)
            shard, off = divmod(lb * LANE, CONV_W_COLS)
            av = pm_ref[:, lanes].astype(F32)
            sg = _sigmoid(pm_ref[:, gates].astype(F32))
            glu_ref[...] = av * sg
            _fill_shifted(sh_ref, dcv_ref[:, lanes])

            def chunk_a(jc, carry):
                base = pl.multiple_of(jc * ra, ra)
                acc = jnp.zeros((ra, LANE), F32)
                for j in range(KW):
                    o = j + 1
                    acc = acc + sh_ref[o % SUB, pl.ds(base + SUB * (o // SUB), ra), :] * cw_ref[
                        shard, KW - 1 - j:KW - j, off:off + LANE]
                dgl_ref[pl.ds(base, ra), :] = acc
                return carry

            lax.fori_loop(0, ts // ra, chunk_a, 0)

            def chunk_b(jc, accs):
                base = pl.multiple_of(jc * rb, rb)
                g = glu_ref[pl.ds(base, rb), :]
                out = []
                for j in range(KW):
                    o = j + 1
                    d = sh_ref[o % SUB, pl.ds(base + SUB * (o // SUB), rb), :]
                    out.append(accs[j] + _fold8(g * d))
                return tuple(out)

            accs = lax.fori_loop(0, ts // rb, chunk_b, tuple(jnp.zeros((SUB, LANE), F32) for _ in range(KW)))
            for j in range(KW):
                gcw_ref[j * SUB:(j + 1) * SUB, lanes] += accs[j]

            dglu = dgl_ref[...]
            dp_ref[:, lanes] = (dglu * sg).astype(BF16)
            dp_ref[:, gates] = (dglu * av * sg * (1.0 - sg)).astype(BF16)

        vh, vrstd = _layer_norm_stats(pm_ref[:, 4 * C_BR:5 * C_BR].astype(F32))
        slg = slg_ref[...]
        vn = (vh * slg + slb_ref[...]).astype(BF16)
        for hd in range(HEADS):
            w_h = ws_ref[hd].astype(BF16)
            wt_h = wst_ref[hd].astype(BF16)
            b_h = bst_ref[:, hd:hd + 1]
            cols = slice(hd * HEAD_DIM, (hd + 1) * HEAD_DIM)
            gws_h = jnp.zeros((CHUNK, CHUNK), F32)
            gbs_h = jnp.zeros((CHUNK, HEAD_DIM), F32)
            for ch in range(ts // CHUNK):
                rows = slice(ch * CHUNK, (ch + 1) * CHUNK)
                vn_b = vn[rows, cols]
                mixed = jnp.dot(w_h, vn_b, preferred_element_type=F32) + b_h
                u = pm_ref[rows, 3 * C_BR + hd * HEAD_DIM:3 * C_BR + (hd + 1) * HEAD_DIM].astype(F32)
                gs = pm_ref[rows, 5 * C_BR + hd * HEAD_DIM:5 * C_BR + (hd + 1) * HEAD_DIM].astype(F32)
                dys = dym_ref[rows, C_BR + hd * HEAD_DIM:C_BR + (hd + 1) * HEAD_DIM].astype(F32)
                sg_gs = _sigmoid(gs)
                silu_gs = gs * sg_gs
                dp_ref[rows, 3 * C_BR + hd * HEAD_DIM:3 * C_BR + (hd + 1) * HEAD_DIM] = (
                    dys * mixed * silu_gs).astype(BF16)
                dp_ref[rows, 5 * C_BR + hd * HEAD_DIM:5 * C_BR + (hd + 1) * HEAD_DIM] = (
                    dys * u * mixed * _dsilu(gs, sg_gs)).astype(BF16)
                d_mixed = dys * u * silu_gs
                dm_b = d_mixed.astype(BF16)
                gws_h = gws_h + _dot_nt(dm_b, vn_b)
                gbs_h = gbs_h + d_mixed
                dcv_ref[HALO + ch * CHUNK:HALO + (ch + 1) * CHUNK, cols] = jnp.dot(
                    wt_h, dm_b, preferred_element_type=F32)
            gws_ref[hd] += gws_h
            gbs_ref[:, cols] += gbs_h
        d_vn = dcv_ref[main, :]
        acc_ref[24:32, :] += _fold8(d_vn * vh)
        acc_ref[32:40, :] += _fold8(d_vn)
        dvh = d_vn * slg
        dp_ref[:, 4 * C_BR:5 * C_BR] = (vrstd * (
            dvh - jnp.mean(dvh, axis=-1, keepdims=True)
            - vh * jnp.mean(dvh * vh, axis=-1, keepdims=True))).astype(BF16)

        @pl.when(i == nt - 1)
        def _():
            small_ref[...] = jnp.zeros_like(small_ref)
            for a in range(5):
                small_ref[1 + a:2 + a, :] = jnp.sum(acc_ref[a * SUB:(a + 1) * SUB, :], axis=0, keepdims=True)
            ones = jnp.ones((SUB, HEAD_DIM), F32)
            for hd in range(HEADS):
                cols = slice(hd * HEAD_DIM, (hd + 1) * HEAD_DIM)
                rowsum = lax.dot_general(ones, gbs_ref[:, cols], (((1,), (1,)), ((), ())),
                                         precision=lax.Precision.HIGHEST, preferred_element_type=F32)
                small_ref[6:7, cols] = rowsum[0:1, :]
            small_ref[7:8, :] = gf_ref[0:1, :]
            small_ref[0:1, :] = jnp.sum(se_ref[...], axis=0, keepdims=True)
            for k in range(KW):
                j = KW - 1 - k
                small_ref[8 + k:9 + k, :] = jnp.sum(gcw_ref[j * SUB:(j + 1) * SUB, :], axis=0, keepdims=True)

    gc_prev, gc_next = _halo_specs(ts, s, C_BR, 2)
    lo_prev, lo_next = _halo_specs(ts, s, C_BR, 0)
    row = pl.BlockSpec((1, C_BR), lambda i: (0, 0))
    return pl.pallas_call(
        body, name="branch_bwd",
        grid=(nt,),
        out_shape=(jax.ShapeDtypeStruct((s, D_IN), BF16), jax.ShapeDtypeStruct((40, C_BR), F32),
                   jax.ShapeDtypeStruct((HEADS, CHUNK, CHUNK), F32)),
        in_specs=[pl.BlockSpec((ts, D_IN), lambda i: (i, 0)),
                  pl.BlockSpec((ts, 2 * C_BR), lambda i: (i, 0)),
                  pl.BlockSpec((ts, C_BR), lambda i: (i, 0)),
                  gc_prev, gc_next, lo_prev, lo_next, lo_prev, lo_next,
                  _whole_vmem(), row, row, row, row, _whole_vmem(), _whole_vmem(), _whole_vmem(),
                  _whole_vmem(), _whole_vmem(), _whole_vmem()],
        out_specs=(pl.BlockSpec((ts, D_IN), lambda i: (i, 0)),
                   pl.BlockSpec((40, C_BR), lambda i: (0, 0)),
                   pl.BlockSpec((HEADS, CHUNK, CHUNK), lambda i: (0, 0, 0))),
        scratch_shapes=[pltpu.VMEM((SUB, te, LANE), F32),
                        pltpu.VMEM((ts, LANE), F32),
                        pltpu.VMEM((ts, LANE), F32),
                        pltpu.VMEM((te, C_BR), F32),
                        pltpu.VMEM((5 * SUB, C_BR), F32),
                        pltpu.VMEM((KW * SUB, C_BR), F32),
                        pltpu.VMEM((CHUNK, C_BR), F32)],
        compiler_params=_cparams(("arbitrary",)),
    )(proj, dy, cv, proj, proj, dy, dy, cv, cv,
      cwg, conv_ln_g, conv_ln_b, sgu_ln_g, sgu_ln_b, w_s, ws_t, bs_t, gf8, se8, dep)


def _in_bwd(dproj, x, dx2, norm_g, wing, dep, first_tile, n_tiles, gx_prev=None, gn_prev=None):
    s = x.shape[0]
    ts = min(TS_INB, s)

    def body(*refs):
        dp_ref, x_ref, dx2_ref, g_ref, w_ref = refs[:5]
        gx_ref, gn_ref, acc_ref = refs[-3:]
        i = pl.program_id(0)

        @pl.when(i == 0)
        def _():
            acc_ref[...] = jnp.zeros_like(acc_ref)

        dh = jnp.zeros((ts, D_MODEL), F32)
        for k in range(N_SHARD):
            dh = dh + _dot_nt(dp_ref[:, k * W_IN_COLS:(k + 1) * W_IN_COLS], w_ref[k])
        xt = x_ref[...]
        r = lax.rsqrt(jnp.mean(xt * xt, axis=-1, keepdims=True) + EPS)
        n = xt * r
        acc_ref[...] += _fold8(dh * n)
        dn = dh * g_ref[...]
        gx_ref[...] = dx2_ref[...] + r * (dn - n * jnp.mean(dn * n, axis=-1, keepdims=True))

        @pl.when(i == n_tiles - 1)
        def _():
            total = jnp.broadcast_to(jnp.sum(acc_ref[...], axis=0, keepdims=True), gn_ref.shape)
            if gn_prev is not None:
                total = total + refs[7][...]
            gn_ref[...] = total

    tile = pl.BlockSpec((ts, D_MODEL), lambda i: (i + first_tile, 0))
    in_specs = [pl.BlockSpec((ts, D_IN), lambda i: (i + first_tile, 0)), tile, tile,
                pl.BlockSpec((1, D_MODEL), lambda i: (0, 0)), _whole_vmem(), pl.BlockSpec(memory_space=pl.ANY)]
    operands = [dproj, x, dx2, norm_g, wing, dep]
    aliases = {}
    if gx_prev is not None:
        in_specs += [pl.BlockSpec(memory_space=pl.ANY), _whole_vmem()]
        operands += [gx_prev, gn_prev]
        aliases = {6: 0}
    return pl.pallas_call(
        body, name="in_bwd_%d" % first_tile,
        grid=(n_tiles,),
        out_shape=(jax.ShapeDtypeStruct((s, D_MODEL), F32), jax.ShapeDtypeStruct((SUB, D_MODEL), F32)),
        in_specs=in_specs,
        out_specs=(tile, pl.BlockSpec((SUB, D_MODEL), lambda i: (0, 0))),
        scratch_shapes=[pltpu.VMEM((SUB, D_MODEL), F32)],
        input_output_aliases=aliases,
        compiler_params=_cparams(("arbitrary",)),
    )(*operands)


def _grad_w_in(h, dproj, dep):
    s = h.shape[0]
    tk = min(TK_GW, s)
    half = D_MODEL // 2

    def body(h_ref, dp_ref, dep_ref, o_ref):
        @pl.when(pl.program_id(1) == 0)
        def _():
            o_ref[...] = jnp.zeros_like(o_ref)

        o_ref[0] += _dot_tn(h_ref[...], dp_ref[...]).reshape(2, half, W_IN_COLS)

    return pl.pallas_call(
        body, name="grad_w_in",
        grid=(N_SHARD, s // tk),
        out_shape=jax.ShapeDtypeStruct((N_SHARD, 2, half, W_IN_COLS), F32),
        in_specs=[pl.BlockSpec((tk, D_MODEL), lambda k, t: (t, 0)),
                  pl.BlockSpec((tk, W_IN_COLS), lambda k, t: (t, k)), _whole_vmem()],
        out_specs=pl.BlockSpec((1, 2, half, W_IN_COLS), lambda k, t: (k, 0, 0, 0)),
        compiler_params=_cparams(("parallel", "arbitrary")),
    )(h, dproj, dep)


HBM_SPEC = pl.BlockSpec(memory_space=pltpu.HBM)
SEM_SPEC = pl.BlockSpec(memory_space=pltpu.SEMAPHORE)
SIDE_EFFECT = pltpu.SideEffectType.DATAFLOW_SIDE_EFFECTING


def _remote_copies(plan, bufs, send_sems, recv_sems):
    x, y, c = _mesh_pos()
    return [pltpu.make_async_remote_copy(src_ref=src, dst_ref=dst, send_sem=send_sems.at[k],
                                         recv_sem=recv_sems.at[k], device_id=dev, device_id_type=MESH)
            for k, (src, dst, dev) in enumerate(plan(x, y, c, *bufs))]


def _start_copies(name, bufs, n_copies, plan):
    n = len(bufs)

    def body(*refs):
        for cp in _remote_copies(plan, refs[:n], refs[n], refs[n + 1]):
            cp.start()
        refs[-1][...] = jnp.zeros_like(refs[-1])

    outs = pl.pallas_call(
        body, name=name,
        out_shape=(pltpu.SemaphoreType.DMA((n_copies,)), pltpu.SemaphoreType.DMA((n_copies,)),
                   *[pltpu.HBM(b.shape, b.dtype) for b in bufs], jax.ShapeDtypeStruct((SUB, LANE), F32)),
        in_specs=[HBM_SPEC] * n,
        out_specs=(SEM_SPEC, SEM_SPEC, *[HBM_SPEC] * n, _whole_vmem()),
        input_output_aliases={i: 2 + i for i in range(n)},
        compiler_params=pltpu.CompilerParams(has_side_effects=SIDE_EFFECT),
    )(*[pltpu.with_memory_space_constraint(b, pltpu.HBM) for b in bufs])
    return outs[0], outs[1], list(outs[2:2 + n]), outs[-1]


def _wait_copies(name, send_sems, recv_sems, bufs, plan, after):
    n = len(bufs)

    def body(*refs):
        for cp in _remote_copies(plan, refs[:n], refs[n], refs[n + 1]):
            cp.wait_send()
            cp.wait_recv()

    outs = pl.pallas_call(
        body, name=name,
        out_shape=tuple(pltpu.HBM(b.shape, b.dtype) for b in bufs),
        in_specs=[HBM_SPEC] * n + [SEM_SPEC, SEM_SPEC, pl.BlockSpec(memory_space=pl.ANY)],
        out_specs=(HBM_SPEC,) * n,
        input_output_aliases={i: i for i in range(n)},
        compiler_params=pltpu.CompilerParams(has_side_effects=SIDE_EFFECT),
    )(*bufs, send_sems, recv_sems, after)
    return list(outs)


def _landing(shape, dtype):
    return lax.empty(shape, dtype)


def _plan_pair_exchange(x, y, c, g, r):
    return [(g.at[k, 1 - c], r.at[k], (x, y, 1 - c)) for k in range(N_SHARD)]


def _plan_chip_exchange(x, y, c, a, r):
    chips = [(1 - x, y), (x, 1 - y), (1 - x, 1 - y)]
    return [(a.at[2 * cx + cy], r.at[j], (cx, cy, c)) for j, (cx, cy) in enumerate(chips)]


def _plan_pair_gather(x, y, c, f):
    return [(f.at[c], f.at[c], (x, y, 1 - c))]


def _plan_all_gather(x, y, c, own, land):
    me = 4 * x + 2 * y + c
    flip = lambda v, bit: 1 - v if bit else v
    return [(own, land.at[me], (flip(x, m >> 2 & 1), flip(y, m >> 1 & 1), flip(c, m & 1))) for m in range(1, 8)]


def _add_pair(g, r, pos, name):
    _, _, rows, cols = g.shape
    tr = min(256, rows)

    def body(pos_ref, g_ref, r_ref, o_ref, ob_ref):
        v = g_ref[0] + r_ref[...]
        o_ref[...] = v
        ob_ref[...] = v.astype(BF16)

    blk = pl.BlockSpec((1, tr, cols), lambda k, t, pos_ref: (k, t, 0))
    return pl.pallas_call(
        body, name=name,
        grid_spec=pltpu.PrefetchScalarGridSpec(
            num_scalar_prefetch=1, grid=(N_SHARD, rows // tr),
            in_specs=[pl.BlockSpec((1, 1, tr, cols), lambda k, t, pos_ref: (k, pos_ref[1], t, 0)), blk],
            out_specs=(blk, blk)),
        out_shape=(jax.ShapeDtypeStruct((N_SHARD, rows, cols), F32), jax.ShapeDtypeStruct((N_SHARD, rows, cols), BF16)),
        compiler_params=_cparams(("parallel", "parallel")),
    )(pos, g, r)


def _add_chips(a, r, pos, name):
    _, rows, cols = a.shape
    tr = min(256, rows)

    def body(pos_ref, a_ref, r_ref, o_ref):
        o_ref[0] = ((a_ref[0] + r_ref[0].astype(F32)) + r_ref[1].astype(F32)) + r_ref[2].astype(F32)

    return pl.pallas_call(
        body, name=name,
        grid_spec=pltpu.PrefetchScalarGridSpec(
            num_scalar_prefetch=1, grid=(rows // tr,),
            in_specs=[pl.BlockSpec((1, tr, cols), lambda t, pos_ref: (pos_ref[0], t, 0)),
                      pl.BlockSpec((3, tr, cols), lambda t, pos_ref: (0, t, 0))],
            out_specs=pl.BlockSpec((1, tr, cols), lambda t, pos_ref: (pos_ref[1], t, 0))),
        out_shape=jax.ShapeDtypeStruct((2, rows, cols), F32),
        compiler_params=_cparams(("parallel",)),
    )(pos, a, r)


def _sum_slots(pos_ref, own, land_ref, rows):
    me = pos_ref[2]
    total = None
    for d in range(8):
        term = jnp.where(me == d, own, land_ref[d] if rows is None else land_ref[d, rows, :])
        total = term if total is None else total + term
    return total


def _sum_small(pos, small, small_land, gws, gws_land):
    def body(pos_ref, sm_ref, sml_ref, gw_ref, gwl_ref, o_sm, o_gw):
        o_sm[...] = _sum_slots(pos_ref, sm_ref[...], sml_ref, None)
        o_gw[...] = _sum_slots(pos_ref, gw_ref[...], gwl_ref, None)

    return pl.pallas_call(
        body, name="sum_small",
        grid_spec=pltpu.PrefetchScalarGridSpec(
            num_scalar_prefetch=1, grid=(1,),
            in_specs=[_whole_vmem()] * 4, out_specs=[_whole_vmem()] * 2),
        out_shape=[jax.ShapeDtypeStruct(small.shape, F32), jax.ShapeDtypeStruct(gws.shape, F32)],
        compiler_params=_cparams(("arbitrary",)),
    )(pos, small, small_land, gws, gws_land)


def _adamw_math(w, g, m, v):
    m = ADAM_B1 * m + (1.0 - ADAM_B1) * g
    v = ADAM_B2 * v + (1.0 - ADAM_B2) * (g * g)
    m_hat = m / (1.0 - ADAM_B1 ** ADAM_STEP)
    v_hat = v / (1.0 - ADAM_B2 ** ADAM_STEP)
    delta = -ADAM_LR * (m_hat / (jnp.sqrt(v_hat) + ADAM_EPS) + ADAM_WD * w)
    return delta, m, v


def _adamw_large(w, g, m, v, dep, name):
    rows, cols = w.shape
    tr = min(256, rows)

    def body(w_ref, g_ref, m_ref, v_ref, dep_ref, d_ref, nm_ref, nv_ref):
        d_ref[...], nm_ref[...], nv_ref[...] = _adamw_math(w_ref[...], g_ref[...], m_ref[...], v_ref[...])

    tile = pl.BlockSpec((tr, cols), lambda t: (t, 0))
    return pl.pallas_call(
        body, name=name,
        grid=(rows // tr,),
        out_shape=(jax.ShapeDtypeStruct(w.shape, F32),) * 3,
        in_specs=[tile] * 4 + [_whole_vmem()], out_specs=(tile,) * 3,
        compiler_params=_cparams(("parallel",)),
    )(w, g, m, v, dep)


_ROW_OF = {"conv_b": 1, "conv_ln_g": 2, "conv_ln_b": 3, "sgu_ln_g": 4, "sgu_ln_b": 5, "b_s": 6, "final_g": 7}
_CONV_W_ROW = 8


def _adamw_small(call_name, names, grads, pos, params):
    def body(pos_ref, p_ref, q_ref, *refs):
        gn_ref, gnl_ref = p_ref, q_ref
        n_in = 3 * len(names)
        ins, outs = refs[:n_in], refs[n_in:]
        me = pos_ref[0]
        for a, name in enumerate(names):
            w_ref, m_ref, v_ref = ins[3 * a:3 * a + 3]
            if name == "conv_w":
                g = jnp.zeros((KW, CONV_W_COLS), F32)
                for k in range(N_SHARD):
                    blk = p_ref[_CONV_W_ROW:_CONV_W_ROW + KW, k * CONV_W_COLS:(k + 1) * CONV_W_COLS]
                    g = jnp.where(me == k, blk, g)
            elif name == "w_s":
                g = q_ref[...]
            elif name == "norm_g":
                g = _sum_slots(pos_ref, gn_ref[0:1, :], gnl_ref, slice(0, 1))
            else:
                g = p_ref[_ROW_OF[name]:_ROW_OF[name] + 1, :]
            delta, nm, nv = _adamw_math(w_ref[...], g, m_ref[...], v_ref[...])
            for o_ref, val in zip(outs[4 * a:4 * a + 4], (g, delta, nm, nv)):
                o_ref[...] = val

    operands, shapes = [], []
    for name in names:
        operands += list(params[name])
        shapes += [jax.ShapeDtypeStruct(params[name][0].shape, F32)] * 4
    outs = pl.pallas_call(
        body, name=call_name,
        grid_spec=pltpu.PrefetchScalarGridSpec(
            num_scalar_prefetch=1, grid=(1,),
            in_specs=[_whole_vmem()] * (2 + len(operands)),
            out_specs=[_whole_vmem()] * len(shapes)),
        out_shape=shapes,
        compiler_params=_cparams(("arbitrary",)),
    )(pos, *grads, *operands)
    return {name: tuple(outs[4 * a:4 * a + 4]) for a, name in enumerate(names)}


def kernel(x, norm_g, w_in, conv_w, conv_b, conv_ln_g, conv_ln_b, sgu_ln_g, sgu_ln_b, w_s, b_s, w_out, final_g, loss_target, m_norm_g, m_w_in, m_conv_w, m_conv_b, m_conv_ln_g, m_conv_ln_b, m_sgu_ln_g, m_sgu_ln_b, m_w_s, m_b_s, m_w_out, m_final_g, v_norm_g, v_w_in, v_conv_w, v_conv_b, v_conv_ln_g, v_conv_ln_b, v_sgu_ln_g, v_sgu_ln_b, v_w_s, v_b_s, v_w_out, v_final_g):
    xi, yi, ci = _mesh_pos()
    pos = jnp.stack([2 * xi + yi, ci, 4 * xi + 2 * yi + ci]).astype(jnp.int32)

    x2d = x[0]
    tgt = loss_target[0]
    fg = final_g.reshape(1, D_MODEL)
    ws3 = w_s[0]
    ws_t = jnp.swapaxes(ws3, 1, 2)
    bs_t = jnp.transpose(b_s[0])

    chip = 2 * xi + yi
    order = jnp.stack([chip, 2 * (1 - xi) + yi, 2 * xi + 1 - yi, 2 * (1 - xi) + 1 - yi]).astype(jnp.int32)
    h, proj, wing, woutg, cwg = _rms_proj_gather(x2d, norm_g, w_in[0], w_out[0], conv_w[0], order)
    y, cv = _branch_fwd(proj, cwg, conv_b, conv_ln_g, conv_ln_b, sgu_ln_g, sgu_ln_b, ws3, bs_t)
    dx2, dy, gwout, gf8, se8 = _out_proj(y, x2d, tgt, woutg, fg)
    in_rows, out_rows = D_MODEL // 2, W_OUT_ROWS // 2
    gwout4 = gwout.reshape(N_SHARD, 2, out_rows, D_MODEL)
    ss, rs, (gwout4, r1_out), tok = _start_copies(
        "start_pair_exchange_w_out", [gwout4, _landing((N_SHARD, out_rows, D_MODEL), F32)], N_SHARD,
        _plan_pair_exchange)
    dproj, small, gws3 = _branch_bwd(proj, dy, cv, cwg, conv_ln_g, conv_ln_b, sgu_ln_g, sgu_ln_b, ws3, ws_t, bs_t,
                                     gf8, se8, tok)
    gws = gws3.reshape(HEADS * CHUNK, CHUNK)
    gwout4, r1_out = _wait_copies("wait_pair_exchange_w_out", ss, rs, [gwout4, r1_out], _plan_pair_exchange, dproj)
    a_out, a_out_bf = _add_pair(gwout4, r1_out, pos, "add_pair_w_out")

    def plan_b(x, y, c, a, r, sm, sml, gw, gwl):
        return (_plan_chip_exchange(x, y, c, a, r) + _plan_all_gather(x, y, c, sm, sml)
                + _plan_all_gather(x, y, c, gw, gwl))

    ss, rs, bufs_b, tok = _start_copies(
        "start_chip_exchange_w_out",
        [a_out_bf, _landing((3, out_rows, D_MODEL), BF16), small, _landing((8,) + small.shape, F32),
         gws, _landing((8,) + gws.shape, F32)], 3 + 7 + 7, plan_b)
    gwin = _grad_w_in(h, dproj, tok)
    ss_c, rs_c, (gwin, r1_in), tok = _start_copies(
        "start_pair_exchange_w_in", [gwin, _landing((N_SHARD, in_rows, W_IN_COLS), F32)], N_SHARD,
        _plan_pair_exchange)
    nt = x2d.shape[0] // min(TS_INB, x2d.shape[0])
    cut = nt // 2
    gx_a, gn_a = _in_bwd(dproj, x2d, dx2, norm_g, wing, tok, 0, cut)

    gwin, r1_in = _wait_copies("wait_pair_exchange_w_in", ss_c, rs_c, [gwin, r1_in], _plan_pair_exchange, gx_a)
    a_in, a_in_bf = _add_pair(gwin, r1_in, pos, "add_pair_w_in")
    a_out_bf, r2_out, small, small_land, gws, gws_land = _wait_copies(
        "wait_chip_exchange_w_out", ss, rs, bufs_b, plan_b, gx_a)
    f_out = _add_chips(a_out, r2_out, pos, "add_chips_w_out")
    p, q = _sum_small(pos, small, small_land, gws, gws_land)
    loss = (0.5 / D_MODEL) * jnp.sum(p[0])

    flat = lambda a: a.reshape(1, C_BR)
    flat_ws = lambda a: a.reshape(HEADS * CHUNK, CHUNK)
    params = {
        "norm_g": (norm_g, m_norm_g, v_norm_g),
        "conv_b": (conv_b, m_conv_b, v_conv_b),
        "conv_ln_g": (conv_ln_g, m_conv_ln_g, v_conv_ln_g),
        "conv_ln_b": (conv_ln_b, m_conv_ln_b, v_conv_ln_b),
        "sgu_ln_g": (sgu_ln_g, m_sgu_ln_g, v_sgu_ln_g),
        "sgu_ln_b": (sgu_ln_b, m_sgu_ln_b, v_sgu_ln_b),
        "b_s": (flat(b_s), flat(m_b_s), flat(v_b_s)),
        "final_g": (flat(final_g), flat(m_final_g), flat(v_final_g)),
        "conv_w": (conv_w[0], m_conv_w[0], v_conv_w[0]),
        "w_s": (flat_ws(w_s), flat_ws(m_w_s), flat_ws(v_w_s)),
    }
    res = _adamw_small("adamw_small", [n for n in params if n != "norm_g"], (p, q), pos, params)

    def plan_d(x, y, c, a, r, f):
        return _plan_chip_exchange(x, y, c, a, r) + _plan_pair_gather(x, y, c, f)

    ss, rs, bufs_d, tok = _start_copies(
        "start_chip_exchange_w_in", [a_in_bf, _landing((3, in_rows, W_IN_COLS), BF16), f_out], 3 + 1, plan_d)
    grad_x, gn8 = _in_bwd(dproj, x2d, dx2, norm_g, wing, tok, cut, nt - cut, gx_a, gn_a)
    a_in_bf, r2_in, f_out = _wait_copies("wait_chip_exchange_w_in", ss, rs, bufs_d, plan_d, grad_x)
    f_in = _add_chips(a_in, r2_in, pos, "add_chips_w_in")

    def plan_e(x, y, c, f, gn, gnl):
        return _plan_pair_gather(x, y, c, f) + _plan_all_gather(x, y, c, gn, gnl)

    ss, rs, bufs_e, tok = _start_copies(
        "start_pair_gather_w_in", [f_in, gn8, _landing((8,) + gn8.shape, F32)], 1 + 7, plan_e)
    g_w_out = f_out.reshape(W_OUT_ROWS, D_MODEL)
    d_w_out, nm_w_out, nv_w_out = _adamw_large(w_out[0], g_w_out, m_w_out[0], v_w_out[0], tok, "adamw_w_out")
    f_in, gn8, gn_land = _wait_copies("wait_pair_gather_w_in", ss, rs, bufs_e, plan_e, d_w_out)
    g_w_in = f_in.reshape(D_MODEL, W_IN_COLS)
    d_w_in, nm_w_in, nv_w_in = _adamw_large(w_in[0], g_w_in, m_w_in[0], v_w_in[0], tok, "adamw_w_in")
    res.update(_adamw_small("adamw_norm_g", ["norm_g"], (gn8, gn_land), pos, params))
    res["w_in"] = tuple(a[None] for a in (g_w_in, d_w_in, nm_w_in, nv_w_in))
    res["w_out"] = tuple(a[None] for a in (g_w_out, d_w_out, nm_w_out, nv_w_out))
    res["conv_w"] = tuple(a[None] for a in res["conv_w"])
    res["w_s"] = tuple(a.reshape(w_s.shape) for a in res["w_s"])
    res["b_s"] = tuple(a.reshape(b_s.shape) for a in res["b_s"])
    res["final_g"] = tuple(a.reshape(final_g.shape) for a in res["final_g"])

    order = ("norm_g", "w_in", "conv_w", "conv_b", "conv_ln_g", "conv_ln_b", "sgu_ln_g", "sgu_ln_b",
             "w_s", "b_s", "w_out", "final_g")
    out = [loss, grad_x[None]]
    for part in range(4):
        out += [res[name][part] for name in order]
    return tuple(out)
```

```python
import jax
import jax.numpy as jnp
from jax import lax
from jax.experimental import pallas as pl
from jax.experimental.pallas import tpu as pltpu

F32 = jnp.float32
BF16 = jnp.bfloat16
MESH = pl.DeviceIdType.MESH

EPS = 1e-6
D_MODEL = 1024
C_BR = 1024
D_IN = 6 * C_BR
N_SHARD = 4
W_IN_COLS = D_IN // N_SHARD
W_OUT_ROWS = 2 * C_BR // N_SHARD
CONV_W_COLS = C_BR // N_SHARD
KW = 31
HALO = 16
HEADS = 8
HEAD_DIM = 128
CHUNK = 128
LANE = 128
SUB = 8

ADAM_LR = 0.001
ADAM_B1 = 0.9
ADAM_B2 = 0.999
ADAM_EPS = 1e-08
ADAM_WD = 0.01
ADAM_STEP = 10

TS_PROJ = 512
TS_FWD = 256
TS_OUT = 512
TS_BWD = 256
TS_INB = 512
TK_GW = 2048
ROWS_A = 128
ROWS_B = 64
VMEM_LIMIT = 56 * 1024 * 1024


def _cparams(sem=None, vmem=VMEM_LIMIT):
    kw = dict(vmem_limit_bytes=vmem)
    if sem is not None:
        kw["dimension_semantics"] = sem
    return pltpu.CompilerParams(**kw)


def _whole_vmem():
    return pl.BlockSpec(memory_space=pltpu.VMEM)


def _sigmoid(v):
    return 1.0 / (1.0 + jnp.exp(-v))


def _fold8(v):
    n, c = v.shape
    return v.reshape(n // SUB, SUB, c).sum(axis=0)


def _dot_nt(a, b):
    return lax.dot_general(a, b, (((1,), (1,)), ((), ())), preferred_element_type=F32)


def _dot_tn(a, b):
    return lax.dot_general(a, b, (((0,), (0,)), ((), ())), preferred_element_type=F32)


def _mesh_pos():
    return lax.axis_index("x"), lax.axis_index("y"), lax.axis_index("c")


def _rms_proj_gather(x, norm_g, w_in, w_out, conv_w, order):
    s = x.shape[0]
    ts = min(TS_PROJ, s)
    nt = s // ts
    hin = w_in.shape[0] // 2
    hout = w_out.shape[0] // 2

    def body(order_ref, x_ref, g_ref, win_ref, wout_ref, cw_ref,
             h_ref, proj_ref, wing_ref, woutg_ref, cwg_ref, wg_ref, wob_ref, hs_ref, send_sems, recv_sems, local_sems):
        p = pl.program_id(0)
        t = pl.program_id(1)
        mx, my, c = _mesh_pos()
        me = 2 * mx + my
        chips = [(1 - mx, my), (mx, 1 - my), (1 - mx, 1 - my)]
        sibling = (mx, my, 1 - c)

        def remote(src, dst, sem, dev):
            return pltpu.make_async_remote_copy(
                src_ref=src, dst_ref=dst, send_sem=send_sems.at[sem], recv_sem=recv_sems.at[sem],
                device_id=dev, device_id_type=MESH)

        def w_in_part(blk, half):
            return wg_ref.at[blk, pl.ds(half * hin, hin)]

        def keep(blk, k):
            return pltpu.make_async_copy(wg_ref.at[blk], wing_ref.at[blk], local_sems.at[2 + k])

        def w_out_part(blk, half):
            return woutg_ref.at[blk, pl.ds(half * hout, hout)]

        def sends():
            out = []
            for j, (cx, cy) in enumerate(chips):
                blk = 2 * cx + cy
                out.append(remote(w_in_part(me, c), w_in_part(me, c), j, (cx, cy, c)))
                out.append(remote(w_in_part(blk, c), w_in_part(blk, c), 3 + j, sibling))
                out.append(remote(wob_ref.at[pl.ds(c * hout, hout)], w_out_part(me, c), 6 + j, (cx, cy, c)))
                out.append(remote(w_out_part(blk, c), w_out_part(blk, c), 9 + j, sibling))
                out.append(remote(cw_ref, cwg_ref.at[me], 12 + j, (cx, cy, c)))
            return out

        @pl.when(jnp.logical_and(p == 0, t == 0))
        def _():
            wg_ref[me] = win_ref[...].astype(BF16)
            wob_ref[...] = wout_ref[...].astype(BF16)
            keep(me, 0).start()
            mine = [pltpu.make_async_copy(wob_ref, woutg_ref.at[me], local_sems.at[0]),
                    pltpu.make_async_copy(cw_ref, cwg_ref.at[me], local_sems.at[1])]
            for cp in mine:
                cp.start()
            for k, cp in enumerate(sends()):
                if k % 5 == 4 or (k % 5 == 0 and k // 5 < 2):
                    cp.start()
            for cp in mine:
                cp.wait()

        for j, (cx, cy) in enumerate(chips):
            blk = 2 * cx + cy

            @pl.when(jnp.logical_and(p == j + 1, t == 0))
            def _():
                remote(w_in_part(blk, c), w_in_part(blk, c), j, (cx, cy, c)).wait_recv()
                remote(w_in_part(blk, c), w_in_part(blk, c), 3 + j, sibling).start()
                remote(w_in_part(blk, 1 - c), w_in_part(blk, 1 - c), 3 + j, sibling).wait_recv()
                keep(blk, j + 1).start()
                if j == 0:
                    sends()[5 * 2].start()
                if j == 1:
                    for jj in range(3):
                        sends()[5 * jj + 2].start()

        rows = pl.ds(pl.multiple_of(t * ts, ts), ts)

        @pl.when(p == 0)
        def _():
            xt = x_ref[...]
            r = lax.rsqrt(jnp.mean(xt * xt, axis=-1, keepdims=True) + EPS)
            hb = (xt * r * g_ref[...]).astype(BF16)
            h_ref[...] = hb
            hs_ref[rows, :] = hb

        proj_ref[...] = jnp.dot(hs_ref[rows, :], wg_ref[order_ref[p]], preferred_element_type=F32).astype(BF16)

        @pl.when(jnp.logical_and(p == N_SHARD - 1, t == nt - 1))
        def _():
            for j, (cx, cy) in enumerate(chips):
                blk = 2 * cx + cy
                remote(w_out_part(blk, c), w_out_part(blk, c), 6 + j, (cx, cy, c)).wait_recv()
                remote(w_out_part(blk, c), w_out_part(blk, c), 9 + j, sibling).start()
            for j, (cx, cy) in enumerate(chips):
                blk = 2 * cx + cy
                remote(w_out_part(blk, 1 - c), w_out_part(blk, 1 - c), 9 + j, sibling).wait_recv()
                remote(cw_ref, cwg_ref.at[blk], 12 + j, (cx, cy, c)).wait_recv()
            for cp in sends():
                cp.wait_send()
            keep(me, 0).wait()
            for j, (cx, cy) in enumerate(chips):
                keep(2 * cx + cy, j + 1).wait()

    hbm = pl.BlockSpec(memory_space=pl.ANY)
    return pl.pallas_call(
        body, name="rms_proj_gather",
        grid_spec=pltpu.PrefetchScalarGridSpec(
            num_scalar_prefetch=1, grid=(N_SHARD, nt),
            in_specs=[pl.BlockSpec((ts, D_MODEL), lambda p, t, o: (jnp.where(p == 0, t, nt - 1), 0)),
                      pl.BlockSpec((1, D_MODEL), lambda p, t, o: (0, 0)),
                      _whole_vmem(), _whole_vmem(), _whole_vmem()],
            out_specs=(pl.BlockSpec((ts, D_MODEL), lambda p, t, o: (jnp.where(p == 0, t, nt - 1), 0)),
                       pl.BlockSpec((ts, W_IN_COLS), lambda p, t, o: (t, o[p])),
                       hbm, hbm, hbm),
            scratch_shapes=[pltpu.VMEM((N_SHARD,) + w_in.shape, BF16), pltpu.VMEM(w_out.shape, BF16),
                            pltpu.VMEM((s, D_MODEL), BF16),
                            pltpu.SemaphoreType.DMA((15,)), pltpu.SemaphoreType.DMA((15,)),
                            pltpu.SemaphoreType.DMA((6,))]),
        out_shape=(jax.ShapeDtypeStruct((s, D_MODEL), BF16), jax.ShapeDtypeStruct((s, D_IN), BF16),
                   jax.ShapeDtypeStruct((N_SHARD,) + w_in.shape, BF16),
                   jax.ShapeDtypeStruct((N_SHARD,) + w_out.shape, BF16),
                   jax.ShapeDtypeStruct((N_SHARD,) + conv_w.shape, F32)),
        compiler_params=_cparams(("arbitrary", "arbitrary")),
    )(order, x, norm_g, w_in, w_out, conv_w)


def _halo_specs(ts, s, width, col_block):
    per = ts // HALO
    last = s // HALO - 1
    prev = pl.BlockSpec((HALO, width), lambda i: (jnp.maximum(i * per - 1, 0), col_block))
    nxt = pl.BlockSpec((HALO, width), lambda i: (jnp.minimum((i + 1) * per, last), col_block))
    return prev, nxt


def _layer_norm_stats(v):
    mu = jnp.mean(v, axis=-1, keepdims=True)
    vc = v - mu
    var = jnp.mean(vc * vc, axis=-1, keepdims=True)
    rstd = lax.rsqrt(var + EPS)
    return vc * rstd, rstd


def _fill_shifted(sh_ref, ext):
    n = ext.shape[0]
    sh_ref[0] = ext
    for r in range(1, SUB):
        sh_ref[r] = pltpu.roll(ext, n - r, axis=0)


def _branch_fwd(proj, cwg, conv_b, conv_ln_g, conv_ln_b, sgu_ln_g, sgu_ln_b, w_s, bs_t):
    s = proj.shape[0]
    ts = min(TS_FWD, s)
    nt = s // ts
    ra = min(ROWS_A, ts)

    def body(pm_ref, pp_ref, pn_ref, cw_ref, cb_ref, clg_ref, clb_ref, slg_ref, slb_ref, ws_ref, bst_ref,
             y_new, cv_ref, sh_ref):
        i = pl.program_id(0)
        keep_prev = (i > 0).astype(F32)
        keep_next = (i < nt - 1).astype(F32)

        for lb in range(C_BR // LANE):
            lanes = slice(lb * LANE, (lb + 1) * LANE)
            gates = slice(C_BR + lb * LANE, C_BR + (lb + 1) * LANE)

            def glu(ref):
                return ref[:, lanes].astype(F32) * _sigmoid(ref[:, gates].astype(F32))

            ext = jnp.concatenate([glu(pp_ref) * keep_prev, glu(pm_ref), glu(pn_ref) * keep_next], axis=0)
            _fill_shifted(sh_ref, ext)
            shard, off = divmod(lb * LANE, CONV_W_COLS)
            bias = cb_ref[:, lanes]

            def chunk(jc, carry):
                base = pl.multiple_of(jc * ra, ra)
                acc = jnp.zeros((ra, LANE), F32) + bias
                for k in range(KW):
                    o = k + 1
                    acc = acc + sh_ref[o % SUB, pl.ds(base + SUB * (o // SUB), ra), :] * cw_ref[
                        shard, k:k + 1, off:off + LANE]
                cv_ref[pl.ds(base, ra), lanes] = acc
                return carry

            lax.fori_loop(0, ts // ra, chunk, 0)

        lnh, _ = _layer_norm_stats(cv_ref[...])
        ln = lnh * clg_ref[...] + clb_ref[...]
        gc = pm_ref[:, 2 * C_BR:3 * C_BR].astype(F32)
        y_new[:, :C_BR] = (ln * _sigmoid(ln) * gc * _sigmoid(gc)).astype(BF16)

        vh, _ = _layer_norm_stats(pm_ref[:, 4 * C_BR:5 * C_BR].astype(F32))
        vn = (vh * slg_ref[...] + slb_ref[...]).astype(BF16)
        for hd in range(HEADS):
            w_h = ws_ref[hd].astype(BF16)
            b_h = bst_ref[:, hd:hd + 1]
            cols = slice(hd * HEAD_DIM, (hd + 1) * HEAD_DIM)
            for ch in range(ts // CHUNK):
                rows = slice(ch * CHUNK, (ch + 1) * CHUNK)
                mixed = jnp.dot(w_h, vn[rows, cols], preferred_element_type=F32) + b_h
                u = pm_ref[rows, 3 * C_BR + hd * HEAD_DIM:3 * C_BR + (hd + 1) * HEAD_DIM].astype(F32)
                gs = pm_ref[rows, 5 * C_BR + hd * HEAD_DIM:5 * C_BR + (hd + 1) * HEAD_DIM].astype(F32)
                y_new[rows, C_BR + hd * HEAD_DIM:C_BR + (hd + 1) * HEAD_DIM] = (
                    u * mixed * gs * _sigmoid(gs)).astype(BF16)

    prev, nxt = _halo_specs(ts, s, 2 * C_BR, 0)
    row = pl.BlockSpec((1, C_BR), lambda i: (0, 0))
    return pl.pallas_call(
        body, name="branch_fwd",
        grid=(nt,),
        out_shape=(jax.ShapeDtypeStruct((s, 2 * C_BR), BF16), jax.ShapeDtypeStruct((s, C_BR), F32)),
        in_specs=[pl.BlockSpec((ts, D_IN), lambda i: (i, 0)), prev, nxt,
                  _whole_vmem(), row, row, row, row, row, _whole_vmem(), _whole_vmem()],
        out_specs=(pl.BlockSpec((ts, 2 * C_BR), lambda i: (i, 0)),
                   pl.BlockSpec((ts, C_BR), lambda i: (i, 0))),
        scratch_shapes=[pltpu.VMEM((SUB, ts + 2 * HALO, LANE), F32)],
        compiler_params=_cparams(("parallel",)),
    )(proj, proj, proj, cwg, conv_b, conv_ln_g, conv_ln_b, sgu_ln_g, sgu_ln_b, w_s, bs_t)


def _out_proj(y, x, target, woutg, final_g):
    s = x.shape[0]
    ts = min(TS_OUT, s)
    nt = s // ts

    def body(y_ref, x_ref, t_ref, w_ref, g_ref, dx2_ref, dy_ref, gw_ref, gf_ref, se_ref):
        i = pl.program_id(0)

        @pl.when(i == 0)
        def _():
            gw_ref[...] = jnp.zeros_like(gw_ref)
            gf_ref[...] = jnp.zeros_like(gf_ref)
            se_ref[...] = jnp.zeros_like(se_ref)

        yt = y_ref[...]
        x2 = x_ref[...]
        for k in range(N_SHARD):
            x2 = x2 + jnp.dot(yt[:, k * W_OUT_ROWS:(k + 1) * W_OUT_ROWS], w_ref[k], preferred_element_type=F32)
        r2 = lax.rsqrt(jnp.mean(x2 * x2, axis=-1, keepdims=True) + EPS)
        n2 = x2 * r2
        g = g_ref[...]
        diff = n2 * g - t_ref[...]
        se_ref[...] += _fold8(diff * diff)
        dout = diff * (1.0 / D_MODEL)
        gf_ref[...] += _fold8(dout * n2)
        dn = dout * g
        dx2 = r2 * (dn - n2 * jnp.mean(dn * n2, axis=-1, keepdims=True))
        dx2_ref[...] = dx2
        dxb = dx2.astype(BF16)
        for k in range(N_SHARD):
            rows = slice(k * W_OUT_ROWS, (k + 1) * W_OUT_ROWS)
            dy_ref[:, rows] = _dot_nt(dxb, w_ref[k]).astype(BF16)
            gw_ref[rows, :] += _dot_tn(yt[:, rows], dxb)

        @pl.when(i == nt - 1)
        def _():
            gf_ref[...] = jnp.broadcast_to(jnp.sum(gf_ref[...], axis=0, keepdims=True), gf_ref.shape)

    tile = pl.BlockSpec((ts, D_MODEL), lambda i: (i, 0))
    wide = pl.BlockSpec((ts, 2 * C_BR), lambda i: (i, 0))
    acc8 = pl.BlockSpec((SUB, D_MODEL), lambda i: (0, 0))
    return pl.pallas_call(
        body, name="out_proj",
        grid=(nt,),
        out_shape=(jax.ShapeDtypeStruct((s, D_MODEL), F32), jax.ShapeDtypeStruct((s, 2 * C_BR), BF16),
                   jax.ShapeDtypeStruct((2 * C_BR, D_MODEL), F32),
                   jax.ShapeDtypeStruct((SUB, D_MODEL), F32), jax.ShapeDtypeStruct((SUB, D_MODEL), F32)),
        in_specs=[wide, tile, tile, _whole_vmem(), pl.BlockSpec((1, D_MODEL), lambda i: (0, 0))],
        out_specs=(tile, wide, pl.BlockSpec((2 * C_BR, D_MODEL), lambda i: (0, 0)), acc8, acc8),
        compiler_params=_cparams(("arbitrary",)),
    )(y, x, target, woutg, final_g)


def _dsilu(v, sg):
    return sg * (1.0 + v * (1.0 - sg))


def _branch_bwd(proj, dy, cv, cwg, conv_ln_g, conv_ln_b, sgu_ln_g, sgu_ln_b, w_s, ws_t, bs_t, gf8, se8, dep):
    s = proj.shape[0]
    ts = min(TS_BWD, s)
    nt = s // ts
    ra = min(ROWS_A, ts)
    rb = min(ROWS_B, ts)
    te = ts + 2 * HALO

    def body(pm_ref, dym_ref, cvm_ref, gcp_ref, gcn_ref, dyp_ref, dyn_ref, cvp_ref, cvn_ref,
             cw_ref, clg_ref, clb_ref, slg_ref, slb_ref, ws_ref, wst_ref, bst_ref, gf_ref, se_ref, dep_ref,
             dp_ref, small_ref, gws_ref,
             sh_ref, glu_ref, dgl_ref, dcv_ref, acc_ref, gcw_ref, gbs_ref):
        i = pl.program_id(0)

        @pl.when(i == 0)
        def _():
            acc_ref[...] = jnp.zeros_like(acc_ref)
            gcw_ref[...] = jnp.zeros_like(gcw_ref)
            gbs_ref[...] = jnp.zeros_like(gbs_ref)
            gws_ref[...] = jnp.zeros_like(gws_ref)

        def ext(prev_ref, main, next_ref):
            return jnp.concatenate([prev_ref[...].astype(F32), main, next_ref[...].astype(F32)], axis=0)

        main = slice(HALO, HALO + ts)

        cv_e = ext(cvp_ref, cvm_ref[...], cvn_ref)
        gc_e = ext(gcp_ref, pm_ref[:, 2 * C_BR:3 * C_BR].astype(F32), gcn_ref)
        dyc_e = ext(dyp_ref, dym_ref[:, :C_BR].astype(F32), dyn_ref)
        lnh, rstd = _layer_norm_stats(cv_e)
        clg = clg_ref[...]
        ln = lnh * clg + clb_ref[...]
        sg_ln = _sigmoid(ln)
        sg_gc = _sigmoid(gc_e)
        d_ln = dyc_e * gc_e * sg_gc * _dsilu(ln, sg_ln)
        dp_ref[:, 2 * C_BR:3 * C_BR] = (
            dyc_e[main] * ln[main] * sg_ln[main] * _dsilu(gc_e[main], sg_gc[main])).astype(BF16)
        dlnh = d_ln * clg
        d_cv = rstd * (dlnh - jnp.mean(dlnh, axis=-1, keepdims=True)
                       - lnh * jnp.mean(dlnh * lnh, axis=-1, keepdims=True))
        row = lax.broadcasted_iota(jnp.int32, (te, 1), 0)
        valid = jnp.logical_and(jnp.logical_or(row >= HALO, i > 0),
                                jnp.logical_or(row < HALO + ts, i < nt - 1))
        d_cv = jnp.where(valid, d_cv, 0.0)
        dcv_ref[...] = d_cv
        acc_ref[0:8, :] += _fold8(d_cv[main])
        acc_ref[8:16, :] += _fold8(d_ln[main] * lnh[main])
        acc_ref[16:24, :] += _fold8(d_ln[main])

        for lb in range(C_BR // LANE):
            lanes = slice(lb * LANE, (lb + 1) * LANE)
            gates = slice(C_BR + lb * LANE, C_BR + (lb + 1) * LANE)
            shard, off = divmod(lb * LANE, CONV_W_COLS)
            av = pm_ref[:, lanes].astype(F32)
            sg = _sigmoid(pm_ref[:, gates].astype(F32))
            glu_ref[...] = av * sg
            _fill_shifted(sh_ref, dcv_ref[:, lanes])

            def chunk_a(jc, carry):
                base = pl.multiple_of(jc * ra, ra)
                acc = jnp.zeros((ra, LANE), F32)
                for j in range(KW):
                    o = j + 1
                    acc = acc + sh_ref[o % SUB, pl.ds(base + SUB * (o // SUB), ra), :] * cw_ref[
                        shard, KW - 1 - j:KW - j, off:off + LANE]
                dgl_ref[pl.ds(base, ra), :] = acc
                return carry

            lax.fori_loop(0, ts // ra, chunk_a, 0)

            def chunk_b(jc, accs):
                base = pl.multiple_of(jc * rb, rb)
                g = glu_ref[pl.ds(base, rb), :]
                out = []
                for j in range(KW):
                    o = j + 1
                    d = sh_ref[o % SUB, pl.ds(base + SUB * (o // SUB), rb), :]
                    out.append(accs[j] + _fold8(g * d))
                return tuple(out)

            accs = lax.fori_loop(0, ts // rb, chunk_b, tuple(jnp.zeros((SUB, LANE), F32) for _ in range(KW)))
            for j in range(KW):
                gcw_ref[j * SUB:(j + 1) * SUB, lanes] += accs[j]

            dglu = dgl_ref[...]
            dp_ref[:, lanes] = (dglu * sg).astype(BF16)
            dp_ref[:, gates] = (dglu * av * sg * (1.0 - sg)).astype(BF16)

        vh, vrstd = _layer_norm_stats(pm_ref[:, 4 * C_BR:5 * C_BR].astype(F32))
        slg = slg_ref[...]
        vn = (vh * slg + slb_ref[...]).astype(BF16)
        for hd in range(HEADS):
            w_h = ws_ref[hd].astype(BF16)
            wt_h = wst_ref[hd].astype(BF16)
            b_h = bst_ref[:, hd:hd + 1]
            cols = slice(hd * HEAD_DIM, (hd + 1) * HEAD_DIM)
            gws_h = jnp.zeros((CHUNK, CHUNK), F32)
            gbs_h = jnp.zeros((CHUNK, HEAD_DIM), F32)
            for ch in range(ts // CHUNK):
                rows = slice(ch * CHUNK, (ch + 1) * CHUNK)
                vn_b = vn[rows, cols]
                mixed = jnp.dot(w_h, vn_b, preferred_element_type=F32) + b_h
                u = pm_ref[rows, 3 * C_BR + hd * HEAD_DIM:3 * C_BR + (hd + 1) * HEAD_DIM].astype(F32)
                gs = pm_ref[rows, 5 * C_BR + hd * HEAD_DIM:5 * C_BR + (hd + 1) * HEAD_DIM].astype(F32)
                dys = dym_ref[rows, C_BR + hd * HEAD_DIM:C_BR + (hd + 1) * HEAD_DIM].astype(F32)
                sg_gs = _sigmoid(gs)
                silu_gs = gs * sg_gs
                dp_ref[rows, 3 * C_BR + hd * HEAD_DIM:3 * C_BR + (hd + 1) * HEAD_DIM] = (
                    dys * mixed * silu_gs).astype(BF16)
                dp_ref[rows, 5 * C_BR + hd * HEAD_DIM:5 * C_BR + (hd + 1) * HEAD_DIM] = (
                    dys * u * mixed * _dsilu(gs, sg_gs)).astype(BF16)
                d_mixed = dys * u * silu_gs
                dm_b = d_mixed.astype(BF16)
                gws_h = gws_h + _dot_nt(dm_b, vn_b)
                gbs_h = gbs_h + d_mixed
                dcv_ref[HALO + ch * CHUNK:HALO + (ch + 1) * CHUNK, cols] = jnp.dot(
                    wt_h, dm_b, preferred_element_type=F32)
            gws_ref[hd] += gws_h
            gbs_ref[:, cols] += gbs_h
        d_vn = dcv_ref[main, :]
        acc_ref[24:32, :] += _fold8(d_vn * vh)
        acc_ref[32:40, :] += _fold8(d_vn)
        dvh = d_vn * slg
        dp_ref[:, 4 * C_BR:5 * C_BR] = (vrstd * (
            dvh - jnp.mean(dvh, axis=-1, keepdims=True)
            - vh * jnp.mean(dvh * vh, axis=-1, keepdims=True))).astype(BF16)

        @pl.when(i == nt - 1)
        def _():
            small_ref[...] = jnp.zeros_like(small_ref)
            for a in range(5):
                small_ref[1 + a:2 + a, :] = jnp.sum(acc_ref[a * SUB:(a + 1) * SUB, :], axis=0, keepdims=True)
            ones = jnp.ones((SUB, HEAD_DIM), F32)
            for hd in range(HEADS):
                cols = slice(hd * HEAD_DIM, (hd + 1) * HEAD_DIM)
                rowsum = lax.dot_general(ones, gbs_ref[:, cols], (((1,), (1,)), ((), ())),
                                         precision=lax.Precision.HIGHEST, preferred_element_type=F32)
                small_ref[6:7, cols] = rowsum[0:1, :]
            small_ref[7:8, :] = gf_ref[0:1, :]
            small_ref[0:1, :] = jnp.sum(se_ref[...], axis=0, keepdims=True)
            for k in range(KW):
                j = KW - 1 - k
                small_ref[8 + k:9 + k, :] = jnp.sum(gcw_ref[j * SUB:(j + 1) * SUB, :], axis=0, keepdims=True)

    gc_prev, gc_next = _halo_specs(ts, s, C_BR, 2)
    lo_prev, lo_next = _halo_specs(ts, s, C_BR, 0)
    row = pl.BlockSpec((1, C_BR), lambda i: (0, 0))
    return pl.pallas_call(
        body, name="branch_bwd",
        grid=(nt,),
        out_shape=(jax.ShapeDtypeStruct((s, D_IN), BF16), jax.ShapeDtypeStruct((40, C_BR), F32),
                   jax.ShapeDtypeStruct((HEADS, CHUNK, CHUNK), F32)),
        in_specs=[pl.BlockSpec((ts, D_IN), lambda i: (i, 0)),
                  pl.BlockSpec((ts, 2 * C_BR), lambda i: (i, 0)),
                  pl.BlockSpec((ts, C_BR), lambda i: (i, 0)),
                  gc_prev, gc_next, lo_prev, lo_next, lo_prev, lo_next,
                  _whole_vmem(), row, row, row, row, _whole_vmem(), _whole_vmem(), _whole_vmem(),
                  _whole_vmem(), _whole_vmem(), _whole_vmem()],
        out_specs=(pl.BlockSpec((ts, D_IN), lambda i: (i, 0)),
                   pl.BlockSpec((40, C_BR), lambda i: (0, 0)),
                   pl.BlockSpec((HEADS, CHUNK, CHUNK), lambda i: (0, 0, 0))),
        scratch_shapes=[pltpu.VMEM((SUB, te, LANE), F32),
                        pltpu.VMEM((ts, LANE), F32),
                        pltpu.VMEM((ts, LANE), F32),
                        pltpu.VMEM((te, C_BR), F32),
                        pltpu.VMEM((5 * SUB, C_BR), F32),
                        pltpu.VMEM((KW * SUB, C_BR), F32),
                        pltpu.VMEM((CHUNK, C_BR), F32)],
        compiler_params=_cparams(("arbitrary",)),
    )(proj, dy, cv, proj, proj, dy, dy, cv, cv,
      cwg, conv_ln_g, conv_ln_b, sgu_ln_g, sgu_ln_b, w_s, ws_t, bs_t, gf8, se8, dep)


def _in_bwd(dproj, x, dx2, norm_g, wing, dep, first_tile, n_tiles, gx_prev=None, gn_prev=None):
    s = x.shape[0]
    ts = min(TS_INB, s)

    def body(*refs):
        dp_ref, x_ref, dx2_ref, g_ref, w_ref = refs[:5]
        gx_ref, gn_ref, acc_ref = refs[-3:]
        i = pl.program_id(0)

        @pl.when(i == 0)
        def _():
            acc_ref[...] = jnp.zeros_like(acc_ref)

        dh = jnp.zeros((ts, D_MODEL), F32)
        for k in range(N_SHARD):
            dh = dh + _dot_nt(dp_ref[:, k * W_IN_COLS:(k + 1) * W_IN_COLS], w_ref[k])
        xt = x_ref[...]
        r = lax.rsqrt(jnp.mean(xt * xt, axis=-1, keepdims=True) + EPS)
        n = xt * r
        acc_ref[...] += _fold8(dh * n)
        dn = dh * g_ref[...]
        gx_ref[...] = dx2_ref[...] + r * (dn - n * jnp.mean(dn * n, axis=-1, keepdims=True))

        @pl.when(i == n_tiles - 1)
        def _():
            total = jnp.broadcast_to(jnp.sum(acc_ref[...], axis=0, keepdims=True), gn_ref.shape)
            if gn_prev is not None:
                total = total + refs[7][...]
            gn_ref[...] = total

    tile = pl.BlockSpec((ts, D_MODEL), lambda i: (i + first_tile, 0))
    in_specs = [pl.BlockSpec((ts, D_IN), lambda i: (i + first_tile, 0)), tile, tile,
                pl.BlockSpec((1, D_MODEL), lambda i: (0, 0)), _whole_vmem(), pl.BlockSpec(memory_space=pl.ANY)]
    operands = [dproj, x, dx2, norm_g, wing, dep]
    aliases = {}
    if gx_prev is not None:
        in_specs += [pl.BlockSpec(memory_space=pl.ANY), _whole_vmem()]
        operands += [gx_prev, gn_prev]
        aliases = {6: 0}
    return pl.pallas_call(
        body, name="in_bwd_%d" % first_tile,
        grid=(n_tiles,),
        out_shape=(jax.ShapeDtypeStruct((s, D_MODEL), F32), jax.ShapeDtypeStruct((SUB, D_MODEL), F32)),
        in_specs=in_specs,
        out_specs=(tile, pl.BlockSpec((SUB, D_MODEL), lambda i: (0, 0))),
        scratch_shapes=[pltpu.VMEM((SUB, D_MODEL), F32)],
        input_output_aliases=aliases,
        compiler_params=_cparams(("arbitrary",)),
    )(*operands)


def _grad_w_in(h, dproj, dep):
    s = h.shape[0]
    tk = min(TK_GW, s)
    nk = s // tk
    half = D_MODEL // 2

    def body(h_ref, dp_ref, dep_ref, o_ref, ob_ref):
        @pl.when(pl.program_id(1) == 0)
        def _():
            o_ref[...] = jnp.zeros_like(o_ref)

        o_ref[0] += _dot_tn(h_ref[...], dp_ref[...]).reshape(2, half, W_IN_COLS)

        @pl.when(pl.program_id(1) == nk - 1)
        def _():
            ob_ref[...] = o_ref[...].astype(BF16)

    shard = pl.BlockSpec((1, 2, half, W_IN_COLS), lambda k, t: (k, 0, 0, 0))
    return pl.pallas_call(
        body, name="grad_w_in",
        grid=(N_SHARD, nk),
        out_shape=(jax.ShapeDtypeStruct((N_SHARD, 2, half, W_IN_COLS), F32),
                   jax.ShapeDtypeStruct((N_SHARD, 2, half, W_IN_COLS), BF16)),
        in_specs=[pl.BlockSpec((tk, D_MODEL), lambda k, t: (t, 0)),
                  pl.BlockSpec((tk, W_IN_COLS), lambda k, t: (t, k)), _whole_vmem()],
        out_specs=(shard, shard),
        compiler_params=_cparams(("parallel", "arbitrary")),
    )(h, dproj, dep)


HBM_SPEC = pl.BlockSpec(memory_space=pltpu.HBM)
SEM_SPEC = pl.BlockSpec(memory_space=pltpu.SEMAPHORE)
SIDE_EFFECT = pltpu.SideEffectType.DATAFLOW_SIDE_EFFECTING


def _remote_copies(plan, bufs, send_sems, recv_sems):
    x, y, c = _mesh_pos()
    return [pltpu.make_async_remote_copy(src_ref=src, dst_ref=dst, send_sem=send_sems.at[k],
                                         recv_sem=recv_sems.at[k], device_id=dev, device_id_type=MESH)
            for k, (src, dst, dev) in enumerate(plan(x, y, c, *bufs))]


def _start_copies(name, bufs, n_copies, plan):
    n = len(bufs)

    def body(*refs):
        for cp in _remote_copies(plan, refs[:n], refs[n], refs[n + 1]):
            cp.start()
        refs[-1][...] = jnp.zeros_like(refs[-1])

    outs = pl.pallas_call(
        body, name=name,
        out_shape=(pltpu.SemaphoreType.DMA((n_copies,)), pltpu.SemaphoreType.DMA((n_copies,)),
                   *[pltpu.HBM(b.shape, b.dtype) for b in bufs], jax.ShapeDtypeStruct((SUB, LANE), F32)),
        in_specs=[HBM_SPEC] * n,
        out_specs=(SEM_SPEC, SEM_SPEC, *[HBM_SPEC] * n, _whole_vmem()),
        input_output_aliases={i: 2 + i for i in range(n)},
        compiler_params=pltpu.CompilerParams(has_side_effects=SIDE_EFFECT),
    )(*[pltpu.with_memory_space_constraint(b, pltpu.HBM) for b in bufs])
    return outs[0], outs[1], list(outs[2:2 + n]), outs[-1]


def _wait_copies(name, send_sems, recv_sems, bufs, plan, after):
    n = len(bufs)

    def body(*refs):
        for cp in _remote_copies(plan, refs[:n], refs[n], refs[n + 1]):
            cp.wait_send()
            cp.wait_recv()

    outs = pl.pallas_call(
        body, name=name,
        out_shape=tuple(pltpu.HBM(b.shape, b.dtype) for b in bufs),
        in_specs=[HBM_SPEC] * n + [SEM_SPEC, SEM_SPEC, pl.BlockSpec(memory_space=pl.ANY)],
        out_specs=(HBM_SPEC,) * n,
        input_output_aliases={i: i for i in range(n)},
        compiler_params=pltpu.CompilerParams(has_side_effects=SIDE_EFFECT),
    )(*bufs, send_sems, recv_sems, after)
    return list(outs)


def _landing(shape, dtype):
    return lax.empty(shape, dtype)


def _plan_pair_exchange(x, y, c, g, r):
    return [(g.at[k, 1 - c], r.at[k], (x, y, 1 - c)) for k in range(N_SHARD)]


def _plan_chip_exchange(x, y, c, a, r):
    chips = [(1 - x, y), (x, 1 - y), (1 - x, 1 - y)]
    return [(a.at[2 * cx + cy], r.at[j], (cx, cy, c)) for j, (cx, cy) in enumerate(chips)]


def _plan_pair_gather(x, y, c, f):
    return [(f.at[c], f.at[c], (x, y, 1 - c))]


def _plan_all_gather(x, y, c, own, land):
    me = 4 * x + 2 * y + c
    flip = lambda v, bit: 1 - v if bit else v
    return [(own, land.at[me], (flip(x, m >> 2 & 1), flip(y, m >> 1 & 1), flip(c, m & 1))) for m in range(1, 8)]


def _add_pair(g, r, pos, name):
    _, _, rows, cols = g.shape
    tr = min(256, rows)

    def body(pos_ref, g_ref, r_ref, o_ref, ob_ref):
        v = g_ref[0] + r_ref[...].astype(F32)
        ob_ref[...] = v.astype(BF16)

        @pl.when(pl.program_id(1) == pos_ref[0])
        def _():
            o_ref[...] = v[0]

    blk = pl.BlockSpec((1, tr, cols), lambda t, k, pos_ref: (k, t, 0))
    return pl.pallas_call(
        body, name=name,
        grid_spec=pltpu.PrefetchScalarGridSpec(
            num_scalar_prefetch=1, grid=(rows // tr, N_SHARD),
            in_specs=[pl.BlockSpec((1, 1, tr, cols), lambda t, k, pos_ref: (k, pos_ref[1], t, 0)), blk],
            out_specs=(pl.BlockSpec((tr, cols), lambda t, k, pos_ref: (t, 0)), blk)),
        out_shape=(jax.ShapeDtypeStruct((rows, cols), F32), jax.ShapeDtypeStruct((N_SHARD, rows, cols), BF16)),
        compiler_params=_cparams(("parallel", "arbitrary")),
    )(pos, g, r)


def _add_chips(a, r, pos, name):
    rows, cols = a.shape
    tr = min(256, rows)

    def body(pos_ref, a_ref, r_ref, o_ref):
        o_ref[0] = ((a_ref[...] + r_ref[0].astype(F32)) + r_ref[1].astype(F32)) + r_ref[2].astype(F32)

    return pl.pallas_call(
        body, name=name,
        grid_spec=pltpu.PrefetchScalarGridSpec(
            num_scalar_prefetch=1, grid=(rows // tr,),
            in_specs=[pl.BlockSpec((tr, cols), lambda t, pos_ref: (t, 0)),
                      pl.BlockSpec((3, tr, cols), lambda t, pos_ref: (0, t, 0))],
            out_specs=pl.BlockSpec((1, tr, cols), lambda t, pos_ref: (pos_ref[1], t, 0))),
        out_shape=jax.ShapeDtypeStruct((2, rows, cols), F32),
        compiler_params=_cparams(("parallel",)),
    )(pos, a, r)


def _sum_slots(pos_ref, own, land_ref, rows):
    me = pos_ref[2]
    total = None
    for d in range(8):
        term = jnp.where(me == d, own, land_ref[d] if rows is None else land_ref[d, rows, :])
        total = term if total is None else total + term
    return total


def _sum_small(pos, small, small_land, gws, gws_land):
    def body(pos_ref, sm_ref, sml_ref, gw_ref, gwl_ref, o_sm, o_gw):
        o_sm[...] = _sum_slots(pos_ref, sm_ref[...], sml_ref, None)
        o_gw[...] = _sum_slots(pos_ref, gw_ref[...], gwl_ref, None)

    return pl.pallas_call(
        body, name="sum_small",
        grid_spec=pltpu.PrefetchScalarGridSpec(
            num_scalar_prefetch=1, grid=(1,),
            in_specs=[_whole_vmem()] * 4, out_specs=[_whole_vmem()] * 2),
        out_shape=[jax.ShapeDtypeStruct(small.shape, F32), jax.ShapeDtypeStruct(gws.shape, F32)],
        compiler_params=_cparams(("arbitrary",)),
    )(pos, small, small_land, gws, gws_land)


def _adamw_math(w, g, m, v):
    m = ADAM_B1 * m + (1.0 - ADAM_B1) * g
    v = ADAM_B2 * v + (1.0 - ADAM_B2) * (g * g)
    m_hat = m / (1.0 - ADAM_B1 ** ADAM_STEP)
    v_hat = v / (1.0 - ADAM_B2 ** ADAM_STEP)
    delta = -ADAM_LR * (m_hat / (jnp.sqrt(v_hat) + ADAM_EPS) + ADAM_WD * w)
    return delta, m, v


def _adamw_large(w, g, m, v, dep, name):
    rows, cols = w.shape
    tr = min(256, rows)

    def body(w_ref, g_ref, m_ref, v_ref, dep_ref, d_ref, nm_ref, nv_ref):
        d_ref[...], nm_ref[...], nv_ref[...] = _adamw_math(w_ref[...], g_ref[...], m_ref[...], v_ref[...])

    tile = pl.BlockSpec((tr, cols), lambda t: (t, 0))
    return pl.pallas_call(
        body, name=name,
        grid=(rows // tr,),
        out_shape=(jax.ShapeDtypeStruct(w.shape, F32),) * 3,
        in_specs=[tile] * 4 + [_whole_vmem()], out_specs=(tile,) * 3,
        compiler_params=_cparams(("parallel",)),
    )(w, g, m, v, dep)


_ROW_OF = {"conv_b": 1, "conv_ln_g": 2, "conv_ln_b": 3, "sgu_ln_g": 4, "sgu_ln_b": 5, "b_s": 6, "final_g": 7}
_CONV_W_ROW = 8


def _adamw_small(call_name, names, grads, pos, params):
    def body(pos_ref, p_ref, q_ref, *refs):
        gn_ref, gnl_ref = p_ref, q_ref
        n_in = 3 * len(names)
        ins, outs = refs[:n_in], refs[n_in:]
        me = pos_ref[0]
        for a, name in enumerate(names):
            w_ref, m_ref, v_ref = ins[3 * a:3 * a + 3]
            if name == "conv_w":
                g = jnp.zeros((KW, CONV_W_COLS), F32)
                for k in range(N_SHARD):
                    blk = p_ref[_CONV_W_ROW:_CONV_W_ROW + KW, k * CONV_W_COLS:(k + 1) * CONV_W_COLS]
                    g = jnp.where(me == k, blk, g)
            elif name == "w_s":
                g = q_ref[...]
            elif name == "norm_g":
                g = _sum_slots(pos_ref, gn_ref[0:1, :], gnl_ref, slice(0, 1))
            else:
                g = p_ref[_ROW_OF[name]:_ROW_OF[name] + 1, :]
            delta, nm, nv = _adamw_math(w_ref[...], g, m_ref[...], v_ref[...])
            for o_ref, val in zip(outs[4 * a:4 * a + 4], (g, delta, nm, nv)):
                o_ref[...] = val

    operands, shapes = [], []
    for name in names:
        operands += list(params[name])
        shapes += [jax.ShapeDtypeStruct(params[name][0].shape, F32)] * 4
    outs = pl.pallas_call(
        body, name=call_name,
        grid_spec=pltpu.PrefetchScalarGridSpec(
            num_scalar_prefetch=1, grid=(1,),
            in_specs=[_whole_vmem()] * (2 + len(operands)),
            out_specs=[_whole_vmem()] * len(shapes)),
        out_shape=shapes,
        compiler_params=_cparams(("arbitrary",)),
    )(pos, *grads, *operands)
    return {name: tuple(outs[4 * a:4 * a + 4]) for a, name in enumerate(names)}


def kernel(x, norm_g, w_in, conv_w, conv_b, conv_ln_g, conv_ln_b, sgu_ln_g, sgu_ln_b, w_s, b_s, w_out, final_g, loss_target, m_norm_g, m_w_in, m_conv_w, m_conv_b, m_conv_ln_g, m_conv_ln_b, m_sgu_ln_g, m_sgu_ln_b, m_w_s, m_b_s, m_w_out, m_final_g, v_norm_g, v_w_in, v_conv_w, v_conv_b, v_conv_ln_g, v_conv_ln_b, v_sgu_ln_g, v_sgu_ln_b, v_w_s, v_b_s, v_w_out, v_final_g):
    xi, yi, ci = _mesh_pos()
    pos = jnp.stack([2 * xi + yi, ci, 4 * xi + 2 * yi + ci]).astype(jnp.int32)

    x2d = x[0]
    tgt = loss_target[0]
    fg = final_g.reshape(1, D_MODEL)
    ws3 = w_s[0]
    ws_t = jnp.swapaxes(ws3, 1, 2)
    bs_t = jnp.transpose(b_s[0])

    chip = 2 * xi + yi
    order = jnp.stack([chip, 2 * (1 - xi) + yi, 2 * xi + 1 - yi, 2 * (1 - xi) + 1 - yi]).astype(jnp.int32)
    h, proj, wing, woutg, cwg = _rms_proj_gather(x2d, norm_g, w_in[0], w_out[0], conv_w[0], order)
    y, cv = _branch_fwd(proj, cwg, conv_b, conv_ln_g, conv_ln_b, sgu_ln_g, sgu_ln_b, ws3, bs_t)
    dx2, dy, gwout, gf8, se8 = _out_proj(y, x2d, tgt, woutg, fg)
    in_rows, out_rows = D_MODEL // 2, W_OUT_ROWS // 2
    gwout4 = gwout.reshape(N_SHARD, 2, out_rows, D_MODEL)
    ss, rs, (gwout4, r1_out), tok = _start_copies(
        "start_pair_exchange_w_out", [gwout4, _landing((N_SHARD, out_rows, D_MODEL), F32)], N_SHARD,
        _plan_pair_exchange)
    dproj, small, gws3 = _branch_bwd(proj, dy, cv, cwg, conv_ln_g, conv_ln_b, sgu_ln_g, sgu_ln_b, ws3, ws_t, bs_t,
                                     gf8, se8, tok)
    gws = gws3.reshape(HEADS * CHUNK, CHUNK)
    gwout4, r1_out = _wait_copies("wait_pair_exchange_w_out", ss, rs, [gwout4, r1_out], _plan_pair_exchange, dproj)
    a_out, a_out_bf = _add_pair(gwout4, r1_out, pos, "add_pair_w_out")

    def plan_b(x, y, c, a, r, sm, sml, gw, gwl):
        return (_plan_chip_exchange(x, y, c, a, r) + _plan_all_gather(x, y, c, sm, sml)
                + _plan_all_gather(x, y, c, gw, gwl))

    ss, rs, bufs_b, tok = _start_copies(
        "start_chip_exchange_w_out",
        [a_out_bf, _landing((3, out_rows, D_MODEL), BF16), small, _landing((8,) + small.shape, F32),
         gws, _landing((8,) + gws.shape, F32)], 3 + 7 + 7, plan_b)
    gwin, gwin_bf = _grad_w_in(h, dproj, tok)
    ss_c, rs_c, (gwin_bf, r1_in), tok = _start_copies(
        "start_pair_exchange_w_in", [gwin_bf, _landing((N_SHARD, in_rows, W_IN_COLS), BF16)], N_SHARD,
        _plan_pair_exchange)
    nt = x2d.shape[0] // min(TS_INB, x2d.shape[0])
    cut = nt // 2
    gx_a, gn_a = _in_bwd(dproj, x2d, dx2, norm_g, wing, tok, 0, cut)

    gwin_bf, r1_in = _wait_copies("wait_pair_exchange_w_in", ss_c, rs_c, [gwin_bf, r1_in], _plan_pair_exchange, gx_a)
    a_in, a_in_bf = _add_pair(gwin, r1_in, pos, "add_pair_w_in")
    a_out_bf, r2_out, small, small_land, gws, gws_land = _wait_copies(
        "wait_chip_exchange_w_out", ss, rs, bufs_b, plan_b, gx_a)
    f_out = _add_chips(a_out, r2_out, pos, "add_chips_w_out")
    p, q = _sum_small(pos, small, small_land, gws, gws_land)
    loss = (0.5 / D_MODEL) * jnp.sum(p[0])

    flat = lambda a: a.reshape(1, C_BR)
    flat_ws = lambda a: a.reshape(HEADS * CHUNK, CHUNK)
    params = {
        "norm_g": (norm_g, m_norm_g, v_norm_g),
        "conv_b": (conv_b, m_conv_b, v_conv_b),
        "conv_ln_g": (conv_ln_g, m_conv_ln_g, v_conv_ln_g),
        "conv_ln_b": (conv_ln_b, m_conv_ln_b, v_conv_ln_b),
        "sgu_ln_g": (sgu_ln_g, m_sgu_ln_g, v_sgu_ln_g),
        "sgu_ln_b": (sgu_ln_b, m_sgu_ln_b, v_sgu_ln_b),
        "b_s": (flat(b_s), flat(m_b_s), flat(v_b_s)),
        "final_g": (flat(final_g), flat(m_final_g), flat(v_final_g)),
        "conv_w": (conv_w[0], m_conv_w[0], v_conv_w[0]),
        "w_s": (flat_ws(w_s), flat_ws(m_w_s), flat_ws(v_w_s)),
    }
    res = _adamw_small("adamw_small", [n for n in params if n != "norm_g"], (p, q), pos, params)

    def plan_d(x, y, c, a, r, f):
        return _plan_chip_exchange(x, y, c, a, r) + _plan_pair_gather(x, y, c, f)

    ss, rs, bufs_d, tok = _start_copies(
        "start_chip_exchange_w_in", [a_in_bf, _landing((3, in_rows, W_IN_COLS), BF16), f_out], 3 + 1, plan_d)
    grad_x, gn8 = _in_bwd(dproj, x2d, dx2, norm_g, wing, tok, cut, nt - cut, gx_a, gn_a)
    a_in_bf, r2_in, f_out = _wait_copies("wait_chip_exchange_w_in", ss, rs, bufs_d, plan_d, grad_x)
    f_in = _add_chips(a_in, r2_in, pos, "add_chips_w_in")

    def plan_e(x, y, c, f, gn, gnl):
        return _plan_pair_gather(x, y, c, f) + _plan_all_gather(x, y, c, gn, gnl)

    ss, rs, bufs_e, tok = _start_copies(
        "start_pair_gather_w_in", [f_in, gn8, _landing((8,) + gn8.shape, F32)], 1 + 7, plan_e)
    g_w_out = f_out.reshape(W_OUT_ROWS, D_MODEL)
    d_w_out, nm_w_out, nv_w_out = _adamw_large(w_out[0], g_w_out, m_w_out[0], v_w_out[0], tok, "adamw_w_out")
    f_in, gn8, gn_land = _wait_copies("wait_pair_gather_w_in", ss, rs, bufs_e, plan_e, d_w_out)
    g_w_in = f_in.reshape(D_MODEL, W_IN_COLS)
    d_w_in, nm_w_in, nv_w_in = _adamw_large(w_in[0], g_w_in, m_w_in[0], v_w_in[0], tok, "adamw_w_in")
    res.update(_adamw_small("adamw_norm_g", ["norm_g"], (gn8, gn_land), pos, params))
    res["w_in"] = tuple(a[None] for a in (g_w_in, d_w_in, nm_w_in, nv_w_in))
    res["w_out"] = tuple(a[None] for a in (g_w_out, d_w_out, nm_w_out, nv_w_out))
    res["conv_w"] = tuple(a[None] for a in res["conv_w"])
    res["w_s"] = tuple(a.reshape(w_s.shape) for a in res["w_s"])
    res["b_s"] = tuple(a.reshape(b_s.shape) for a in res["b_s"])
    res["final_g"] = tuple(a.reshape(final_g.shape) for a in res["final_g"])

    order = ("norm_g", "w_in", "conv_w", "conv_b", "conv_ln_g", "conv_ln_b", "sgu_ln_g", "sgu_ln_b",
             "w_s", "b_s", "w_out", "final_g")
    out = [loss, grad_x[None]]
    for part in range(4):
        out += [res[name][part] for name in order]
    return tuple(out)
```

```python
import jax
import jax.numpy as jnp
from jax import lax
from jax.experimental import pallas as pl
from jax.experimental.pallas import tpu as pltpu

F32 = jnp.float32
BF16 = jnp.bfloat16
MESH = pl.DeviceIdType.MESH

EPS = 1e-6
D_MODEL = 1024
C_BR = 1024
D_IN = 6 * C_BR
N_SHARD = 4
W_IN_COLS = D_IN // N_SHARD
W_OUT_ROWS = 2 * C_BR // N_SHARD
CONV_W_COLS = C_BR // N_SHARD
KW = 31
HALO = 16
HEADS = 8
HEAD_DIM = 128
CHUNK = 128
LANE = 128
SUB = 8

ADAM_LR = 0.001
ADAM_B1 = 0.9
ADAM_B2 = 0.999
ADAM_EPS = 1e-08
ADAM_WD = 0.01
ADAM_STEP = 10

TS_PROJ = 512
TS_FWD = 256
TS_OUT = 512
TS_BWD = 256
TS_INB = 512
TK_GW = 2048
ROWS_A = 128
ROWS_B = 64
VMEM_LIMIT = 56 * 1024 * 1024


def _cparams(sem=None, vmem=VMEM_LIMIT):
    kw = dict(vmem_limit_bytes=vmem)
    if sem is not None:
        kw["dimension_semantics"] = sem
    return pltpu.CompilerParams(**kw)


def _whole_vmem():
    return pl.BlockSpec(memory_space=pltpu.VMEM)


def _sigmoid(v):
    return 0.5 * jnp.tanh(0.5 * v) + 0.5


def _silu(v):
    h = 0.5 * v
    return h + h * jnp.tanh(h)


def _fold8(v):
    n, c = v.shape
    return v.reshape(n // SUB, SUB, c).sum(axis=0)


def _dot_nt(a, b):
    return lax.dot_general(a, b, (((1,), (1,)), ((), ())), preferred_element_type=F32)


def _dot_tn(a, b):
    return lax.dot_general(a, b, (((0,), (0,)), ((), ())), preferred_element_type=F32)


def _mesh_pos():
    return lax.axis_index("x"), lax.axis_index("y"), lax.axis_index("c")


def _rms_proj_gather(x, norm_g, w_in, w_out, conv_w, order):
    s = x.shape[0]
    ts = min(TS_PROJ, s)
    nt = s // ts
    hin = w_in.shape[0] // 2
    hout = w_out.shape[0] // 2

    def body(order_ref, x_ref, g_ref, win_ref, wout_ref, cw_ref,
             h_ref, proj_ref, wing_ref, woutg_ref, cwg_ref, wg_ref, wob_ref, hs_ref, send_sems, recv_sems, local_sems):
        p = pl.program_id(0)
        t = pl.program_id(1)
        mx, my, c = _mesh_pos()
        me = 2 * mx + my
        chips = [(1 - mx, my), (mx, 1 - my), (1 - mx, 1 - my)]
        sibling = (mx, my, 1 - c)

        def remote(src, dst, sem, dev):
            return pltpu.make_async_remote_copy(
                src_ref=src, dst_ref=dst, send_sem=send_sems.at[sem], recv_sem=recv_sems.at[sem],
                device_id=dev, device_id_type=MESH)

        def w_in_part(blk, half):
            return wg_ref.at[blk, pl.ds(half * hin, hin)]

        def keep(blk, k):
            return pltpu.make_async_copy(wg_ref.at[blk], wing_ref.at[blk], local_sems.at[2 + k])

        def w_out_part(blk, half):
            return woutg_ref.at[blk, pl.ds(half * hout, hout)]

        def sends():
            out = []
            for j, (cx, cy) in enumerate(chips):
                blk = 2 * cx + cy
                out.append(remote(w_in_part(me, c), w_in_part(me, c), j, (cx, cy, c)))
                out.append(remote(w_in_part(blk, c), w_in_part(blk, c), 3 + j, sibling))
                out.append(remote(wob_ref.at[pl.ds(c * hout, hout)], w_out_part(me, c), 6 + j, (cx, cy, c)))
                out.append(remote(w_out_part(blk, c), w_out_part(blk, c), 9 + j, sibling))
                out.append(remote(cw_ref, cwg_ref.at[me], 12 + j, (cx, cy, c)))
            return out

        @pl.when(jnp.logical_and(p == 0, t == 0))
        def _():
            wg_ref[me] = win_ref[...].astype(BF16)
            wob_ref[...] = wout_ref[...].astype(BF16)
            keep(me, 0).start()
            mine = [pltpu.make_async_copy(wob_ref, woutg_ref.at[me], local_sems.at[0]),
                    pltpu.make_async_copy(cw_ref, cwg_ref.at[me], local_sems.at[1])]
            for cp in mine:
                cp.start()
            for k, cp in enumerate(sends()):
                if k % 5 == 4 or (k % 5 == 0 and k // 5 < 2):
                    cp.start()
            for cp in mine:
                cp.wait()

        for j, (cx, cy) in enumerate(chips):
            blk = 2 * cx + cy

            @pl.when(jnp.logical_and(p == j + 1, t == 0))
            def _():
                remote(w_in_part(blk, c), w_in_part(blk, c), j, (cx, cy, c)).wait_recv()
                remote(w_in_part(blk, c), w_in_part(blk, c), 3 + j, sibling).start()
                remote(w_in_part(blk, 1 - c), w_in_part(blk, 1 - c), 3 + j, sibling).wait_recv()
                keep(blk, j + 1).start()
                if j == 0:
                    sends()[5 * 2].start()
                if j == 1:
                    for jj in range(3):
                        sends()[5 * jj + 2].start()

        rows = pl.ds(pl.multiple_of(t * ts, ts), ts)

        @pl.when(p == 0)
        def _():
            xt = x_ref[...]
            r = lax.rsqrt(jnp.mean(xt * xt, axis=-1, keepdims=True) + EPS)
            hb = (xt * r * g_ref[...]).astype(BF16)
            h_ref[...] = hb
            hs_ref[rows, :] = hb

        proj_ref[...] = jnp.dot(hs_ref[rows, :], wg_ref[order_ref[p]], preferred_element_type=F32).astype(BF16)

        @pl.when(jnp.logical_and(p == N_SHARD - 1, t == nt - 1))
        def _():
            for j, (cx, cy) in enumerate(chips):
                blk = 2 * cx + cy
                remote(w_out_part(blk, c), w_out_part(blk, c), 6 + j, (cx, cy, c)).wait_recv()
                remote(w_out_part(blk, c), w_out_part(blk, c), 9 + j, sibling).start()
            for j, (cx, cy) in enumerate(chips):
                blk = 2 * cx + cy
                remote(w_out_part(blk, 1 - c), w_out_part(blk, 1 - c), 9 + j, sibling).wait_recv()
                remote(cw_ref, cwg_ref.at[blk], 12 + j, (cx, cy, c)).wait_recv()
            for cp in sends():
                cp.wait_send()
            keep(me, 0).wait()
            for j, (cx, cy) in enumerate(chips):
                keep(2 * cx + cy, j + 1).wait()

    hbm = pl.BlockSpec(memory_space=pl.ANY)
    return pl.pallas_call(
        body, name="rms_proj_gather",
        grid_spec=pltpu.PrefetchScalarGridSpec(
            num_scalar_prefetch=1, grid=(N_SHARD, nt),
            in_specs=[pl.BlockSpec((ts, D_MODEL), lambda p, t, o: (jnp.where(p == 0, t, nt - 1), 0)),
                      pl.BlockSpec((1, D_MODEL), lambda p, t, o: (0, 0)),
                      _whole_vmem(), _whole_vmem(), _whole_vmem()],
            out_specs=(pl.BlockSpec((ts, D_MODEL), lambda p, t, o: (jnp.where(p == 0, t, nt - 1), 0)),
                       pl.BlockSpec((ts, W_IN_COLS), lambda p, t, o: (t, o[p])),
                       hbm, hbm, hbm),
            scratch_shapes=[pltpu.VMEM((N_SHARD,) + w_in.shape, BF16), pltpu.VMEM(w_out.shape, BF16),
                            pltpu.VMEM((s, D_MODEL), BF16),
                            pltpu.SemaphoreType.DMA((15,)), pltpu.SemaphoreType.DMA((15,)),
                            pltpu.SemaphoreType.DMA((6,))]),
        out_shape=(jax.ShapeDtypeStruct((s, D_MODEL), BF16), jax.ShapeDtypeStruct((s, D_IN), BF16),
                   jax.ShapeDtypeStruct((N_SHARD,) + w_in.shape, BF16),
                   jax.ShapeDtypeStruct((N_SHARD,) + w_out.shape, BF16),
                   jax.ShapeDtypeStruct((N_SHARD,) + conv_w.shape, F32)),
        compiler_params=_cparams(("arbitrary", "arbitrary")),
    )(order, x, norm_g, w_in, w_out, conv_w)


def _halo_specs(ts, s, width, col_block):
    per = ts // HALO
    last = s // HALO - 1
    prev = pl.BlockSpec((HALO, width), lambda i: (jnp.maximum(i * per - 1, 0), col_block))
    nxt = pl.BlockSpec((HALO, width), lambda i: (jnp.minimum((i + 1) * per, last), col_block))
    return prev, nxt


def _layer_norm_stats(v):
    mu = jnp.mean(v, axis=-1, keepdims=True)
    vc = v - mu
    var = jnp.mean(vc * vc, axis=-1, keepdims=True)
    rstd = lax.rsqrt(var + EPS)
    return vc * rstd, rstd


def _fill_shifted(sh_ref, ext):
    n = ext.shape[0]
    sh_ref[0] = ext
    for r in range(1, SUB):
        sh_ref[r] = pltpu.roll(ext, n - r, axis=0)


def _branch_fwd(proj, cwg, conv_b, conv_ln_g, conv_ln_b, sgu_ln_g, sgu_ln_b, w_s, bs_t):
    s = proj.shape[0]
    ts = min(TS_FWD, s)
    nt = s // ts
    ra = min(ROWS_A, ts)

    def body(pm_ref, pp_ref, pn_ref, cw_ref, cb_ref, clg_ref, clb_ref, slg_ref, slb_ref, ws_ref, bst_ref,
             y_new, cv_ref, sh_ref):
        i = pl.program_id(0)
        keep_prev = (i > 0).astype(F32)
        keep_next = (i < nt - 1).astype(F32)

        for lb in range(C_BR // LANE):
            lanes = slice(lb * LANE, (lb + 1) * LANE)
            gates = slice(C_BR + lb * LANE, C_BR + (lb + 1) * LANE)

            def glu(ref):
                return ref[:, lanes].astype(F32) * _sigmoid(ref[:, gates].astype(F32))

            ext = jnp.concatenate([glu(pp_ref) * keep_prev, glu(pm_ref), glu(pn_ref) * keep_next], axis=0)
            _fill_shifted(sh_ref, ext)
            shard, off = divmod(lb * LANE, CONV_W_COLS)
            bias = cb_ref[:, lanes]

            def chunk(jc, carry):
                base = pl.multiple_of(jc * ra, ra)
                acc = jnp.zeros((ra, LANE), F32) + bias
                for k in range(KW):
                    o = k + 1
                    acc = acc + sh_ref[o % SUB, pl.ds(base + SUB * (o // SUB), ra), :] * cw_ref[
                        shard, k:k + 1, off:off + LANE]
                cv_ref[pl.ds(base, ra), lanes] = acc
                return carry

            lax.fori_loop(0, ts // ra, chunk, 0)

        lnh, _ = _layer_norm_stats(cv_ref[...])
        ln = lnh * clg_ref[...] + clb_ref[...]
        gc = pm_ref[:, 2 * C_BR:3 * C_BR].astype(F32)
        y_new[:, :C_BR] = (_silu(ln) * _silu(gc)).astype(BF16)

        vh, _ = _layer_norm_stats(pm_ref[:, 4 * C_BR:5 * C_BR].astype(F32))
        vn = (vh * slg_ref[...] + slb_ref[...]).astype(BF16)
        for hd in range(HEADS):
            w_h = ws_ref[hd].astype(BF16)
            b_h = bst_ref[:, hd:hd + 1]
            cols = slice(hd * HEAD_DIM, (hd + 1) * HEAD_DIM)
            for ch in range(ts // CHUNK):
                rows = slice(ch * CHUNK, (ch + 1) * CHUNK)
                mixed = jnp.dot(w_h, vn[rows, cols], preferred_element_type=F32) + b_h
                u = pm_ref[rows, 3 * C_BR + hd * HEAD_DIM:3 * C_BR + (hd + 1) * HEAD_DIM].astype(F32)
                gs = pm_ref[rows, 5 * C_BR + hd * HEAD_DIM:5 * C_BR + (hd + 1) * HEAD_DIM].astype(F32)
                y_new[rows, C_BR + hd * HEAD_DIM:C_BR + (hd + 1) * HEAD_DIM] = (
                    u * mixed * _silu(gs)).astype(BF16)

    prev, nxt = _halo_specs(ts, s, 2 * C_BR, 0)
    row = pl.BlockSpec((1, C_BR), lambda i: (0, 0))
    return pl.pallas_call(
        body, name="branch_fwd",
        grid=(nt,),
        out_shape=(jax.ShapeDtypeStruct((s, 2 * C_BR), BF16), jax.ShapeDtypeStruct((s, C_BR), F32)),
        in_specs=[pl.BlockSpec((ts, D_IN), lambda i: (i, 0)), prev, nxt,
                  _whole_vmem(), row, row, row, row, row, _whole_vmem(), _whole_vmem()],
        out_specs=(pl.BlockSpec((ts, 2 * C_BR), lambda i: (i, 0)),
                   pl.BlockSpec((ts, C_BR), lambda i: (i, 0))),
        scratch_shapes=[pltpu.VMEM((SUB, ts + 2 * HALO, LANE), F32)],
        compiler_params=_cparams(("parallel",)),
    )(proj, proj, proj, cwg, conv_b, conv_ln_g, conv_ln_b, sgu_ln_g, sgu_ln_b, w_s, bs_t)


def _out_proj(y, x, target, woutg, final_g):
    s = x.shape[0]
    ts = min(TS_OUT, s)
    nt = s // ts

    def body(y_ref, x_ref, t_ref, w_ref, g_ref, dx2_ref, dy_ref, gw_ref, gf_ref, se_ref):
        i = pl.program_id(0)

        @pl.when(i == 0)
        def _():
            gw_ref[...] = jnp.zeros_like(gw_ref)
            gf_ref[...] = jnp.zeros_like(gf_ref)
            se_ref[...] = jnp.zeros_like(se_ref)

        yt = y_ref[...]
        x2 = x_ref[...]
        for k in range(N_SHARD):
            x2 = x2 + jnp.dot(yt[:, k * W_OUT_ROWS:(k + 1) * W_OUT_ROWS], w_ref[k], preferred_element_type=F32)
        r2 = lax.rsqrt(jnp.mean(x2 * x2, axis=-1, keepdims=True) + EPS)
        n2 = x2 * r2
        g = g_ref[...]
        diff = n2 * g - t_ref[...]
        se_ref[...] += _fold8(diff * diff)
        dout = diff * (1.0 / D_MODEL)
        gf_ref[...] += _fold8(dout * n2)
        dn = dout * g
        dx2 = r2 * (dn - n2 * jnp.mean(dn * n2, axis=-1, keepdims=True))
        dx2_ref[...] = dx2
        dxb = dx2.astype(BF16)
        for k in range(N_SHARD):
            rows = slice(k * W_OUT_ROWS, (k + 1) * W_OUT_ROWS)
            dy_ref[:, rows] = _dot_nt(dxb, w_ref[k]).astype(BF16)
            gw_ref[rows, :] += _dot_tn(yt[:, rows], dxb)

        @pl.when(i == nt - 1)
        def _():
            gf_ref[...] = jnp.broadcast_to(jnp.sum(gf_ref[...], axis=0, keepdims=True), gf_ref.shape)

    tile = pl.BlockSpec((ts, D_MODEL), lambda i: (i, 0))
    wide = pl.BlockSpec((ts, 2 * C_BR), lambda i: (i, 0))
    acc8 = pl.BlockSpec((SUB, D_MODEL), lambda i: (0, 0))
    return pl.pallas_call(
        body, name="out_proj",
        grid=(nt,),
        out_shape=(jax.ShapeDtypeStruct((s, D_MODEL), F32), jax.ShapeDtypeStruct((s, 2 * C_BR), BF16),
                   jax.ShapeDtypeStruct((2 * C_BR, D_MODEL), F32),
                   jax.ShapeDtypeStruct((SUB, D_MODEL), F32), jax.ShapeDtypeStruct((SUB, D_MODEL), F32)),
        in_specs=[wide, tile, tile, _whole_vmem(), pl.BlockSpec((1, D_MODEL), lambda i: (0, 0))],
        out_specs=(tile, wide, pl.BlockSpec((2 * C_BR, D_MODEL), lambda i: (0, 0)), acc8, acc8),
        compiler_params=_cparams(("arbitrary",)),
    )(y, x, target, woutg, final_g)


def _dsilu(v, sg):
    return sg * (1.0 + v * (1.0 - sg))


def _branch_bwd(proj, dy, cv, cwg, conv_ln_g, conv_ln_b, sgu_ln_g, sgu_ln_b, w_s, ws_t, bs_t, gf8, se8, dep):
    s = proj.shape[0]
    ts = min(TS_BWD, s)
    nt = s // ts
    ra = min(ROWS_A, ts)
    rb = min(ROWS_B, ts)
    te = ts + 2 * HALO

    def body(pm_ref, dym_ref, cvm_ref, gcp_ref, gcn_ref, dyp_ref, dyn_ref, cvp_ref, cvn_ref,
             cw_ref, clg_ref, clb_ref, slg_ref, slb_ref, ws_ref, wst_ref, bst_ref, gf_ref, se_ref, dep_ref,
             dp_ref, small_ref, gws_ref,
             sh_ref, glu_ref, dgl_ref, dcv_ref, acc_ref, gcw_ref, gbs_ref):
        i = pl.program_id(0)

        @pl.when(i == 0)
        def _():
            acc_ref[...] = jnp.zeros_like(acc_ref)
            gcw_ref[...] = jnp.zeros_like(gcw_ref)
            gbs_ref[...] = jnp.zeros_like(gbs_ref)
            gws_ref[...] = jnp.zeros_like(gws_ref)

        def ext(prev_ref, main, next_ref):
            return jnp.concatenate([prev_ref[...].astype(F32), main, next_ref[...].astype(F32)], axis=0)

        main = slice(HALO, HALO + ts)

        cv_e = ext(cvp_ref, cvm_ref[...], cvn_ref)
        gc_e = ext(gcp_ref, pm_ref[:, 2 * C_BR:3 * C_BR].astype(F32), gcn_ref)
        dyc_e = ext(dyp_ref, dym_ref[:, :C_BR].astype(F32), dyn_ref)
        lnh, rstd = _layer_norm_stats(cv_e)
        clg = clg_ref[...]
        ln = lnh * clg + clb_ref[...]
        sg_ln = _sigmoid(ln)
        sg_gc = _sigmoid(gc_e)
        d_ln = dyc_e * gc_e * sg_gc * _dsilu(ln, sg_ln)
        dp_ref[:, 2 * C_BR:3 * C_BR] = (
            dyc_e[main] * ln[main] * sg_ln[main] * _dsilu(gc_e[main], sg_gc[main])).astype(BF16)
        dlnh = d_ln * clg
        d_cv = rstd * (dlnh - jnp.mean(dlnh, axis=-1, keepdims=True)
                       - lnh * jnp.mean(dlnh * lnh, axis=-1, keepdims=True))
        row = lax.broadcasted_iota(jnp.int32, (te, 1), 0)
        valid = jnp.logical_and(jnp.logical_or(row >= HALO, i > 0),
                                jnp.logical_or(row < HALO + ts, i < nt - 1))
        d_cv = jnp.where(valid, d_cv, 0.0)
        dcv_ref[...] = d_cv
        acc_ref[0:8, :] += _fold8(d_cv[main])
        acc_ref[8:16, :] += _fold8(d_ln[main] * lnh[main])
        acc_ref[16:24, :] += _fold8(d_ln[main])

        for lb in range(C_BR // LANE):
            lanes = slice(lb * LANE, (lb + 1) * LANE)
            gates = slice(C_BR + lb * LANE, C_BR + (lb + 1) * LANE)
            shard, off = divmod(lb * LANE, CONV_W_COLS)
            av = pm_ref[:, lanes].astype(F32)
            sg = _sigmoid(pm_ref[:, gates].astype(F32))
            glu_ref[...] = av * sg
            _fill_shifted(sh_ref, dcv_ref[:, lanes])

            def chunk_a(jc, carry):
                base = pl.multiple_of(jc * ra, ra)
                acc = jnp.zeros((ra, LANE), F32)
                for j in range(KW):
                    o = j + 1
                    acc = acc + sh_ref[o % SUB, pl.ds(base + SUB * (o // SUB), ra), :] * cw_ref[
                        shard, KW - 1 - j:KW - j, off:off + LANE]
                dgl_ref[pl.ds(base, ra), :] = acc
                return carry

            lax.fori_loop(0, ts // ra, chunk_a, 0)

            def chunk_b(jc, accs):
                base = pl.multiple_of(jc * rb, rb)
                g = glu_ref[pl.ds(base, rb), :]
                out = []
                for j in range(KW):
                    o = j + 1
                    d = sh_ref[o % SUB, pl.ds(base + SUB * (o // SUB), rb), :]
                    out.append(accs[j] + _fold8(g * d))
                return tuple(out)

            accs = lax.fori_loop(0, ts // rb, chunk_b, tuple(jnp.zeros((SUB, LANE), F32) for _ in range(KW)))
            for j in range(KW):
                gcw_ref[j * SUB:(j + 1) * SUB, lanes] += accs[j]

            dglu = dgl_ref[...]
            dp_ref[:, lanes] = (dglu * sg).astype(BF16)
            dp_ref[:, gates] = (dglu * av * sg * (1.0 - sg)).astype(BF16)

        vh, vrstd = _layer_norm_stats(pm_ref[:, 4 * C_BR:5 * C_BR].astype(F32))
        slg = slg_ref[...]
        vn = (vh * slg + slb_ref[...]).astype(BF16)
        for hd in range(HEADS):
            w_h = ws_ref[hd].astype(BF16)
            wt_h = wst_ref[hd].astype(BF16)
            b_h = bst_ref[:, hd:hd + 1]
            cols = slice(hd * HEAD_DIM, (hd + 1) * HEAD_DIM)
            gws_h = jnp.zeros((CHUNK, CHUNK), F32)
            gbs_h = jnp.zeros((CHUNK, HEAD_DIM), F32)
            for ch in range(ts // CHUNK):
                rows = slice(ch * CHUNK, (ch + 1) * CHUNK)
                vn_b = vn[rows, cols]
                mixed = jnp.dot(w_h, vn_b, preferred_element_type=F32) + b_h
                u = pm_ref[rows, 3 * C_BR + hd * HEAD_DIM:3 * C_BR + (hd + 1) * HEAD_DIM].astype(F32)
                gs = pm_ref[rows, 5 * C_BR + hd * HEAD_DIM:5 * C_BR + (hd + 1) * HEAD_DIM].astype(F32)
                dys = dym_ref[rows, C_BR + hd * HEAD_DIM:C_BR + (hd + 1) * HEAD_DIM].astype(F32)
                sg_gs = _sigmoid(gs)
                silu_gs = gs * sg_gs
                dp_ref[rows, 3 * C_BR + hd * HEAD_DIM:3 * C_BR + (hd + 1) * HEAD_DIM] = (
                    dys * mixed * silu_gs).astype(BF16)
                dp_ref[rows, 5 * C_BR + hd * HEAD_DIM:5 * C_BR + (hd + 1) * HEAD_DIM] = (
                    dys * u * mixed * _dsilu(gs, sg_gs)).astype(BF16)
                d_mixed = dys * u * silu_gs
                dm_b = d_mixed.astype(BF16)
                gws_h = gws_h + _dot_nt(dm_b, vn_b)
                gbs_h = gbs_h + d_mixed
                dcv_ref[HALO + ch * CHUNK:HALO + (ch + 1) * CHUNK, cols] = jnp.dot(
                    wt_h, dm_b, preferred_element_type=F32)
            gws_ref[hd] += gws_h
            gbs_ref[:, cols] += gbs_h
        d_vn = dcv_ref[main, :]
        acc_ref[24:32, :] += _fold8(d_vn * vh)
        acc_ref[32:40, :] += _fold8(d_vn)
        dvh = d_vn * slg
        dp_ref[:, 4 * C_BR:5 * C_BR] = (vrstd * (
            dvh - jnp.mean(dvh, axis=-1, keepdims=True)
            - vh * jnp.mean(dvh * vh, axis=-1, keepdims=True))).astype(BF16)

        @pl.when(i == nt - 1)
        def _():
            small_ref[...] = jnp.zeros_like(small_ref)
            for a in range(5):
                small_ref[1 + a:2 + a, :] = jnp.sum(acc_ref[a * SUB:(a + 1) * SUB, :], axis=0, keepdims=True)
            ones = jnp.ones((SUB, HEAD_DIM), F32)
            for hd in range(HEADS):
                cols = slice(hd * HEAD_DIM, (hd + 1) * HEAD_DIM)
                rowsum = lax.dot_general(ones, gbs_ref[:, cols], (((1,), (1,)), ((), ())),
                                         precision=lax.Precision.HIGHEST, preferred_element_type=F32)
                small_ref[6:7, cols] = rowsum[0:1, :]
            small_ref[7:8, :] = gf_ref[0:1, :]
            small_ref[0:1, :] = jnp.sum(se_ref[...], axis=0, keepdims=True)
            for k in range(KW):
                j = KW - 1 - k
                small_ref[8 + k:9 + k, :] = jnp.sum(gcw_ref[j * SUB:(j + 1) * SUB, :], axis=0, keepdims=True)

    gc_prev, gc_next = _halo_specs(ts, s, C_BR, 2)
    lo_prev, lo_next = _halo_specs(ts, s, C_BR, 0)
    row = pl.BlockSpec((1, C_BR), lambda i: (0, 0))
    return pl.pallas_call(
        body, name="branch_bwd",
        grid=(nt,),
        out_shape=(jax.ShapeDtypeStruct((s, D_IN), BF16), jax.ShapeDtypeStruct((40, C_BR), F32),
                   jax.ShapeDtypeStruct((HEADS, CHUNK, CHUNK), F32)),
        in_specs=[pl.BlockSpec((ts, D_IN), lambda i: (i, 0)),
                  pl.BlockSpec((ts, 2 * C_BR), lambda i: (i, 0)),
                  pl.BlockSpec((ts, C_BR), lambda i: (i, 0)),
                  gc_prev, gc_next, lo_prev, lo_next, lo_prev, lo_next,
                  _whole_vmem(), row, row, row, row, _whole_vmem(), _whole_vmem(), _whole_vmem(),
                  _whole_vmem(), _whole_vmem(), _whole_vmem()],
        out_specs=(pl.BlockSpec((ts, D_IN), lambda i: (i, 0)),
                   pl.BlockSpec((40, C_BR), lambda i: (0, 0)),
                   pl.BlockSpec((HEADS, CHUNK, CHUNK), lambda i: (0, 0, 0))),
        scratch_shapes=[pltpu.VMEM((SUB, te, LANE), F32),
                        pltpu.VMEM((ts, LANE), F32),
                        pltpu.VMEM((ts, LANE), F32),
                        pltpu.VMEM((te, C_BR), F32),
                        pltpu.VMEM((5 * SUB, C_BR), F32),
                        pltpu.VMEM((KW * SUB, C_BR), F32),
                        pltpu.VMEM((CHUNK, C_BR), F32)],
        compiler_params=_cparams(("arbitrary",)),
    )(proj, dy, cv, proj, proj, dy, dy, cv, cv,
      cwg, conv_ln_g, conv_ln_b, sgu_ln_g, sgu_ln_b, w_s, ws_t, bs_t, gf8, se8, dep)


def _in_bwd(dproj, x, dx2, norm_g, wing, dep, first_tile, n_tiles, gx_prev=None, gn_prev=None):
    s = x.shape[0]
    ts = min(TS_INB, s)

    def body(*refs):
        dp_ref, x_ref, dx2_ref, g_ref, w_ref = refs[:5]
        gx_ref, gn_ref, acc_ref = refs[-3:]
        i = pl.program_id(0)

        @pl.when(i == 0)
        def _():
            acc_ref[...] = jnp.zeros_like(acc_ref)

        dh = jnp.zeros((ts, D_MODEL), F32)
        for k in range(N_SHARD):
            dh = dh + _dot_nt(dp_ref[:, k * W_IN_COLS:(k + 1) * W_IN_COLS], w_ref[k])
        xt = x_ref[...]
        r = lax.rsqrt(jnp.mean(xt * xt, axis=-1, keepdims=True) + EPS)
        n = xt * r
        acc_ref[...] += _fold8(dh * n)
        dn = dh * g_ref[...]
        gx_ref[...] = dx2_ref[...] + r * (dn - n * jnp.mean(dn * n, axis=-1, keepdims=True))

        @pl.when(i == n_tiles - 1)
        def _():
            total = jnp.broadcast_to(jnp.sum(acc_ref[...], axis=0, keepdims=True), gn_ref.shape)
            if gn_prev is not None:
                total = total + refs[7][...]
            gn_ref[...] = total

    tile = pl.BlockSpec((ts, D_MODEL), lambda i: (i + first_tile, 0))
    in_specs = [pl.BlockSpec((ts, D_IN), lambda i: (i + first_tile, 0)), tile, tile,
                pl.BlockSpec((1, D_MODEL), lambda i: (0, 0)), _whole_vmem(), pl.BlockSpec(memory_space=pl.ANY)]
    operands = [dproj, x, dx2, norm_g, wing, dep]
    aliases = {}
    if gx_prev is not None:
        in_specs += [pl.BlockSpec(memory_space=pl.ANY), _whole_vmem()]
        operands += [gx_prev, gn_prev]
        aliases = {6: 0}
    return pl.pallas_call(
        body, name="in_bwd_%d" % first_tile,
        grid=(n_tiles,),
        out_shape=(jax.ShapeDtypeStruct((s, D_MODEL), F32), jax.ShapeDtypeStruct((SUB, D_MODEL), F32)),
        in_specs=in_specs,
        out_specs=(tile, pl.BlockSpec((SUB, D_MODEL), lambda i: (0, 0))),
        scratch_shapes=[pltpu.VMEM((SUB, D_MODEL), F32)],
        input_output_aliases=aliases,
        compiler_params=_cparams(("arbitrary",)),
    )(*operands)


def _grad_w_in(h, dproj, dep):
    s = h.shape[0]
    tk = min(TK_GW, s)
    nk = s // tk
    half = D_MODEL // 2

    def body(h_ref, dp_ref, dep_ref, o_ref, ob_ref):
        @pl.when(pl.program_id(1) == 0)
        def _():
            o_ref[...] = jnp.zeros_like(o_ref)

        o_ref[0] += _dot_tn(h_ref[...], dp_ref[...]).reshape(2, half, W_IN_COLS)

        @pl.when(pl.program_id(1) == nk - 1)
        def _():
            ob_ref[...] = o_ref[...].astype(BF16)

    shard = pl.BlockSpec((1, 2, half, W_IN_COLS), lambda k, t: (k, 0, 0, 0))
    return pl.pallas_call(
        body, name="grad_w_in",
        grid=(N_SHARD, nk),
        out_shape=(jax.ShapeDtypeStruct((N_SHARD, 2, half, W_IN_COLS), F32),
                   jax.ShapeDtypeStruct((N_SHARD, 2, half, W_IN_COLS), BF16)),
        in_specs=[pl.BlockSpec((tk, D_MODEL), lambda k, t: (t, 0)),
                  pl.BlockSpec((tk, W_IN_COLS), lambda k, t: (t, k)), _whole_vmem()],
        out_specs=(shard, shard),
        compiler_params=_cparams(("parallel", "arbitrary")),
    )(h, dproj, dep)


HBM_SPEC = pl.BlockSpec(memory_space=pltpu.HBM)
SEM_SPEC = pl.BlockSpec(memory_space=pltpu.SEMAPHORE)
SIDE_EFFECT = pltpu.SideEffectType.DATAFLOW_SIDE_EFFECTING


def _remote_copies(plan, bufs, send_sems, recv_sems):
    x, y, c = _mesh_pos()
    return [pltpu.make_async_remote_copy(src_ref=src, dst_ref=dst, send_sem=send_sems.at[k],
                                         recv_sem=recv_sems.at[k], device_id=dev, device_id_type=MESH)
            for k, (src, dst, dev) in enumerate(plan(x, y, c, *bufs))]


def _start_copies(name, bufs, n_copies, plan):
    n = len(bufs)

    def body(*refs):
        for cp in _remote_copies(plan, refs[:n], refs[n], refs[n + 1]):
            cp.start()
        refs[-1][...] = jnp.zeros_like(refs[-1])

    outs = pl.pallas_call(
        body, name=name,
        out_shape=(pltpu.SemaphoreType.DMA((n_copies,)), pltpu.SemaphoreType.DMA((n_copies,)),
                   *[pltpu.HBM(b.shape, b.dtype) for b in bufs], jax.ShapeDtypeStruct((SUB, LANE), F32)),
        in_specs=[HBM_SPEC] * n,
        out_specs=(SEM_SPEC, SEM_SPEC, *[HBM_SPEC] * n, _whole_vmem()),
        input_output_aliases={i: 2 + i for i in range(n)},
        compiler_params=pltpu.CompilerParams(has_side_effects=SIDE_EFFECT),
    )(*[pltpu.with_memory_space_constraint(b, pltpu.HBM) for b in bufs])
    return outs[0], outs[1], list(outs[2:2 + n]), outs[-1]


def _wait_copies(name, send_sems, recv_sems, bufs, plan, after):
    n = len(bufs)

    def body(*refs):
        for cp in _remote_copies(plan, refs[:n], refs[n], refs[n + 1]):
            cp.wait_send()
            cp.wait_recv()

    outs = pl.pallas_call(
        body, name=name,
        out_shape=tuple(pltpu.HBM(b.shape, b.dtype) for b in bufs),
        in_specs=[HBM_SPEC] * n + [SEM_SPEC, SEM_SPEC, pl.BlockSpec(memory_space=pl.ANY)],
        out_specs=(HBM_SPEC,) * n,
        input_output_aliases={i: i for i in range(n)},
        compiler_params=pltpu.CompilerParams(has_side_effects=SIDE_EFFECT),
    )(*bufs, send_sems, recv_sems, after)
    return list(outs)


def _landing(shape, dtype):
    return lax.empty(shape, dtype)


def _plan_pair_exchange(x, y, c, g, r):
    return [(g.at[k, 1 - c], r.at[k], (x, y, 1 - c)) for k in range(N_SHARD)]


def _plan_chip_exchange(x, y, c, a, r):
    chips = [(1 - x, y), (x, 1 - y), (1 - x, 1 - y)]
    return [(a.at[2 * cx + cy], r.at[j], (cx, cy, c)) for j, (cx, cy) in enumerate(chips)]


def _plan_pair_gather(x, y, c, f):
    return [(f.at[c], f.at[c], (x, y, 1 - c))]


def _plan_all_gather(x, y, c, own, land):
    me = 4 * x + 2 * y + c
    flip = lambda v, bit: 1 - v if bit else v
    return [(own, land.at[me], (flip(x, m >> 2 & 1), flip(y, m >> 1 & 1), flip(c, m & 1))) for m in range(1, 8)]


def _add_pair(g, r, pos, name):
    _, _, rows, cols = g.shape
    tr = min(256, rows)

    def body(pos_ref, g_ref, r_ref, o_ref, ob_ref):
        v = g_ref[0] + r_ref[...].astype(F32)
        ob_ref[...] = v.astype(BF16)

        @pl.when(pl.program_id(1) == pos_ref[0])
        def _():
            o_ref[...] = v[0]

    blk = pl.BlockSpec((1, tr, cols), lambda t, k, pos_ref: (k, t, 0))
    return pl.pallas_call(
        body, name=name,
        grid_spec=pltpu.PrefetchScalarGridSpec(
            num_scalar_prefetch=1, grid=(rows // tr, N_SHARD),
            in_specs=[pl.BlockSpec((1, 1, tr, cols), lambda t, k, pos_ref: (k, pos_ref[1], t, 0)), blk],
            out_specs=(pl.BlockSpec((tr, cols), lambda t, k, pos_ref: (t, 0)), blk)),
        out_shape=(jax.ShapeDtypeStruct((rows, cols), F32), jax.ShapeDtypeStruct((N_SHARD, rows, cols), BF16)),
        compiler_params=_cparams(("parallel", "arbitrary")),
    )(pos, g, r)


def _add_chips(a, r, pos, name):
    rows, cols = a.shape
    tr = min(256, rows)

    def body(pos_ref, a_ref, r_ref, o_ref):
        o_ref[0] = ((a_ref[...] + r_ref[0].astype(F32)) + r_ref[1].astype(F32)) + r_ref[2].astype(F32)

    return pl.pallas_call(
        body, name=name,
        grid_spec=pltpu.PrefetchScalarGridSpec(
            num_scalar_prefetch=1, grid=(rows // tr,),
            in_specs=[pl.BlockSpec((tr, cols), lambda t, pos_ref: (t, 0)),
                      pl.BlockSpec((3, tr, cols), lambda t, pos_ref: (0, t, 0))],
            out_specs=pl.BlockSpec((1, tr, cols), lambda t, pos_ref: (pos_ref[1], t, 0))),
        out_shape=jax.ShapeDtypeStruct((2, rows, cols), F32),
        compiler_params=_cparams(("parallel",)),
    )(pos, a, r)


def _sum_slots(pos_ref, own, land_ref, rows):
    me = pos_ref[2]
    total = None
    for d in range(8):
        term = jnp.where(me == d, own, land_ref[d] if rows is None else land_ref[d, rows, :])
        total = term if total is None else total + term
    return total


def _sum_small(pos, small, small_land, gws, gws_land):
    def body(pos_ref, sm_ref, sml_ref, gw_ref, gwl_ref, o_sm, o_gw):
        o_sm[...] = _sum_slots(pos_ref, sm_ref[...], sml_ref, None)
        o_gw[...] = _sum_slots(pos_ref, gw_ref[...], gwl_ref, None)

    return pl.pallas_call(
        body, name="sum_small",
        grid_spec=pltpu.PrefetchScalarGridSpec(
            num_scalar_prefetch=1, grid=(1,),
            in_specs=[_whole_vmem()] * 4, out_specs=[_whole_vmem()] * 2),
        out_shape=[jax.ShapeDtypeStruct(small.shape, F32), jax.ShapeDtypeStruct(gws.shape, F32)],
        compiler_params=_cparams(("arbitrary",)),
    )(pos, small, small_land, gws, gws_land)


def _adamw_math(w, g, m, v):
    m = ADAM_B1 * m + (1.0 - ADAM_B1) * g
    v = ADAM_B2 * v + (1.0 - ADAM_B2) * (g * g)
    m_hat = m / (1.0 - ADAM_B1 ** ADAM_STEP)
    v_hat = v / (1.0 - ADAM_B2 ** ADAM_STEP)
    delta = -ADAM_LR * (m_hat / (jnp.sqrt(v_hat) + ADAM_EPS) + ADAM_WD * w)
    return delta, m, v


def _adamw_large(w, g, m, v, dep, name):
    rows, cols = w.shape
    tr = min(256, rows)

    def body(w_ref, g_ref, m_ref, v_ref, dep_ref, d_ref, nm_ref, nv_ref):
        d_ref[...], nm_ref[...], nv_ref[...] = _adamw_math(w_ref[...], g_ref[...], m_ref[...], v_ref[...])

    tile = pl.BlockSpec((tr, cols), lambda t: (t, 0))
    return pl.pallas_call(
        body, name=name,
        grid=(rows // tr,),
        out_shape=(jax.ShapeDtypeStruct(w.shape, F32),) * 3,
        in_specs=[tile] * 4 + [_whole_vmem()], out_specs=(tile,) * 3,
        compiler_params=_cparams(("parallel",)),
    )(w, g, m, v, dep)


_ROW_OF = {"conv_b": 1, "conv_ln_g": 2, "conv_ln_b": 3, "sgu_ln_g": 4, "sgu_ln_b": 5, "b_s": 6, "final_g": 7}
_CONV_W_ROW = 8


def _adamw_small(call_name, names, grads, pos, params):
    def body(pos_ref, p_ref, q_ref, *refs):
        gn_ref, gnl_ref = p_ref, q_ref
        n_in = 3 * len(names)
        ins, outs = refs[:n_in], refs[n_in:]
        me = pos_ref[0]
        for a, name in enumerate(names):
            w_ref, m_ref, v_ref = ins[3 * a:3 * a + 3]
            if name == "conv_w":
                g = jnp.zeros((KW, CONV_W_COLS), F32)
                for k in range(N_SHARD):
                    blk = p_ref[_CONV_W_ROW:_CONV_W_ROW + KW, k * CONV_W_COLS:(k + 1) * CONV_W_COLS]
                    g = jnp.where(me == k, blk, g)
            elif name == "w_s":
                g = q_ref[...]
            elif name == "norm_g":
                g = _sum_slots(pos_ref, gn_ref[0:1, :], gnl_ref, slice(0, 1))
            else:
                g = p_ref[_ROW_OF[name]:_ROW_OF[name] + 1, :]
            delta, nm, nv = _adamw_math(w_ref[...], g, m_ref[...], v_ref[...])
            for o_ref, val in zip(outs[4 * a:4 * a + 4], (g, delta, nm, nv)):
                o_ref[...] = val

    operands, shapes = [], []
    for name in names:
        operands += list(params[name])
        shapes += [jax.ShapeDtypeStruct(params[name][0].shape, F32)] * 4
    outs = pl.pallas_call(
        body, name=call_name,
        grid_spec=pltpu.PrefetchScalarGridSpec(
            num_scalar_prefetch=1, grid=(1,),
            in_specs=[_whole_vmem()] * (2 + len(operands)),
            out_specs=[_whole_vmem()] * len(shapes)),
        out_shape=shapes,
        compiler_params=_cparams(("arbitrary",)),
    )(pos, *grads, *operands)
    return {name: tuple(outs[4 * a:4 * a + 4]) for a, name in enumerate(names)}


def kernel(x, norm_g, w_in, conv_w, conv_b, conv_ln_g, conv_ln_b, sgu_ln_g, sgu_ln_b, w_s, b_s, w_out, final_g, loss_target, m_norm_g, m_w_in, m_conv_w, m_conv_b, m_conv_ln_g, m_conv_ln_b, m_sgu_ln_g, m_sgu_ln_b, m_w_s, m_b_s, m_w_out, m_final_g, v_norm_g, v_w_in, v_conv_w, v_conv_b, v_conv_ln_g, v_conv_ln_b, v_sgu_ln_g, v_sgu_ln_b, v_w_s, v_b_s, v_w_out, v_final_g):
    xi, yi, ci = _mesh_pos()
    pos = jnp.stack([2 * xi + yi, ci, 4 * xi + 2 * yi + ci]).astype(jnp.int32)

    x2d = x[0]
    tgt = loss_target[0]
    fg = final_g.reshape(1, D_MODEL)
    ws3 = w_s[0]
    ws_t = jnp.swapaxes(ws3, 1, 2)
    bs_t = jnp.transpose(b_s[0])

    chip = 2 * xi + yi
    order = jnp.stack([chip, 2 * (1 - xi) + yi, 2 * xi + 1 - yi, 2 * (1 - xi) + 1 - yi]).astype(jnp.int32)
    h, proj, wing, woutg, cwg = _rms_proj_gather(x2d, norm_g, w_in[0], w_out[0], conv_w[0], order)
    y, cv = _branch_fwd(proj, cwg, conv_b, conv_ln_g, conv_ln_b, sgu_ln_g, sgu_ln_b, ws3, bs_t)
    dx2, dy, gwout, gf8, se8 = _out_proj(y, x2d, tgt, woutg, fg)
    in_rows, out_rows = D_MODEL // 2, W_OUT_ROWS // 2
    gwout4 = gwout.reshape(N_SHARD, 2, out_rows, D_MODEL)
    ss, rs, (gwout4, r1_out), tok = _start_copies(
        "start_pair_exchange_w_out", [gwout4, _landing((N_SHARD, out_rows, D_MODEL), F32)], N_SHARD,
        _plan_pair_exchange)
    dproj, small, gws3 = _branch_bwd(proj, dy, cv, cwg, conv_ln_g, conv_ln_b, sgu_ln_g, sgu_ln_b, ws3, ws_t, bs_t,
                                     gf8, se8, tok)
    gws = gws3.reshape(HEADS * CHUNK, CHUNK)
    gwout4, r1_out = _wait_copies("wait_pair_exchange_w_out", ss, rs, [gwout4, r1_out], _plan_pair_exchange, dproj)
    a_out, a_out_bf = _add_pair(gwout4, r1_out, pos, "add_pair_w_out")

    def plan_b(x, y, c, a, r, sm, sml, gw, gwl):
        return (_plan_chip_exchange(x, y, c, a, r) + _plan_all_gather(x, y, c, sm, sml)
                + _plan_all_gather(x, y, c, gw, gwl))

    ss, rs, bufs_b, tok = _start_copies(
        "start_chip_exchange_w_out",
        [a_out_bf, _landing((3, out_rows, D_MODEL), BF16), small, _landing((8,) + small.shape, F32),
         gws, _landing((8,) + gws.shape, F32)], 3 + 7 + 7, plan_b)
    gwin, gwin_bf = _grad_w_in(h, dproj, tok)
    ss_c, rs_c, (gwin_bf, r1_in), tok = _start_copies(
        "start_pair_exchange_w_in", [gwin_bf, _landing((N_SHARD, in_rows, W_IN_COLS), BF16)], N_SHARD,
        _plan_pair_exchange)
    nt = x2d.shape[0] // min(TS_INB, x2d.shape[0])
    cut = nt // 2
    gx_a, gn_a = _in_bwd(dproj, x2d, dx2, norm_g, wing, tok, 0, cut)

    gwin_bf, r1_in = _wait_copies("wait_pair_exchange_w_in", ss_c, rs_c, [gwin_bf, r1_in], _plan_pair_exchange, gx_a)
    a_in, a_in_bf = _add_pair(gwin, r1_in, pos, "add_pair_w_in")
    a_out_bf, r2_out, small, small_land, gws, gws_land = _wait_copies(
        "wait_chip_exchange_w_out", ss, rs, bufs_b, plan_b, gx_a)
    f_out = _add_chips(a_out, r2_out, pos, "add_chips_w_out")
    p, q = _sum_small(pos, small, small_land, gws, gws_land)
    loss = (0.5 / D_MODEL) * jnp.sum(p[0])

    flat = lambda a: a.reshape(1, C_BR)
    flat_ws = lambda a: a.reshape(HEADS * CHUNK, CHUNK)
    params = {
        "norm_g": (norm_g, m_norm_g, v_norm_g),
        "conv_b": (conv_b, m_conv_b, v_conv_b),
        "conv_ln_g": (conv_ln_g, m_conv_ln_g, v_conv_ln_g),
        "conv_ln_b": (conv_ln_b, m_conv_ln_b, v_conv_ln_b),
        "sgu_ln_g": (sgu_ln_g, m_sgu_ln_g, v_sgu_ln_g),
        "sgu_ln_b": (sgu_ln_b, m_sgu_ln_b, v_sgu_ln_b),
        "b_s": (flat(b_s), flat(m_b_s), flat(v_b_s)),
        "final_g": (flat(final_g), flat(m_final_g), flat(v_final_g)),
        "conv_w": (conv_w[0], m_conv_w[0], v_conv_w[0]),
        "w_s": (flat_ws(w_s), flat_ws(m_w_s), flat_ws(v_w_s)),
    }
    res = _adamw_small("adamw_small", [n for n in params if n != "norm_g"], (p, q), pos, params)

    def plan_d(x, y, c, a, r, f):
        return _plan_chip_exchange(x, y, c, a, r) + _plan_pair_gather(x, y, c, f)

    ss, rs, bufs_d, tok = _start_copies(
        "start_chip_exchange_w_in", [a_in_bf, _landing((3, in_rows, W_IN_COLS), BF16), f_out], 3 + 1, plan_d)
    grad_x, gn8 = _in_bwd(dproj, x2d, dx2, norm_g, wing, tok, cut, nt - cut, gx_a, gn_a)
    a_in_bf, r2_in, f_out = _wait_copies("wait_chip_exchange_w_in", ss, rs, bufs_d, plan_d, grad_x)
    f_in = _add_chips(a_in, r2_in, pos, "add_chips_w_in")

    def plan_e(x, y, c, f, gn, gnl):
        return _plan_pair_gather(x, y, c, f) + _plan_all_gather(x, y, c, gn, gnl)

    ss, rs, bufs_e, tok = _start_copies(
        "start_pair_gather_w_in", [f_in, gn8, _landing((8,) + gn8.shape, F32)], 1 + 7, plan_e)
    g_w_out = f_out.reshape(W_OUT_ROWS, D_MODEL)
    d_w_out, nm_w_out, nv_w_out = _adamw_large(w_out[0], g_w_out, m_w_out[0], v_w_out[0], tok, "adamw_w_out")
    f_in, gn8, gn_land = _wait_copies("wait_pair_gather_w_in", ss, rs, bufs_e, plan_e, d_w_out)
    g_w_in = f_in.reshape(D_MODEL, W_IN_COLS)
    d_w_in, nm_w_in, nv_w_in = _adamw_large(w_in[0], g_w_in, m_w_in[0], v_w_in[0], tok, "adamw_w_in")
    res.update(_adamw_small("adamw_norm_g", ["norm_g"], (gn8, gn_land), pos, params))
    res["w_in"] = tuple(a[None] for a in (g_w_in, d_w_in, nm_w_in, nv_w_in))
    res["w_out"] = tuple(a[None] for a in (g_w_out, d_w_out, nm_w_out, nv_w_out))
    res["conv_w"] = tuple(a[None] for a in res["conv_w"])
    res["w_s"] = tuple(a.reshape(w_s.shape) for a in res["w_s"])
    res["b_s"] = tuple(a.reshape(b_s.shape) for a in res["b_s"])
    res["final_g"] = tuple(a.reshape(final_g.shape) for a in res["final_g"])

    order = ("norm_g", "w_in", "conv_w", "conv_b", "conv_ln_g", "conv_ln_b", "sgu_ln_g", "sgu_ln_b",
             "w_s", "b_s", "w_out", "final_g")
    out = [loss, grad_x[None]]
    for part in range(4):
        out += [res[name][part] for name in order]
    return tuple(out)
```

```python
import jax
import jax.numpy as jnp
from jax import lax
from jax.experimental import pallas as pl
from jax.experimental.pallas import tpu as pltpu

F32 = jnp.float32
BF16 = jnp.bfloat16
MESH = pl.DeviceIdType.MESH

EPS = 1e-6
D_MODEL = 1024
C_BR = 1024
D_IN = 6 * C_BR
N_SHARD = 4
W_IN_COLS = D_IN // N_SHARD
W_OUT_ROWS = 2 * C_BR // N_SHARD
CONV_W_COLS = C_BR // N_SHARD
KW = 31
HALO = 16
HEADS = 8
HEAD_DIM = 128
CHUNK = 128
LANE = 128
SUB = 8

ADAM_LR = 0.001
ADAM_B1 = 0.9
ADAM_B2 = 0.999
ADAM_EPS = 1e-08
ADAM_WD = 0.01
ADAM_STEP = 10

TS_PROJ = 512
TS_FWD = 256
TS_OUT = 512
TS_BWD = 256
TS_INB = 512
TK_GW = 2048
ROWS_A = 128
ROWS_B = 64
VMEM_LIMIT = 56 * 1024 * 1024


def _cparams(sem=None, vmem=VMEM_LIMIT):
    kw = dict(vmem_limit_bytes=vmem)
    if sem is not None:
        kw["dimension_semantics"] = sem
    return pltpu.CompilerParams(**kw)


def _whole_vmem():
    return pl.BlockSpec(memory_space=pltpu.VMEM)


def _sigmoid(v):
    return 0.5 * jnp.tanh(0.5 * v) + 0.5


def _silu(v):
    h = 0.5 * v
    return h + h * jnp.tanh(h)


def _fold8(v):
    n, c = v.shape
    return v.reshape(n // SUB, SUB, c).sum(axis=0)


def _dot_nt(a, b):
    return lax.dot_general(a, b, (((1,), (1,)), ((), ())), preferred_element_type=F32)


def _dot_tn(a, b):
    return lax.dot_general(a, b, (((0,), (0,)), ((), ())), preferred_element_type=F32)


def _mesh_pos():
    return lax.axis_index("x"), lax.axis_index("y"), lax.axis_index("c")


def _rms_proj_gather(x, norm_g, w_in, w_out, conv_w, order):
    s = x.shape[0]
    ts = min(TS_PROJ, s)
    nt = s // ts
    hin = w_in.shape[0] // 2
    hout = w_out.shape[0] // 2

    def body(order_ref, x_ref, g_ref, win_ref, wout_ref, cw_ref,
             h_ref, proj_ref, wing_ref, woutg_ref, cwg_ref, wg_ref, wob_ref, hs_ref, send_sems, recv_sems, local_sems):
        p = pl.program_id(0)
        t = pl.program_id(1)
        mx, my, c = _mesh_pos()
        me = 2 * mx + my
        chips = [(1 - mx, my), (mx, 1 - my), (1 - mx, 1 - my)]
        sibling = (mx, my, 1 - c)

        def remote(src, dst, sem, dev):
            return pltpu.make_async_remote_copy(
                src_ref=src, dst_ref=dst, send_sem=send_sems.at[sem], recv_sem=recv_sems.at[sem],
                device_id=dev, device_id_type=MESH)

        def w_in_part(blk, half):
            return wg_ref.at[blk, pl.ds(half * hin, hin)]

        def keep(blk, k):
            return pltpu.make_async_copy(wg_ref.at[blk], wing_ref.at[blk], local_sems.at[2 + k])

        def w_out_part(blk, half):
            return woutg_ref.at[blk, pl.ds(half * hout, hout)]

        def sends():
            out = []
            for j, (cx, cy) in enumerate(chips):
                blk = 2 * cx + cy
                out.append(remote(w_in_part(me, c), w_in_part(me, c), j, (cx, cy, c)))
                out.append(remote(w_in_part(blk, c), w_in_part(blk, c), 3 + j, sibling))
                out.append(remote(wob_ref.at[pl.ds(c * hout, hout)], w_out_part(me, c), 6 + j, (cx, cy, c)))
                out.append(remote(w_out_part(blk, c), w_out_part(blk, c), 9 + j, sibling))
                out.append(remote(cw_ref, cwg_ref.at[me], 12 + j, (cx, cy, c)))
            return out

        @pl.when(jnp.logical_and(p == 0, t == 0))
        def _():
            wg_ref[me] = win_ref[...].astype(BF16)
            wob_ref[...] = wout_ref[...].astype(BF16)
            keep(me, 0).start()
            mine = [pltpu.make_async_copy(wob_ref, woutg_ref.at[me], local_sems.at[0]),
                    pltpu.make_async_copy(cw_ref, cwg_ref.at[me], local_sems.at[1])]
            for cp in mine:
                cp.start()
            for k, cp in enumerate(sends()):
                if k % 5 == 4 or (k % 5 == 0 and k // 5 < 2):
                    cp.start()
            for cp in mine:
                cp.wait()

        for j, (cx, cy) in enumerate(chips):
            blk = 2 * cx + cy

            @pl.when(jnp.logical_and(p == j + 1, t == 0))
            def _():
                remote(w_in_part(blk, c), w_in_part(blk, c), j, (cx, cy, c)).wait_recv()
                remote(w_in_part(blk, c), w_in_part(blk, c), 3 + j, sibling).start()
                remote(w_in_part(blk, 1 - c), w_in_part(blk, 1 - c), 3 + j, sibling).wait_recv()
                keep(blk, j + 1).start()
                if j == 0:
                    sends()[5 * 2].start()
                if j == 1:
                    for jj in range(3):
                        sends()[5 * jj + 2].start()

        rows = pl.ds(pl.multiple_of(t * ts, ts), ts)

        @pl.when(p == 0)
        def _():
            xt = x_ref[...]
            r = lax.rsqrt(jnp.mean(xt * xt, axis=-1, keepdims=True) + EPS)
            hb = (xt * r * g_ref[...]).astype(BF16)
            h_ref[...] = hb
            hs_ref[rows, :] = hb

        proj_ref[...] = jnp.dot(hs_ref[rows, :], wg_ref[order_ref[p]], preferred_element_type=F32).astype(BF16)

        @pl.when(jnp.logical_and(p == N_SHARD - 1, t == nt - 1))
        def _():
            for j, (cx, cy) in enumerate(chips):
                blk = 2 * cx + cy
                remote(w_out_part(blk, c), w_out_part(blk, c), 6 + j, (cx, cy, c)).wait_recv()
                remote(w_out_part(blk, c), w_out_part(blk, c), 9 + j, sibling).start()
            for j, (cx, cy) in enumerate(chips):
                blk = 2 * cx + cy
                remote(w_out_part(blk, 1 - c), w_out_part(blk, 1 - c), 9 + j, sibling).wait_recv()
                remote(cw_ref, cwg_ref.at[blk], 12 + j, (cx, cy, c)).wait_recv()
            for cp in sends():
                cp.wait_send()
            keep(me, 0).wait()
            for j, (cx, cy) in enumerate(chips):
                keep(2 * cx + cy, j + 1).wait()

    hbm = pl.BlockSpec(memory_space=pl.ANY)
    return pl.pallas_call(
        body, name="rms_proj_gather",
        grid_spec=pltpu.PrefetchScalarGridSpec(
            num_scalar_prefetch=1, grid=(N_SHARD, nt),
            in_specs=[pl.BlockSpec((ts, D_MODEL), lambda p, t, o: (jnp.where(p == 0, t, nt - 1), 0)),
                      pl.BlockSpec((1, D_MODEL), lambda p, t, o: (0, 0)),
                      _whole_vmem(), _whole_vmem(), _whole_vmem()],
            out_specs=(pl.BlockSpec((ts, D_MODEL), lambda p, t, o: (jnp.where(p == 0, t, nt - 1), 0)),
                       pl.BlockSpec((ts, W_IN_COLS), lambda p, t, o: (t, o[p])),
                       hbm, hbm, hbm),
            scratch_shapes=[pltpu.VMEM((N_SHARD,) + w_in.shape, BF16), pltpu.VMEM(w_out.shape, BF16),
                            pltpu.VMEM((s, D_MODEL), BF16),
                            pltpu.SemaphoreType.DMA((15,)), pltpu.SemaphoreType.DMA((15,)),
                            pltpu.SemaphoreType.DMA((6,))]),
        out_shape=(jax.ShapeDtypeStruct((s, D_MODEL), BF16), jax.ShapeDtypeStruct((s, D_IN), BF16),
                   jax.ShapeDtypeStruct((N_SHARD,) + w_in.shape, BF16),
                   jax.ShapeDtypeStruct((N_SHARD,) + w_out.shape, BF16),
                   jax.ShapeDtypeStruct((N_SHARD,) + conv_w.shape, F32)),
        compiler_params=_cparams(("arbitrary", "arbitrary")),
    )(order, x, norm_g, w_in, w_out, conv_w)


def _halo_specs(ts, s, width, col_block):
    per = ts // HALO
    last = s // HALO - 1
    prev = pl.BlockSpec((HALO, width), lambda i: (jnp.maximum(i * per - 1, 0), col_block))
    nxt = pl.BlockSpec((HALO, width), lambda i: (jnp.minimum((i + 1) * per, last), col_block))
    return prev, nxt


def _layer_norm_stats(v):
    mu = jnp.mean(v, axis=-1, keepdims=True)
    vc = v - mu
    var = jnp.mean(vc * vc, axis=-1, keepdims=True)
    rstd = lax.rsqrt(var + EPS)
    return vc * rstd, rstd


def _fill_shifted(sh_ref, ext):
    n = ext.shape[0]
    sh_ref[0] = ext
    for r in range(1, SUB):
        sh_ref[r] = pltpu.roll(ext, n - r, axis=0)


def _branch_fwd(proj, cwg, conv_b, conv_ln_g, conv_ln_b, sgu_ln_g, sgu_ln_b, w_s, bs_t):
    s = proj.shape[0]
    ts = min(TS_FWD, s)
    nt = s // ts
    ra = min(ROWS_A, ts)

    def body(pm_ref, pp_ref, pn_ref, cw_ref, cb_ref, clg_ref, clb_ref, slg_ref, slb_ref, ws_ref, bst_ref,
             y_new, cv_ref, sh_ref):
        i = pl.program_id(0)
        keep_prev = (i > 0).astype(F32)
        keep_next = (i < nt - 1).astype(F32)

        for lb in range(C_BR // LANE):
            lanes = slice(lb * LANE, (lb + 1) * LANE)
            gates = slice(C_BR + lb * LANE, C_BR + (lb + 1) * LANE)

            def glu(ref):
                return ref[:, lanes].astype(F32) * _sigmoid(ref[:, gates].astype(F32))

            ext = jnp.concatenate([glu(pp_ref) * keep_prev, glu(pm_ref), glu(pn_ref) * keep_next], axis=0)
            _fill_shifted(sh_ref, ext)
            shard, off = divmod(lb * LANE, CONV_W_COLS)
            bias = cb_ref[:, lanes]

            def chunk(jc, carry):
                base = pl.multiple_of(jc * ra, ra)
                acc = jnp.zeros((ra, LANE), F32) + bias
                for k in range(KW):
                    o = k + 1
                    acc = acc + sh_ref[o % SUB, pl.ds(base + SUB * (o // SUB), ra), :] * cw_ref[
                        shard, k:k + 1, off:off + LANE]
                cv_ref[pl.ds(base, ra), lanes] = acc
                return carry

            lax.fori_loop(0, ts // ra, chunk, 0)

        lnh, _ = _layer_norm_stats(cv_ref[...])
        ln = lnh * clg_ref[...] + clb_ref[...]
        gc = pm_ref[:, 2 * C_BR:3 * C_BR].astype(F32)
        y_new[:, :C_BR] = (_silu(ln) * _silu(gc)).astype(BF16)

        vh, _ = _layer_norm_stats(pm_ref[:, 4 * C_BR:5 * C_BR].astype(F32))
        vn = (vh * slg_ref[...] + slb_ref[...]).astype(BF16)
        for hd in range(HEADS):
            w_h = ws_ref[hd].astype(BF16)
            b_h = bst_ref[:, hd:hd + 1]
            cols = slice(hd * HEAD_DIM, (hd + 1) * HEAD_DIM)
            for ch in range(ts // CHUNK):
                rows = slice(ch * CHUNK, (ch + 1) * CHUNK)
                mixed = jnp.dot(w_h, vn[rows, cols], preferred_element_type=F32) + b_h
                u = pm_ref[rows, 3 * C_BR + hd * HEAD_DIM:3 * C_BR + (hd + 1) * HEAD_DIM].astype(F32)
                gs = pm_ref[rows, 5 * C_BR + hd * HEAD_DIM:5 * C_BR + (hd + 1) * HEAD_DIM].astype(F32)
                y_new[rows, C_BR + hd * HEAD_DIM:C_BR + (hd + 1) * HEAD_DIM] = (
                    u * mixed * _silu(gs)).astype(BF16)

    prev, nxt = _halo_specs(ts, s, 2 * C_BR, 0)
    row = pl.BlockSpec((1, C_BR), lambda i: (0, 0))
    return pl.pallas_call(
        body, name="branch_fwd",
        grid=(nt,),
        out_shape=(jax.ShapeDtypeStruct((s, 2 * C_BR), BF16), jax.ShapeDtypeStruct((s, C_BR), F32)),
        in_specs=[pl.BlockSpec((ts, D_IN), lambda i: (i, 0)), prev, nxt,
                  _whole_vmem(), row, row, row, row, row, _whole_vmem(), _whole_vmem()],
        out_specs=(pl.BlockSpec((ts, 2 * C_BR), lambda i: (i, 0)),
                   pl.BlockSpec((ts, C_BR), lambda i: (i, 0))),
        scratch_shapes=[pltpu.VMEM((SUB, ts + 2 * HALO, LANE), F32)],
        compiler_params=_cparams(("parallel",)),
    )(proj, proj, proj, cwg, conv_b, conv_ln_g, conv_ln_b, sgu_ln_g, sgu_ln_b, w_s, bs_t)


def _out_proj(y, x, target, woutg, final_g):
    s = x.shape[0]
    ts = min(TS_OUT, s)
    nt = s // ts

    def body(y_ref, x_ref, t_ref, w_ref, g_ref, dx2_ref, dy_ref, gw_ref, gf_ref, se_ref):
        i = pl.program_id(0)

        @pl.when(i == 0)
        def _():
            gw_ref[...] = jnp.zeros_like(gw_ref)
            gf_ref[...] = jnp.zeros_like(gf_ref)
            se_ref[...] = jnp.zeros_like(se_ref)

        yt = y_ref[...]
        x2 = x_ref[...]
        for k in range(N_SHARD):
            x2 = x2 + jnp.dot(yt[:, k * W_OUT_ROWS:(k + 1) * W_OUT_ROWS], w_ref[k], preferred_element_type=F32)
        r2 = lax.rsqrt(jnp.mean(x2 * x2, axis=-1, keepdims=True) + EPS)
        n2 = x2 * r2
        g = g_ref[...]
        diff = n2 * g - t_ref[...]
        se_ref[...] += _fold8(diff * diff)
        dout = diff * (1.0 / D_MODEL)
        gf_ref[...] += _fold8(dout * n2)
        dn = dout * g
        dx2 = r2 * (dn - n2 * jnp.mean(dn * n2, axis=-1, keepdims=True))
        dx2_ref[...] = dx2
        dxb = dx2.astype(BF16)
        for k in range(N_SHARD):
            rows = slice(k * W_OUT_ROWS, (k + 1) * W_OUT_ROWS)
            dy_ref[:, rows] = _dot_nt(dxb, w_ref[k]).astype(BF16)
            gw_ref[rows, :] += _dot_tn(yt[:, rows], dxb)

        @pl.when(i == nt - 1)
        def _():
            gf_ref[...] = jnp.broadcast_to(jnp.sum(gf_ref[...], axis=0, keepdims=True), gf_ref.shape)

    tile = pl.BlockSpec((ts, D_MODEL), lambda i: (i, 0))
    wide = pl.BlockSpec((ts, 2 * C_BR), lambda i: (i, 0))
    acc8 = pl.BlockSpec((SUB, D_MODEL), lambda i: (0, 0))
    return pl.pallas_call(
        body, name="out_proj",
        grid=(nt,),
        out_shape=(jax.ShapeDtypeStruct((s, D_MODEL), F32), jax.ShapeDtypeStruct((s, 2 * C_BR), BF16),
                   jax.ShapeDtypeStruct((2 * C_BR, D_MODEL), F32),
                   jax.ShapeDtypeStruct((SUB, D_MODEL), F32), jax.ShapeDtypeStruct((SUB, D_MODEL), F32)),
        in_specs=[wide, tile, tile, _whole_vmem(), pl.BlockSpec((1, D_MODEL), lambda i: (0, 0))],
        out_specs=(tile, wide, pl.BlockSpec((2 * C_BR, D_MODEL), lambda i: (0, 0)), acc8, acc8),
        compiler_params=_cparams(("arbitrary",)),
    )(y, x, target, woutg, final_g)


def _dsilu(v, sg):
    return sg * (1.0 + v * (1.0 - sg))


def _branch_bwd(proj, dy, cv, cwg, conv_ln_g, conv_ln_b, sgu_ln_g, sgu_ln_b, w_s, ws_t, bs_t, gf8, se8, dep):
    s = proj.shape[0]
    ts = min(TS_BWD, s)
    nt = s // ts
    ra = min(ROWS_A, ts)
    rb = min(ROWS_B, ts)
    te = ts + 2 * HALO

    def body(pm_ref, dym_ref, cvm_ref, gcp_ref, gcn_ref, dyp_ref, dyn_ref, cvp_ref, cvn_ref,
             cw_ref, clg_ref, clb_ref, slg_ref, slb_ref, ws_ref, wst_ref, bst_ref, gf_ref, se_ref, dep_ref,
             dp_ref, small_ref, gws_ref,
             sh_ref, glu_ref, dgl_ref, dcv_ref, acc_ref, gcw_ref, gbs_ref):
        i = pl.program_id(0)

        @pl.when(i == 0)
        def _():
            acc_ref[...] = jnp.zeros_like(acc_ref)
            gcw_ref[...] = jnp.zeros_like(gcw_ref)
            gbs_ref[...] = jnp.zeros_like(gbs_ref)
            gws_ref[...] = jnp.zeros_like(gws_ref)

        def ext(prev_ref, main, next_ref):
            return jnp.concatenate([prev_ref[...].astype(F32), main, next_ref[...].astype(F32)], axis=0)

        main = slice(HALO, HALO + ts)

        hm = ts // 2
        clg = clg_ref[...]
        for half in range(2):
            rows_m = slice(half * hm, (half + 1) * hm)
            halo_refs = (cvp_ref, gcp_ref, dyp_ref) if half == 0 else (cvn_ref, gcn_ref, dyn_ref)

            def piece(halo_ref, main_val):
                halo = halo_ref[...].astype(F32)
                return jnp.concatenate([halo, main_val] if half == 0 else [main_val, halo], axis=0)

            cv_e = piece(halo_refs[0], cvm_ref[rows_m, :])
            gc_e = piece(halo_refs[1], pm_ref[rows_m, 2 * C_BR:3 * C_BR].astype(F32))
            dyc_e = piece(halo_refs[2], dym_ref[rows_m, :C_BR].astype(F32))
            mn = slice(HALO, HALO + hm) if half == 0 else slice(0, hm)
            lnh, rstd = _layer_norm_stats(cv_e)
            ln = lnh * clg + clb_ref[...]
            sg_ln = _sigmoid(ln)
            sg_gc = _sigmoid(gc_e)
            d_ln = dyc_e * gc_e * sg_gc * _dsilu(ln, sg_ln)
            dp_ref[rows_m, 2 * C_BR:3 * C_BR] = (
                dyc_e[mn] * ln[mn] * sg_ln[mn] * _dsilu(gc_e[mn], sg_gc[mn])).astype(BF16)
            dlnh = d_ln * clg
            d_cv = rstd * (dlnh - jnp.mean(dlnh, axis=-1, keepdims=True)
                           - lnh * jnp.mean(dlnh * lnh, axis=-1, keepdims=True))
            row = lax.broadcasted_iota(jnp.int32, (hm + HALO, 1), 0)
            if half == 0:
                valid = jnp.logical_or(row >= HALO, i > 0)
            else:
                valid = jnp.logical_or(row < hm, i < nt - 1)
            d_cv = jnp.where(valid, d_cv, 0.0)
            dcv_ref[half * (hm + HALO):(half + 1) * (hm + HALO), :] = d_cv
            acc_ref[0:8, :] += _fold8(d_cv[mn])
            acc_ref[8:16, :] += _fold8(d_ln[mn] * lnh[mn])
            acc_ref[16:24, :] += _fold8(d_ln[mn])

        for lb in range(C_BR // LANE):
            lanes = slice(lb * LANE, (lb + 1) * LANE)
            gates = slice(C_BR + lb * LANE, C_BR + (lb + 1) * LANE)
            shard, off = divmod(lb * LANE, CONV_W_COLS)
            av = pm_ref[:, lanes].astype(F32)
            sg = _sigmoid(pm_ref[:, gates].astype(F32))
            glu_ref[...] = av * sg
            _fill_shifted(sh_ref, dcv_ref[:, lanes])

            def chunk_a(jc, carry):
                base = pl.multiple_of(jc * ra, ra)
                acc = jnp.zeros((ra, LANE), F32)
                for j in range(KW):
                    o = j + 1
                    acc = acc + sh_ref[o % SUB, pl.ds(base + SUB * (o // SUB), ra), :] * cw_ref[
                        shard, KW - 1 - j:KW - j, off:off + LANE]
                dgl_ref[pl.ds(base, ra), :] = acc
                return carry

            lax.fori_loop(0, ts // ra, chunk_a, 0)

            def chunk_b(jc, accs):
                base = pl.multiple_of(jc * rb, rb)
                g = glu_ref[pl.ds(base, rb), :]
                out = []
                for j in range(KW):
                    o = j + 1
                    d = sh_ref[o % SUB, pl.ds(base + SUB * (o // SUB), rb), :]
                    out.append(accs[j] + _fold8(g * d))
                return tuple(out)

            accs = lax.fori_loop(0, ts // rb, chunk_b, tuple(jnp.zeros((SUB, LANE), F32) for _ in range(KW)))
            for j in range(KW):
                gcw_ref[j * SUB:(j + 1) * SUB, lanes] += accs[j]

            dglu = dgl_ref[...]
            dp_ref[:, lanes] = (dglu * sg).astype(BF16)
            dp_ref[:, gates] = (dglu * av * sg * (1.0 - sg)).astype(BF16)

        vh, vrstd = _layer_norm_stats(pm_ref[:, 4 * C_BR:5 * C_BR].astype(F32))
        slg = slg_ref[...]
        vn = (vh * slg + slb_ref[...]).astype(BF16)
        for hd in range(HEADS):
            w_h = ws_ref[hd].astype(BF16)
            wt_h = wst_ref[hd].astype(BF16)
            b_h = bst_ref[:, hd:hd + 1]
            cols = slice(hd * HEAD_DIM, (hd + 1) * HEAD_DIM)
            gws_h = jnp.zeros((CHUNK, CHUNK), F32)
            gbs_h = jnp.zeros((CHUNK, HEAD_DIM), F32)
            for ch in range(ts // CHUNK):
                rows = slice(ch * CHUNK, (ch + 1) * CHUNK)
                vn_b = vn[rows, cols]
                mixed = jnp.dot(w_h, vn_b, preferred_element_type=F32) + b_h
                u = pm_ref[rows, 3 * C_BR + hd * HEAD_DIM:3 * C_BR + (hd + 1) * HEAD_DIM].astype(F32)
                gs = pm_ref[rows, 5 * C_BR + hd * HEAD_DIM:5 * C_BR + (hd + 1) * HEAD_DIM].astype(F32)
                dys = dym_ref[rows, C_BR + hd * HEAD_DIM:C_BR + (hd + 1) * HEAD_DIM].astype(F32)
                sg_gs = _sigmoid(gs)
                silu_gs = gs * sg_gs
                dp_ref[rows, 3 * C_BR + hd * HEAD_DIM:3 * C_BR + (hd + 1) * HEAD_DIM] = (
                    dys * mixed * silu_gs).astype(BF16)
                dp_ref[rows, 5 * C_BR + hd * HEAD_DIM:5 * C_BR + (hd + 1) * HEAD_DIM] = (
                    dys * u * mixed * _dsilu(gs, sg_gs)).astype(BF16)
                d_mixed = dys * u * silu_gs
                dm_b = d_mixed.astype(BF16)
                gws_h = gws_h + _dot_nt(dm_b, vn_b)
                gbs_h = gbs_h + d_mixed
                dcv_ref[HALO + ch * CHUNK:HALO + (ch + 1) * CHUNK, cols] = jnp.dot(
                    wt_h, dm_b, preferred_element_type=F32)
            gws_ref[hd] += gws_h
            gbs_ref[:, cols] += gbs_h
        d_vn = dcv_ref[main, :]
        acc_ref[24:32, :] += _fold8(d_vn * vh)
        acc_ref[32:40, :] += _fold8(d_vn)
        dvh = d_vn * slg
        dp_ref[:, 4 * C_BR:5 * C_BR] = (vrstd * (
            dvh - jnp.mean(dvh, axis=-1, keepdims=True)
            - vh * jnp.mean(dvh * vh, axis=-1, keepdims=True))).astype(BF16)

        @pl.when(i == nt - 1)
        def _():
            small_ref[...] = jnp.zeros_like(small_ref)
            for a in range(5):
                small_ref[1 + a:2 + a, :] = jnp.sum(acc_ref[a * SUB:(a + 1) * SUB, :], axis=0, keepdims=True)
            ones = jnp.ones((SUB, HEAD_DIM), F32)
            for hd in range(HEADS):
                cols = slice(hd * HEAD_DIM, (hd + 1) * HEAD_DIM)
                rowsum = lax.dot_general(ones, gbs_ref[:, cols], (((1,), (1,)), ((), ())),
                                         precision=lax.Precision.HIGHEST, preferred_element_type=F32)
                small_ref[6:7, cols] = rowsum[0:1, :]
            small_ref[7:8, :] = gf_ref[0:1, :]
            small_ref[0:1, :] = jnp.sum(se_ref[...], axis=0, keepdims=True)
            for k in range(KW):
                j = KW - 1 - k
                small_ref[8 + k:9 + k, :] = jnp.sum(gcw_ref[j * SUB:(j + 1) * SUB, :], axis=0, keepdims=True)

    gc_prev, gc_next = _halo_specs(ts, s, C_BR, 2)
    lo_prev, lo_next = _halo_specs(ts, s, C_BR, 0)
    row = pl.BlockSpec((1, C_BR), lambda i: (0, 0))
    return pl.pallas_call(
        body, name="branch_bwd",
        grid=(nt,),
        out_shape=(jax.ShapeDtypeStruct((s, D_IN), BF16), jax.ShapeDtypeStruct((40, C_BR), F32),
                   jax.ShapeDtypeStruct((HEADS, CHUNK, CHUNK), F32)),
        in_specs=[pl.BlockSpec((ts, D_IN), lambda i: (i, 0)),
                  pl.BlockSpec((ts, 2 * C_BR), lambda i: (i, 0)),
                  pl.BlockSpec((ts, C_BR), lambda i: (i, 0)),
                  gc_prev, gc_next, lo_prev, lo_next, lo_prev, lo_next,
                  _whole_vmem(), row, row, row, row, _whole_vmem(), _whole_vmem(), _whole_vmem(),
                  _whole_vmem(), _whole_vmem(), _whole_vmem()],
        out_specs=(pl.BlockSpec((ts, D_IN), lambda i: (i, 0)),
                   pl.BlockSpec((40, C_BR), lambda i: (0, 0)),
                   pl.BlockSpec((HEADS, CHUNK, CHUNK), lambda i: (0, 0, 0))),
        scratch_shapes=[pltpu.VMEM((SUB, te, LANE), F32),
                        pltpu.VMEM((ts, LANE), F32),
                        pltpu.VMEM((ts, LANE), F32),
                        pltpu.VMEM((te, C_BR), F32),
                        pltpu.VMEM((5 * SUB, C_BR), F32),
                        pltpu.VMEM((KW * SUB, C_BR), F32),
                        pltpu.VMEM((CHUNK, C_BR), F32)],
        compiler_params=_cparams(("arbitrary",)),
    )(proj, dy, cv, proj, proj, dy, dy, cv, cv,
      cwg, conv_ln_g, conv_ln_b, sgu_ln_g, sgu_ln_b, w_s, ws_t, bs_t, gf8, se8, dep)


def _in_bwd(dproj, x, dx2, norm_g, wing, dep, first_tile, n_tiles, gx_prev=None, gn_prev=None):
    s = x.shape[0]
    ts = min(TS_INB, s)

    def body(*refs):
        dp_ref, x_ref, dx2_ref, g_ref, w_ref = refs[:5]
        gx_ref, gn_ref, acc_ref = refs[-3:]
        i = pl.program_id(0)

        @pl.when(i == 0)
        def _():
            acc_ref[...] = jnp.zeros_like(acc_ref)

        dh = jnp.zeros((ts, D_MODEL), F32)
        for k in range(N_SHARD):
            dh = dh + _dot_nt(dp_ref[:, k * W_IN_COLS:(k + 1) * W_IN_COLS], w_ref[k])
        xt = x_ref[...]
        r = lax.rsqrt(jnp.mean(xt * xt, axis=-1, keepdims=True) + EPS)
        n = xt * r
        acc_ref[...] += _fold8(dh * n)
        dn = dh * g_ref[...]
        gx_ref[...] = dx2_ref[...] + r * (dn - n * jnp.mean(dn * n, axis=-1, keepdims=True))

        @pl.when(i == n_tiles - 1)
        def _():
            total = jnp.broadcast_to(jnp.sum(acc_ref[...], axis=0, keepdims=True), gn_ref.shape)
            if gn_prev is not None:
                total = total + refs[7][...]
            gn_ref[...] = total

    tile = pl.BlockSpec((ts, D_MODEL), lambda i: (i + first_tile, 0))
    in_specs = [pl.BlockSpec((ts, D_IN), lambda i: (i + first_tile, 0)), tile, tile,
                pl.BlockSpec((1, D_MODEL), lambda i: (0, 0)), _whole_vmem(), pl.BlockSpec(memory_space=pl.ANY)]
    operands = [dproj, x, dx2, norm_g, wing, dep]
    aliases = {}
    if gx_prev is not None:
        in_specs += [pl.BlockSpec(memory_space=pl.ANY), _whole_vmem()]
        operands += [gx_prev, gn_prev]
        aliases = {6: 0}
    return pl.pallas_call(
        body, name="in_bwd_%d" % first_tile,
        grid=(n_tiles,),
        out_shape=(jax.ShapeDtypeStruct((s, D_MODEL), F32), jax.ShapeDtypeStruct((SUB, D_MODEL), F32)),
        in_specs=in_specs,
        out_specs=(tile, pl.BlockSpec((SUB, D_MODEL), lambda i: (0, 0))),
        scratch_shapes=[pltpu.VMEM((SUB, D_MODEL), F32)],
        input_output_aliases=aliases,
        compiler_params=_cparams(("arbitrary",)),
    )(*operands)


def _grad_w_in(h, dproj, dep):
    s = h.shape[0]
    tk = min(TK_GW, s)
    nk = s // tk
    half = D_MODEL // 2

    def body(h_ref, dp_ref, dep_ref, o_ref, ob_ref):
        @pl.when(pl.program_id(1) == 0)
        def _():
            o_ref[...] = jnp.zeros_like(o_ref)

        o_ref[0] += _dot_tn(h_ref[...], dp_ref[...]).reshape(2, half, W_IN_COLS)

        @pl.when(pl.program_id(1) == nk - 1)
        def _():
            ob_ref[...] = o_ref[...].astype(BF16)

    shard = pl.BlockSpec((1, 2, half, W_IN_COLS), lambda k, t: (k, 0, 0, 0))
    return pl.pallas_call(
        body, name="grad_w_in",
        grid=(N_SHARD, nk),
        out_shape=(jax.ShapeDtypeStruct((N_SHARD, 2, half, W_IN_COLS), F32),
                   jax.ShapeDtypeStruct((N_SHARD, 2, half, W_IN_COLS), BF16)),
        in_specs=[pl.BlockSpec((tk, D_MODEL), lambda k, t: (t, 0)),
                  pl.BlockSpec((tk, W_IN_COLS), lambda k, t: (t, k)), _whole_vmem()],
        out_specs=(shard, shard),
        compiler_params=_cparams(("parallel", "arbitrary")),
    )(h, dproj, dep)


HBM_SPEC = pl.BlockSpec(memory_space=pltpu.HBM)
SEM_SPEC = pl.BlockSpec(memory_space=pltpu.SEMAPHORE)
SIDE_EFFECT = pltpu.SideEffectType.DATAFLOW_SIDE_EFFECTING


def _remote_copies(plan, bufs, send_sems, recv_sems):
    x, y, c = _mesh_pos()
    return [pltpu.make_async_remote_copy(src_ref=src, dst_ref=dst, send_sem=send_sems.at[k],
                                         recv_sem=recv_sems.at[k], device_id=dev, device_id_type=MESH)
            for k, (src, dst, dev) in enumerate(plan(x, y, c, *bufs))]


def _start_copies(name, bufs, n_copies, plan):
    n = len(bufs)

    def body(*refs):
        for cp in _remote_copies(plan, refs[:n], refs[n], refs[n + 1]):
            cp.start()
        refs[-1][...] = jnp.zeros_like(refs[-1])

    outs = pl.pallas_call(
        body, name=name,
        out_shape=(pltpu.SemaphoreType.DMA((n_copies,)), pltpu.SemaphoreType.DMA((n_copies,)),
                   *[pltpu.HBM(b.shape, b.dtype) for b in bufs], jax.ShapeDtypeStruct((SUB, LANE), F32)),
        in_specs=[HBM_SPEC] * n,
        out_specs=(SEM_SPEC, SEM_SPEC, *[HBM_SPEC] * n, _whole_vmem()),
        input_output_aliases={i: 2 + i for i in range(n)},
        compiler_params=pltpu.CompilerParams(has_side_effects=SIDE_EFFECT),
    )(*[pltpu.with_memory_space_constraint(b, pltpu.HBM) for b in bufs])
    return outs[0], outs[1], list(outs[2:2 + n]), outs[-1]


def _wait_copies(name, send_sems, recv_sems, bufs, plan, after):
    n = len(bufs)

    def body(*refs):
        for cp in _remote_copies(plan, refs[:n], refs[n], refs[n + 1]):
            cp.wait_send()
            cp.wait_recv()

    outs = pl.pallas_call(
        body, name=name,
        out_shape=tuple(pltpu.HBM(b.shape, b.dtype) for b in bufs),
        in_specs=[HBM_SPEC] * n + [SEM_SPEC, SEM_SPEC, pl.BlockSpec(memory_space=pl.ANY)],
        out_specs=(HBM_SPEC,) * n,
        input_output_aliases={i: i for i in range(n)},
        compiler_params=pltpu.CompilerParams(has_side_effects=SIDE_EFFECT),
    )(*bufs, send_sems, recv_sems, after)
    return list(outs)


def _landing(shape, dtype):
    return lax.empty(shape, dtype)


def _plan_pair_exchange(x, y, c, g, r):
    return [(g.at[k, 1 - c], r.at[k], (x, y, 1 - c)) for k in range(N_SHARD)]


def _plan_chip_exchange(x, y, c, a, r):
    chips = [(1 - x, y), (x, 1 - y), (1 - x, 1 - y)]
    return [(a.at[2 * cx + cy], r.at[j], (cx, cy, c)) for j, (cx, cy) in enumerate(chips)]


def _plan_pair_gather(x, y, c, f):
    return [(f.at[c], f.at[c], (x, y, 1 - c))]


def _plan_all_gather(x, y, c, own, land):
    me = 4 * x + 2 * y + c
    flip = lambda v, bit: 1 - v if bit else v
    return [(own, land.at[me], (flip(x, m >> 2 & 1), flip(y, m >> 1 & 1), flip(c, m & 1))) for m in range(1, 8)]


def _add_pair(g, r, pos, name):
    _, _, rows, cols = g.shape
    tr = min(256, rows)

    def body(pos_ref, g_ref, r_ref, o_ref, ob_ref):
        v = g_ref[0] + r_ref[...].astype(F32)
        ob_ref[...] = v.astype(BF16)

        @pl.when(pl.program_id(1) == pos_ref[0])
        def _():
            o_ref[...] = v[0]

    blk = pl.BlockSpec((1, tr, cols), lambda t, k, pos_ref: (k, t, 0))
    return pl.pallas_call(
        body, name=name,
        grid_spec=pltpu.PrefetchScalarGridSpec(
            num_scalar_prefetch=1, grid=(rows // tr, N_SHARD),
            in_specs=[pl.BlockSpec((1, 1, tr, cols), lambda t, k, pos_ref: (k, pos_ref[1], t, 0)), blk],
            out_specs=(pl.BlockSpec((tr, cols), lambda t, k, pos_ref: (t, 0)), blk)),
        out_shape=(jax.ShapeDtypeStruct((rows, cols), F32), jax.ShapeDtypeStruct((N_SHARD, rows, cols), BF16)),
        compiler_params=_cparams(("parallel", "arbitrary")),
    )(pos, g, r)


def _add_chips(a, r, pos, name):
    rows, cols = a.shape
    tr = min(256, rows)

    def body(pos_ref, a_ref, r_ref, o_ref):
        o_ref[0] = ((a_ref[...] + r_ref[0].astype(F32)) + r_ref[1].astype(F32)) + r_ref[2].astype(F32)

    return pl.pallas_call(
        body, name=name,
        grid_spec=pltpu.PrefetchScalarGridSpec(
            num_scalar_prefetch=1, grid=(rows // tr,),
            in_specs=[pl.BlockSpec((tr, cols), lambda t, pos_ref: (t, 0)),
                      pl.BlockSpec((3, tr, cols), lambda t, pos_ref: (0, t, 0))],
            out_specs=pl.BlockSpec((1, tr, cols), lambda t, pos_ref: (pos_ref[1], t, 0))),
        out_shape=jax.ShapeDtypeStruct((2, rows, cols), F32),
        compiler_params=_cparams(("parallel",)),
    )(pos, a, r)


def _sum_slots(pos_ref, own, land_ref, rows):
    me = pos_ref[2]
    total = None
    for d in range(8):
        term = jnp.where(me == d, own, land_ref[d] if rows is None else land_ref[d, rows, :])
        total = term if total is None else total + term
    return total


def _sum_small(pos, small, small_land, gws, gws_land):
    def body(pos_ref, sm_ref, sml_ref, gw_ref, gwl_ref, o_sm, o_gw):
        o_sm[...] = _sum_slots(pos_ref, sm_ref[...], sml_ref, None)
        o_gw[...] = _sum_slots(pos_ref, gw_ref[...], gwl_ref, None)

    return pl.pallas_call(
        body, name="sum_small",
        grid_spec=pltpu.PrefetchScalarGridSpec(
            num_scalar_prefetch=1, grid=(1,),
            in_specs=[_whole_vmem()] * 4, out_specs=[_whole_vmem()] * 2),
        out_shape=[jax.ShapeDtypeStruct(small.shape, F32), jax.ShapeDtypeStruct(gws.shape, F32)],
        compiler_params=_cparams(("arbitrary",)),
    )(pos, small, small_land, gws, gws_land)


def _adamw_math(w, g, m, v):
    m = ADAM_B1 * m + (1.0 - ADAM_B1) * g
    v = ADAM_B2 * v + (1.0 - ADAM_B2) * (g * g)
    m_hat = m / (1.0 - ADAM_B1 ** ADAM_STEP)
    v_hat = v / (1.0 - ADAM_B2 ** ADAM_STEP)
    delta = -ADAM_LR * (m_hat / (jnp.sqrt(v_hat) + ADAM_EPS) + ADAM_WD * w)
    return delta, m, v


def _adamw_large(w, g, m, v, dep, name):
    rows, cols = w.shape
    tr = min(256, rows)

    def body(w_ref, g_ref, m_ref, v_ref, dep_ref, d_ref, nm_ref, nv_ref):
        d_ref[...], nm_ref[...], nv_ref[...] = _adamw_math(w_ref[...], g_ref[...], m_ref[...], v_ref[...])

    tile = pl.BlockSpec((tr, cols), lambda t: (t, 0))
    return pl.pallas_call(
        body, name=name,
        grid=(rows // tr,),
        out_shape=(jax.ShapeDtypeStruct(w.shape, F32),) * 3,
        in_specs=[tile] * 4 + [_whole_vmem()], out_specs=(tile,) * 3,
        compiler_params=_cparams(("parallel",)),
    )(w, g, m, v, dep)


_ROW_OF = {"conv_b": 1, "conv_ln_g": 2, "conv_ln_b": 3, "sgu_ln_g": 4, "sgu_ln_b": 5, "b_s": 6, "final_g": 7}
_CONV_W_ROW = 8


def _adamw_small(call_name, names, grads, pos, params):
    def body(pos_ref, p_ref, q_ref, *refs):
        gn_ref, gnl_ref = p_ref, q_ref
        n_in = 3 * len(names)
        ins, outs = refs[:n_in], refs[n_in:]
        me = pos_ref[0]
        for a, name in enumerate(names):
            w_ref, m_ref, v_ref = ins[3 * a:3 * a + 3]
            if name == "conv_w":
                g = jnp.zeros((KW, CONV_W_COLS), F32)
                for k in range(N_SHARD):
                    blk = p_ref[_CONV_W_ROW:_CONV_W_ROW + KW, k * CONV_W_COLS:(k + 1) * CONV_W_COLS]
                    g = jnp.where(me == k, blk, g)
            elif name == "w_s":
                g = q_ref[...]
            elif name == "norm_g":
                g = _sum_slots(pos_ref, gn_ref[0:1, :], gnl_ref, slice(0, 1))
            else:
                g = p_ref[_ROW_OF[name]:_ROW_OF[name] + 1, :]
            delta, nm, nv = _adamw_math(w_ref[...], g, m_ref[...], v_ref[...])
            for o_ref, val in zip(outs[4 * a:4 * a + 4], (g, delta, nm, nv)):
                o_ref[...] = val

    operands, shapes = [], []
    for name in names:
        operands += list(params[name])
        shapes += [jax.ShapeDtypeStruct(params[name][0].shape, F32)] * 4
    outs = pl.pallas_call(
        body, name=call_name,
        grid_spec=pltpu.PrefetchScalarGridSpec(
            num_scalar_prefetch=1, grid=(1,),
            in_specs=[_whole_vmem()] * (2 + len(operands)),
            out_specs=[_whole_vmem()] * len(shapes)),
        out_shape=shapes,
        compiler_params=_cparams(("arbitrary",)),
    )(pos, *grads, *operands)
    return {name: tuple(outs[4 * a:4 * a + 4]) for a, name in enumerate(names)}


def kernel(x, norm_g, w_in, conv_w, conv_b, conv_ln_g, conv_ln_b, sgu_ln_g, sgu_ln_b, w_s, b_s, w_out, final_g, loss_target, m_norm_g, m_w_in, m_conv_w, m_conv_b, m_conv_ln_g, m_conv_ln_b, m_sgu_ln_g, m_sgu_ln_b, m_w_s, m_b_s, m_w_out, m_final_g, v_norm_g, v_w_in, v_conv_w, v_conv_b, v_conv_ln_g, v_conv_ln_b, v_sgu_ln_g, v_sgu_ln_b, v_w_s, v_b_s, v_w_out, v_final_g):
    xi, yi, ci = _mesh_pos()
    pos = jnp.stack([2 * xi + yi, ci, 4 * xi + 2 * yi + ci]).astype(jnp.int32)

    x2d = x[0]
    tgt = loss_target[0]
    fg = final_g.reshape(1, D_MODEL)
    ws3 = w_s[0]
    ws_t = jnp.swapaxes(ws3, 1, 2)
    bs_t = jnp.transpose(b_s[0])

    chip = 2 * xi + yi
    order = jnp.stack([chip, 2 * (1 - xi) + yi, 2 * xi + 1 - yi, 2 * (1 - xi) + 1 - yi]).astype(jnp.int32)
    h, proj, wing, woutg, cwg = _rms_proj_gather(x2d, norm_g, w_in[0], w_out[0], conv_w[0], order)
    y, cv = _branch_fwd(proj, cwg, conv_b, conv_ln_g, conv_ln_b, sgu_ln_g, sgu_ln_b, ws3, bs_t)
    dx2, dy, gwout, gf8, se8 = _out_proj(y, x2d, tgt, woutg, fg)
    in_rows, out_rows = D_MODEL // 2, W_OUT_ROWS // 2
    gwout4 = gwout.reshape(N_SHARD, 2, out_rows, D_MODEL)
    ss, rs, (gwout4, r1_out), tok = _start_copies(
        "start_pair_exchange_w_out", [gwout4, _landing((N_SHARD, out_rows, D_MODEL), F32)], N_SHARD,
        _plan_pair_exchange)
    dproj, small, gws3 = _branch_bwd(proj, dy, cv, cwg, conv_ln_g, conv_ln_b, sgu_ln_g, sgu_ln_b, ws3, ws_t, bs_t,
                                     gf8, se8, tok)
    gws = gws3.reshape(HEADS * CHUNK, CHUNK)
    gwout4, r1_out = _wait_copies("wait_pair_exchange_w_out", ss, rs, [gwout4, r1_out], _plan_pair_exchange, dproj)
    a_out, a_out_bf = _add_pair(gwout4, r1_out, pos, "add_pair_w_out")

    def plan_b(x, y, c, a, r, sm, sml, gw, gwl):
        return (_plan_chip_exchange(x, y, c, a, r) + _plan_all_gather(x, y, c, sm, sml)
                + _plan_all_gather(x, y, c, gw, gwl))

    ss, rs, bufs_b, tok = _start_copies(
        "start_chip_exchange_w_out",
        [a_out_bf, _landing((3, out_rows, D_MODEL), BF16), small, _landing((8,) + small.shape, F32),
         gws, _landing((8,) + gws.shape, F32)], 3 + 7 + 7, plan_b)
    gwin, gwin_bf = _grad_w_in(h, dproj, tok)
    ss_c, rs_c, (gwin_bf, r1_in), tok = _start_copies(
        "start_pair_exchange_w_in", [gwin_bf, _landing((N_SHARD, in_rows, W_IN_COLS), BF16)], N_SHARD,
        _plan_pair_exchange)
    nt = x2d.shape[0] // min(TS_INB, x2d.shape[0])
    cut = nt // 2
    gx_a, gn_a = _in_bwd(dproj, x2d, dx2, norm_g, wing, tok, 0, cut)

    gwin_bf, r1_in = _wait_copies("wait_pair_exchange_w_in", ss_c, rs_c, [gwin_bf, r1_in], _plan_pair_exchange, gx_a)
    a_in, a_in_bf = _add_pair(gwin, r1_in, pos, "add_pair_w_in")
    a_out_bf, r2_out, small, small_land, gws, gws_land = _wait_copies(
        "wait_chip_exchange_w_out", ss, rs, bufs_b, plan_b, gx_a)
    f_out = _add_chips(a_out, r2_out, pos, "add_chips_w_out")
    p, q = _sum_small(pos, small, small_land, gws, gws_land)
    loss = (0.5 / D_MODEL) * jnp.sum(p[0])

    flat = lambda a: a.reshape(1, C_BR)
    flat_ws = lambda a: a.reshape(HEADS * CHUNK, CHUNK)
    params = {
        "norm_g": (norm_g, m_norm_g, v_norm_g),
        "conv_b": (conv_b, m_conv_b, v_conv_b),
        "conv_ln_g": (conv_ln_g, m_conv_ln_g, v_conv_ln_g),
        "conv_ln_b": (conv_ln_b, m_conv_ln_b, v_conv_ln_b),
        "sgu_ln_g": (sgu_ln_g, m_sgu_ln_g, v_sgu_ln_g),
        "sgu_ln_b": (sgu_ln_b, m_sgu_ln_b, v_sgu_ln_b),
        "b_s": (flat(b_s), flat(m_b_s), flat(v_b_s)),
        "final_g": (flat(final_g), flat(m_final_g), flat(v_final_g)),
        "conv_w": (conv_w[0], m_conv_w[0], v_conv_w[0]),
        "w_s": (flat_ws(w_s), flat_ws(m_w_s), flat_ws(v_w_s)),
    }
    res = _adamw_small("adamw_small", [n for n in params if n != "norm_g"], (p, q), pos, params)

    def plan_d(x, y, c, a, r, f):
        return _plan_chip_exchange(x, y, c, a, r) + _plan_pair_gather(x, y, c, f)

    ss, rs, bufs_d, tok = _start_copies(
        "start_chip_exchange_w_in", [a_in_bf, _landing((3, in_rows, W_IN_COLS), BF16), f_out], 3 + 1, plan_d)
    grad_x, gn8 = _in_bwd(dproj, x2d, dx2, norm_g, wing, tok, cut, nt - cut, gx_a, gn_a)
    a_in_bf, r2_in, f_out = _wait_copies("wait_chip_exchange_w_in", ss, rs, bufs_d, plan_d, grad_x)
    f_in = _add_chips(a_in, r2_in, pos, "add_chips_w_in")

    def plan_e(x, y, c, f, gn, gnl):
        return _plan_pair_gather(x, y, c, f) + _plan_all_gather(x, y, c, gn, gnl)

    ss, rs, bufs_e, tok = _start_copies(
        "start_pair_gather_w_in", [f_in, gn8, _landing((8,) + gn8.shape, F32)], 1 + 7, plan_e)
    g_w_out = f_out.reshape(W_OUT_ROWS, D_MODEL)
    d_w_out, nm_w_out, nv_w_out = _adamw_large(w_out[0], g_w_out, m_w_out[0], v_w_out[0], tok, "adamw_w_out")
    f_in, gn8, gn_land = _wait_copies("wait_pair_gather_w_in", ss, rs, bufs_e, plan_e, d_w_out)
    g_w_in = f_in.reshape(D_MODEL, W_IN_COLS)
    d_w_in, nm_w_in, nv_w_in = _adamw_large(w_in[0], g_w_in, m_w_in[0], v_w_in[0], tok, "adamw_w_in")
    res.update(_adamw_small("adamw_norm_g", ["norm_g"], (gn8, gn_land), pos, params))
    res["w_in"] = tuple(a[None] for a in (g_w_in, d_w_in, nm_w_in, nv_w_in))
    res["w_out"] = tuple(a[None] for a in (g_w_out, d_w_out, nm_w_out, nv_w_out))
    res["conv_w"] = tuple(a[None] for a in res["conv_w"])
    res["w_s"] = tuple(a.reshape(w_s.shape) for a in res["w_s"])
    res["b_s"] = tuple(a.reshape(b_s.shape) for a in res["b_s"])
    res["final_g"] = tuple(a.reshape(final_g.shape) for a in res["final_g"])

    order = ("norm_g", "w_in", "conv_w", "conv_b", "conv_ln_g", "conv_ln_b", "sgu_ln_g", "sgu_ln_b",
             "w_s", "b_s", "w_out", "final_g")
    out = [loss, grad_x[None]]
    for part in range(4):
        out += [res[name][part] for name in order]
    return tuple(out)
```

```python
import jax
import jax.numpy as jnp
from jax import lax
from jax.experimental import pallas as pl
from jax.experimental.pallas import tpu as pltpu

F32 = jnp.float32
BF16 = jnp.bfloat16
MESH = pl.DeviceIdType.MESH

EPS = 1e-6
D_MODEL = 1024
C_BR = 1024
D_IN = 6 * C_BR
N_SHARD = 4
W_IN_COLS = D_IN // N_SHARD
W_OUT_ROWS = 2 * C_BR // N_SHARD
CONV_W_COLS = C_BR // N_SHARD
KW = 31
HALO = 16
HEADS = 8
HEAD_DIM = 128
CHUNK = 128
LANE = 128
SUB = 8

ADAM_LR = 0.001
ADAM_B1 = 0.9
ADAM_B2 = 0.999
ADAM_EPS = 1e-08
ADAM_WD = 0.01
ADAM_STEP = 10

TS_PROJ = 512
TS_FWD = 256
TS_OUT = 512
TS_BWD = 256
TS_INB = 512
TK_GW = 2048
ROWS_A = 128
ROWS_B = 64
VMEM_LIMIT = 56 * 1024 * 1024


def _cparams(sem=None, vmem=VMEM_LIMIT):
    kw = dict(vmem_limit_bytes=vmem)
    if sem is not None:
        kw["dimension_semantics"] = sem
    return pltpu.CompilerParams(**kw)


def _whole_vmem():
    return pl.BlockSpec(memory_space=pltpu.VMEM)


def _sigmoid(v):
    return 0.5 * jnp.tanh(0.5 * v) + 0.5


def _silu(v):
    h = 0.5 * v
    return h + h * jnp.tanh(h)


def _fold8(v):
    n, c = v.shape
    return v.reshape(n // SUB, SUB, c).sum(axis=0)


def _dot_nt(a, b):
    return lax.dot_general(a, b, (((1,), (1,)), ((), ())), preferred_element_type=F32)


def _dot_tn(a, b):
    return lax.dot_general(a, b, (((0,), (0,)), ((), ())), preferred_element_type=F32)


def _mesh_pos():
    return lax.axis_index("x"), lax.axis_index("y"), lax.axis_index("c")


def _rms_proj_gather(x, norm_g, w_in, w_out, conv_w, order):
    s = x.shape[0]
    ts = min(TS_PROJ, s)
    nt = s // ts
    hin = w_in.shape[0] // 2
    hout = w_out.shape[0] // 2

    def body(order_ref, x_ref, g_ref, win_ref, wout_ref, cw_ref,
             h_ref, proj_ref, wing_ref, woutg_ref, cwg_ref, wg_ref, wob_ref, hs_ref, send_sems, recv_sems, local_sems):
        p = pl.program_id(0)
        t = pl.program_id(1)
        mx, my, c = _mesh_pos()
        me = 2 * mx + my
        chips = [(1 - mx, my), (mx, 1 - my), (1 - mx, 1 - my)]
        sibling = (mx, my, 1 - c)

        def remote(src, dst, sem, dev):
            return pltpu.make_async_remote_copy(
                src_ref=src, dst_ref=dst, send_sem=send_sems.at[sem], recv_sem=recv_sems.at[sem],
                device_id=dev, device_id_type=MESH)

        def w_in_part(blk, half):
            return wg_ref.at[blk, pl.ds(half * hin, hin)]

        def keep(blk, k):
            return pltpu.make_async_copy(wg_ref.at[blk], wing_ref.at[blk], local_sems.at[2 + k])

        def w_out_part(blk, half):
            return woutg_ref.at[blk, pl.ds(half * hout, hout)]

        def sends():
            out = []
            for j, (cx, cy) in enumerate(chips):
                blk = 2 * cx + cy
                out.append(remote(w_in_part(me, c), w_in_part(me, c), j, (cx, cy, c)))
                out.append(remote(w_in_part(blk, c), w_in_part(blk, c), 3 + j, sibling))
                out.append(remote(wob_ref.at[pl.ds(c * hout, hout)], w_out_part(me, c), 6 + j, (cx, cy, c)))
                out.append(remote(w_out_part(blk, c), w_out_part(blk, c), 9 + j, sibling))
                out.append(remote(cw_ref, cwg_ref.at[me], 12 + j, (cx, cy, c)))
            return out

        @pl.when(jnp.logical_and(p == 0, t == 0))
        def _():
            wg_ref[me] = win_ref[...].astype(BF16)
            wob_ref[...] = wout_ref[...].astype(BF16)
            keep(me, 0).start()
            mine = [pltpu.make_async_copy(wob_ref, woutg_ref.at[me], local_sems.at[0]),
                    pltpu.make_async_copy(cw_ref, cwg_ref.at[me], local_sems.at[1])]
            for cp in mine:
                cp.start()
            for k, cp in enumerate(sends()):
                if k % 5 == 4 or (k % 5 == 0 and k // 5 < 2):
                    cp.start()
            for cp in mine:
                cp.wait()

        for j, (cx, cy) in enumerate(chips):
            blk = 2 * cx + cy

            @pl.when(jnp.logical_and(p == j + 1, t == 0))
            def _():
                remote(w_in_part(blk, c), w_in_part(blk, c), j, (cx, cy, c)).wait_recv()
                remote(w_in_part(blk, c), w_in_part(blk, c), 3 + j, sibling).start()
                remote(w_in_part(blk, 1 - c), w_in_part(blk, 1 - c), 3 + j, sibling).wait_recv()
                keep(blk, j + 1).start()
                if j == 0:
                    sends()[5 * 2].start()
                if j == 1:
                    for jj in range(3):
                        sends()[5 * jj + 2].start()

        rows = pl.ds(pl.multiple_of(t * ts, ts), ts)

        @pl.when(p == 0)
        def _():
            xt = x_ref[...]
            r = lax.rsqrt(jnp.mean(xt * xt, axis=-1, keepdims=True) + EPS)
            hb = (xt * r * g_ref[...]).astype(BF16)
            h_ref[...] = hb
            hs_ref[rows, :] = hb

        proj_ref[...] = jnp.dot(hs_ref[rows, :], wg_ref[order_ref[p]], preferred_element_type=F32).astype(BF16)

        @pl.when(jnp.logical_and(p == N_SHARD - 1, t == nt - 1))
        def _():
            for j, (cx, cy) in enumerate(chips):
                blk = 2 * cx + cy
                remote(w_out_part(blk, c), w_out_part(blk, c), 6 + j, (cx, cy, c)).wait_recv()
                remote(w_out_part(blk, c), w_out_part(blk, c), 9 + j, sibling).start()
            for j, (cx, cy) in enumerate(chips):
                blk = 2 * cx + cy
                remote(w_out_part(blk, 1 - c), w_out_part(blk, 1 - c), 9 + j, sibling).wait_recv()
                remote(cw_ref, cwg_ref.at[blk], 12 + j, (cx, cy, c)).wait_recv()
            for cp in sends():
                cp.wait_send()
            keep(me, 0).wait()
            for j, (cx, cy) in enumerate(chips):
                keep(2 * cx + cy, j + 1).wait()

    hbm = pl.BlockSpec(memory_space=pl.ANY)
    return pl.pallas_call(
        body, name="rms_proj_gather",
        grid_spec=pltpu.PrefetchScalarGridSpec(
            num_scalar_prefetch=1, grid=(N_SHARD, nt),
            in_specs=[pl.BlockSpec((ts, D_MODEL), lambda p, t, o: (jnp.where(p == 0, t, nt - 1), 0)),
                      pl.BlockSpec((1, D_MODEL), lambda p, t, o: (0, 0)),
                      _whole_vmem(), _whole_vmem(), _whole_vmem()],
            out_specs=(pl.BlockSpec((ts, D_MODEL), lambda p, t, o: (jnp.where(p == 0, t, nt - 1), 0)),
                       pl.BlockSpec((ts, W_IN_COLS), lambda p, t, o: (t, o[p])),
                       hbm, hbm, hbm),
            scratch_shapes=[pltpu.VMEM((N_SHARD,) + w_in.shape, BF16), pltpu.VMEM(w_out.shape, BF16),
                            pltpu.VMEM((s, D_MODEL), BF16),
                            pltpu.SemaphoreType.DMA((15,)), pltpu.SemaphoreType.DMA((15,)),
                            pltpu.SemaphoreType.DMA((6,))]),
        out_shape=(jax.ShapeDtypeStruct((s, D_MODEL), BF16), jax.ShapeDtypeStruct((s, D_IN), BF16),
                   jax.ShapeDtypeStruct((N_SHARD,) + w_in.shape, BF16),
                   jax.ShapeDtypeStruct((N_SHARD,) + w_out.shape, BF16),
                   jax.ShapeDtypeStruct((N_SHARD,) + conv_w.shape, F32)),
        compiler_params=_cparams(("arbitrary", "arbitrary")),
    )(order, x, norm_g, w_in, w_out, conv_w)


def _halo_specs(ts, s, width, col_block):
    per = ts // HALO
    last = s // HALO - 1
    prev = pl.BlockSpec((HALO, width), lambda i: (jnp.maximum(i * per - 1, 0), col_block))
    nxt = pl.BlockSpec((HALO, width), lambda i: (jnp.minimum((i + 1) * per, last), col_block))
    return prev, nxt


def _layer_norm_stats(v):
    mu = jnp.mean(v, axis=-1, keepdims=True)
    vc = v - mu
    var = jnp.mean(vc * vc, axis=-1, keepdims=True)
    rstd = lax.rsqrt(var + EPS)
    return vc * rstd, rstd


def _fill_shifted(sh_ref, ext):
    n = ext.shape[0]
    sh_ref[0] = ext
    for r in range(1, SUB):
        sh_ref[r] = pltpu.roll(ext, n - r, axis=0)


def _branch_fwd(proj, cwg, conv_b, conv_ln_g, conv_ln_b, sgu_ln_g, sgu_ln_b, w_s, bs_t):
    s = proj.shape[0]
    ts = min(TS_FWD, s)
    nt = s // ts
    ra = min(ROWS_A, ts)

    def body(pm_ref, pp_ref, pn_ref, cw_ref, cb_ref, clg_ref, clb_ref, slg_ref, slb_ref, ws_ref, bst_ref,
             y_new, cv_ref, sh_ref):
        i = pl.program_id(0)
        keep_prev = (i > 0).astype(F32)
        keep_next = (i < nt - 1).astype(F32)

        for lb in range(C_BR // LANE):
            lanes = slice(lb * LANE, (lb + 1) * LANE)
            gates = slice(C_BR + lb * LANE, C_BR + (lb + 1) * LANE)

            def glu(ref):
                return ref[:, lanes].astype(F32) * _sigmoid(ref[:, gates].astype(F32))

            ext = jnp.concatenate([glu(pp_ref) * keep_prev, glu(pm_ref), glu(pn_ref) * keep_next], axis=0)
            _fill_shifted(sh_ref, ext)
            shard, off = divmod(lb * LANE, CONV_W_COLS)
            bias = cb_ref[:, lanes]

            def chunk(jc, carry):
                base = pl.multiple_of(jc * ra, ra)
                acc = jnp.zeros((ra, LANE), F32) + bias
                for k in range(KW):
                    o = k + 1
                    acc = acc + sh_ref[o % SUB, pl.ds(base + SUB * (o // SUB), ra), :] * cw_ref[
                        shard, k:k + 1, off:off + LANE]
                cv_ref[pl.ds(base, ra), lanes] = acc
                return carry

            lax.fori_loop(0, ts // ra, chunk, 0)

        lnh, _ = _layer_norm_stats(cv_ref[...])
        ln = lnh * clg_ref[...] + clb_ref[...]
        gc = pm_ref[:, 2 * C_BR:3 * C_BR].astype(F32)
        y_new[:, :C_BR] = (_silu(ln) * _silu(gc)).astype(BF16)

        vh, _ = _layer_norm_stats(pm_ref[:, 4 * C_BR:5 * C_BR].astype(F32))
        vn = (vh * slg_ref[...] + slb_ref[...]).astype(BF16)
        for hd in range(HEADS):
            w_h = ws_ref[hd].astype(BF16)
            b_h = bst_ref[:, hd:hd + 1]
            cols = slice(hd * HEAD_DIM, (hd + 1) * HEAD_DIM)
            for ch in range(ts // CHUNK):
                rows = slice(ch * CHUNK, (ch + 1) * CHUNK)
                mixed = jnp.dot(w_h, vn[rows, cols], preferred_element_type=F32) + b_h
                u = pm_ref[rows, 3 * C_BR + hd * HEAD_DIM:3 * C_BR + (hd + 1) * HEAD_DIM].astype(F32)
                gs = pm_ref[rows, 5 * C_BR + hd * HEAD_DIM:5 * C_BR + (hd + 1) * HEAD_DIM].astype(F32)
                y_new[rows, C_BR + hd * HEAD_DIM:C_BR + (hd + 1) * HEAD_DIM] = (
                    u * mixed * _silu(gs)).astype(BF16)

    prev, nxt = _halo_specs(ts, s, 2 * C_BR, 0)
    row = pl.BlockSpec((1, C_BR), lambda i: (0, 0))
    return pl.pallas_call(
        body, name="branch_fwd",
        grid=(nt,),
        out_shape=(jax.ShapeDtypeStruct((s, 2 * C_BR), BF16), jax.ShapeDtypeStruct((s, C_BR), F32)),
        in_specs=[pl.BlockSpec((ts, D_IN), lambda i: (i, 0)), prev, nxt,
                  _whole_vmem(), row, row, row, row, row, _whole_vmem(), _whole_vmem()],
        out_specs=(pl.BlockSpec((ts, 2 * C_BR), lambda i: (i, 0)),
                   pl.BlockSpec((ts, C_BR), lambda i: (i, 0))),
        scratch_shapes=[pltpu.VMEM((SUB, ts + 2 * HALO, LANE), F32)],
        compiler_params=_cparams(("parallel",)),
    )(proj, proj, proj, cwg, conv_b, conv_ln_g, conv_ln_b, sgu_ln_g, sgu_ln_b, w_s, bs_t)


def _out_proj(y, x, target, woutg, final_g):
    s = x.shape[0]
    ts = min(TS_OUT, s)
    nt = s // ts

    def body(y_ref, x_ref, t_ref, w_ref, g_ref, dx2_ref, dy_ref, gw_ref, gf_ref, se_ref):
        i = pl.program_id(0)

        @pl.when(i == 0)
        def _():
            gw_ref[...] = jnp.zeros_like(gw_ref)
            gf_ref[...] = jnp.zeros_like(gf_ref)
            se_ref[...] = jnp.zeros_like(se_ref)

        yt = y_ref[...]
        x2 = x_ref[...]
        for k in range(N_SHARD):
            x2 = x2 + jnp.dot(yt[:, k * W_OUT_ROWS:(k + 1) * W_OUT_ROWS], w_ref[k], preferred_element_type=F32)
        r2 = lax.rsqrt(jnp.mean(x2 * x2, axis=-1, keepdims=True) + EPS)
        n2 = x2 * r2
        g = g_ref[...]
        diff = n2 * g - t_ref[...]
        se_ref[...] += _fold8(diff * diff)
        dout = diff * (1.0 / D_MODEL)
        gf_ref[...] += _fold8(dout * n2)
        dn = dout * g
        dx2 = r2 * (dn - n2 * jnp.mean(dn * n2, axis=-1, keepdims=True))
        dx2_ref[...] = dx2
        dxb = dx2.astype(BF16)
        for k in range(N_SHARD):
            rows = slice(k * W_OUT_ROWS, (k + 1) * W_OUT_ROWS)
            dy_ref[:, rows] = _dot_nt(dxb, w_ref[k]).astype(BF16)
            gw_ref[rows, :] += _dot_tn(yt[:, rows], dxb)

        @pl.when(i == nt - 1)
        def _():
            gf_ref[...] = jnp.broadcast_to(jnp.sum(gf_ref[...], axis=0, keepdims=True), gf_ref.shape)

    tile = pl.BlockSpec((ts, D_MODEL), lambda i: (i, 0))
    wide = pl.BlockSpec((ts, 2 * C_BR), lambda i: (i, 0))
    acc8 = pl.BlockSpec((SUB, D_MODEL), lambda i: (0, 0))
    return pl.pallas_call(
        body, name="out_proj",
        grid=(nt,),
        out_shape=(jax.ShapeDtypeStruct((s, D_MODEL), F32), jax.ShapeDtypeStruct((s, 2 * C_BR), BF16),
                   jax.ShapeDtypeStruct((2 * C_BR, D_MODEL), F32),
                   jax.ShapeDtypeStruct((SUB, D_MODEL), F32), jax.ShapeDtypeStruct((SUB, D_MODEL), F32)),
        in_specs=[wide, tile, tile, _whole_vmem(), pl.BlockSpec((1, D_MODEL), lambda i: (0, 0))],
        out_specs=(tile, wide, pl.BlockSpec((2 * C_BR, D_MODEL), lambda i: (0, 0)), acc8, acc8),
        compiler_params=_cparams(("arbitrary",)),
    )(y, x, target, woutg, final_g)


def _dsilu(v, sg):
    return sg * (1.0 + v * (1.0 - sg))


def _branch_bwd(proj, dy, cv, cwg, conv_ln_g, conv_ln_b, sgu_ln_g, sgu_ln_b, w_s, ws_t, bs_t, gf8, se8, dep):
    s = proj.shape[0]
    ts = min(TS_BWD, s)
    nt = s // ts
    ra = min(ROWS_A, ts)
    rb = min(ROWS_B, ts)
    te = ts + 2 * HALO

    def body(pm_ref, dym_ref, cvm_ref, gcp_ref, gcn_ref, dyp_ref, dyn_ref, cvp_ref, cvn_ref,
             cw_ref, clg_ref, clb_ref, slg_ref, slb_ref, ws_ref, wst_ref, bst_ref, gf_ref, se_ref, dep_ref,
             dp_ref, small_ref, gws_ref,
             sh_ref, glu_ref, dgl_ref, dcv_ref, acc_ref, gcw_ref, gbs_ref):
        i = pl.program_id(0)

        @pl.when(i == 0)
        def _():
            acc_ref[...] = jnp.zeros_like(acc_ref)
            gcw_ref[...] = jnp.zeros_like(gcw_ref)
            gbs_ref[...] = jnp.zeros_like(gbs_ref)
            gws_ref[...] = jnp.zeros_like(gws_ref)

        def ext(prev_ref, main, next_ref):
            return jnp.concatenate([prev_ref[...].astype(F32), main, next_ref[...].astype(F32)], axis=0)

        main = slice(HALO, HALO + ts)

        hm = ts // 2
        clg = clg_ref[...]
        for half in range(2):
            rows_m = slice(half * hm, (half + 1) * hm)
            halo_refs = (cvp_ref, gcp_ref, dyp_ref) if half == 0 else (cvn_ref, gcn_ref, dyn_ref)

            def piece(halo_ref, main_val):
                halo = halo_ref[...].astype(F32)
                return jnp.concatenate([halo, main_val] if half == 0 else [main_val, halo], axis=0)

            cv_e = piece(halo_refs[0], cvm_ref[rows_m, :])
            gc_e = piece(halo_refs[1], pm_ref[rows_m, 2 * C_BR:3 * C_BR].astype(F32))
            dyc_e = piece(halo_refs[2], dym_ref[rows_m, :C_BR].astype(F32))
            mn = slice(HALO, HALO + hm) if half == 0 else slice(0, hm)
            lnh, rstd = _layer_norm_stats(cv_e)
            ln = lnh * clg + clb_ref[...]
            sg_ln = _sigmoid(ln)
            sg_gc = _sigmoid(gc_e)
            d_ln = dyc_e * gc_e * sg_gc * _dsilu(ln, sg_ln)
            dp_ref[rows_m, 2 * C_BR:3 * C_BR] = (
                dyc_e[mn] * ln[mn] * sg_ln[mn] * _dsilu(gc_e[mn], sg_gc[mn])).astype(BF16)
            dlnh = d_ln * clg
            d_cv = rstd * (dlnh - jnp.mean(dlnh, axis=-1, keepdims=True)
                           - lnh * jnp.mean(dlnh * lnh, axis=-1, keepdims=True))
            row = lax.broadcasted_iota(jnp.int32, (hm + HALO, 1), 0)
            if half == 0:
                valid = jnp.logical_or(row >= HALO, i > 0)
            else:
                valid = jnp.logical_or(row < hm, i < nt - 1)
            d_cv = jnp.where(valid, d_cv, 0.0)
            dcv_ref[half * (hm + HALO):(half + 1) * (hm + HALO), :] = d_cv
            acc_ref[0:8, :] += _fold8(d_cv[mn])
            acc_ref[8:16, :] += _fold8(d_ln[mn] * lnh[mn])
            acc_ref[16:24, :] += _fold8(d_ln[mn])

        for lb in range(C_BR // LANE):
            lanes = slice(lb * LANE, (lb + 1) * LANE)
            gates = slice(C_BR + lb * LANE, C_BR + (lb + 1) * LANE)
            shard, off = divmod(lb * LANE, CONV_W_COLS)
            av = pm_ref[:, lanes].astype(F32)
            sg = _sigmoid(pm_ref[:, gates].astype(F32))
            glu_ref[...] = av * sg
            _fill_shifted(sh_ref, dcv_ref[:, lanes])

            def chunk_a(jc, carry):
                base = pl.multiple_of(jc * ra, ra)
                acc = jnp.zeros((ra, LANE), F32)
                for j in range(KW):
                    o = j + 1
                    acc = acc + sh_ref[o % SUB, pl.ds(base + SUB * (o // SUB), ra), :] * cw_ref[
                        shard, KW - 1 - j:KW - j, off:off + LANE]
                dgl_ref[pl.ds(base, ra), :] = acc
                return carry

            lax.fori_loop(0, ts // ra, chunk_a, 0)

            def chunk_b(jc, accs):
                base = pl.multiple_of(jc * rb, rb)
                g = glu_ref[pl.ds(base, rb), :]
                out = []
                for j in range(KW):
                    o = j + 1
                    d = sh_ref[o % SUB, pl.ds(base + SUB * (o // SUB), rb), :]
                    out.append(accs[j] + _fold8(g * d))
                return tuple(out)

            accs = lax.fori_loop(0, ts // rb, chunk_b, tuple(jnp.zeros((SUB, LANE), F32) for _ in range(KW)))
            for j in range(KW):
                gcw_ref[j * SUB:(j + 1) * SUB, lanes] += accs[j]

            dglu = dgl_ref[...]
            dp_ref[:, lanes] = (dglu * sg).astype(BF16)
            dp_ref[:, gates] = (dglu * av * sg * (1.0 - sg)).astype(BF16)

        vh, vrstd = _layer_norm_stats(pm_ref[:, 4 * C_BR:5 * C_BR].astype(F32))
        slg = slg_ref[...]
        vn = (vh * slg + slb_ref[...]).astype(BF16)
        for hd in range(HEADS):
            w_h = ws_ref[hd].astype(BF16)
            wt_h = wst_ref[hd].astype(BF16)
            b_h = bst_ref[:, hd:hd + 1]
            cols = slice(hd * HEAD_DIM, (hd + 1) * HEAD_DIM)
            gws_h = jnp.zeros((CHUNK, CHUNK), F32)
            gbs_h = jnp.zeros((CHUNK, HEAD_DIM), F32)
            for ch in range(ts // CHUNK):
                rows = slice(ch * CHUNK, (ch + 1) * CHUNK)
                vn_b = vn[rows, cols]
                mixed = jnp.dot(w_h, vn_b, preferred_element_type=F32) + b_h
                u = pm_ref[rows, 3 * C_BR + hd * HEAD_DIM:3 * C_BR + (hd + 1) * HEAD_DIM].astype(F32)
                gs = pm_ref[rows, 5 * C_BR + hd * HEAD_DIM:5 * C_BR + (hd + 1) * HEAD_DIM].astype(F32)
                dys = dym_ref[rows, C_BR + hd * HEAD_DIM:C_BR + (hd + 1) * HEAD_DIM].astype(F32)
                sg_gs = _sigmoid(gs)
                silu_gs = gs * sg_gs
                dp_ref[rows, 3 * C_BR + hd * HEAD_DIM:3 * C_BR + (hd + 1) * HEAD_DIM] = (
                    dys * mixed * silu_gs).astype(BF16)
                dp_ref[rows, 5 * C_BR + hd * HEAD_DIM:5 * C_BR + (hd + 1) * HEAD_DIM] = (
                    dys * u * mixed * _dsilu(gs, sg_gs)).astype(BF16)
                d_mixed = dys * u * silu_gs
                dm_b = d_mixed.astype(BF16)
                gws_h = gws_h + _dot_nt(dm_b, vn_b)
                gbs_h = gbs_h + d_mixed
                dcv_ref[HALO + ch * CHUNK:HALO + (ch + 1) * CHUNK, cols] = jnp.dot(
                    wt_h, dm_b, preferred_element_type=F32)
            gws_ref[hd] += gws_h
            gbs_ref[:, cols] += gbs_h
        d_vn = dcv_ref[main, :]
        acc_ref[24:32, :] += _fold8(d_vn * vh)
        acc_ref[32:40, :] += _fold8(d_vn)
        dvh = d_vn * slg
        dp_ref[:, 4 * C_BR:5 * C_BR] = (vrstd * (
            dvh - jnp.mean(dvh, axis=-1, keepdims=True)
            - vh * jnp.mean(dvh * vh, axis=-1, keepdims=True))).astype(BF16)

        @pl.when(i == nt - 1)
        def _():
            small_ref[...] = jnp.zeros_like(small_ref)
            for a in range(5):
                small_ref[1 + a:2 + a, :] = jnp.sum(acc_ref[a * SUB:(a + 1) * SUB, :], axis=0, keepdims=True)
            ones = jnp.ones((SUB, HEAD_DIM), F32)
            for hd in range(HEADS):
                cols = slice(hd * HEAD_DIM, (hd + 1) * HEAD_DIM)
                rowsum = lax.dot_general(ones, gbs_ref[:, cols], (((1,), (1,)), ((), ())),
                                         precision=lax.Precision.HIGHEST, preferred_element_type=F32)
                small_ref[6:7, cols] = rowsum[0:1, :]
            small_ref[7:8, :] = gf_ref[0:1, :]
            small_ref[0:1, :] = jnp.sum(se_ref[...], axis=0, keepdims=True)
            for k in range(KW):
                j = KW - 1 - k
                small_ref[8 + k:9 + k, :] = jnp.sum(gcw_ref[j * SUB:(j + 1) * SUB, :], axis=0, keepdims=True)

    gc_prev, gc_next = _halo_specs(ts, s, C_BR, 2)
    lo_prev, lo_next = _halo_specs(ts, s, C_BR, 0)
    row = pl.BlockSpec((1, C_BR), lambda i: (0, 0))
    return pl.pallas_call(
        body, name="branch_bwd",
        grid=(nt,),
        out_shape=(jax.ShapeDtypeStruct((s, D_IN), BF16), jax.ShapeDtypeStruct((40, C_BR), F32),
                   jax.ShapeDtypeStruct((HEADS, CHUNK, CHUNK), F32)),
        in_specs=[pl.BlockSpec((ts, D_IN), lambda i: (i, 0)),
                  pl.BlockSpec((ts, 2 * C_BR), lambda i: (i, 0)),
                  pl.BlockSpec((ts, C_BR), lambda i: (i, 0)),
                  gc_prev, gc_next, lo_prev, lo_next, lo_prev, lo_next,
                  _whole_vmem(), row, row, row, row, _whole_vmem(), _whole_vmem(), _whole_vmem(),
                  _whole_vmem(), _whole_vmem(), _whole_vmem()],
        out_specs=(pl.BlockSpec((ts, D_IN), lambda i: (i, 0)),
                   pl.BlockSpec((40, C_BR), lambda i: (0, 0)),
                   pl.BlockSpec((HEADS, CHUNK, CHUNK), lambda i: (0, 0, 0))),
        scratch_shapes=[pltpu.VMEM((SUB, te, LANE), F32),
                        pltpu.VMEM((ts, LANE), F32),
                        pltpu.VMEM((ts, LANE), F32),
                        pltpu.VMEM((te, C_BR), F32),
                        pltpu.VMEM((5 * SUB, C_BR), F32),
                        pltpu.VMEM((KW * SUB, C_BR), F32),
                        pltpu.VMEM((CHUNK, C_BR), F32)],
        compiler_params=_cparams(("arbitrary",)),
    )(proj, dy, cv, proj, proj, dy, dy, cv, cv,
      cwg, conv_ln_g, conv_ln_b, sgu_ln_g, sgu_ln_b, w_s, ws_t, bs_t, gf8, se8, dep)


def _in_bwd(dproj, x, dx2, norm_g, wing, dep, first_tile, n_tiles, gx_prev=None, gn_prev=None):
    s = x.shape[0]
    ts = min(TS_INB, s)

    def body(*refs):
        dp_ref, x_ref, dx2_ref, g_ref, w_ref = refs[:5]
        gx_ref, gn_ref, acc_ref = refs[-3:]
        i = pl.program_id(0)

        @pl.when(i == 0)
        def _():
            acc_ref[...] = jnp.zeros_like(acc_ref)

        dh = jnp.zeros((ts, D_MODEL), F32)
        for k in range(N_SHARD):
            dh = dh + _dot_nt(dp_ref[:, k * W_IN_COLS:(k + 1) * W_IN_COLS], w_ref[k])
        xt = x_ref[...]
        r = lax.rsqrt(jnp.mean(xt * xt, axis=-1, keepdims=True) + EPS)
        n = xt * r
        acc_ref[...] += _fold8(dh * n)
        dn = dh * g_ref[...]
        gx_ref[...] = dx2_ref[...] + r * (dn - n * jnp.mean(dn * n, axis=-1, keepdims=True))

        @pl.when(i == n_tiles - 1)
        def _():
            total = jnp.broadcast_to(jnp.sum(acc_ref[...], axis=0, keepdims=True), gn_ref.shape)
            if gn_prev is not None:
                total = total + refs[7][...]
            gn_ref[...] = total

    tile = pl.BlockSpec((ts, D_MODEL), lambda i: (i + first_tile, 0))
    in_specs = [pl.BlockSpec((ts, D_IN), lambda i: (i + first_tile, 0)), tile, tile,
                pl.BlockSpec((1, D_MODEL), lambda i: (0, 0)), _whole_vmem(), pl.BlockSpec(memory_space=pl.ANY)]
    operands = [dproj, x, dx2, norm_g, wing, dep]
    aliases = {}
    if gx_prev is not None:
        in_specs += [pl.BlockSpec(memory_space=pl.ANY), _whole_vmem()]
        operands += [gx_prev, gn_prev]
        aliases = {6: 0}
    return pl.pallas_call(
        body, name="in_bwd_%d" % first_tile,
        grid=(n_tiles,),
        out_shape=(jax.ShapeDtypeStruct((s, D_MODEL), F32), jax.ShapeDtypeStruct((SUB, D_MODEL), F32)),
        in_specs=in_specs,
        out_specs=(tile, pl.BlockSpec((SUB, D_MODEL), lambda i: (0, 0))),
        scratch_shapes=[pltpu.VMEM((SUB, D_MODEL), F32)],
        input_output_aliases=aliases,
        compiler_params=_cparams(("arbitrary",)),
    )(*operands)


def _grad_w_in(h, dproj, dep):
    s = h.shape[0]
    tk = min(TK_GW, s)
    nk = s // tk
    half = D_MODEL // 2

    def body(h_ref, dp_ref, dep_ref, o_ref, ob_ref):
        @pl.when(pl.program_id(1) == 0)
        def _():
            o_ref[...] = jnp.zeros_like(o_ref)

        o_ref[0] += _dot_tn(h_ref[...], dp_ref[...]).reshape(2, half, W_IN_COLS)

        @pl.when(pl.program_id(1) == nk - 1)
        def _():
            ob_ref[...] = o_ref[...].astype(BF16)

    shard = pl.BlockSpec((1, 2, half, W_IN_COLS), lambda k, t: (k, 0, 0, 0))
    return pl.pallas_call(
        body, name="grad_w_in",
        grid=(N_SHARD, nk),
        out_shape=(jax.ShapeDtypeStruct((N_SHARD, 2, half, W_IN_COLS), F32),
                   jax.ShapeDtypeStruct((N_SHARD, 2, half, W_IN_COLS), BF16)),
        in_specs=[pl.BlockSpec((tk, D_MODEL), lambda k, t: (t, 0)),
                  pl.BlockSpec((tk, W_IN_COLS), lambda k, t: (t, k)), _whole_vmem()],
        out_specs=(shard, shard),
        compiler_params=_cparams(("parallel", "arbitrary")),
    )(h, dproj, dep)


HBM_SPEC = pl.BlockSpec(memory_space=pltpu.HBM)
SEM_SPEC = pl.BlockSpec(memory_space=pltpu.SEMAPHORE)
SIDE_EFFECT = pltpu.SideEffectType.DATAFLOW_SIDE_EFFECTING


def _remote_copies(plan, bufs, send_sems, recv_sems):
    x, y, c = _mesh_pos()
    return [pltpu.make_async_remote_copy(src_ref=src, dst_ref=dst, send_sem=send_sems.at[k],
                                         recv_sem=recv_sems.at[k], device_id=dev, device_id_type=MESH)
            for k, (src, dst, dev) in enumerate(plan(x, y, c, *bufs))]


def _start_copies(name, bufs, n_copies, plan):
    n = len(bufs)

    def body(*refs):
        for cp in _remote_copies(plan, refs[:n], refs[n], refs[n + 1]):
            cp.start()
        refs[-1][...] = jnp.zeros_like(refs[-1])

    outs = pl.pallas_call(
        body, name=name,
        out_shape=(pltpu.SemaphoreType.DMA((n_copies,)), pltpu.SemaphoreType.DMA((n_copies,)),
                   *[pltpu.HBM(b.shape, b.dtype) for b in bufs], jax.ShapeDtypeStruct((SUB, LANE), F32)),
        in_specs=[HBM_SPEC] * n,
        out_specs=(SEM_SPEC, SEM_SPEC, *[HBM_SPEC] * n, _whole_vmem()),
        input_output_aliases={i: 2 + i for i in range(n)},
        compiler_params=pltpu.CompilerParams(has_side_effects=SIDE_EFFECT),
    )(*[pltpu.with_memory_space_constraint(b, pltpu.HBM) for b in bufs])
    return outs[0], outs[1], list(outs[2:2 + n]), outs[-1]


def _wait_copies(name, send_sems, recv_sems, bufs, plan, after):
    n = len(bufs)

    def body(*refs):
        for cp in _remote_copies(plan, refs[:n], refs[n], refs[n + 1]):
            cp.wait_send()
            cp.wait_recv()

    outs = pl.pallas_call(
        body, name=name,
        out_shape=tuple(pltpu.HBM(b.shape, b.dtype) for b in bufs),
        in_specs=[HBM_SPEC] * n + [SEM_SPEC, SEM_SPEC, pl.BlockSpec(memory_space=pl.ANY)],
        out_specs=(HBM_SPEC,) * n,
        input_output_aliases={i: i for i in range(n)},
        compiler_params=pltpu.CompilerParams(has_side_effects=SIDE_EFFECT),
    )(*bufs, send_sems, recv_sems, after)
    return list(outs)


def _landing(shape, dtype):
    return lax.empty(shape, dtype)


def _plan_pair_exchange(x, y, c, g, r):
    return [(g.at[k, 1 - c], r.at[k], (x, y, 1 - c)) for k in range(N_SHARD)]


def _plan_chip_exchange(x, y, c, a, r):
    chips = [(1 - x, y), (x, 1 - y), (1 - x, 1 - y)]
    return [(a.at[2 * cx + cy], r.at[j], (cx, cy, c)) for j, (cx, cy) in enumerate(chips)]


def _plan_pair_gather(x, y, c, f):
    return [(f.at[c], f.at[c], (x, y, 1 - c))]


def _plan_all_gather(x, y, c, own, land):
    me = 4 * x + 2 * y + c
    flip = lambda v, bit: 1 - v if bit else v
    return [(own, land.at[me], (flip(x, m >> 2 & 1), flip(y, m >> 1 & 1), flip(c, m & 1))) for m in range(1, 8)]


def _add_pair(g, r, pos, name):
    _, _, rows, cols = g.shape
    tr = min(256, rows)

    def body(pos_ref, g_ref, r_ref, o_ref, ob_ref):
        v = g_ref[0] + r_ref[...].astype(F32)
        ob_ref[...] = v.astype(BF16)

        @pl.when(pl.program_id(1) == pos_ref[0])
        def _():
            o_ref[...] = v[0]

    blk = pl.BlockSpec((1, tr, cols), lambda t, k, pos_ref: (k, t, 0))
    return pl.pallas_call(
        body, name=name,
        grid_spec=pltpu.PrefetchScalarGridSpec(
            num_scalar_prefetch=1, grid=(rows // tr, N_SHARD),
            in_specs=[pl.BlockSpec((1, 1, tr, cols), lambda t, k, pos_ref: (k, pos_ref[1], t, 0)), blk],
            out_specs=(pl.BlockSpec((tr, cols), lambda t, k, pos_ref: (t, 0)), blk)),
        out_shape=(jax.ShapeDtypeStruct((rows, cols), F32), jax.ShapeDtypeStruct((N_SHARD, rows, cols), BF16)),
        compiler_params=_cparams(("parallel", "arbitrary")),
    )(pos, g, r)


def _add_chips(a, r, pos, name):
    rows, cols = a.shape
    tr = min(256, rows)

    def body(pos_ref, a_ref, r_ref, o_ref):
        o_ref[0] = ((a_ref[...] + r_ref[0].astype(F32)) + r_ref[1].astype(F32)) + r_ref[2].astype(F32)

    return pl.pallas_call(
        body, name=name,
        grid_spec=pltpu.PrefetchScalarGridSpec(
            num_scalar_prefetch=1, grid=(rows // tr,),
            in_specs=[pl.BlockSpec((tr, cols), lambda t, pos_ref: (t, 0)),
                      pl.BlockSpec((3, tr, cols), lambda t, pos_ref: (0, t, 0))],
            out_specs=pl.BlockSpec((1, tr, cols), lambda t, pos_ref: (pos_ref[1], t, 0))),
        out_shape=jax.ShapeDtypeStruct((2, rows, cols), F32),
        compiler_params=_cparams(("parallel",)),
    )(pos, a, r)


def _sum_slots(pos_ref, own, land_ref, rows):
    me = pos_ref[2]
    total = None
    for d in range(8):
        term = jnp.where(me == d, own, land_ref[d] if rows is None else land_ref[d, rows, :])
        total = term if total is None else total + term
    return total


def _sum_small(pos, small, small_land, gws, gws_land):
    def body(pos_ref, sm_ref, sml_ref, gw_ref, gwl_ref, o_sm, o_gw):
        o_sm[...] = _sum_slots(pos_ref, sm_ref[...], sml_ref, None)
        o_gw[...] = _sum_slots(pos_ref, gw_ref[...], gwl_ref, None)

    return pl.pallas_call(
        body, name="sum_small",
        grid_spec=pltpu.PrefetchScalarGridSpec(
            num_scalar_prefetch=1, grid=(1,),
            in_specs=[_whole_vmem()] * 4, out_specs=[_whole_vmem()] * 2),
        out_shape=[jax.ShapeDtypeStruct(small.shape, F32), jax.ShapeDtypeStruct(gws.shape, F32)],
        compiler_params=_cparams(("arbitrary",)),
    )(pos, small, small_land, gws, gws_land)


def _adamw_math(w, g, m, v):
    m = ADAM_B1 * m + (1.0 - ADAM_B1) * g
    v = ADAM_B2 * v + (1.0 - ADAM_B2) * (g * g)
    m_hat = m / (1.0 - ADAM_B1 ** ADAM_STEP)
    v_hat = v / (1.0 - ADAM_B2 ** ADAM_STEP)
    delta = -ADAM_LR * (m_hat / (jnp.sqrt(v_hat) + ADAM_EPS) + ADAM_WD * w)
    return delta, m, v


def _adamw_large(w, g, m, v, dep, name):
    rows, cols = w.shape
    tr = min(256, rows)

    def body(w_ref, g_ref, m_ref, v_ref, dep_ref, g_out, d_ref, nm_ref, nv_ref):
        g = g_ref[...]
        g_out[...] = g
        d_ref[...], nm_ref[...], nv_ref[...] = _adamw_math(w_ref[...], g, m_ref[...], v_ref[...])

    tile = pl.BlockSpec((tr, cols), lambda t: (t, 0))
    return pl.pallas_call(
        body, name=name,
        grid=(rows // tr,),
        out_shape=(jax.ShapeDtypeStruct(w.shape, F32),) * 4,
        in_specs=[tile] * 4 + [_whole_vmem()], out_specs=(tile,) * 4,
        compiler_params=_cparams(("parallel",)),
    )(w, g, m, v, dep)


_ROW_OF = {"conv_b": 1, "conv_ln_g": 2, "conv_ln_b": 3, "sgu_ln_g": 4, "sgu_ln_b": 5, "b_s": 6, "final_g": 7}
_CONV_W_ROW = 8


def _adamw_small(call_name, names, grads, pos, params):
    def body(pos_ref, p_ref, q_ref, *refs):
        gn_ref, gnl_ref = p_ref, q_ref
        n_in = 3 * len(names)
        ins, outs = refs[:n_in], refs[n_in:]
        me = pos_ref[0]
        for a, name in enumerate(names):
            w_ref, m_ref, v_ref = ins[3 * a:3 * a + 3]
            if name == "conv_w":
                g = jnp.zeros((KW, CONV_W_COLS), F32)
                for k in range(N_SHARD):
                    blk = p_ref[_CONV_W_ROW:_CONV_W_ROW + KW, k * CONV_W_COLS:(k + 1) * CONV_W_COLS]
                    g = jnp.where(me == k, blk, g)
            elif name == "w_s":
                g = q_ref[...]
            elif name == "norm_g":
                g = _sum_slots(pos_ref, gn_ref[0:1, :], gnl_ref, slice(0, 1))
            else:
                g = p_ref[_ROW_OF[name]:_ROW_OF[name] + 1, :]
            delta, nm, nv = _adamw_math(w_ref[...], g, m_ref[...], v_ref[...])
            for o_ref, val in zip(outs[4 * a:4 * a + 4], (g, delta, nm, nv)):
                o_ref[...] = val

    operands, shapes = [], []
    for name in names:
        operands += list(params[name])
        shapes += [jax.ShapeDtypeStruct(params[name][0].shape, F32)] * 4
    outs = pl.pallas_call(
        body, name=call_name,
        grid_spec=pltpu.PrefetchScalarGridSpec(
            num_scalar_prefetch=1, grid=(1,),
            in_specs=[_whole_vmem()] * (2 + len(operands)),
            out_specs=[_whole_vmem()] * len(shapes)),
        out_shape=shapes,
        compiler_params=_cparams(("arbitrary",)),
    )(pos, *grads, *operands)
    return {name: tuple(outs[4 * a:4 * a + 4]) for a, name in enumerate(names)}


def kernel(x, norm_g, w_in, conv_w, conv_b, conv_ln_g, conv_ln_b, sgu_ln_g, sgu_ln_b, w_s, b_s, w_out, final_g, loss_target, m_norm_g, m_w_in, m_conv_w, m_conv_b, m_conv_ln_g, m_conv_ln_b, m_sgu_ln_g, m_sgu_ln_b, m_w_s, m_b_s, m_w_out, m_final_g, v_norm_g, v_w_in, v_conv_w, v_conv_b, v_conv_ln_g, v_conv_ln_b, v_sgu_ln_g, v_sgu_ln_b, v_w_s, v_b_s, v_w_out, v_final_g):
    xi, yi, ci = _mesh_pos()
    pos = jnp.stack([2 * xi + yi, ci, 4 * xi + 2 * yi + ci]).astype(jnp.int32)

    x2d = x[0]
    tgt = loss_target[0]
    fg = final_g.reshape(1, D_MODEL)
    ws3 = w_s[0]
    ws_t = jnp.swapaxes(ws3, 1, 2)
    bs_t = jnp.transpose(b_s[0])

    chip = 2 * xi + yi
    order = jnp.stack([chip, 2 * (1 - xi) + yi, 2 * xi + 1 - yi, 2 * (1 - xi) + 1 - yi]).astype(jnp.int32)
    h, proj, wing, woutg, cwg = _rms_proj_gather(x2d, norm_g, w_in[0], w_out[0], conv_w[0], order)
    y, cv = _branch_fwd(proj, cwg, conv_b, conv_ln_g, conv_ln_b, sgu_ln_g, sgu_ln_b, ws3, bs_t)
    dx2, dy, gwout, gf8, se8 = _out_proj(y, x2d, tgt, woutg, fg)
    in_rows, out_rows = D_MODEL // 2, W_OUT_ROWS // 2
    gwout4 = gwout.reshape(N_SHARD, 2, out_rows, D_MODEL)
    ss, rs, (gwout4, r1_out), tok = _start_copies(
        "start_pair_exchange_w_out", [gwout4, _landing((N_SHARD, out_rows, D_MODEL), F32)], N_SHARD,
        _plan_pair_exchange)
    dproj, small, gws3 = _branch_bwd(proj, dy, cv, cwg, conv_ln_g, conv_ln_b, sgu_ln_g, sgu_ln_b, ws3, ws_t, bs_t,
                                     gf8, se8, tok)
    gws = gws3.reshape(HEADS * CHUNK, CHUNK)
    gwout4, r1_out = _wait_copies("wait_pair_exchange_w_out", ss, rs, [gwout4, r1_out], _plan_pair_exchange, dproj)
    a_out, a_out_bf = _add_pair(gwout4, r1_out, pos, "add_pair_w_out")

    def plan_b(x, y, c, a, r, sm, sml, gw, gwl):
        return (_plan_chip_exchange(x, y, c, a, r) + _plan_all_gather(x, y, c, sm, sml)
                + _plan_all_gather(x, y, c, gw, gwl))

    ss, rs, bufs_b, tok = _start_copies(
        "start_chip_exchange_w_out",
        [a_out_bf, _landing((3, out_rows, D_MODEL), BF16), small, _landing((8,) + small.shape, F32),
         gws, _landing((8,) + gws.shape, F32)], 3 + 7 + 7, plan_b)
    gwin, gwin_bf = _grad_w_in(h, dproj, tok)
    ss_c, rs_c, (gwin_bf, r1_in), tok = _start_copies(
        "start_pair_exchange_w_in", [gwin_bf, _landing((N_SHARD, in_rows, W_IN_COLS), BF16)], N_SHARD,
        _plan_pair_exchange)
    nt = x2d.shape[0] // min(TS_INB, x2d.shape[0])
    cut = nt // 2
    gx_a, gn_a = _in_bwd(dproj, x2d, dx2, norm_g, wing, tok, 0, cut)

    gwin_bf, r1_in = _wait_copies("wait_pair_exchange_w_in", ss_c, rs_c, [gwin_bf, r1_in], _plan_pair_exchange, gx_a)
    a_in, a_in_bf = _add_pair(gwin, r1_in, pos, "add_pair_w_in")
    a_out_bf, r2_out, small, small_land, gws, gws_land = _wait_copies(
        "wait_chip_exchange_w_out", ss, rs, bufs_b, plan_b, gx_a)
    f_out = _add_chips(a_out, r2_out, pos, "add_chips_w_out")
    p, q = _sum_small(pos, small, small_land, gws, gws_land)
    loss = (0.5 / D_MODEL) * jnp.sum(p[0])

    flat = lambda a: a.reshape(1, C_BR)
    flat_ws = lambda a: a.reshape(HEADS * CHUNK, CHUNK)
    params = {
        "norm_g": (norm_g, m_norm_g, v_norm_g),
        "conv_b": (conv_b, m_conv_b, v_conv_b),
        "conv_ln_g": (conv_ln_g, m_conv_ln_g, v_conv_ln_g),
        "conv_ln_b": (conv_ln_b, m_conv_ln_b, v_conv_ln_b),
        "sgu_ln_g": (sgu_ln_g, m_sgu_ln_g, v_sgu_ln_g),
        "sgu_ln_b": (sgu_ln_b, m_sgu_ln_b, v_sgu_ln_b),
        "b_s": (flat(b_s), flat(m_b_s), flat(v_b_s)),
        "final_g": (flat(final_g), flat(m_final_g), flat(v_final_g)),
        "conv_w": (conv_w[0], m_conv_w[0], v_conv_w[0]),
        "w_s": (flat_ws(w_s), flat_ws(m_w_s), flat_ws(v_w_s)),
    }
    res = _adamw_small("adamw_small", [n for n in params if n != "norm_g"], (p, q), pos, params)

    def plan_d(x, y, c, a, r, f):
        return _plan_chip_exchange(x, y, c, a, r) + _plan_pair_gather(x, y, c, f)

    ss, rs, bufs_d, tok = _start_copies(
        "start_chip_exchange_w_in", [a_in_bf, _landing((3, in_rows, W_IN_COLS), BF16), f_out], 3 + 1, plan_d)
    grad_x, gn8 = _in_bwd(dproj, x2d, dx2, norm_g, wing, tok, cut, nt - cut, gx_a, gn_a)
    a_in_bf, r2_in, f_out = _wait_copies("wait_chip_exchange_w_in", ss, rs, bufs_d, plan_d, grad_x)
    f_in = _add_chips(a_in, r2_in, pos, "add_chips_w_in")

    def plan_e(x, y, c, f, gn, gnl):
        return _plan_pair_gather(x, y, c, f) + _plan_all_gather(x, y, c, gn, gnl)

    ss, rs, bufs_e, tok = _start_copies(
        "start_pair_gather_w_in", [f_in, gn8, _landing((8,) + gn8.shape, F32)], 1 + 7, plan_e)
    g_w_out = f_out.reshape(W_OUT_ROWS, D_MODEL)
    g_w_out, d_w_out, nm_w_out, nv_w_out = _adamw_large(w_out[0], g_w_out, m_w_out[0], v_w_out[0], tok, "adamw_w_out")
    f_in, gn8, gn_land = _wait_copies("wait_pair_gather_w_in", ss, rs, bufs_e, plan_e, d_w_out)
    g_w_in = f_in.reshape(D_MODEL, W_IN_COLS)
    g_w_in, d_w_in, nm_w_in, nv_w_in = _adamw_large(w_in[0], g_w_in, m_w_in[0], v_w_in[0], tok, "adamw_w_in")
    res.update(_adamw_small("adamw_norm_g", ["norm_g"], (gn8, gn_land), pos, params))
    res["w_in"] = tuple(a[None] for a in (g_w_in, d_w_in, nm_w_in, nv_w_in))
    res["w_out"] = tuple(a[None] for a in (g_w_out, d_w_out, nm_w_out, nv_w_out))
    res["conv_w"] = tuple(a[None] for a in res["conv_w"])
    res["w_s"] = tuple(a.reshape(w_s.shape) for a in res["w_s"])
    res["b_s"] = tuple(a.reshape(b_s.shape) for a in res["b_s"])
    res["final_g"] = tuple(a.reshape(final_g.shape) for a in res["final_g"])

    order = ("norm_g", "w_in", "conv_w", "conv_b", "conv_ln_g", "conv_ln_b", "sgu_ln_g", "sgu_ln_b",
             "w_s", "b_s", "w_out", "final_g")
    out = [loss, grad_x[None]]
    for part in range(4):
        out += [res[name][part] for name in order]
    return tuple(out)
```

```python
import jax
import jax.numpy as jnp
from jax import lax
from jax.experimental import pallas as pl
from jax.experimental.pallas import tpu as pltpu

F32 = jnp.float32
BF16 = jnp.bfloat16
MESH = pl.DeviceIdType.MESH

EPS = 1e-6
D_MODEL = 1024
C_BR = 1024
D_IN = 6 * C_BR
N_SHARD = 4
W_IN_COLS = D_IN // N_SHARD
W_OUT_ROWS = 2 * C_BR // N_SHARD
CONV_W_COLS = C_BR // N_SHARD
KW = 31
HALO = 16
HEADS = 8
HEAD_DIM = 128
CHUNK = 128
LANE = 128
SUB = 8

ADAM_LR = 0.001
ADAM_B1 = 0.9
ADAM_B2 = 0.999
ADAM_EPS = 1e-08
ADAM_WD = 0.01
ADAM_STEP = 10

TS_PROJ = 512
TS_FWD = 256
TS_OUT = 512
TS_BWD = 256
TS_INB = 512
TK_GW = 2048
ROWS_A = 128
ROWS_B = 64
VMEM_LIMIT = 56 * 1024 * 1024


def _cparams(sem=None, vmem=VMEM_LIMIT):
    kw = dict(vmem_limit_bytes=vmem)
    if sem is not None:
        kw["dimension_semantics"] = sem
    return pltpu.CompilerParams(**kw)


def _whole_vmem():
    return pl.BlockSpec(memory_space=pltpu.VMEM)


def _sigmoid(v):
    return 0.5 * jnp.tanh(0.5 * v) + 0.5


def _silu(v):
    h = 0.5 * v
    return h + h * jnp.tanh(h)


def _fold8(v):
    n, c = v.shape
    return v.reshape(n // SUB, SUB, c).sum(axis=0)


def _dot_nt(a, b):
    return lax.dot_general(a, b, (((1,), (1,)), ((), ())), preferred_element_type=F32)


def _dot_tn(a, b):
    return lax.dot_general(a, b, (((0,), (0,)), ((), ())), preferred_element_type=F32)


def _mesh_pos():
    return lax.axis_index("x"), lax.axis_index("y"), lax.axis_index("c")


def _rms_proj_gather(x, norm_g, w_in, w_out, conv_w, order):
    s = x.shape[0]
    ts = min(TS_PROJ, s)
    nt = s // ts
    hin = w_in.shape[0] // 2
    hout = w_out.shape[0] // 2

    def body(order_ref, x_ref, g_ref, win_ref, wout_ref, cw_ref,
             h_ref, proj_ref, wing_ref, woutg_ref, cwg_ref, wg_ref, wob_ref, hs_ref, send_sems, recv_sems, local_sems):
        p = pl.program_id(0)
        t = pl.program_id(1)
        mx, my, c = _mesh_pos()
        me = 2 * mx + my
        chips = [(1 - mx, my), (mx, 1 - my), (1 - mx, 1 - my)]
        sibling = (mx, my, 1 - c)

        def remote(src, dst, sem, dev):
            return pltpu.make_async_remote_copy(
                src_ref=src, dst_ref=dst, send_sem=send_sems.at[sem], recv_sem=recv_sems.at[sem],
                device_id=dev, device_id_type=MESH)

        def w_in_part(blk, half):
            return wg_ref.at[blk, pl.ds(half * hin, hin)]

        def keep(blk, k):
            return pltpu.make_async_copy(wg_ref.at[blk], wing_ref.at[blk], local_sems.at[2 + k])

        def w_out_part(blk, half):
            return woutg_ref.at[blk, pl.ds(half * hout, hout)]

        def sends():
            out = []
            for j, (cx, cy) in enumerate(chips):
                blk = 2 * cx + cy
                out.append(remote(w_in_part(me, c), w_in_part(me, c), j, (cx, cy, c)))
                out.append(remote(w_in_part(blk, c), w_in_part(blk, c), 3 + j, sibling))
                out.append(remote(wob_ref.at[pl.ds(c * hout, hout)], w_out_part(me, c), 6 + j, (cx, cy, c)))
                out.append(remote(w_out_part(blk, c), w_out_part(blk, c), 9 + j, sibling))
                out.append(remote(cw_ref, cwg_ref.at[me], 12 + j, (cx, cy, c)))
            return out

        @pl.when(jnp.logical_and(p == 0, t == 0))
        def _():
            wg_ref[me] = win_ref[...].astype(BF16)
            wob_ref[...] = wout_ref[...].astype(BF16)
            keep(me, 0).start()
            mine = [pltpu.make_async_copy(wob_ref, woutg_ref.at[me], local_sems.at[0]),
                    pltpu.make_async_copy(cw_ref, cwg_ref.at[me], local_sems.at[1])]
            for cp in mine:
                cp.start()
            for k, cp in enumerate(sends()):
                if k % 5 == 4 or (k % 5 == 0 and k // 5 < 2):
                    cp.start()
            for cp in mine:
                cp.wait()

        for j, (cx, cy) in enumerate(chips):
            blk = 2 * cx + cy

            @pl.when(jnp.logical_and(p == j + 1, t == 0))
            def _():
                remote(w_in_part(blk, c), w_in_part(blk, c), j, (cx, cy, c)).wait_recv()
                remote(w_in_part(blk, c), w_in_part(blk, c), 3 + j, sibling).start()
                remote(w_in_part(blk, 1 - c), w_in_part(blk, 1 - c), 3 + j, sibling).wait_recv()
                keep(blk, j + 1).start()
                if j == 0:
                    sends()[5 * 2].start()
                if j == 1:
                    for jj in range(3):
                        sends()[5 * jj + 2].start()

        rows = pl.ds(pl.multiple_of(t * ts, ts), ts)

        @pl.when(p == 0)
        def _():
            xt = x_ref[...]
            r = lax.rsqrt(jnp.mean(xt * xt, axis=-1, keepdims=True) + EPS)
            hb = (xt * r * g_ref[...]).astype(BF16)
            h_ref[...] = hb
            hs_ref[rows, :] = hb

        proj_ref[...] = jnp.dot(hs_ref[rows, :], wg_ref[order_ref[p]], preferred_element_type=F32).astype(BF16)

        @pl.when(jnp.logical_and(p == N_SHARD - 1, t == nt - 1))
        def _():
            for j, (cx, cy) in enumerate(chips):
                blk = 2 * cx + cy
                remote(w_out_part(blk, c), w_out_part(blk, c), 6 + j, (cx, cy, c)).wait_recv()
                remote(w_out_part(blk, c), w_out_part(blk, c), 9 + j, sibling).start()
            for j, (cx, cy) in enumerate(chips):
                blk = 2 * cx + cy
                remote(w_out_part(blk, 1 - c), w_out_part(blk, 1 - c), 9 + j, sibling).wait_recv()
                remote(cw_ref, cwg_ref.at[blk], 12 + j, (cx, cy, c)).wait_recv()
            for cp in sends():
                cp.wait_send()
            keep(me, 0).wait()
            for j, (cx, cy) in enumerate(chips):
                keep(2 * cx + cy, j + 1).wait()

    hbm = pl.BlockSpec(memory_space=pl.ANY)
    return pl.pallas_call(
        body, name="rms_proj_gather",
        grid_spec=pltpu.PrefetchScalarGridSpec(
            num_scalar_prefetch=1, grid=(N_SHARD, nt),
            in_specs=[pl.BlockSpec((ts, D_MODEL), lambda p, t, o: (jnp.where(p == 0, t, nt - 1), 0)),
                      pl.BlockSpec((1, D_MODEL), lambda p, t, o: (0, 0)),
                      _whole_vmem(), _whole_vmem(), _whole_vmem()],
            out_specs=(pl.BlockSpec((ts, D_MODEL), lambda p, t, o: (jnp.where(p == 0, t, nt - 1), 0)),
                       pl.BlockSpec((ts, W_IN_COLS), lambda p, t, o: (t, o[p])),
                       hbm, hbm, hbm),
            scratch_shapes=[pltpu.VMEM((N_SHARD,) + w_in.shape, BF16), pltpu.VMEM(w_out.shape, BF16),
                            pltpu.VMEM((s, D_MODEL), BF16),
                            pltpu.SemaphoreType.DMA((15,)), pltpu.SemaphoreType.DMA((15,)),
                            pltpu.SemaphoreType.DMA((6,))]),
        out_shape=(jax.ShapeDtypeStruct((s, D_MODEL), BF16), jax.ShapeDtypeStruct((s, D_IN), BF16),
                   jax.ShapeDtypeStruct((N_SHARD,) + w_in.shape, BF16),
                   jax.ShapeDtypeStruct((N_SHARD,) + w_out.shape, BF16),
                   jax.ShapeDtypeStruct((N_SHARD,) + conv_w.shape, F32)),
        compiler_params=_cparams(("arbitrary", "arbitrary")),
    )(order, x, norm_g, w_in, w_out, conv_w)


def _halo_specs(ts, s, width, col_block):
    per = ts // HALO
    last = s // HALO - 1
    prev = pl.BlockSpec((HALO, width), lambda i: (jnp.maximum(i * per - 1, 0), col_block))
    nxt = pl.BlockSpec((HALO, width), lambda i: (jnp.minimum((i + 1) * per, last), col_block))
    return prev, nxt


def _layer_norm_stats(v):
    mu = jnp.mean(v, axis=-1, keepdims=True)
    vc = v - mu
    var = jnp.mean(vc * vc, axis=-1, keepdims=True)
    rstd = lax.rsqrt(var + EPS)
    return vc * rstd, rstd


def _fill_shifted(sh_ref, ext):
    n = ext.shape[0]
    sh_ref[0] = ext
    for r in range(1, SUB):
        sh_ref[r] = pltpu.roll(ext, n - r, axis=0)


def _branch_fwd(proj, cwg, conv_b, conv_ln_g, conv_ln_b, sgu_ln_g, sgu_ln_b, w_s, bs_t):
    s = proj.shape[0]
    ts = min(TS_FWD, s)
    nt = s // ts
    ra = min(ROWS_A, ts)

    def body(pm_ref, pp_ref, pn_ref, cw_ref, cb_ref, clg_ref, clb_ref, slg_ref, slb_ref, ws_ref, bst_ref,
             y_new, cv_ref, sh_ref):
        i = pl.program_id(0)
        keep_prev = (i > 0).astype(F32)
        keep_next = (i < nt - 1).astype(F32)

        for lb in range(C_BR // LANE):
            lanes = slice(lb * LANE, (lb + 1) * LANE)
            gates = slice(C_BR + lb * LANE, C_BR + (lb + 1) * LANE)

            def glu(ref):
                return ref[:, lanes].astype(F32) * _sigmoid(ref[:, gates].astype(F32))

            ext = jnp.concatenate([glu(pp_ref) * keep_prev, glu(pm_ref), glu(pn_ref) * keep_next], axis=0)
            _fill_shifted(sh_ref, ext)
            shard, off = divmod(lb * LANE, CONV_W_COLS)
            bias = cb_ref[:, lanes]

            def chunk(jc, carry):
                base = pl.multiple_of(jc * ra, ra)
                acc = jnp.zeros((ra, LANE), F32) + bias
                for k in range(KW):
                    o = k + 1
                    acc = acc + sh_ref[o % SUB, pl.ds(base + SUB * (o // SUB), ra), :] * cw_ref[
                        shard, k:k + 1, off:off + LANE]
                cv_ref[pl.ds(base, ra), lanes] = acc
                return carry

            lax.fori_loop(0, ts // ra, chunk, 0)

        lnh, _ = _layer_norm_stats(cv_ref[...])
        ln = lnh * clg_ref[...] + clb_ref[...]
        gc = pm_ref[:, 2 * C_BR:3 * C_BR].astype(F32)
        y_new[:, :C_BR] = (_silu(ln) * _silu(gc)).astype(BF16)

        vh, _ = _layer_norm_stats(pm_ref[:, 4 * C_BR:5 * C_BR].astype(F32))
        vn = (vh * slg_ref[...] + slb_ref[...]).astype(BF16)
        for hd in range(HEADS):
            w_h = ws_ref[hd].astype(BF16)
            b_h = bst_ref[:, hd:hd + 1]
            cols = slice(hd * HEAD_DIM, (hd + 1) * HEAD_DIM)
            for ch in range(ts // CHUNK):
                rows = slice(ch * CHUNK, (ch + 1) * CHUNK)
                mixed = jnp.dot(w_h, vn[rows, cols], preferred_element_type=F32) + b_h
                u = pm_ref[rows, 3 * C_BR + hd * HEAD_DIM:3 * C_BR + (hd + 1) * HEAD_DIM].astype(F32)
                gs = pm_ref[rows, 5 * C_BR + hd * HEAD_DIM:5 * C_BR + (hd + 1) * HEAD_DIM].astype(F32)
                y_new[rows, C_BR + hd * HEAD_DIM:C_BR + (hd + 1) * HEAD_DIM] = (
                    u * mixed * _silu(gs)).astype(BF16)

    prev, nxt = _halo_specs(ts, s, 2 * C_BR, 0)
    row = pl.BlockSpec((1, C_BR), lambda i: (0, 0))
    return pl.pallas_call(
        body, name="branch_fwd",
        grid=(nt,),
        out_shape=(jax.ShapeDtypeStruct((s, 2 * C_BR), BF16), jax.ShapeDtypeStruct((s, C_BR), F32)),
        in_specs=[pl.BlockSpec((ts, D_IN), lambda i: (i, 0)), prev, nxt,
                  _whole_vmem(), row, row, row, row, row, _whole_vmem(), _whole_vmem()],
        out_specs=(pl.BlockSpec((ts, 2 * C_BR), lambda i: (i, 0)),
                   pl.BlockSpec((ts, C_BR), lambda i: (i, 0))),
        scratch_shapes=[pltpu.VMEM((SUB, ts + 2 * HALO, LANE), F32)],
        compiler_params=_cparams(("parallel",)),
    )(proj, proj, proj, cwg, conv_b, conv_ln_g, conv_ln_b, sgu_ln_g, sgu_ln_b, w_s, bs_t)


def _out_proj(y, x, target, woutg, final_g):
    s = x.shape[0]
    ts = min(TS_OUT, s)
    nt = s // ts

    def body(y_ref, x_ref, t_ref, w_ref, g_ref, dx2_ref, dy_ref, gw_ref, gf_ref, se_ref):
        i = pl.program_id(0)

        @pl.when(i == 0)
        def _():
            gw_ref[...] = jnp.zeros_like(gw_ref)
            gf_ref[...] = jnp.zeros_like(gf_ref)
            se_ref[...] = jnp.zeros_like(se_ref)

        yt = y_ref[...]
        x2 = x_ref[...]
        for k in range(N_SHARD):
            x2 = x2 + jnp.dot(yt[:, k * W_OUT_ROWS:(k + 1) * W_OUT_ROWS], w_ref[k], preferred_element_type=F32)
        r2 = lax.rsqrt(jnp.mean(x2 * x2, axis=-1, keepdims=True) + EPS)
        n2 = x2 * r2
        g = g_ref[...]
        diff = n2 * g - t_ref[...]
        se_ref[...] += _fold8(diff * diff)
        dout = diff * (1.0 / D_MODEL)
        gf_ref[...] += _fold8(dout * n2)
        dn = dout * g
        dx2 = r2 * (dn - n2 * jnp.mean(dn * n2, axis=-1, keepdims=True))
        dx2_ref[...] = dx2
        dxb = dx2.astype(BF16)
        for k in range(N_SHARD):
            rows = slice(k * W_OUT_ROWS, (k + 1) * W_OUT_ROWS)
            dy_ref[:, rows] = _dot_nt(dxb, w_ref[k]).astype(BF16)
            gw_ref[rows, :] += _dot_tn(yt[:, rows], dxb)

        @pl.when(i == nt - 1)
        def _():
            gf_ref[...] = jnp.broadcast_to(jnp.sum(gf_ref[...], axis=0, keepdims=True), gf_ref.shape)

    tile = pl.BlockSpec((ts, D_MODEL), lambda i: (i, 0))
    wide = pl.BlockSpec((ts, 2 * C_BR), lambda i: (i, 0))
    acc8 = pl.BlockSpec((SUB, D_MODEL), lambda i: (0, 0))
    return pl.pallas_call(
        body, name="out_proj",
        grid=(nt,),
        out_shape=(jax.ShapeDtypeStruct((s, D_MODEL), F32), jax.ShapeDtypeStruct((s, 2 * C_BR), BF16),
                   jax.ShapeDtypeStruct((2 * C_BR, D_MODEL), F32),
                   jax.ShapeDtypeStruct((SUB, D_MODEL), F32), jax.ShapeDtypeStruct((SUB, D_MODEL), F32)),
        in_specs=[wide, tile, tile, _whole_vmem(), pl.BlockSpec((1, D_MODEL), lambda i: (0, 0))],
        out_specs=(tile, wide, pl.BlockSpec((2 * C_BR, D_MODEL), lambda i: (0, 0)), acc8, acc8),
        compiler_params=_cparams(("arbitrary",)),
    )(y, x, target, woutg, final_g)


def _dsilu(v, sg):
    return sg * (1.0 + v * (1.0 - sg))


def _branch_bwd(proj, dy, cv, cwg, conv_ln_g, conv_ln_b, sgu_ln_g, sgu_ln_b, w_s, ws_t, bs_t, gf8, se8, dep):
    s = proj.shape[0]
    ts = min(TS_BWD, s)
    nt = s // ts
    ra = min(ROWS_A, ts)
    rb = min(ROWS_B, ts)
    te = ts + 2 * HALO

    def body(pm_ref, dym_ref, cvm_ref, gcp_ref, gcn_ref, dyp_ref, dyn_ref, cvp_ref, cvn_ref,
             cw_ref, clg_ref, clb_ref, slg_ref, slb_ref, ws_ref, wst_ref, bst_ref, gf_ref, se_ref, dep_ref,
             dp_ref, small_ref, gws_ref,
             sh_ref, glu_ref, dgl_ref, dcv_ref, acc_ref, gcw_ref, gbs_ref):
        i = pl.program_id(0)

        @pl.when(i == 0)
        def _():
            acc_ref[...] = jnp.zeros_like(acc_ref)
            gcw_ref[...] = jnp.zeros_like(gcw_ref)
            gbs_ref[...] = jnp.zeros_like(gbs_ref)
            gws_ref[...] = jnp.zeros_like(gws_ref)

        def ext(prev_ref, main, next_ref):
            return jnp.concatenate([prev_ref[...].astype(F32), main, next_ref[...].astype(F32)], axis=0)

        main = slice(HALO, HALO + ts)

        hm = ts // 2
        clg = clg_ref[...]
        for half in range(2):
            rows_m = slice(half * hm, (half + 1) * hm)
            halo_refs = (cvp_ref, gcp_ref, dyp_ref) if half == 0 else (cvn_ref, gcn_ref, dyn_ref)

            def piece(halo_ref, main_val):
                halo = halo_ref[...].astype(F32)
                return jnp.concatenate([halo, main_val] if half == 0 else [main_val, halo], axis=0)

            cv_e = piece(halo_refs[0], cvm_ref[rows_m, :])
            gc_e = piece(halo_refs[1], pm_ref[rows_m, 2 * C_BR:3 * C_BR].astype(F32))
            dyc_e = piece(halo_refs[2], dym_ref[rows_m, :C_BR].astype(F32))
            mn = slice(HALO, HALO + hm) if half == 0 else slice(0, hm)
            lnh, rstd = _layer_norm_stats(cv_e)
            ln = lnh * clg + clb_ref[...]
            sg_ln = _sigmoid(ln)
            sg_gc = _sigmoid(gc_e)
            d_ln = dyc_e * gc_e * sg_gc * _dsilu(ln, sg_ln)
            dp_ref[rows_m, 2 * C_BR:3 * C_BR] = (
                dyc_e[mn] * ln[mn] * sg_ln[mn] * _dsilu(gc_e[mn], sg_gc[mn])).astype(BF16)
            dlnh = d_ln * clg
            d_cv = rstd * (dlnh - jnp.mean(dlnh, axis=-1, keepdims=True)
                           - lnh * jnp.mean(dlnh * lnh, axis=-1, keepdims=True))
            row = lax.broadcasted_iota(jnp.int32, (hm + HALO, 1), 0)
            if half == 0:
                valid = jnp.logical_or(row >= HALO, i > 0)
            else:
                valid = jnp.logical_or(row < hm, i < nt - 1)
            d_cv = jnp.where(valid, d_cv, 0.0)
            dcv_ref[half * (hm + HALO):(half + 1) * (hm + HALO), :] = d_cv
            acc_ref[0:8, :] += _fold8(d_cv[mn])
            acc_ref[8:16, :] += _fold8(d_ln[mn] * lnh[mn])
            acc_ref[16:24, :] += _fold8(d_ln[mn])

        for lb in range(C_BR // LANE):
            lanes = slice(lb * LANE, (lb + 1) * LANE)
            gates = slice(C_BR + lb * LANE, C_BR + (lb + 1) * LANE)
            shard, off = divmod(lb * LANE, CONV_W_COLS)
            av = pm_ref[:, lanes].astype(F32)
            sg = _sigmoid(pm_ref[:, gates].astype(F32))
            glu_ref[...] = av * sg
            _fill_shifted(sh_ref, dcv_ref[:, lanes])

            def chunk_a(jc, carry):
                base = pl.multiple_of(jc * ra, ra)
                acc = jnp.zeros((ra, LANE), F32)
                for j in range(KW):
                    o = j + 1
                    acc = acc + sh_ref[o % SUB, pl.ds(base + SUB * (o // SUB), ra), :] * cw_ref[
                        shard, KW - 1 - j:KW - j, off:off + LANE]
                dgl_ref[pl.ds(base, ra), :] = acc
                return carry

            lax.fori_loop(0, ts // ra, chunk_a, 0)

            def chunk_b(jc, accs):
                base = pl.multiple_of(jc * rb, rb)
                g = glu_ref[pl.ds(base, rb), :]
                out = []
                for j in range(KW):
                    o = j + 1
                    d = sh_ref[o % SUB, pl.ds(base + SUB * (o // SUB), rb), :]
                    out.append(accs[j] + _fold8(g * d))
                return tuple(out)

            accs = lax.fori_loop(0, ts // rb, chunk_b, tuple(jnp.zeros((SUB, LANE), F32) for _ in range(KW)))
            for j in range(KW):
                gcw_ref[j * SUB:(j + 1) * SUB, lanes] += accs[j]

            dglu = dgl_ref[...]
            dp_ref[:, lanes] = (dglu * sg).astype(BF16)
            dp_ref[:, gates] = (dglu * av * sg * (1.0 - sg)).astype(BF16)

        vh, vrstd = _layer_norm_stats(pm_ref[:, 4 * C_BR:5 * C_BR].astype(F32))
        slg = slg_ref[...]
        vn = (vh * slg + slb_ref[...]).astype(BF16)
        for hd in range(HEADS):
            w_h = ws_ref[hd].astype(BF16)
            wt_h = wst_ref[hd].astype(BF16)
            b_h = bst_ref[:, hd:hd + 1]
            cols = slice(hd * HEAD_DIM, (hd + 1) * HEAD_DIM)
            gws_h = jnp.zeros((CHUNK, CHUNK), F32)
            gbs_h = jnp.zeros((CHUNK, HEAD_DIM), F32)
            for ch in range(ts // CHUNK):
                rows = slice(ch * CHUNK, (ch + 1) * CHUNK)
                vn_b = vn[rows, cols]
                mixed = jnp.dot(w_h, vn_b, preferred_element_type=F32) + b_h
                u = pm_ref[rows, 3 * C_BR + hd * HEAD_DIM:3 * C_BR + (hd + 1) * HEAD_DIM].astype(F32)
                gs = pm_ref[rows, 5 * C_BR + hd * HEAD_DIM:5 * C_BR + (hd + 1) * HEAD_DIM].astype(F32)
                dys = dym_ref[rows, C_BR + hd * HEAD_DIM:C_BR + (hd + 1) * HEAD_DIM].astype(F32)
                sg_gs = _sigmoid(gs)
                silu_gs = gs * sg_gs
                dp_ref[rows, 3 * C_BR + hd * HEAD_DIM:3 * C_BR + (hd + 1) * HEAD_DIM] = (
                    dys * mixed * silu_gs).astype(BF16)
                dp_ref[rows, 5 * C_BR + hd * HEAD_DIM:5 * C_BR + (hd + 1) * HEAD_DIM] = (
                    dys * u * mixed * _dsilu(gs, sg_gs)).astype(BF16)
                d_mixed = dys * u * silu_gs
                dm_b = d_mixed.astype(BF16)
                gws_h = gws_h + _dot_nt(dm_b, vn_b)
                gbs_h = gbs_h + d_mixed
                dcv_ref[HALO + ch * CHUNK:HALO + (ch + 1) * CHUNK, cols] = jnp.dot(
                    wt_h, dm_b, preferred_element_type=F32)
            gws_ref[hd] += gws_h
            gbs_ref[:, cols] += gbs_h
        d_vn = dcv_ref[main, :]
        acc_ref[24:32, :] += _fold8(d_vn * vh)
        acc_ref[32:40, :] += _fold8(d_vn)
        dvh = d_vn * slg
        dp_ref[:, 4 * C_BR:5 * C_BR] = (vrstd * (
            dvh - jnp.mean(dvh, axis=-1, keepdims=True)
            - vh * jnp.mean(dvh * vh, axis=-1, keepdims=True))).astype(BF16)

        @pl.when(i == nt - 1)
        def _():
            small_ref[...] = jnp.zeros_like(small_ref)
            for a in range(5):
                small_ref[1 + a:2 + a, :] = jnp.sum(acc_ref[a * SUB:(a + 1) * SUB, :], axis=0, keepdims=True)
            ones = jnp.ones((SUB, HEAD_DIM), F32)
            for hd in range(HEADS):
                cols = slice(hd * HEAD_DIM, (hd + 1) * HEAD_DIM)
                rowsum = lax.dot_general(ones, gbs_ref[:, cols], (((1,), (1,)), ((), ())),
                                         precision=lax.Precision.HIGHEST, preferred_element_type=F32)
                small_ref[6:7, cols] = rowsum[0:1, :]
            small_ref[7:8, :] = gf_ref[0:1, :]
            small_ref[0:1, :] = jnp.sum(se_ref[...], axis=0, keepdims=True)
            for k in range(KW):
                j = KW - 1 - k
                small_ref[8 + k:9 + k, :] = jnp.sum(gcw_ref[j * SUB:(j + 1) * SUB, :], axis=0, keepdims=True)

    gc_prev, gc_next = _halo_specs(ts, s, C_BR, 2)
    lo_prev, lo_next = _halo_specs(ts, s, C_BR, 0)
    row = pl.BlockSpec((1, C_BR), lambda i: (0, 0))
    return pl.pallas_call(
        body, name="branch_bwd",
        grid=(nt,),
        out_shape=(jax.ShapeDtypeStruct((s, D_IN), BF16), jax.ShapeDtypeStruct((40, C_BR), F32),
                   jax.ShapeDtypeStruct((HEADS, CHUNK, CHUNK), F32)),
        in_specs=[pl.BlockSpec((ts, D_IN), lambda i: (i, 0)),
                  pl.BlockSpec((ts, 2 * C_BR), lambda i: (i, 0)),
                  pl.BlockSpec((ts, C_BR), lambda i: (i, 0)),
                  gc_prev, gc_next, lo_prev, lo_next, lo_prev, lo_next,
                  _whole_vmem(), row, row, row, row, _whole_vmem(), _whole_vmem(), _whole_vmem(),
                  _whole_vmem(), _whole_vmem(), _whole_vmem()],
        out_specs=(pl.BlockSpec((ts, D_IN), lambda i: (i, 0)),
                   pl.BlockSpec((40, C_BR), lambda i: (0, 0)),
                   pl.BlockSpec((HEADS, CHUNK, CHUNK), lambda i: (0, 0, 0))),
        scratch_shapes=[pltpu.VMEM((SUB, te, LANE), F32),
                        pltpu.VMEM((ts, LANE), F32),
                        pltpu.VMEM((ts, LANE), F32),
                        pltpu.VMEM((te, C_BR), F32),
                        pltpu.VMEM((5 * SUB, C_BR), F32),
                        pltpu.VMEM((KW * SUB, C_BR), F32),
                        pltpu.VMEM((CHUNK, C_BR), F32)],
        compiler_params=_cparams(("arbitrary",)),
    )(proj, dy, cv, proj, proj, dy, dy, cv, cv,
      cwg, conv_ln_g, conv_ln_b, sgu_ln_g, sgu_ln_b, w_s, ws_t, bs_t, gf8, se8, dep)


def _in_bwd(dproj, x, dx2, norm_g, wing, dep, first_tile, n_tiles, gx_prev=None, gn_prev=None):
    s = x.shape[0]
    ts = min(TS_INB, s)

    def body(*refs):
        dp_ref, x_ref, dx2_ref, g_ref, w_hbm = refs[:5]
        gx_ref, gn_ref, acc_ref, w_ref, w_sems = refs[-5:]
        i = pl.program_id(0)

        def w_copy(k):
            return pltpu.make_async_copy(w_hbm.at[k], w_ref.at[k], w_sems.at[k])

        @pl.when(i == 0)
        def _():
            acc_ref[...] = jnp.zeros_like(acc_ref)
            for k in range(N_SHARD):
                w_copy(k).start()

        dh = jnp.zeros((ts, D_MODEL), F32)
        for k in range(N_SHARD):
            @pl.when(i == 0)
            def _():
                w_copy(k).wait()

            dh = dh + _dot_nt(dp_ref[:, k * W_IN_COLS:(k + 1) * W_IN_COLS], w_ref[k])
        xt = x_ref[...]
        r = lax.rsqrt(jnp.mean(xt * xt, axis=-1, keepdims=True) + EPS)
        n = xt * r
        acc_ref[...] += _fold8(dh * n)
        dn = dh * g_ref[...]
        gx_ref[...] = dx2_ref[...] + r * (dn - n * jnp.mean(dn * n, axis=-1, keepdims=True))

        @pl.when(i == n_tiles - 1)
        def _():
            total = jnp.broadcast_to(jnp.sum(acc_ref[...], axis=0, keepdims=True), gn_ref.shape)
            if gn_prev is not None:
                total = total + refs[7][...]
            gn_ref[...] = total

    tile = pl.BlockSpec((ts, D_MODEL), lambda i: (i + first_tile, 0))
    in_specs = [pl.BlockSpec((ts, D_IN), lambda i: (i + first_tile, 0)), tile, tile,
                pl.BlockSpec((1, D_MODEL), lambda i: (0, 0)), pl.BlockSpec(memory_space=pl.ANY),
                pl.BlockSpec(memory_space=pl.ANY)]
    operands = [dproj, x, dx2, norm_g, wing, dep]
    aliases = {}
    if gx_prev is not None:
        in_specs += [pl.BlockSpec(memory_space=pl.ANY), _whole_vmem()]
        operands += [gx_prev, gn_prev]
        aliases = {6: 0}
    return pl.pallas_call(
        body, name="in_bwd_%d" % first_tile,
        grid=(n_tiles,),
        out_shape=(jax.ShapeDtypeStruct((s, D_MODEL), F32), jax.ShapeDtypeStruct((SUB, D_MODEL), F32)),
        in_specs=in_specs,
        out_specs=(tile, pl.BlockSpec((SUB, D_MODEL), lambda i: (0, 0))),
        scratch_shapes=[pltpu.VMEM((SUB, D_MODEL), F32), pltpu.VMEM(wing.shape, BF16),
                        pltpu.SemaphoreType.DMA((N_SHARD,))],
        input_output_aliases=aliases,
        compiler_params=_cparams(("arbitrary",)),
    )(*operands)


def _grad_w_in(h, dproj, dep):
    s = h.shape[0]
    tk = min(TK_GW, s)
    nk = s // tk
    half = D_MODEL // 2

    def body(h_ref, dp_ref, dep_ref, o_ref, ob_ref):
        @pl.when(pl.program_id(1) == 0)
        def _():
            o_ref[...] = jnp.zeros_like(o_ref)

        o_ref[0] += _dot_tn(h_ref[...], dp_ref[...]).reshape(2, half, W_IN_COLS)

        @pl.when(pl.program_id(1) == nk - 1)
        def _():
            ob_ref[...] = o_ref[...].astype(BF16)

    shard = pl.BlockSpec((1, 2, half, W_IN_COLS), lambda k, t: (k, 0, 0, 0))
    return pl.pallas_call(
        body, name="grad_w_in",
        grid=(N_SHARD, nk),
        out_shape=(jax.ShapeDtypeStruct((N_SHARD, 2, half, W_IN_COLS), F32),
                   jax.ShapeDtypeStruct((N_SHARD, 2, half, W_IN_COLS), BF16)),
        in_specs=[pl.BlockSpec((tk, D_MODEL), lambda k, t: (t, 0)),
                  pl.BlockSpec((tk, W_IN_COLS), lambda k, t: (t, k)), _whole_vmem()],
        out_specs=(shard, shard),
        compiler_params=_cparams(("parallel", "arbitrary")),
    )(h, dproj, dep)


HBM_SPEC = pl.BlockSpec(memory_space=pltpu.HBM)
SEM_SPEC = pl.BlockSpec(memory_space=pltpu.SEMAPHORE)
SIDE_EFFECT = pltpu.SideEffectType.DATAFLOW_SIDE_EFFECTING


def _remote_copies(plan, bufs, send_sems, recv_sems):
    x, y, c = _mesh_pos()
    return [pltpu.make_async_remote_copy(src_ref=src, dst_ref=dst, send_sem=send_sems.at[k],
                                         recv_sem=recv_sems.at[k], device_id=dev, device_id_type=MESH)
            for k, (src, dst, dev) in enumerate(plan(x, y, c, *bufs))]


def _start_copies(name, bufs, n_copies, plan):
    n = len(bufs)

    def body(*refs):
        for cp in _remote_copies(plan, refs[:n], refs[n], refs[n + 1]):
            cp.start()
        refs[-1][...] = jnp.zeros_like(refs[-1])

    outs = pl.pallas_call(
        body, name=name,
        out_shape=(pltpu.SemaphoreType.DMA((n_copies,)), pltpu.SemaphoreType.DMA((n_copies,)),
                   *[pltpu.HBM(b.shape, b.dtype) for b in bufs], jax.ShapeDtypeStruct((SUB, LANE), F32)),
        in_specs=[HBM_SPEC] * n,
        out_specs=(SEM_SPEC, SEM_SPEC, *[HBM_SPEC] * n, _whole_vmem()),
        input_output_aliases={i: 2 + i for i in range(n)},
        compiler_params=pltpu.CompilerParams(has_side_effects=SIDE_EFFECT),
    )(*[pltpu.with_memory_space_constraint(b, pltpu.HBM) for b in bufs])
    return outs[0], outs[1], list(outs[2:2 + n]), outs[-1]


def _wait_copies(name, send_sems, recv_sems, bufs, plan, after):
    n = len(bufs)

    def body(*refs):
        for cp in _remote_copies(plan, refs[:n], refs[n], refs[n + 1]):
            cp.wait_send()
            cp.wait_recv()

    outs = pl.pallas_call(
        body, name=name,
        out_shape=tuple(pltpu.HBM(b.shape, b.dtype) for b in bufs),
        in_specs=[HBM_SPEC] * n + [SEM_SPEC, SEM_SPEC, pl.BlockSpec(memory_space=pl.ANY)],
        out_specs=(HBM_SPEC,) * n,
        input_output_aliases={i: i for i in range(n)},
        compiler_params=pltpu.CompilerParams(has_side_effects=SIDE_EFFECT),
    )(*bufs, send_sems, recv_sems, after)
    return list(outs)


def _landing(shape, dtype):
    return lax.empty(shape, dtype)


def _plan_pair_exchange(x, y, c, g, r):
    return [(g.at[k, 1 - c], r.at[k], (x, y, 1 - c)) for k in range(N_SHARD)]


def _plan_chip_exchange(x, y, c, a, r):
    chips = [(1 - x, y), (x, 1 - y), (1 - x, 1 - y)]
    return [(a.at[2 * cx + cy], r.at[j], (cx, cy, c)) for j, (cx, cy) in enumerate(chips)]


def _plan_pair_gather(x, y, c, f):
    return [(f.at[c], f.at[c], (x, y, 1 - c))]


def _plan_all_gather(x, y, c, own, land):
    me = 4 * x + 2 * y + c
    flip = lambda v, bit: 1 - v if bit else v
    return [(own, land.at[me], (flip(x, m >> 2 & 1), flip(y, m >> 1 & 1), flip(c, m & 1))) for m in range(1, 8)]


def _add_pair(g, r, pos, name):
    _, _, rows, cols = g.shape
    tr = min(256, rows)

    def body(pos_ref, g_ref, r_ref, o_ref, ob_ref):
        v = g_ref[0] + r_ref[...].astype(F32)
        ob_ref[...] = v.astype(BF16)

        @pl.when(pl.program_id(1) == pos_ref[0])
        def _():
            o_ref[...] = v[0]

    blk = pl.BlockSpec((1, tr, cols), lambda t, k, pos_ref: (k, t, 0))
    return pl.pallas_call(
        body, name=name,
        grid_spec=pltpu.PrefetchScalarGridSpec(
            num_scalar_prefetch=1, grid=(rows // tr, N_SHARD),
            in_specs=[pl.BlockSpec((1, 1, tr, cols), lambda t, k, pos_ref: (k, pos_ref[1], t, 0)), blk],
            out_specs=(pl.BlockSpec((tr, cols), lambda t, k, pos_ref: (t, 0)), blk)),
        out_shape=(jax.ShapeDtypeStruct((rows, cols), F32), jax.ShapeDtypeStruct((N_SHARD, rows, cols), BF16)),
        compiler_params=_cparams(("parallel", "arbitrary")),
    )(pos, g, r)


def _add_chips(a, r, pos, name):
    rows, cols = a.shape
    tr = min(256, rows)

    def body(pos_ref, a_ref, r_ref, o_ref):
        o_ref[0] = ((a_ref[...] + r_ref[0].astype(F32)) + r_ref[1].astype(F32)) + r_ref[2].astype(F32)

    return pl.pallas_call(
        body, name=name,
        grid_spec=pltpu.PrefetchScalarGridSpec(
            num_scalar_prefetch=1, grid=(rows // tr,),
            in_specs=[pl.BlockSpec((tr, cols), lambda t, pos_ref: (t, 0)),
                      pl.BlockSpec((3, tr, cols), lambda t, pos_ref: (0, t, 0))],
            out_specs=pl.BlockSpec((1, tr, cols), lambda t, pos_ref: (pos_ref[1], t, 0))),
        out_shape=jax.ShapeDtypeStruct((2, rows, cols), F32),
        compiler_params=_cparams(("parallel",)),
    )(pos, a, r)


def _sum_slots(pos_ref, own, land_ref, rows):
    me = pos_ref[2]
    total = None
    for d in range(8):
        term = jnp.where(me == d, own, land_ref[d] if rows is None else land_ref[d, rows, :])
        total = term if total is None else total + term
    return total


def _sum_small(pos, small, small_land, gws, gws_land):
    def body(pos_ref, sm_ref, sml_ref, gw_ref, gwl_ref, o_sm, o_gw):
        o_sm[...] = _sum_slots(pos_ref, sm_ref[...], sml_ref, None)
        o_gw[...] = _sum_slots(pos_ref, gw_ref[...], gwl_ref, None)

    return pl.pallas_call(
        body, name="sum_small",
        grid_spec=pltpu.PrefetchScalarGridSpec(
            num_scalar_prefetch=1, grid=(1,),
            in_specs=[_whole_vmem()] * 4, out_specs=[_whole_vmem()] * 2),
        out_shape=[jax.ShapeDtypeStruct(small.shape, F32), jax.ShapeDtypeStruct(gws.shape, F32)],
        compiler_params=_cparams(("arbitrary",)),
    )(pos, small, small_land, gws, gws_land)


def _adamw_math(w, g, m, v):
    m = ADAM_B1 * m + (1.0 - ADAM_B1) * g
    v = ADAM_B2 * v + (1.0 - ADAM_B2) * (g * g)
    m_hat = m / (1.0 - ADAM_B1 ** ADAM_STEP)
    v_hat = v / (1.0 - ADAM_B2 ** ADAM_STEP)
    delta = -ADAM_LR * (m_hat / (jnp.sqrt(v_hat) + ADAM_EPS) + ADAM_WD * w)
    return delta, m, v


def _adamw_large(w, g, m, v, dep, name):
    rows, cols = w.shape
    tr = min(256, rows)

    def body(w_ref, g_ref, m_ref, v_ref, dep_ref, d_ref, nm_ref, nv_ref):
        d_ref[...], nm_ref[...], nv_ref[...] = _adamw_math(w_ref[...], g_ref[...], m_ref[...], v_ref[...])

    tile = pl.BlockSpec((tr, cols), lambda t: (t, 0))
    return pl.pallas_call(
        body, name=name,
        grid=(rows // tr,),
        out_shape=(jax.ShapeDtypeStruct(w.shape, F32),) * 3,
        in_specs=[tile] * 4 + [_whole_vmem()], out_specs=(tile,) * 3,
        compiler_params=_cparams(("parallel",)),
    )(w, g, m, v, dep)


_ROW_OF = {"conv_b": 1, "conv_ln_g": 2, "conv_ln_b": 3, "sgu_ln_g": 4, "sgu_ln_b": 5, "b_s": 6, "final_g": 7}
_CONV_W_ROW = 8


def _adamw_small(call_name, names, grads, pos, params):
    def body(pos_ref, p_ref, q_ref, *refs):
        gn_ref, gnl_ref = p_ref, q_ref
        n_in = 3 * len(names)
        ins, outs = refs[:n_in], refs[n_in:]
        me = pos_ref[0]
        for a, name in enumerate(names):
            w_ref, m_ref, v_ref = ins[3 * a:3 * a + 3]
            if name == "conv_w":
                g = jnp.zeros((KW, CONV_W_COLS), F32)
                for k in range(N_SHARD):
                    blk = p_ref[_CONV_W_ROW:_CONV_W_ROW + KW, k * CONV_W_COLS:(k + 1) * CONV_W_COLS]
                    g = jnp.where(me == k, blk, g)
            elif name == "w_s":
                g = q_ref[...]
            elif name == "norm_g":
                g = _sum_slots(pos_ref, gn_ref[0:1, :], gnl_ref, slice(0, 1))
            else:
                g = p_ref[_ROW_OF[name]:_ROW_OF[name] + 1, :]
            delta, nm, nv = _adamw_math(w_ref[...], g, m_ref[...], v_ref[...])
            for o_ref, val in zip(outs[4 * a:4 * a + 4], (g, delta, nm, nv)):
                o_ref[...] = val

    operands, shapes = [], []
    for name in names:
        operands += list(params[name])
        shapes += [jax.ShapeDtypeStruct(params[name][0].shape, F32)] * 4
    outs = pl.pallas_call(
        body, name=call_name,
        grid_spec=pltpu.PrefetchScalarGridSpec(
            num_scalar_prefetch=1, grid=(1,),
            in_specs=[_whole_vmem()] * (2 + len(operands)),
            out_specs=[_whole_vmem()] * len(shapes)),
        out_shape=shapes,
        compiler_params=_cparams(("arbitrary",)),
    )(pos, *grads, *operands)
    return {name: tuple(outs[4 * a:4 * a + 4]) for a, name in enumerate(names)}


def kernel(x, norm_g, w_in, conv_w, conv_b, conv_ln_g, conv_ln_b, sgu_ln_g, sgu_ln_b, w_s, b_s, w_out, final_g, loss_target, m_norm_g, m_w_in, m_conv_w, m_conv_b, m_conv_ln_g, m_conv_ln_b, m_sgu_ln_g, m_sgu_ln_b, m_w_s, m_b_s, m_w_out, m_final_g, v_norm_g, v_w_in, v_conv_w, v_conv_b, v_conv_ln_g, v_conv_ln_b, v_sgu_ln_g, v_sgu_ln_b, v_w_s, v_b_s, v_w_out, v_final_g):
    xi, yi, ci = _mesh_pos()
    pos = jnp.stack([2 * xi + yi, ci, 4 * xi + 2 * yi + ci]).astype(jnp.int32)

    x2d = x[0]
    tgt = loss_target[0]
    fg = final_g.reshape(1, D_MODEL)
    ws3 = w_s[0]
    ws_t = jnp.swapaxes(ws3, 1, 2)
    bs_t = jnp.transpose(b_s[0])

    chip = 2 * xi + yi
    order = jnp.stack([chip, 2 * (1 - xi) + yi, 2 * xi + 1 - yi, 2 * (1 - xi) + 1 - yi]).astype(jnp.int32)
    h, proj, wing, woutg, cwg = _rms_proj_gather(x2d, norm_g, w_in[0], w_out[0], conv_w[0], order)
    y, cv = _branch_fwd(proj, cwg, conv_b, conv_ln_g, conv_ln_b, sgu_ln_g, sgu_ln_b, ws3, bs_t)
    dx2, dy, gwout, gf8, se8 = _out_proj(y, x2d, tgt, woutg, fg)
    in_rows, out_rows = D_MODEL // 2, W_OUT_ROWS // 2
    gwout4 = gwout.reshape(N_SHARD, 2, out_rows, D_MODEL)
    ss, rs, (gwout4, r1_out), tok = _start_copies(
        "start_pair_exchange_w_out", [gwout4, _landing((N_SHARD, out_rows, D_MODEL), F32)], N_SHARD,
        _plan_pair_exchange)
    dproj, small, gws3 = _branch_bwd(proj, dy, cv, cwg, conv_ln_g, conv_ln_b, sgu_ln_g, sgu_ln_b, ws3, ws_t, bs_t,
                                     gf8, se8, tok)
    gws = gws3.reshape(HEADS * CHUNK, CHUNK)
    gwout4, r1_out = _wait_copies("wait_pair_exchange_w_out", ss, rs, [gwout4, r1_out], _plan_pair_exchange, dproj)
    a_out, a_out_bf = _add_pair(gwout4, r1_out, pos, "add_pair_w_out")

    def plan_b(x, y, c, a, r, sm, sml, gw, gwl):
        return (_plan_chip_exchange(x, y, c, a, r) + _plan_all_gather(x, y, c, sm, sml)
                + _plan_all_gather(x, y, c, gw, gwl))

    ss, rs, bufs_b, tok = _start_copies(
        "start_chip_exchange_w_out",
        [a_out_bf, _landing((3, out_rows, D_MODEL), BF16), small, _landing((8,) + small.shape, F32),
         gws, _landing((8,) + gws.shape, F32)], 3 + 7 + 7, plan_b)
    gwin, gwin_bf = _grad_w_in(h, dproj, tok)
    ss_c, rs_c, (gwin_bf, r1_in), tok = _start_copies(
        "start_pair_exchange_w_in", [gwin_bf, _landing((N_SHARD, in_rows, W_IN_COLS), BF16)], N_SHARD,
        _plan_pair_exchange)
    nt = x2d.shape[0] // min(TS_INB, x2d.shape[0])
    cut = nt // 2
    gx_a, gn_a = _in_bwd(dproj, x2d, dx2, norm_g, wing, tok, 0, cut)

    gwin_bf, r1_in = _wait_copies("wait_pair_exchange_w_in", ss_c, rs_c, [gwin_bf, r1_in], _plan_pair_exchange, gx_a)
    a_in, a_in_bf = _add_pair(gwin, r1_in, pos, "add_pair_w_in")
    a_out_bf, r2_out, small, small_land, gws, gws_land = _wait_copies(
        "wait_chip_exchange_w_out", ss, rs, bufs_b, plan_b, gx_a)
    f_out = _add_chips(a_out, r2_out, pos, "add_chips_w_out")
    p, q = _sum_small(pos, small, small_land, gws, gws_land)
    loss = (0.5 / D_MODEL) * jnp.sum(p[0])

    flat = lambda a: a.reshape(1, C_BR)
    flat_ws = lambda a: a.reshape(HEADS * CHUNK, CHUNK)
    params = {
        "norm_g": (norm_g, m_norm_g, v_norm_g),
        "conv_b": (conv_b, m_conv_b, v_conv_b),
        "conv_ln_g": (conv_ln_g, m_conv_ln_g, v_conv_ln_g),
        "conv_ln_b": (conv_ln_b, m_conv_ln_b, v_conv_ln_b),
        "sgu_ln_g": (sgu_ln_g, m_sgu_ln_g, v_sgu_ln_g),
        "sgu_ln_b": (sgu_ln_b, m_sgu_ln_b, v_sgu_ln_b),
        "b_s": (flat(b_s), flat(m_b_s), flat(v_b_s)),
        "final_g": (flat(final_g), flat(m_final_g), flat(v_final_g)),
        "conv_w": (conv_w[0], m_conv_w[0], v_conv_w[0]),
        "w_s": (flat_ws(w_s), flat_ws(m_w_s), flat_ws(v_w_s)),
    }
    res = _adamw_small("adamw_small", [n for n in params if n != "norm_g"], (p, q), pos, params)

    def plan_d(x, y, c, a, r, f):
        return _plan_chip_exchange(x, y, c, a, r) + _plan_pair_gather(x, y, c, f)

    ss, rs, bufs_d, tok = _start_copies(
        "start_chip_exchange_w_in", [a_in_bf, _landing((3, in_rows, W_IN_COLS), BF16), f_out], 3 + 1, plan_d)
    grad_x, gn8 = _in_bwd(dproj, x2d, dx2, norm_g, wing, tok, cut, nt - cut, gx_a, gn_a)
    a_in_bf, r2_in, f_out = _wait_copies("wait_chip_exchange_w_in", ss, rs, bufs_d, plan_d, grad_x)
    f_in = _add_chips(a_in, r2_in, pos, "add_chips_w_in")

    def plan_e(x, y, c, f, gn, gnl):
        return _plan_pair_gather(x, y, c, f) + _plan_all_gather(x, y, c, gn, gnl)

    ss, rs, bufs_e, tok = _start_copies(
        "start_pair_gather_w_in", [f_in, gn8, _landing((8,) + gn8.shape, F32)], 1 + 7, plan_e)
    g_w_out = f_out.reshape(W_OUT_ROWS, D_MODEL)
    d_w_out, nm_w_out, nv_w_out = _adamw_large(w_out[0], g_w_out, m_w_out[0], v_w_out[0], tok, "adamw_w_out")
    f_in, gn8, gn_land = _wait_copies("wait_pair_gather_w_in", ss, rs, bufs_e, plan_e, d_w_out)
    g_w_in = f_in.reshape(D_MODEL, W_IN_COLS)
    d_w_in, nm_w_in, nv_w_in = _adamw_large(w_in[0], g_w_in, m_w_in[0], v_w_in[0], tok, "adamw_w_in")
    res.update(_adamw_small("adamw_norm_g", ["norm_g"], (gn8, gn_land), pos, params))
    res["w_in"] = tuple(a[None] for a in (g_w_in, d_w_in, nm_w_in, nv_w_in))
    res["w_out"] = tuple(a[None] for a in (g_w_out, d_w_out, nm_w_out, nv_w_out))
    res["conv_w"] = tuple(a[None] for a in res["conv_w"])
    res["w_s"] = tuple(a.reshape(w_s.shape) for a in res["w_s"])
    res["b_s"] = tuple(a.reshape(b_s.shape) for a in res["b_s"])
    res["final_g"] = tuple(a.reshape(final_g.shape) for a in res["final_g"])

    order = ("norm_g", "w_in", "conv_w", "conv_b", "conv_ln_g", "conv_ln_b", "sgu_ln_g", "sgu_ln_b",
             "w_s", "b_s", "w_out", "final_g")
    out = [loss, grad_x[None]]
    for part in range(4):
        out += [res[name][part] for name in order]
    return tuple(out)
```
